```python
import jax, jax.numpy as jnp
from jax import lax
import numpy as np

D_MODEL = 2048
BATCH = 4
SEQ = 4096
DEPTH = 1

GM_WIDTH = D_MODEL // 2
GM_GROUPS = 8
GM_CHUNK = 128
MLA_HEADS = 16
Q_LORA = D_MODEL // 4
KV_LORA = D_MODEL // 4
QK_NOPE = 128
QK_ROPE = 64
V_DIM = 128
QK_DIM = QK_NOPE + QK_ROPE
MLA_SCALE = QK_DIM ** -0.5
ROPE_THETA = 10000.0
Q_BLOCK = 128
MEM_LEN = 256
MEM_HEADS = 4
MEM_HEAD_DIM = D_MODEL // 8
MEM_WIDTH = MEM_HEADS * MEM_HEAD_DIM
MEM_SCALE = MEM_HEAD_DIM ** -0.5
N_BRANCH = 3
IN_COLS = 2 * GM_WIDTH + Q_LORA + KV_LORA + QK_ROPE + MEM_WIDTH
N_EXPERTS = 32
TOP_K = 4
D_FF = D_MODEL
SWIGLU_LIMIT = 7.0
SWIGLU_ALPHA = 1.702
ROW_BLOCK = 256
EPS = 1e-6

kernel_name = "hybrid_gmlp_mla_memory_moe_block"


def rms_norm(t, g):
    t32 = t.astype(jnp.float32)
    y = t32 * lax.rsqrt(jnp.mean(t32 * t32, axis=-1, keepdims=True) + EPS)
    return (y * g.astype(jnp.float32)).astype(t.dtype)


def layer_norm(t, g, b):
    t32 = t.astype(jnp.float32)
    mu = jnp.mean(t32, axis=-1, keepdims=True)
    c = t32 - mu
    y = c * lax.rsqrt(jnp.mean(c * c, axis=-1, keepdims=True) + EPS)
    return (y * g.astype(jnp.float32) + b.astype(jnp.float32)).astype(t.dtype)


def rope_tables(positions):
    inv = 1.0 / (ROPE_THETA ** (jnp.arange(0, QK_ROPE, 2, dtype=jnp.float32) / QK_ROPE))
    ang = positions.astype(jnp.float32)[..., None] * inv
    return jnp.cos(ang)[:, :, None, :], jnp.sin(ang)[:, :, None, :]


def apply_rope(t, cos, sin):
    half = t.shape[-1] // 2
    t32 = t.astype(jnp.float32)
    t1, t2 = t32[..., :half], t32[..., half:]
    return jnp.concatenate([t1 * cos - t2 * sin, t1 * sin + t2 * cos], axis=-1).astype(t.dtype)


def gmlp_branch(u, v, ln_g, ln_b, w_s, b_s):
    B, S, _ = v.shape
    v = layer_norm(v, ln_g, ln_b)
    cg = GM_WIDTH // GM_GROUPS
    vc = v.reshape(B, S // GM_CHUNK, GM_CHUNK, GM_GROUPS, cg)
    causal = jnp.tril(jnp.ones((GM_CHUNK, GM_CHUNK), dtype=bool))
    w = jnp.where(causal[None], w_s, jnp.zeros_like(w_s)).astype(v.dtype)
    mixed = jnp.einsum('gts,bnsgc->bntgc', w, vc) + b_s.T.astype(v.dtype)[:, :, None]
    return u * mixed.reshape(B, S, GM_WIDTH)


def causal_block_attention(q, k, v, scale):
    B, S, H, Dq = q.shape
    n_blocks = S // Q_BLOCK
    qb = q.reshape(B, n_blocks, Q_BLOCK, H, Dq).transpose(1, 0, 2, 3, 4)
    key_pos = jnp.arange(S)

    def attend(args):
        q_blk, i = args
        s = jnp.einsum('bqhd,bkhd->bhqk', q_blk, k).astype(jnp.float32) * scale
        q_pos = i * Q_BLOCK + jnp.arange(Q_BLOCK)
        mask = key_pos[None, :] <= q_pos[:, None]
        s = jnp.where(mask[None, None], s, -jnp.inf)
        p = jax.nn.softmax(s, axis=-1).astype(v.dtype)
        return jnp.einsum('bhqk,bkhd->bqhd', p, v)

    o = lax.map(attend, (qb, jnp.arange(n_blocks)))
    return o.transpose(1, 0, 2, 3, 4).reshape(B, S, H, v.shape[-1])


def mla_branch(c_q, c_kv, k_rope, cos, sin, cq_g, w_uq, ckv_g, w_ukv, qn_g, kn_g):
    B, S, _ = c_q.shape
    H = MLA_HEADS
    q = (rms_norm(c_q, cq_g) @ w_uq).reshape(B, S, H, QK_DIM)
    kv = (rms_norm(c_kv, ckv_g) @ w_ukv).reshape(B, S, H, QK_NOPE + V_DIM)
    k_nope, v = kv[..., :QK_NOPE], kv[..., QK_NOPE:]
    k = jnp.concatenate([k_nope, jnp.broadcast_to(k_rope[:, :, None, :], (B, S, H, QK_ROPE))], axis=-1)
    q = rms_norm(q, qn_g)
    k = rms_norm(k, kn_g)
    q = jnp.concatenate([q[..., :QK_NOPE], apply_rope(q[..., QK_NOPE:], cos, sin)], axis=-1)
    k = jnp.concatenate([k[..., :QK_NOPE], apply_rope(k[..., QK_NOPE:], cos, sin)], axis=-1)
    o = causal_block_attention(q, k, v, MLA_SCALE)
    return o.reshape(B, S, H * V_DIM)


def memory_branch(q_mem, mem, mem_g, w_k, w_v, qn_g, kn_g):
    B, S, _ = q_mem.shape
    M = mem.shape[1]
    m = rms_norm(mem, mem_g)
    q = rms_norm(q_mem.reshape(B, S, MEM_HEADS, MEM_HEAD_DIM), qn_g)
    k = rms_norm((m @ w_k).reshape(B, M, MEM_HEADS, MEM_HEAD_DIM), kn_g)
    v = (m @ w_v).reshape(B, M, MEM_HEADS, MEM_HEAD_DIM)
    s = jnp.einsum('bqhd,bkhd->bhqk', q, k).astype(jnp.float32) * MEM_SCALE
    p = jax.nn.softmax(s, axis=-1).astype(v.dtype)
    o = jnp.einsum('bhqk,bkhd->bqhd', p, v)
    return o.reshape(B, S, MEM_WIDTH)


def clamped_swiglu_expert(xb, w_gu, b_gu, w_down, b_down):
    gu = xb @ w_gu + b_gu
    gate, up = gu[..., :D_FF], gu[..., D_FF:]
    gate = jnp.minimum(gate, SWIGLU_LIMIT)
    up = jnp.clip(up, -SWIGLU_LIMIT, SWIGLU_LIMIT)
    glu = gate * jax.nn.sigmoid(gate * SWIGLU_ALPHA)
    return ((up + 1.0) * glu) @ w_down + b_down


def moe(h, router_w, router_b, w_gu, b_gu, w_down, b_down):
    T, D = h.shape
    logits = (h @ router_w + router_b).astype(jnp.float32)
    top_vals, top_idx = lax.top_k(logits, TOP_K)
    gates = jax.nn.softmax(top_vals, axis=-1).astype(h.dtype)
    flat_e = top_idx.reshape(-1).astype(jnp.int32)
    flat_tok = jnp.repeat(jnp.arange(T, dtype=jnp.int32), TOP_K)
    flat_g = gates.reshape(-1)
    order = jnp.argsort(flat_e)
    e_sorted = flat_e[order]
    counts = jnp.bincount(flat_e, length=N_EXPERTS).astype(jnp.int32)
    starts = jnp.cumsum(counts) - counts
    padded = ((counts + ROW_BLOCK - 1) // ROW_BLOCK) * ROW_BLOCK
    pad_ends = jnp.cumsum(padded)
    pad_starts = pad_ends - padded
    rank = jnp.arange(T * TOP_K, dtype=jnp.int32) - starts[e_sorted]
    dest = pad_starts[e_sorted] + rank
    n_rows = T * TOP_K + N_EXPERTS * ROW_BLOCK
    n_blocks = n_rows // ROW_BLOCK
    row_tok = jnp.full((n_rows,), T, dtype=jnp.int32).at[dest].set(flat_tok[order])
    row_gate = jnp.zeros((n_rows,), h.dtype).at[dest].set(flat_g[order])
    block_start = jnp.arange(n_blocks, dtype=jnp.int32) * ROW_BLOCK
    block_e = jnp.minimum(jnp.searchsorted(pad_ends, block_start, side='right'), N_EXPERTS - 1)
    h_pad = jnp.concatenate([h, jnp.zeros((1, D), h.dtype)], axis=0)

    def run_block(args):
        tok, e = args
        return clamped_swiglu_expert(h_pad[tok], w_gu[e], b_gu[e], w_down[e], b_down[e])

    ys = lax.map(run_block, (row_tok.reshape(n_blocks, ROW_BLOCK), block_e))
    ys = ys.reshape(n_rows, D) * row_gate[:, None]
    out = jnp.zeros((T + 1, D), ys.dtype).at[row_tok].add(ys)
    return out[:T]


def setup_inputs(seed: int = 0) -> dict:
    key = jax.random.key(seed)
    ks = iter(jax.random.split(key, 48))
    f32 = jnp.float32

    def nrm(shape, scale):
        return jax.random.normal(next(ks), shape, f32) * scale

    def gain(shape):
        return 1.0 + 0.1 * jax.random.normal(next(ks), shape, f32)

    L = DEPTH
    x = nrm((BATCH, SEQ, D_MODEL), 1.0)
    mem = nrm((BATCH, MEM_LEN, D_MODEL), 1.0)
    offs = jax.random.randint(next(ks), (BATCH, 1), 0, 1024, dtype=jnp.int32)
    positions = offs + jnp.arange(SEQ, dtype=jnp.int32)[None, :]
    return {
        "x": x,
        "mem": mem,
        "positions": positions,
        "ln1_g": gain((L, D_MODEL)),
        "w_in": nrm((L, D_MODEL, IN_COLS), D_MODEL ** -0.5),
        "w_gate": nrm((L, D_MODEL, N_BRANCH * D_MODEL), D_MODEL ** -0.5),
        "b_gate": nrm((L, N_BRANCH * D_MODEL), 0.1),
        "gmlp_ln_g": gain((L, GM_WIDTH)),
        "gmlp_ln_b": nrm((L, GM_WIDTH), 0.02),
        "gmlp_ws": nrm((L, GM_GROUPS, GM_CHUNK, GM_CHUNK), GM_CHUNK ** -0.5),
        "gmlp_bs": gain((L, GM_GROUPS, GM_CHUNK)),
        "w_pa": nrm((L, GM_WIDTH, D_MODEL), GM_WIDTH ** -0.5),
        "mla_cq_g": gain((L, Q_LORA)),
        "mla_w_uq": nrm((L, Q_LORA, MLA_HEADS * QK_DIM), Q_LORA ** -0.5),
        "mla_ckv_g": gain((L, KV_LORA)),
        "mla_w_ukv": nrm((L, KV_LORA, MLA_HEADS * (QK_NOPE + V_DIM)), KV_LORA ** -0.5),
        "mla_qn_g": gain((L, QK_DIM)),
        "mla_kn_g": gain((L, QK_DIM)),
        "w_pb": nrm((L, MLA_HEADS * V_DIM, D_MODEL), (MLA_HEADS * V_DIM) ** -0.5),
        "mem_ln_g": gain((L, D_MODEL)),
        "mem_w_k": nrm((L, D_MODEL, MEM_WIDTH), D_MODEL ** -0.5),
        "mem_w_v": nrm((L, D_MODEL, MEM_WIDTH), D_MODEL ** -0.5),
        "mem_qn_g": gain((L, MEM_HEAD_DIM)),
        "mem_kn_g": gain((L, MEM_HEAD_DIM)),
        "w_pc": nrm((L, MEM_WIDTH, D_MODEL), MEM_WIDTH ** -0.5),
        "w_o": nrm((L, D_MODEL, D_MODEL), D_MODEL ** -0.5),
        "ln2_g": gain((L, D_MODEL)),
        "router_w": nrm((L, D_MODEL, N_EXPERTS), D_MODEL ** -0.5),
        "router_b": nrm((L, N_EXPERTS), 0.01),
        "moe_w_gu": nrm((L, N_EXPERTS, D_MODEL, 2 * D_FF), D_MODEL ** -0.5),
        "moe_b_gu": nrm((L, N_EXPERTS, 2 * D_FF), 0.01),
        "moe_w_down": nrm((L, N_EXPERTS, D_FF, D_MODEL), D_FF ** -0.5),
        "moe_b_down": nrm((L, N_EXPERTS, D_MODEL), 0.01),
    }


def reference(x, mem, positions, ln1_g, w_in, w_gate, b_gate,
              gmlp_ln_g, gmlp_ln_b, gmlp_ws, gmlp_bs, w_pa,
              mla_cq_g, mla_w_uq, mla_ckv_g, mla_w_ukv, mla_qn_g, mla_kn_g, w_pb,
              mem_ln_g, mem_w_k, mem_w_v, mem_qn_g, mem_kn_g, w_pc,
              w_o, ln2_g, router_w, router_b, moe_w_gu, moe_b_gu, moe_w_down, moe_b_down):
    B, S, D = x.shape
    cos, sin = rope_tables(positions)
    splits = [GM_WIDTH, 2 * GM_WIDTH, 2 * GM_WIDTH + Q_LORA,
              2 * GM_WIDTH + Q_LORA + KV_LORA, 2 * GM_WIDTH + Q_LORA + KV_LORA + QK_ROPE]
    for l in range(DEPTH):
        h = rms_norm(x, ln1_g[l])
        proj = h @ w_in[l]
        u, v, c_q, c_kv, k_rope, q_mem = jnp.split(proj, splits, axis=-1)
        a = gmlp_branch(u, v, gmlp_ln_g[l], gmlp_ln_b[l], gmlp_ws[l], gmlp_bs[l])
        b = mla_branch(c_q, c_kv, k_rope, cos, sin, mla_cq_g[l], mla_w_uq[l],
                       mla_ckv_g[l], mla_w_ukv[l], mla_qn_g[l], mla_kn_g[l])
        c = memory_branch(q_mem, mem, mem_ln_g[l], mem_w_k[l], mem_w_v[l], mem_qn_g[l], mem_kn_g[l])
        g = jax.nn.sigmoid(h @ w_gate[l] + b_gate[l]).reshape(B, S, N_BRANCH, D)
        merged = (g[:, :, 0] * (a @ w_pa[l])
                  + g[:, :, 1] * (b @ w_pb[l])
                  + g[:, :, 2] * (c @ w_pc[l]))
        x = x + merged @ w_o[l]
        h2 = rms_norm(x, ln2_g[l]).reshape(B * S, D)
        y = moe(h2, router_w[l], router_b[l], moe_w_gu[l], moe_b_gu[l], moe_w_down[l], moe_b_down[l])
        x = x + y.reshape(B, S, D)
    return x
```

```python
import functools

import numpy as np
import jax
import jax.numpy as jnp
from jax import lax
from jax.experimental import pallas as pl
from jax.experimental.pallas import tpu as pltpu

F32 = jnp.float32
BF16 = jnp.bfloat16

D_MODEL = 2048
GM_WIDTH = 1024
GM_GROUPS = 8
GM_CHUNK = 128
MLA_HEADS = 16
Q_LORA = 512
KV_LORA = 512
QK_NOPE = 128
QK_ROPE = 64
V_DIM = 128
QK_DIM = QK_NOPE + QK_ROPE
QK_PAD = 256
MLA_SCALE = QK_DIM ** -0.5
ROPE_THETA = 10000.0
MEM_LEN = 256
MEM_HEADS = 4
MEM_HEAD_DIM = 256
MEM_WIDTH = MEM_HEADS * MEM_HEAD_DIM
MEM_SCALE = MEM_HEAD_DIM ** -0.5
N_EXPERTS = 32
TOP_K = 4
D_FF = 2048
SWIGLU_LIMIT = 7.0
SWIGLU_ALPHA = 1.702
ROW_BLOCK = 256
EPS = 1e-6
LANES = 128
NEG_INF = float("-inf")

PROJ_COLS = 2 * GM_WIDTH + Q_LORA + KV_LORA + MEM_WIDTH
PG_COLS = PROJ_COLS + 3 * D_MODEL

MIB = 1024 * 1024


def _params(semantics, vmem_mib):
    return pltpu.CompilerParams(dimension_semantics=semantics, vmem_limit_bytes=vmem_mib * MIB)


def _const_spec(shape):
    nd = len(shape)
    return pl.BlockSpec(shape, lambda *_: (0,) * nd, pipeline_mode=pl.Buffered(1))


P1_TM = 1024
P1_TN = 1024


def _norm_proj_kernel(x_ref, g_ref, w_ref, b_ref, wr_ref, o_ref, kr_ref, h_scr, *, n_plain):
    j = pl.program_id(1)

    @pl.when(j == 0)
    def _():
        def body(c, carry):
            rows = pl.ds(pl.multiple_of(c * 128, 128), 128)
            x = x_ref[rows, :]
            ms = jnp.mean(x * x, axis=-1, keepdims=True)
            h_scr[rows, :] = (x * lax.rsqrt(ms + EPS) * g_ref[...]).astype(BF16)
            return carry

        lax.fori_loop(0, P1_TM // 128, body, 0)
        kr_ref[...] = jnp.dot(h_scr[...], wr_ref[...], preferred_element_type=F32)

    acc = jnp.dot(h_scr[...], w_ref[...], preferred_element_type=F32)
    gated = jax.nn.sigmoid(acc + b_ref[...])
    o_ref[...] = jnp.where(j >= n_plain, gated, acc).astype(BF16)


def _norm_proj(x2, ln1_g, w1, b1, wr):
    T = x2.shape[0]
    grid = (T // P1_TM, PG_COLS // P1_TN)
    return pl.pallas_call(
        functools.partial(_norm_proj_kernel, n_plain=PROJ_COLS // P1_TN),
        grid=grid,
        in_specs=[
            pl.BlockSpec((P1_TM, D_MODEL), lambda i, j: (i, 0)),
            pl.BlockSpec((1, D_MODEL), lambda i, j: (0, 0)),
            pl.BlockSpec((D_MODEL, P1_TN), lambda i, j: (0, j)),
            pl.BlockSpec((1, P1_TN), lambda i, j: (0, j)),
            pl.BlockSpec((D_MODEL, LANES), lambda i, j: (0, 0)),
        ],
        out_specs=[
            pl.BlockSpec((P1_TM, P1_TN), lambda i, j: (i, j)),
            pl.BlockSpec((P1_TM, LANES), lambda i, j: (i, 0)),
        ],
        out_shape=[
            jax.ShapeDtypeStruct((T, PG_COLS), BF16),
            jax.ShapeDtypeStruct((T, LANES), F32),
        ],
        scratch_shapes=[pltpu.VMEM((P1_TM, D_MODEL), BF16)],
        compiler_params=_params(("parallel", "arbitrary"), 48),
        name="norm_proj",
    )(x2, ln1_g, w1, b1, wr)


GM_TM = 512


def _gmlp_kernel(u_ref, v_ref, g0_ref, lng_ref, lnb_ref, ws_ref, bias_ref, wpa_ref, o_ref, vb_scr, a_scr):
    v = v_ref[...].astype(F32)
    mu = jnp.mean(v, axis=-1, keepdims=True)
    c = v - mu
    var = jnp.mean(c * c, axis=-1, keepdims=True)
    vb_scr[...] = (c * lax.rsqrt(var + EPS) * lng_ref[...] + lnb_ref[...]).astype(BF16)

    row = lax.broadcasted_iota(jnp.int32, (GM_CHUNK, GM_CHUNK), 0)
    col = lax.broadcasted_iota(jnp.int32, (GM_CHUNK, GM_CHUNK), 1)
    causal = col <= row
    for g in range(GM_GROUPS):
        cols = slice(g * GM_CHUNK, (g + 1) * GM_CHUNK)
        wg = jnp.where(causal, ws_ref[g], jnp.zeros((), BF16))
        for ch in range(GM_TM // GM_CHUNK):
            rows = slice(ch * GM_CHUNK, (ch + 1) * GM_CHUNK)
            mixed = jnp.dot(wg, vb_scr[rows, cols], preferred_element_type=F32) + bias_ref[:, cols]
            a_scr[rows, cols] = (u_ref[rows, cols].astype(F32) * mixed).astype(BF16)

    ma = jnp.dot(a_scr[...], wpa_ref[...], preferred_element_type=F32)
    o_ref[...] = (g0_ref[...].astype(F32) * ma).astype(BF16)


def _gmlp(pg, ln_g, ln_b, ws, bias_full, wpa):
    T = pg.shape[0]
    return pl.pallas_call(
        _gmlp_kernel,
        grid=(T // GM_TM,),
        in_specs=[
            pl.BlockSpec((GM_TM, GM_WIDTH), lambda i: (i, 0)),
            pl.BlockSpec((GM_TM, GM_WIDTH), lambda i: (i, 1)),
            pl.BlockSpec((GM_TM, D_MODEL), lambda i: (i, PROJ_COLS // D_MODEL)),
            _const_spec((1, GM_WIDTH)),
            _const_spec((1, GM_WIDTH)),
            _const_spec((GM_GROUPS, GM_CHUNK, GM_CHUNK)),
            _const_spec((GM_CHUNK, GM_WIDTH)),
            _const_spec((GM_WIDTH, D_MODEL)),
        ],
        out_specs=pl.BlockSpec((GM_TM, D_MODEL), lambda i: (i, 0)),
        out_shape=jax.ShapeDtypeStruct((T, D_MODEL), BF16),
        scratch_shapes=[pltpu.VMEM((GM_TM, GM_WIDTH), BF16), pltpu.VMEM((GM_TM, GM_WIDTH), BF16)],
        compiler_params=_params(("parallel",), 40),
        name="gmlp",
    )(pg, pg, pg, ln_g, ln_b, ws, bias_full, wpa)


QKV_TM = 1024


def _rope(t, cos, sin):
    return t * cos + pltpu.roll(t, QK_ROPE, 1) * sin


def _mla_qkv_kernel(cq_ref, ckv_ref, kr_ref, pos_ref, gcq_ref, gckv_ref, wq_ref, wkv_ref, gq_ref, gk_ref,
                    cst_ref, q_ref, k_ref, v_ref, cqn_scr, ckvn_scr, cos_scr, sin_scr):
    h = pl.program_id(1)

    @pl.when(h == 0)
    def _():
        cq = cq_ref[...].astype(F32)
        ms = jnp.mean(cq * cq, axis=-1, keepdims=True)
        cqn_scr[...] = (cq * lax.rsqrt(ms + EPS) * gcq_ref[...]).astype(BF16)
        ckv = ckv_ref[...].astype(F32)
        ms = jnp.mean(ckv * ckv, axis=-1, keepdims=True)
        ckvn_scr[...] = (ckv * lax.rsqrt(ms + EPS) * gckv_ref[...]).astype(BF16)
        ang = pos_ref[...].astype(F32) * cst_ref[0:1, :]
        cos_scr[...] = jnp.cos(ang) * cst_ref[1:2, :]
        sin_scr[...] = jnp.sin(ang) * cst_ref[2:3, :]

    cos = cos_scr[...]
    sin = sin_scr[...]
    inv_dim = 1.0 / QK_DIM

    yq = jnp.dot(cqn_scr[...], wq_ref[...], preferred_element_type=F32)
    lane = lax.broadcasted_iota(jnp.int32, (1, QK_PAD), 1)
    ssq = jnp.sum(jnp.where(lane < QK_DIM, yq * yq, 0.0), axis=-1, keepdims=True)
    yn = yq * (lax.rsqrt(ssq * inv_dim + EPS) * MLA_SCALE) * gq_ref[...]
    q_ref[:, :QK_NOPE] = yn[:, :QK_NOPE].astype(BF16)
    q_ref[:, QK_NOPE:] = _rope(yn[:, QK_NOPE:], cos, sin).astype(BF16)

    ykv = jnp.dot(ckvn_scr[...], wkv_ref[...], preferred_element_type=F32)
    kn = ykv[:, :QK_NOPE]
    kr = kr_ref[...]
    lane1 = lax.broadcasted_iota(jnp.int32, (1, LANES), 1)
    ssq = (jnp.sum(kn * kn, axis=-1, keepdims=True)
           + jnp.sum(jnp.where(lane1 < QK_ROPE, kr * kr, 0.0), axis=-1, keepdims=True))
    rs = lax.rsqrt(ssq * inv_dim + EPS)
    k_ref[:, :QK_NOPE] = (kn * rs * gk_ref[:, :QK_NOPE]).astype(BF16)
    k_ref[:, QK_NOPE:] = _rope(kr * rs * gk_ref[:, QK_NOPE:], cos, sin).astype(BF16)
    v_ref[...] = ykv[:, QK_NOPE:].astype(BF16)


def _mla_qkv(pg, kr, pos, gcq, gckv, wq, wkv, gq, gk, cst, B, S):
    T = pg.shape[0]
    spb = S // QKV_TM
    head_spec = lambda w: pl.BlockSpec((None, None, QKV_TM, w), lambda i, h: (i // spb, h, i % spb, 0))
    return pl.pallas_call(
        _mla_qkv_kernel,
        grid=(T // QKV_TM, MLA_HEADS),
        in_specs=[
            pl.BlockSpec((QKV_TM, Q_LORA), lambda i, h: (i, 2 * GM_WIDTH // Q_LORA)),
            pl.BlockSpec((QKV_TM, KV_LORA), lambda i, h: (i, 2 * GM_WIDTH // KV_LORA + 1)),
            pl.BlockSpec((QKV_TM, LANES), lambda i, h: (i, 0)),
            pl.BlockSpec((QKV_TM, 1), lambda i, h: (i, 0)),
            pl.BlockSpec((1, Q_LORA), lambda i, h: (0, 0)),
            pl.BlockSpec((1, KV_LORA), lambda i, h: (0, 0)),
            pl.BlockSpec((Q_LORA, QK_PAD), lambda i, h: (0, h)),
            pl.BlockSpec((KV_LORA, QK_NOPE + V_DIM), lambda i, h: (0, h)),
            pl.BlockSpec((1, QK_PAD), lambda i, h: (0, 0)),
            pl.BlockSpec((1, QK_PAD), lambda i, h: (0, 0)),
            pl.BlockSpec((8, LANES), lambda i, h: (0, 0)),
        ],
        out_specs=[head_spec(QK_PAD), head_spec(QK_PAD), head_spec(V_DIM)],
        out_shape=[
            jax.ShapeDtypeStruct((B, MLA_HEADS, S, QK_PAD), BF16),
            jax.ShapeDtypeStruct((B, MLA_HEADS, S, QK_PAD), BF16),
            jax.ShapeDtypeStruct((B, MLA_HEADS, S, V_DIM), BF16),
        ],
        scratch_shapes=[
            pltpu.VMEM((QKV_TM, Q_LORA), BF16),
            pltpu.VMEM((QKV_TM, KV_LORA), BF16),
            pltpu.VMEM((QKV_TM, LANES), F32),
            pltpu.VMEM((QKV_TM, LANES), F32),
        ],
        compiler_params=_params(("parallel", "arbitrary"), 40),
        name="mla_qkv",
    )(pg, pg, kr, pos, gcq, gckv, wq, wkv, gq, gk, cst)


FA_T = 512


def _flash_kernel(q_ref, k_ref, v_ref, o_ref, m_scr, l_scr, acc_scr):
    qi = pl.program_id(2)
    q = q_ref[...]
    m_scr[...] = jnp.full_like(m_scr, NEG_INF)
    l_scr[...] = jnp.zeros_like(l_scr)
    acc_scr[...] = jnp.zeros_like(acc_scr)

    def step(kb, diagonal):
        ks = pl.ds(pl.multiple_of(kb * FA_T, FA_T), FA_T)
        s = lax.dot_general(q, k_ref[ks, :], (((1,), (1,)), ((), ())), preferred_element_type=F32)
        if diagonal:
            row = lax.broadcasted_iota(jnp.int32, (FA_T, FA_T), 0)
            col = lax.broadcasted_iota(jnp.int32, (FA_T, FA_T), 1)
            s = jnp.where(col <= row, s, NEG_INF)
        m_prev = m_scr[...]
        m_new = jnp.maximum(m_prev, jnp.max(s, axis=-1, keepdims=True))
        alpha = jnp.exp(m_prev - m_new)
        p = jnp.exp(s - m_new)
        l_scr[...] = alpha * l_scr[...] + jnp.sum(p, axis=-1, keepdims=True)
        acc_scr[...] = alpha * acc_scr[...] + jnp.dot(p.astype(BF16), v_ref[ks, :], preferred_element_type=F32)
        m_scr[...] = m_new

    def body(kb, carry):
        step(kb, False)
        return carry

    lax.fori_loop(0, qi, body, 0)
    step(qi, True)
    o_ref[...] = (acc_scr[...] / l_scr[...]).astype(BF16)


def _flash(q, k, v):
    B, H, S, _ = q.shape
    nq = S // FA_T
    return pl.pallas_call(
        _flash_kernel,
        grid=(B, H, nq),
        in_specs=[
            pl.BlockSpec((None, None, FA_T, QK_PAD), lambda b, h, i: (b, h, i, 0)),
            pl.BlockSpec((None, None, S, QK_PAD), lambda b, h, i: (b, h, 0, 0)),
            pl.BlockSpec((None, None, S, V_DIM), lambda b, h, i: (b, h, 0, 0)),
        ],
        out_specs=pl.BlockSpec((FA_T, V_DIM), lambda b, h, i: (b * nq + i, h)),
        out_shape=jax.ShapeDtypeStruct((B * S, H * V_DIM), BF16),
        scratch_shapes=[
            pltpu.VMEM((FA_T, 1), F32),
            pltpu.VMEM((FA_T, 1), F32),
            pltpu.VMEM((FA_T, V_DIM), F32),
        ],
        compiler_params=_params(("parallel", "parallel", "arbitrary"), 32),
        name="mla_attention",
    )(q, k, v)


def _mem_kv_kernel(mem_ref, g_ref, wk_ref, wv_ref, kng_ref, k_ref, v_ref):
    m = mem_ref[...]
    ms = jnp.mean(m * m, axis=-1, keepdims=True)
    mb = (m * lax.rsqrt(ms + EPS) * g_ref[...]).astype(BF16)
    kk = jnp.dot(mb, wk_ref[...], preferred_element_type=F32)
    for h in range(MEM_HEADS):
        cols = slice(h * MEM_HEAD_DIM, (h + 1) * MEM_HEAD_DIM)
        kh = kk[:, cols]
        ms = jnp.mean(kh * kh, axis=-1, keepdims=True)
        k_ref[:, cols] = (kh * lax.rsqrt(ms + EPS) * kng_ref[...]).astype(BF16)
    v_ref[...] = jnp.dot(mb, wv_ref[...], preferred_element_type=F32).astype(BF16)


def _mem_kv(mem, g, wk, wv, kng):
    B = mem.shape[0]
    return pl.pallas_call(
        _mem_kv_kernel,
        grid=(B,),
        in_specs=[
            pl.BlockSpec((None, MEM_LEN, D_MODEL), lambda b: (b, 0, 0)),
            _const_spec((1, D_MODEL)),
            _const_spec((D_MODEL, MEM_WIDTH)),
            _const_spec((D_MODEL, MEM_WIDTH)),
            _const_spec((1, MEM_HEAD_DIM)),
        ],
        out_specs=[
            pl.BlockSpec((None, MEM_LEN, MEM_WIDTH), lambda b: (b, 0, 0)),
            pl.BlockSpec((None, MEM_LEN, MEM_WIDTH), lambda b: (b, 0, 0)),
        ],
        out_shape=[
            jax.ShapeDtypeStruct((B, MEM_LEN, MEM_WIDTH), BF16),
            jax.ShapeDtypeStruct((B, MEM_LEN, MEM_WIDTH), BF16),
        ],
        compiler_params=_params(("parallel",), 32),
        name="mem_kv",
    )(mem, g, wk, wv, kng)


MA_TM = 512


def _mem_attn_kernel(qm_ref, km_ref, vm_ref, g2_ref, m0_ref, qng_ref, wpc_ref, o_ref, c_scr):
    for h in range(MEM_HEADS):
        cols = slice(h * MEM_HEAD_DIM, (h + 1) * MEM_HEAD_DIM)
        qh = qm_ref[:, cols].astype(F32)
        ms = jnp.mean(qh * qh, axis=-1, keepdims=True)
        qn = (qh * (lax.rsqrt(ms + EPS) * MEM_SCALE) * qng_ref[...]).astype(BF16)
        s = lax.dot_general(qn, km_ref[:, cols], (((1,), (1,)), ((), ())), preferred_element_type=F32)
        e = jnp.exp(s - jnp.max(s, axis=-1, keepdims=True))
        p = (e / jnp.sum(e, axis=-1, keepdims=True)).astype(BF16)
        c_scr[:, cols] = jnp.dot(p, vm_ref[:, cols], preferred_element_type=F32).astype(BF16)
    mc = jnp.dot(c_scr[...], wpc_ref[...], preferred_element_type=F32)
    o_ref[...] = (m0_ref[...].astype(F32) + g2_ref[...].astype(F32) * mc).astype(BF16)


def _mem_attn(pg, km, vm, m0, qng, wpc, S):
    T = pg.shape[0]
    spb = S // MA_TM
    return pl.pallas_call(
        _mem_attn_kernel,
        grid=(T // MA_TM,),
        in_specs=[
            pl.BlockSpec((MA_TM, MEM_WIDTH), lambda i: (i, (2 * GM_WIDTH + Q_LORA + KV_LORA) // MEM_WIDTH)),
            pl.BlockSpec((None, MEM_LEN, MEM_WIDTH), lambda i: (i // spb, 0, 0)),
            pl.BlockSpec((None, MEM_LEN, MEM_WIDTH), lambda i: (i // spb, 0, 0)),
            pl.BlockSpec((MA_TM, D_MODEL), lambda i: (i, PROJ_COLS // D_MODEL + 2)),
            pl.BlockSpec((MA_TM, D_MODEL), lambda i: (i, 0)),
            _const_spec((1, MEM_HEAD_DIM)),
            _const_spec((MEM_WIDTH, D_MODEL)),
        ],
        out_specs=pl.BlockSpec((MA_TM, D_MODEL), lambda i: (i, 0)),
        out_shape=jax.ShapeDtypeStruct((T, D_MODEL), BF16),
        scratch_shapes=[pltpu.VMEM((MA_TM, MEM_WIDTH), BF16)],
        compiler_params=_params(("parallel",), 40),
        name="mem_attention",
    )(pg, km, vm, pg, m0, qng, wpc)


MG_TM = 256


def _merge_kernel(b_ref, g1_ref, m1_ref, x_ref, wpb_ref, wo_ref, ln2_ref, rw_ref, rb_ref,
                  x1_ref, h2_ref, idx_ref, gate_ref):
    mb = jnp.dot(b_ref[...], wpb_ref[...], preferred_element_type=F32)
    merged = m1_ref[...].astype(F32) + g1_ref[...].astype(F32) * mb
    x1 = x_ref[...] + jnp.dot(merged.astype(BF16), wo_ref[...], preferred_element_type=F32)
    x1_ref[...] = x1
    ms = jnp.mean(x1 * x1, axis=-1, keepdims=True)
    h2 = x1 * lax.rsqrt(ms + EPS) * ln2_ref[...]
    h2_ref[...] = h2

    logits = jnp.dot(h2, rw_ref[...], preferred_element_type=F32, precision=lax.Precision.HIGHEST)
    lane = lax.broadcasted_iota(jnp.int32, (MG_TM, LANES), 1)
    work = jnp.where(lane < N_EXPERTS, logits + rb_ref[...], NEG_INF)
    idx_out = jnp.zeros((MG_TM, LANES), jnp.int32)
    val_out = jnp.zeros((MG_TM, LANES), F32)
    top = None
    denom = jnp.zeros((MG_TM, 1), F32)
    for k in range(TOP_K):
        mk = jnp.max(work, axis=-1, keepdims=True)
        ik = jnp.min(jnp.where(work == mk, lane, LANES), axis=-1, keepdims=True)
        work = jnp.where(lane == ik, NEG_INF, work)
        if top is None:
            top = mk
        ek = jnp.exp(mk - top)
        denom = denom + ek
        idx_out = jnp.where(lane == k, ik, idx_out)
        val_out = jnp.where(lane == k, ek, val_out)
    idx_ref[...] = idx_out
    gate_ref[...] = val_out / denom


def _merge(b_attn, pg, m1, x2, wpb, wo, ln2, rw, rb):
    T = x2.shape[0]
    row = lambda w: pl.BlockSpec((MG_TM, w), lambda i: (i, 0))
    return pl.pallas_call(
        _merge_kernel,
        grid=(T // MG_TM,),
        in_specs=[
            row(D_MODEL),
            pl.BlockSpec((MG_TM, D_MODEL), lambda i: (i, PROJ_COLS // D_MODEL + 1)),
            row(D_MODEL),
            row(D_MODEL),
            _const_spec((MLA_HEADS * V_DIM, D_MODEL)),
            _const_spec((D_MODEL, D_MODEL)),
            _const_spec((1, D_MODEL)),
            _const_spec((D_MODEL, LANES)),
            _const_spec((1, LANES)),
        ],
        out_specs=[row(D_MODEL), row(D_MODEL), row(LANES), row(LANES)],
        out_shape=[
            jax.ShapeDtypeStruct((T, D_MODEL), F32),
            jax.ShapeDtypeStruct((T, D_MODEL), F32),
            jax.ShapeDtypeStruct((T, LANES), jnp.int32),
            jax.ShapeDtypeStruct((T, LANES), F32),
        ],
        compiler_params=_params(("parallel",), 48),
        name="merge_router",
    )(b_attn, pg, m1, x2, wpb, wo, ln2, rw, rb)


FF_CHUNK = 1024


def _row_copy(src_hbm, src_row, dst, dst_row, sem):
    return pltpu.make_async_copy(src_hbm.at[pl.ds(src_row, 1), :], dst.at[pl.ds(dst_row, 1), :], sem)


def _expert_kernel(be_ref, nb_ref, tok_ref, h2_hbm, wgu_ref, bgu_ref, wd_ref, bd_ref, y_ref, xbuf, sem):
    m = pl.program_id(0)

    @pl.when(m < nb_ref[0])
    def _():
        def issue(r, carry):
            _row_copy(h2_hbm, tok_ref[0, r], xbuf, r, sem).start()
            return carry

        lax.fori_loop(0, ROW_BLOCK, issue, 0)

        def drain(r, carry):
            _row_copy(h2_hbm, 0, xbuf, 0, sem).wait()
            return carry

        lax.fori_loop(0, ROW_BLOCK, drain, 0)

        xb = xbuf[...].astype(BF16)
        acc = jnp.zeros((ROW_BLOCK, D_MODEL), F32)
        for fc in range(D_FF // FF_CHUNK):
            gc = slice(fc * FF_CHUNK, (fc + 1) * FF_CHUNK)
            uc = slice(D_FF + fc * FF_CHUNK, D_FF + (fc + 1) * FF_CHUNK)
            gate = jnp.dot(xb, wgu_ref[:, gc], preferred_element_type=F32) + bgu_ref[:, gc]
            up = jnp.dot(xb, wgu_ref[:, uc], preferred_element_type=F32) + bgu_ref[:, uc]
            gate = jnp.minimum(gate, SWIGLU_LIMIT)
            up = jnp.clip(up, -SWIGLU_LIMIT, SWIGLU_LIMIT)
            glu = gate * jax.nn.sigmoid(gate * SWIGLU_ALPHA)
            act = ((up + 1.0) * glu).astype(BF16)
            acc = acc + jnp.dot(act, wd_ref[gc, :], preferred_element_type=F32)
        y_ref[...] = acc + bd_ref[...]

    @pl.when(m >= nb_ref[0])
    def _():
        y_ref[...] = jnp.zeros_like(y_ref)


def _experts(block_e, n_used, row_tok3, h2, wgu, bgu, wd, bd):
    n_blocks = row_tok3.shape[0]
    grid_spec = pltpu.PrefetchScalarGridSpec(
        num_scalar_prefetch=2,
        grid=(n_blocks,),
        in_specs=[
            pl.BlockSpec((None, 1, ROW_BLOCK), lambda m, be, nb: (m, 0, 0), memory_space=pltpu.SMEM),
            pl.BlockSpec(memory_space=pl.ANY),
            pl.BlockSpec((None, D_MODEL, 2 * D_FF), lambda m, be, nb: (be[m], 0, 0),
                         pipeline_mode=pl.Buffered(1)),
            pl.BlockSpec((None, 1, 2 * D_FF), lambda m, be, nb: (be[m], 0, 0)),
            pl.BlockSpec((None, D_FF, D_MODEL), lambda m, be, nb: (be[m], 0, 0),
                         pipeline_mode=pl.Buffered(1)),
            pl.BlockSpec((None, 1, D_MODEL), lambda m, be, nb: (be[m], 0, 0)),
        ],
        out_specs=pl.BlockSpec((ROW_BLOCK, D_MODEL), lambda m, be, nb: (m, 0)),
        scratch_shapes=[pltpu.VMEM((ROW_BLOCK, D_MODEL), F32), pltpu.SemaphoreType.DMA(())],
    )
    return pl.pallas_call(
        _expert_kernel,
        grid_spec=grid_spec,
        out_shape=jax.ShapeDtypeStruct((n_blocks * ROW_BLOCK, D_MODEL), F32),
        compiler_params=_params(("arbitrary",), 56),
        name="moe_experts",
    )(block_e, n_used, row_tok3, h2, wgu, bgu, wd, bd)


CB_TM = 128


def _combine_kernel(pos_ref, ys_hbm, x1_ref, gate_ref, o_ref, buf, sem):
    def issue(j, carry):
        _row_copy(ys_hbm, pos_ref[0, j], buf.at[j & (TOP_K - 1)], j >> 2, sem).start()
        return carry

    lax.fori_loop(0, CB_TM * TOP_K, issue, 0)

    def drain(j, carry):
        _row_copy(ys_hbm, 0, buf.at[0], 0, sem).wait()
        return carry

    lax.fori_loop(0, CB_TM * TOP_K, drain, 0)

    out = x1_ref[...]
    gates = gate_ref[...]
    for k in range(TOP_K):
        out = out + gates[:, k:k + 1] * buf[k]
    o_ref[...] = out


def _combine(pos3, ys, x1, gates):
    T = x1.shape[0]
    return pl.pallas_call(
        _combine_kernel,
        grid=(T // CB_TM,),
        in_specs=[
            pl.BlockSpec((None, 1, CB_TM * TOP_K), lambda i: (i, 0, 0), memory_space=pltpu.SMEM),
            pl.BlockSpec(memory_space=pl.ANY),
            pl.BlockSpec((CB_TM, D_MODEL), lambda i: (i, 0)),
            pl.BlockSpec((CB_TM, LANES), lambda i: (i, 0)),
        ],
        out_specs=pl.BlockSpec((CB_TM, D_MODEL), lambda i: (i, 0)),
        out_shape=jax.ShapeDtypeStruct((T, D_MODEL), F32),
        scratch_shapes=[pltpu.VMEM((TOP_K, CB_TM, D_MODEL), F32), pltpu.SemaphoreType.DMA(())],
        compiler_params=_params(("arbitrary",), 32),
        name="moe_combine",
    )(pos3, ys, x1, gates)


def _dispatch_tables(top_idx, T):
    flat_e = top_idx.reshape(-1)
    flat_tok = jnp.repeat(jnp.arange(T, dtype=jnp.int32), TOP_K)
    order = jnp.argsort(flat_e)
    e_sorted = flat_e[order]
    counts = jnp.bincount(flat_e, length=N_EXPERTS).astype(jnp.int32)
    starts = jnp.cumsum(counts) - counts
    padded = ((counts + ROW_BLOCK - 1) // ROW_BLOCK) * ROW_BLOCK
    pad_ends = jnp.cumsum(padded)
    pad_starts = pad_ends - padded
    rank = jnp.arange(T * TOP_K, dtype=jnp.int32) - starts[e_sorted]
    dest = (pad_starts[e_sorted] + rank).astype(jnp.int32)
    n_rows = T * TOP_K + N_EXPERTS * ROW_BLOCK
    n_blocks = n_rows // ROW_BLOCK
    row_tok = jnp.zeros((n_rows,), jnp.int32).at[dest].set(flat_tok[order])
    pos = jnp.zeros((T * TOP_K,), jnp.int32).at[order].set(dest)
    block_start = jnp.arange(n_blocks, dtype=jnp.int32) * ROW_BLOCK
    block_e = jnp.minimum(jnp.searchsorted(pad_ends, block_start, side='right'), N_EXPERTS - 1).astype(jnp.int32)
    n_used = (pad_ends[-1:] // ROW_BLOCK).astype(jnp.int32)
    return block_e, n_used, row_tok.reshape(n_blocks, 1, ROW_BLOCK), pos.reshape(T // CB_TM, 1, CB_TM * TOP_K)


def _rope_constants():
    lane = np.arange(LANES)
    half = QK_ROPE // 2
    inv = 1.0 / (ROPE_THETA ** (np.arange(0, QK_ROPE, 2, dtype=np.float32) / QK_ROPE))
    cst = np.zeros((8, LANES), np.float32)
    cst[0, :QK_ROPE] = inv.astype(np.float32)[lane[:QK_ROPE] % half]
    cst[1, :QK_ROPE] = 1.0
    cst[2, :half] = -1.0
    cst[2, half:QK_ROPE] = 1.0
    return jnp.asarray(cst)


def _swap_halves(a):
    half = QK_ROPE // 2
    return jnp.concatenate([a[..., half:], a[..., :half]], axis=-1)


def kernel(x, mem, positions, ln1_g, w_in, w_gate, b_gate, gmlp_ln_g, gmlp_ln_b, gmlp_ws, gmlp_bs, w_pa,
           mla_cq_g, mla_w_uq, mla_ckv_g, mla_w_ukv, mla_qn_g, mla_kn_g, w_pb, mem_ln_g, mem_w_k, mem_w_v,
           mem_qn_g, mem_kn_g, w_pc, w_o, ln2_g, router_w, router_b, moe_w_gu, moe_b_gu, moe_w_down,
           moe_b_down):
    B, S, D = x.shape
    T = B * S
    x2 = x.reshape(T, D)
    for l in range(ln1_g.shape[0]):
        o_kr = 2 * GM_WIDTH + Q_LORA + KV_LORA
        o_qm = o_kr + QK_ROPE
        wi = w_in[l]
        w1 = jnp.concatenate([wi[:, :o_kr], wi[:, o_qm:], w_gate[l]], axis=1).astype(BF16)
        b1 = jnp.concatenate([jnp.zeros((PROJ_COLS,), F32), b_gate[l]])[None, :]
        w_kr = wi[:, o_kr:o_qm]
        wr = jnp.concatenate([w_kr, _swap_halves(w_kr)], axis=1).astype(BF16)

        wq3 = mla_w_uq[l].reshape(Q_LORA, MLA_HEADS, QK_DIM)
        wq = jnp.concatenate([wq3, _swap_halves(wq3[..., QK_NOPE:])], axis=-1)
        wq = wq.reshape(Q_LORA, MLA_HEADS * QK_PAD).astype(BF16)
        wkv = mla_w_ukv[l].astype(BF16)
        gq = jnp.concatenate([mla_qn_g[l], _swap_halves(mla_qn_g[l][QK_NOPE:])])[None, :]
        gk = jnp.concatenate([mla_kn_g[l], _swap_halves(mla_kn_g[l][QK_NOPE:])])[None, :]

        bias_full = jnp.broadcast_to(gmlp_bs[l].T[:, :, None], (GM_CHUNK, GM_GROUPS, GM_CHUNK))
        bias_full = bias_full.reshape(GM_CHUNK, GM_WIDTH)

        rw = jnp.pad(router_w[l], ((0, 0), (0, LANES - N_EXPERTS)))
        rb = jnp.pad(router_b[l], (0, LANES - N_EXPERTS))[None, :]

        pg, kr = _norm_proj(x2, ln1_g[l][None, :], w1, b1, wr)
        m0 = _gmlp(pg, gmlp_ln_g[l][None, :], gmlp_ln_b[l][None, :], gmlp_ws[l].astype(BF16), bias_full,
                   w_pa[l].astype(BF16))
        q, k, v = _mla_qkv(pg, kr, positions.reshape(T, 1), mla_cq_g[l][None, :], mla_ckv_g[l][None, :],
                           wq, wkv, gq, gk, _rope_constants(), B, S)
        b_attn = _flash(q, k, v)
        km, vm = _mem_kv(mem, mem_ln_g[l][None, :], mem_w_k[l].astype(BF16), mem_w_v[l].astype(BF16),
                         mem_kn_g[l][None, :])
        m1 = _mem_attn(pg, km, vm, m0, mem_qn_g[l][None, :], w_pc[l].astype(BF16), S)
        x1, h2, top_idx, gates = _merge(b_attn, pg, m1, x2, w_pb[l].astype(BF16), w_o[l].astype(BF16),
                                        ln2_g[l][None, :], rw, rb)

        block_e, n_used, row_tok3, pos3 = _dispatch_tables(top_idx[:, :TOP_K], T)
        ys = _experts(block_e, n_used, row_tok3, h2, moe_w_gu[l].astype(BF16), moe_b_gu[l][:, None, :],
                      moe_w_down[l].astype(BF16), moe_b_down[l][:, None, :])
        x2 = _combine(pos3, ys, x1, gates)
    return x2.reshape(B, S, D)
```

```python
import functools

import numpy as np
import jax
import jax.numpy as jnp
from jax import lax
from jax.experimental import pallas as pl
from jax.experimental.pallas import tpu as pltpu

F32 = jnp.float32
BF16 = jnp.bfloat16

D_MODEL = 2048
GM_WIDTH = 1024
GM_GROUPS = 8
GM_CHUNK = 128
MLA_HEADS = 16
Q_LORA = 512
KV_LORA = 512
QK_NOPE = 128
QK_ROPE = 64
V_DIM = 128
QK_DIM = QK_NOPE + QK_ROPE
QK_PAD = 256
MLA_SCALE = QK_DIM ** -0.5
ROPE_THETA = 10000.0
MEM_LEN = 256
MEM_HEADS = 4
MEM_HEAD_DIM = 256
MEM_WIDTH = MEM_HEADS * MEM_HEAD_DIM
MEM_SCALE = MEM_HEAD_DIM ** -0.5
N_EXPERTS = 32
TOP_K = 4
D_FF = 2048
SWIGLU_LIMIT = 7.0
SWIGLU_ALPHA = 1.702
ROW_BLOCK = 256
EPS = 1e-6
LANES = 128
NEG_INF = float("-inf")

PROJ_COLS = 2 * GM_WIDTH + Q_LORA + KV_LORA + MEM_WIDTH
PG_COLS = PROJ_COLS + 3 * D_MODEL

MIB = 1024 * 1024


def _params(semantics, vmem_mib):
    return pltpu.CompilerParams(dimension_semantics=semantics, vmem_limit_bytes=vmem_mib * MIB)


def _const_spec(shape):
    nd = len(shape)
    return pl.BlockSpec(shape, lambda *_: (0,) * nd, pipeline_mode=pl.Buffered(1))


P1_TM = 1024
P1_TN = 1024


def _norm_proj_kernel(x_ref, g_ref, w_ref, b_ref, wr_ref, o_ref, kr_ref, h_scr, *, n_plain):
    j = pl.program_id(1)

    @pl.when(j == 0)
    def _():
        def body(c, carry):
            rows = pl.ds(pl.multiple_of(c * 128, 128), 128)
            x = x_ref[rows, :]
            ms = jnp.mean(x * x, axis=-1, keepdims=True)
            h_scr[rows, :] = (x * lax.rsqrt(ms + EPS) * g_ref[...]).astype(BF16)
            return carry

        lax.fori_loop(0, P1_TM // 128, body, 0)
        kr_ref[...] = jnp.dot(h_scr[...], wr_ref[...], preferred_element_type=F32)

    acc = jnp.dot(h_scr[...], w_ref[...], preferred_element_type=F32)
    gated = jax.nn.sigmoid(acc + b_ref[...])
    o_ref[...] = jnp.where(j >= n_plain, gated, acc).astype(BF16)


def _norm_proj(x2, ln1_g, w1, b1, wr):
    T = x2.shape[0]
    grid = (T // P1_TM, PG_COLS // P1_TN)
    return pl.pallas_call(
        functools.partial(_norm_proj_kernel, n_plain=PROJ_COLS // P1_TN),
        grid=grid,
        in_specs=[
            pl.BlockSpec((P1_TM, D_MODEL), lambda i, j: (i, 0)),
            pl.BlockSpec((1, D_MODEL), lambda i, j: (0, 0)),
            pl.BlockSpec((D_MODEL, P1_TN), lambda i, j: (0, j)),
            pl.BlockSpec((1, P1_TN), lambda i, j: (0, j)),
            pl.BlockSpec((D_MODEL, LANES), lambda i, j: (0, 0)),
        ],
        out_specs=[
            pl.BlockSpec((P1_TM, P1_TN), lambda i, j: (i, j)),
            pl.BlockSpec((P1_TM, LANES), lambda i, j: (i, 0)),
        ],
        out_shape=[
            jax.ShapeDtypeStruct((T, PG_COLS), BF16),
            jax.ShapeDtypeStruct((T, LANES), F32),
        ],
        scratch_shapes=[pltpu.VMEM((P1_TM, D_MODEL), BF16)],
        compiler_params=_params(("parallel", "arbitrary"), 48),
        name="norm_proj",
    )(x2, ln1_g, w1, b1, wr)


GM_TM = 512


def _gmlp_kernel(u_ref, v_ref, g0_ref, lng_ref, lnb_ref, ws_ref, bias_ref, wpa_ref, o_ref, vb_scr, a_scr):
    v = v_ref[...].astype(F32)
    mu = jnp.mean(v, axis=-1, keepdims=True)
    c = v - mu
    var = jnp.mean(c * c, axis=-1, keepdims=True)
    vb_scr[...] = (c * lax.rsqrt(var + EPS) * lng_ref[...] + lnb_ref[...]).astype(BF16)

    row = lax.broadcasted_iota(jnp.int32, (GM_CHUNK, GM_CHUNK), 0)
    col = lax.broadcasted_iota(jnp.int32, (GM_CHUNK, GM_CHUNK), 1)
    causal = col <= row
    for g in range(GM_GROUPS):
        cols = slice(g * GM_CHUNK, (g + 1) * GM_CHUNK)
        wg = jnp.where(causal, ws_ref[g], jnp.zeros((), BF16))
        for ch in range(GM_TM // GM_CHUNK):
            rows = slice(ch * GM_CHUNK, (ch + 1) * GM_CHUNK)
            mixed = jnp.dot(wg, vb_scr[rows, cols], preferred_element_type=F32) + bias_ref[:, cols]
            a_scr[rows, cols] = (u_ref[rows, cols].astype(F32) * mixed).astype(BF16)

    ma = jnp.dot(a_scr[...], wpa_ref[...], preferred_element_type=F32)
    o_ref[...] = (g0_ref[...].astype(F32) * ma).astype(BF16)


def _gmlp(pg, ln_g, ln_b, ws, bias_full, wpa):
    T = pg.shape[0]
    return pl.pallas_call(
        _gmlp_kernel,
        grid=(T // GM_TM,),
        in_specs=[
            pl.BlockSpec((GM_TM, GM_WIDTH), lambda i: (i, 0)),
            pl.BlockSpec((GM_TM, GM_WIDTH), lambda i: (i, 1)),
            pl.BlockSpec((GM_TM, D_MODEL), lambda i: (i, PROJ_COLS // D_MODEL)),
            _const_spec((1, GM_WIDTH)),
            _const_spec((1, GM_WIDTH)),
            _const_spec((GM_GROUPS, GM_CHUNK, GM_CHUNK)),
            _const_spec((GM_CHUNK, GM_WIDTH)),
            _const_spec((GM_WIDTH, D_MODEL)),
        ],
        out_specs=pl.BlockSpec((GM_TM, D_MODEL), lambda i: (i, 0)),
        out_shape=jax.ShapeDtypeStruct((T, D_MODEL), BF16),
        scratch_shapes=[pltpu.VMEM((GM_TM, GM_WIDTH), BF16), pltpu.VMEM((GM_TM, GM_WIDTH), BF16)],
        compiler_params=_params(("parallel",), 40),
        name="gmlp",
    )(pg, pg, pg, ln_g, ln_b, ws, bias_full, wpa)


QKV_TM = 1024
HEAD_BLK = 2
FA_T = 512


def _rope(t, cos, sin):
    return t * cos + pltpu.roll(t, QK_ROPE, 1) * sin


def _mla_qkv_kernel(cq_ref, ckv_ref, kr_ref, pos_ref, gcq_ref, gckv_ref, wq_ref, wkv_ref, gq_ref, gk_ref,
                    cst_ref, q_ref, k_ref, vt_ref, cqn_scr, ckvn_scr, cos_scr, sin_scr, krsq_scr):
    hb = pl.program_id(1)

    @pl.when(hb == 0)
    def _():
        cq = cq_ref[...].astype(F32)
        ms = jnp.mean(cq * cq, axis=-1, keepdims=True)
        cqn_scr[...] = (cq * lax.rsqrt(ms + EPS) * gcq_ref[...]).astype(BF16)
        ckv = ckv_ref[...].astype(F32)
        ms = jnp.mean(ckv * ckv, axis=-1, keepdims=True)
        ckvn_scr[...] = (ckv * lax.rsqrt(ms + EPS) * gckv_ref[...]).astype(BF16)
        ang = pos_ref[...].astype(F32) * cst_ref[0:1, :]
        cos_scr[...] = jnp.cos(ang) * cst_ref[1:2, :]
        sin_scr[...] = jnp.sin(ang) * cst_ref[2:3, :]
        kr = kr_ref[...]
        krsq_scr[...] = 0.5 * jnp.sum(kr * kr, axis=-1, keepdims=True)

    cos = cos_scr[...]
    sin = sin_scr[...]
    inv_dim = 1.0 / QK_DIM
    kr = kr_ref[...]

    yq2 = jnp.dot(cqn_scr[...], wq_ref[...], preferred_element_type=F32)
    ykv2 = jnp.dot(ckvn_scr[...], wkv_ref[...], preferred_element_type=F32)
    for hh in range(HEAD_BLK):
        yq = yq2[:, hh * QK_PAD:(hh + 1) * QK_PAD]
        qn = yq[:, :QK_NOPE]
        qt = yq[:, QK_NOPE:]
        ssq = jnp.sum(qn * qn, axis=-1, keepdims=True) + 0.5 * jnp.sum(qt * qt, axis=-1, keepdims=True)
        rs = lax.rsqrt(ssq * inv_dim + EPS) * MLA_SCALE
        q_ref[hh, :, :QK_NOPE] = (qn * rs * gq_ref[:, :QK_NOPE]).astype(BF16)
        q_ref[hh, :, QK_NOPE:] = _rope(qt * rs * gq_ref[:, QK_NOPE:], cos, sin).astype(BF16)

        ykv = ykv2[:, hh * (QK_NOPE + V_DIM):(hh + 1) * (QK_NOPE + V_DIM)]
        kn = ykv[:, :QK_NOPE]
        ssq = jnp.sum(kn * kn, axis=-1, keepdims=True) + krsq_scr[...]
        rs = lax.rsqrt(ssq * inv_dim + EPS)
        k_ref[hh, :, :QK_NOPE] = (kn * rs * gk_ref[:, :QK_NOPE]).astype(BF16)
        k_ref[hh, :, QK_NOPE:] = _rope(kr * rs * gk_ref[:, QK_NOPE:], cos, sin).astype(BF16)
        vv = ykv[:, QK_NOPE:]
        for c in range(QKV_TM // FA_T):
            vt_ref[hh, c] = vv[c * FA_T:(c + 1) * FA_T, :].T.astype(BF16)


def _mla_qkv(pg, kr, pos, gcq, gckv, wq, wkv, gq, gk, cst, B, S):
    T = pg.shape[0]
    spb = S // QKV_TM
    cpt = QKV_TM // FA_T
    head_spec = lambda w: pl.BlockSpec((None, HEAD_BLK, QKV_TM, w), lambda i, h: (i // spb, h, i % spb, 0))
    return pl.pallas_call(
        _mla_qkv_kernel,
        grid=(T // QKV_TM, MLA_HEADS // HEAD_BLK),
        in_specs=[
            pl.BlockSpec((QKV_TM, Q_LORA), lambda i, h: (i, 2 * GM_WIDTH // Q_LORA)),
            pl.BlockSpec((QKV_TM, KV_LORA), lambda i, h: (i, 2 * GM_WIDTH // KV_LORA + 1)),
            pl.BlockSpec((QKV_TM, LANES), lambda i, h: (i, 0)),
            pl.BlockSpec((QKV_TM, 1), lambda i, h: (i, 0)),
            pl.BlockSpec((1, Q_LORA), lambda i, h: (0, 0)),
            pl.BlockSpec((1, KV_LORA), lambda i, h: (0, 0)),
            pl.BlockSpec((Q_LORA, HEAD_BLK * QK_PAD), lambda i, h: (0, h)),
            pl.BlockSpec((KV_LORA, HEAD_BLK * (QK_NOPE + V_DIM)), lambda i, h: (0, h)),
            pl.BlockSpec((1, QK_PAD), lambda i, h: (0, 0)),
            pl.BlockSpec((1, QK_PAD), lambda i, h: (0, 0)),
            pl.BlockSpec((8, LANES), lambda i, h: (0, 0)),
        ],
        out_specs=[
            head_spec(QK_PAD),
            head_spec(QK_PAD),
            pl.BlockSpec((None, HEAD_BLK, cpt, V_DIM, FA_T), lambda i, h: (i // spb, h, i % spb, 0, 0)),
        ],
        out_shape=[
            jax.ShapeDtypeStruct((B, MLA_HEADS, S, QK_PAD), BF16),
            jax.ShapeDtypeStruct((B, MLA_HEADS, S, QK_PAD), BF16),
            jax.ShapeDtypeStruct((B, MLA_HEADS, S // FA_T, V_DIM, FA_T), BF16),
        ],
        scratch_shapes=[
            pltpu.VMEM((QKV_TM, Q_LORA), BF16),
            pltpu.VMEM((QKV_TM, KV_LORA), BF16),
            pltpu.VMEM((QKV_TM, LANES), F32),
            pltpu.VMEM((QKV_TM, LANES), F32),
            pltpu.VMEM((QKV_TM, 1), F32),
        ],
        compiler_params=_params(("parallel", "arbitrary"), 48),
        name="mla_qkv",
    )(pg, pg, kr, pos, gcq, gckv, wq, wkv, gq, gk, cst)


def _flash_kernel(q_ref, k_ref, vt_ref, o_ref, m_scr, l_scr, acc_scr):
    qi = pl.program_id(2)
    m_scr[...] = jnp.full_like(m_scr, NEG_INF)
    l_scr[...] = jnp.zeros_like(l_scr)
    acc_scr[...] = jnp.zeros_like(acc_scr)

    def step(kb, diagonal):
        ks = pl.ds(pl.multiple_of(kb * FA_T, FA_T), FA_T)
        for hh in range(HEAD_BLK):
            st = lax.dot_general(k_ref[hh, ks, :], q_ref[hh], (((1,), (1,)), ((), ())),
                                 preferred_element_type=F32)
            if diagonal:
                krow = lax.broadcasted_iota(jnp.int32, (FA_T, FA_T), 0)
                qcol = lax.broadcasted_iota(jnp.int32, (FA_T, FA_T), 1)
                st = jnp.where(krow <= qcol, st, NEG_INF)
            m_prev = m_scr[hh]
            m_new = jnp.maximum(m_prev, jnp.max(st, axis=0, keepdims=True))
            alpha = jnp.exp(m_prev - m_new)
            p = jnp.exp(st - m_new)
            l_scr[hh] = alpha * l_scr[hh] + jnp.sum(p, axis=0, keepdims=True)
            acc_scr[hh] = alpha * acc_scr[hh] + jnp.dot(vt_ref[hh, kb], p.astype(BF16),
                                                        preferred_element_type=F32)
            m_scr[hh] = m_new

    def body(kb, carry):
        step(kb, False)
        return carry

    lax.fori_loop(0, qi, body, 0)
    step(qi, True)
    for hh in range(HEAD_BLK):
        o_ref[:, hh * V_DIM:(hh + 1) * V_DIM] = (acc_scr[hh] / l_scr[hh]).T.astype(BF16)


def _flash(q, k, vt):
    B, H, S, _ = q.shape
    nq = S // FA_T
    return pl.pallas_call(
        _flash_kernel,
        grid=(B, H // HEAD_BLK, nq),
        in_specs=[
            pl.BlockSpec((None, HEAD_BLK, FA_T, QK_PAD), lambda b, h, i: (b, h, i, 0)),
            pl.BlockSpec((None, HEAD_BLK, S, QK_PAD), lambda b, h, i: (b, h, 0, 0)),
            pl.BlockSpec((None, HEAD_BLK, nq, V_DIM, FA_T), lambda b, h, i: (b, h, 0, 0, 0)),
        ],
        out_specs=pl.BlockSpec((FA_T, HEAD_BLK * V_DIM), lambda b, h, i: (b * nq + i, h)),
        out_shape=jax.ShapeDtypeStruct((B * S, H * V_DIM), BF16),
        scratch_shapes=[
            pltpu.VMEM((HEAD_BLK, 1, FA_T), F32),
            pltpu.VMEM((HEAD_BLK, 1, FA_T), F32),
            pltpu.VMEM((HEAD_BLK, V_DIM, FA_T), F32),
        ],
        compiler_params=_params(("parallel", "parallel", "arbitrary"), 40),
        name="mla_attention",
    )(q, k, vt)


def _mem_kv_kernel(mem_ref, g_ref, wk_ref, wv_ref, kng_ref, k_ref, v_ref):
    m = mem_ref[...]
    ms = jnp.mean(m * m, axis=-1, keepdims=True)
    mb = (m * lax.rsqrt(ms + EPS) * g_ref[...]).astype(BF16)
    kk = jnp.dot(mb, wk_ref[...], preferred_element_type=F32)
    for h in range(MEM_HEADS):
        cols = slice(h * MEM_HEAD_DIM, (h + 1) * MEM_HEAD_DIM)
        kh = kk[:, cols]
        ms = jnp.mean(kh * kh, axis=-1, keepdims=True)
        k_ref[:, cols] = (kh * lax.rsqrt(ms + EPS) * kng_ref[...]).astype(BF16)
    v_ref[...] = jnp.dot(mb, wv_ref[...], preferred_element_type=F32).astype(BF16)


def _mem_kv(mem, g, wk, wv, kng):
    B = mem.shape[0]
    return pl.pallas_call(
        _mem_kv_kernel,
        grid=(B,),
        in_specs=[
            pl.BlockSpec((None, MEM_LEN, D_MODEL), lambda b: (b, 0, 0)),
            _const_spec((1, D_MODEL)),
            _const_spec((D_MODEL, MEM_WIDTH)),
            _const_spec((D_MODEL, MEM_WIDTH)),
            _const_spec((1, MEM_HEAD_DIM)),
        ],
        out_specs=[
            pl.BlockSpec((None, MEM_LEN, MEM_WIDTH), lambda b: (b, 0, 0)),
            pl.BlockSpec((None, MEM_LEN, MEM_WIDTH), lambda b: (b, 0, 0)),
        ],
        out_shape=[
            jax.ShapeDtypeStruct((B, MEM_LEN, MEM_WIDTH), BF16),
            jax.ShapeDtypeStruct((B, MEM_LEN, MEM_WIDTH), BF16),
        ],
        compiler_params=_params(("parallel",), 32),
        name="mem_kv",
    )(mem, g, wk, wv, kng)


MA_TM = 512


def _mem_attn_kernel(qm_ref, km_ref, vm_ref, g2_ref, m0_ref, qng_ref, wpc_ref, o_ref, c_scr):
    for h in range(MEM_HEADS):
        cols = slice(h * MEM_HEAD_DIM, (h + 1) * MEM_HEAD_DIM)
        qh = qm_ref[:, cols].astype(F32)
        ms = jnp.mean(qh * qh, axis=-1, keepdims=True)
        qn = (qh * (lax.rsqrt(ms + EPS) * MEM_SCALE) * qng_ref[...]).astype(BF16)
        s = lax.dot_general(qn, km_ref[:, cols], (((1,), (1,)), ((), ())), preferred_element_type=F32)
        e = jnp.exp(s - jnp.max(s, axis=-1, keepdims=True))
        p = (e / jnp.sum(e, axis=-1, keepdims=True)).astype(BF16)
        c_scr[:, cols] = jnp.dot(p, vm_ref[:, cols], preferred_element_type=F32).astype(BF16)
    mc = jnp.dot(c_scr[...], wpc_ref[...], preferred_element_type=F32)
    o_ref[...] = (m0_ref[...].astype(F32) + g2_ref[...].astype(F32) * mc).astype(BF16)


def _mem_attn(pg, km, vm, m0, qng, wpc, S):
    T = pg.shape[0]
    spb = S // MA_TM
    return pl.pallas_call(
        _mem_attn_kernel,
        grid=(T // MA_TM,),
        in_specs=[
            pl.BlockSpec((MA_TM, MEM_WIDTH), lambda i: (i, (2 * GM_WIDTH + Q_LORA + KV_LORA) // MEM_WIDTH)),
            pl.BlockSpec((None, MEM_LEN, MEM_WIDTH), lambda i: (i // spb, 0, 0)),
            pl.BlockSpec((None, MEM_LEN, MEM_WIDTH), lambda i: (i // spb, 0, 0)),
            pl.BlockSpec((MA_TM, D_MODEL), lambda i: (i, PROJ_COLS // D_MODEL + 2)),
            pl.BlockSpec((MA_TM, D_MODEL), lambda i: (i, 0)),
            _const_spec((1, MEM_HEAD_DIM)),
            _const_spec((MEM_WIDTH, D_MODEL)),
        ],
        out_specs=pl.BlockSpec((MA_TM, D_MODEL), lambda i: (i, 0)),
        out_shape=jax.ShapeDtypeStruct((T, D_MODEL), BF16),
        scratch_shapes=[pltpu.VMEM((MA_TM, MEM_WIDTH), BF16)],
        compiler_params=_params(("parallel",), 40),
        name="mem_attention",
    )(pg, km, vm, pg, m0, qng, wpc)


MG_TM = 256


def _merge_kernel(b_ref, g1_ref, m1_ref, x_ref, wpb_ref, wo_ref, ln2_ref, rw_ref, rb_ref,
                  x1_ref, h2_ref, idx_ref, gate_ref):
    mb = jnp.dot(b_ref[...], wpb_ref[...], preferred_element_type=F32)
    merged = m1_ref[...].astype(F32) + g1_ref[...].astype(F32) * mb
    x1 = x_ref[...] + jnp.dot(merged.astype(BF16), wo_ref[...], preferred_element_type=F32)
    x1_ref[...] = x1
    ms = jnp.mean(x1 * x1, axis=-1, keepdims=True)
    h2 = x1 * lax.rsqrt(ms + EPS) * ln2_ref[...]
    h2_ref[...] = h2

    logits = jnp.dot(h2, rw_ref[...], preferred_element_type=F32, precision=lax.Precision.HIGHEST)
    lane = lax.broadcasted_iota(jnp.int32, (MG_TM, LANES), 1)
    work = jnp.where(lane < N_EXPERTS, logits + rb_ref[...], NEG_INF)
    idx_out = jnp.zeros((MG_TM, LANES), jnp.int32)
    val_out = jnp.zeros((MG_TM, LANES), F32)
    top = None
    denom = jnp.zeros((MG_TM, 1), F32)
    for k in range(TOP_K):
        mk = jnp.max(work, axis=-1, keepdims=True)
        ik = jnp.min(jnp.where(work == mk, lane, LANES), axis=-1, keepdims=True)
        work = jnp.where(lane == ik, NEG_INF, work)
        if top is None:
            top = mk
        ek = jnp.exp(mk - top)
        denom = denom + ek
        idx_out = jnp.where(lane == k, ik, idx_out)
        val_out = jnp.where(lane == k, ek, val_out)
    idx_ref[...] = idx_out
    gate_ref[...] = val_out / denom


def _merge(b_attn, pg, m1, x2, wpb, wo, ln2, rw, rb):
    T = x2.shape[0]
    row = lambda w: pl.BlockSpec((MG_TM, w), lambda i: (i, 0))
    return pl.pallas_call(
        _merge_kernel,
        grid=(T // MG_TM,),
        in_specs=[
            row(D_MODEL),
            pl.BlockSpec((MG_TM, D_MODEL), lambda i: (i, PROJ_COLS // D_MODEL + 1)),
            row(D_MODEL),
            row(D_MODEL),
            _const_spec((MLA_HEADS * V_DIM, D_MODEL)),
            _const_spec((D_MODEL, D_MODEL)),
            _const_spec((1, D_MODEL)),
            _const_spec((D_MODEL, LANES)),
            _const_spec((1, LANES)),
        ],
        out_specs=[row(D_MODEL), row(D_MODEL), row(LANES), row(LANES)],
        out_shape=[
            jax.ShapeDtypeStruct((T, D_MODEL), F32),
            jax.ShapeDtypeStruct((T, D_MODEL), F32),
            jax.ShapeDtypeStruct((T, LANES), jnp.int32),
            jax.ShapeDtypeStruct((T, LANES), F32),
        ],
        compiler_params=_params(("parallel",), 48),
        name="merge_router",
    )(b_attn, pg, m1, x2, wpb, wo, ln2, rw, rb)


FF_CHUNK = 1024


def _row_copy(src_hbm, src_row, dst, dst_row, sem):
    return pltpu.make_async_copy(src_hbm.at[pl.ds(src_row, 1), :], dst.at[pl.ds(dst_row, 1), :], sem)


def _expert_kernel(be_ref, nb_ref, tok_ref, h2_hbm, wgu_ref, bgu_ref, wd_ref, bd_ref, y_ref, xbuf, sem):
    m = pl.program_id(0)

    @pl.when(m < nb_ref[0])
    def _():
        def issue(r, carry):
            _row_copy(h2_hbm, tok_ref[0, r], xbuf, r, sem).start()
            return carry

        lax.fori_loop(0, ROW_BLOCK, issue, 0)

        def drain(r, carry):
            _row_copy(h2_hbm, 0, xbuf, 0, sem).wait()
            return carry

        lax.fori_loop(0, ROW_BLOCK, drain, 0)

        xb = xbuf[...].astype(BF16)
        acc = jnp.zeros((ROW_BLOCK, D_MODEL), F32)
        for fc in range(D_FF // FF_CHUNK):
            gc = slice(fc * FF_CHUNK, (fc + 1) * FF_CHUNK)
            uc = slice(D_FF + fc * FF_CHUNK, D_FF + (fc + 1) * FF_CHUNK)
            gate = jnp.dot(xb, wgu_ref[:, gc], preferred_element_type=F32) + bgu_ref[:, gc]
            up = jnp.dot(xb, wgu_ref[:, uc], preferred_element_type=F32) + bgu_ref[:, uc]
            gate = jnp.minimum(gate, SWIGLU_LIMIT)
            up = jnp.clip(up, -SWIGLU_LIMIT, SWIGLU_LIMIT)
            glu = gate * jax.nn.sigmoid(gate * SWIGLU_ALPHA)
            act = ((up + 1.0) * glu).astype(BF16)
            acc = acc + jnp.dot(act, wd_ref[gc, :], preferred_element_type=F32)
        y_ref[...] = acc + bd_ref[...]

    @pl.when(m >= nb_ref[0])
    def _():
        y_ref[...] = jnp.zeros_like(y_ref)


def _experts(block_e, n_used, row_tok3, h2, wgu, bgu, wd, bd):
    n_blocks = row_tok3.shape[0]
    grid_spec = pltpu.PrefetchScalarGridSpec(
        num_scalar_prefetch=2,
        grid=(n_blocks,),
        in_specs=[
            pl.BlockSpec((None, 1, ROW_BLOCK), lambda m, be, nb: (m, 0, 0), memory_space=pltpu.SMEM),
            pl.BlockSpec(memory_space=pl.ANY),
            pl.BlockSpec((None, D_MODEL, 2 * D_FF), lambda m, be, nb: (be[m], 0, 0),
                         pipeline_mode=pl.Buffered(1)),
            pl.BlockSpec((None, 1, 2 * D_FF), lambda m, be, nb: (be[m], 0, 0)),
            pl.BlockSpec((None, D_FF, D_MODEL), lambda m, be, nb: (be[m], 0, 0),
                         pipeline_mode=pl.Buffered(1)),
            pl.BlockSpec((None, 1, D_MODEL), lambda m, be, nb: (be[m], 0, 0)),
        ],
        out_specs=pl.BlockSpec((ROW_BLOCK, D_MODEL), lambda m, be, nb: (m, 0)),
        scratch_shapes=[pltpu.VMEM((ROW_BLOCK, D_MODEL), F32), pltpu.SemaphoreType.DMA(())],
    )
    return pl.pallas_call(
        _expert_kernel,
        grid_spec=grid_spec,
        out_shape=jax.ShapeDtypeStruct((n_blocks * ROW_BLOCK, D_MODEL), F32),
        compiler_params=_params(("arbitrary",), 56),
        name="moe_experts",
    )(block_e, n_used, row_tok3, h2, wgu, bgu, wd, bd)


CB_TM = 128


def _combine_kernel(pos_ref, ys_hbm, x1_ref, gate_ref, o_ref, buf, sem):
    def issue(j, carry):
        _row_copy(ys_hbm, pos_ref[0, j], buf.at[j & (TOP_K - 1)], j >> 2, sem).start()
        return carry

    lax.fori_loop(0, CB_TM * TOP_K, issue, 0)

    def drain(j, carry):
        _row_copy(ys_hbm, 0, buf.at[0], 0, sem).wait()
        return carry

    lax.fori_loop(0, CB_TM * TOP_K, drain, 0)

    out = x1_ref[...]
    gates = gate_ref[...]
    for k in range(TOP_K):
        out = out + gates[:, k:k + 1] * buf[k]
    o_ref[...] = out


def _combine(pos3, ys, x1, gates):
    T = x1.shape[0]
    return pl.pallas_call(
        _combine_kernel,
        grid=(T // CB_TM,),
        in_specs=[
            pl.BlockSpec((None, 1, CB_TM * TOP_K), lambda i: (i, 0, 0), memory_space=pltpu.SMEM),
            pl.BlockSpec(memory_space=pl.ANY),
            pl.BlockSpec((CB_TM, D_MODEL), lambda i: (i, 0)),
            pl.BlockSpec((CB_TM, LANES), lambda i: (i, 0)),
        ],
        out_specs=pl.BlockSpec((CB_TM, D_MODEL), lambda i: (i, 0)),
        out_shape=jax.ShapeDtypeStruct((T, D_MODEL), F32),
        scratch_shapes=[pltpu.VMEM((TOP_K, CB_TM, D_MODEL), F32), pltpu.SemaphoreType.DMA(())],
        compiler_params=_params(("arbitrary",), 32),
        name="moe_combine",
    )(pos3, ys, x1, gates)


def _dispatch_tables(top_idx, T):
    flat_e = top_idx.reshape(-1)
    flat_tok = jnp.repeat(jnp.arange(T, dtype=jnp.int32), TOP_K)
    order = jnp.argsort(flat_e)
    e_sorted = flat_e[order]
    counts = jnp.bincount(flat_e, length=N_EXPERTS).astype(jnp.int32)
    starts = jnp.cumsum(counts) - counts
    padded = ((counts + ROW_BLOCK - 1) // ROW_BLOCK) * ROW_BLOCK
    pad_ends = jnp.cumsum(padded)
    pad_starts = pad_ends - padded
    rank = jnp.arange(T * TOP_K, dtype=jnp.int32) - starts[e_sorted]
    dest = (pad_starts[e_sorted] + rank).astype(jnp.int32)
    n_rows = T * TOP_K + N_EXPERTS * ROW_BLOCK
    n_blocks = n_rows // ROW_BLOCK
    row_tok = jnp.zeros((n_rows,), jnp.int32).at[dest].set(flat_tok[order])
    pos = jnp.zeros((T * TOP_K,), jnp.int32).at[order].set(dest)
    block_start = jnp.arange(n_blocks, dtype=jnp.int32) * ROW_BLOCK
    block_e = jnp.minimum(jnp.searchsorted(pad_ends, block_start, side='right'), N_EXPERTS - 1).astype(jnp.int32)
    n_used = (pad_ends[-1:] // ROW_BLOCK).astype(jnp.int32)
    return block_e, n_used, row_tok.reshape(n_blocks, 1, ROW_BLOCK), pos.reshape(T // CB_TM, 1, CB_TM * TOP_K)


def _rope_constants():
    lane = np.arange(LANES)
    half = QK_ROPE // 2
    inv = 1.0 / (ROPE_THETA ** (np.arange(0, QK_ROPE, 2, dtype=np.float32) / QK_ROPE))
    cst = np.zeros((8, LANES), np.float32)
    cst[0, :QK_ROPE] = inv.astype(np.float32)[lane[:QK_ROPE] % half]
    cst[1, :QK_ROPE] = 1.0
    cst[2, :half] = -1.0
    cst[2, half:QK_ROPE] = 1.0
    return jnp.asarray(cst)


def _swap_halves(a):
    half = QK_ROPE // 2
    return jnp.concatenate([a[..., half:], a[..., :half]], axis=-1)


def kernel(x, mem, positions, ln1_g, w_in, w_gate, b_gate, gmlp_ln_g, gmlp_ln_b, gmlp_ws, gmlp_bs, w_pa,
           mla_cq_g, mla_w_uq, mla_ckv_g, mla_w_ukv, mla_qn_g, mla_kn_g, w_pb, mem_ln_g, mem_w_k, mem_w_v,
           mem_qn_g, mem_kn_g, w_pc, w_o, ln2_g, router_w, router_b, moe_w_gu, moe_b_gu, moe_w_down,
           moe_b_down):
    B, S, D = x.shape
    T = B * S
    x2 = x.reshape(T, D)
    for l in range(ln1_g.shape[0]):
        o_kr = 2 * GM_WIDTH + Q_LORA + KV_LORA
        o_qm = o_kr + QK_ROPE
        wi = w_in[l]
        w1 = jnp.concatenate([wi[:, :o_kr], wi[:, o_qm:], w_gate[l]], axis=1).astype(BF16)
        b1 = jnp.concatenate([jnp.zeros((PROJ_COLS,), F32), b_gate[l]])[None, :]
        w_kr = wi[:, o_kr:o_qm]
        wr = jnp.concatenate([w_kr, _swap_halves(w_kr)], axis=1).astype(BF16)

        wq3 = mla_w_uq[l].reshape(Q_LORA, MLA_HEADS, QK_DIM)
        wq = jnp.concatenate([wq3, _swap_halves(wq3[..., QK_NOPE:])], axis=-1)
        wq = wq.reshape(Q_LORA, MLA_HEADS * QK_PAD).astype(BF16)
        wkv = mla_w_ukv[l].astype(BF16)
        gq = jnp.concatenate([mla_qn_g[l], _swap_halves(mla_qn_g[l][QK_NOPE:])])[None, :]
        gk = jnp.concatenate([mla_kn_g[l], _swap_halves(mla_kn_g[l][QK_NOPE:])])[None, :]

        bias_full = jnp.broadcast_to(gmlp_bs[l].T[:, :, None], (GM_CHUNK, GM_GROUPS, GM_CHUNK))
        bias_full = bias_full.reshape(GM_CHUNK, GM_WIDTH)

        rw = jnp.pad(router_w[l], ((0, 0), (0, LANES - N_EXPERTS)))
        rb = jnp.pad(router_b[l], (0, LANES - N_EXPERTS))[None, :]

        pg, kr = _norm_proj(x2, ln1_g[l][None, :], w1, b1, wr)
        m0 = _gmlp(pg, gmlp_ln_g[l][None, :], gmlp_ln_b[l][None, :], gmlp_ws[l].astype(BF16), bias_full,
                   w_pa[l].astype(BF16))
        q, k, vt = _mla_qkv(pg, kr, positions.reshape(T, 1), mla_cq_g[l][None, :], mla_ckv_g[l][None, :],
                           wq, wkv, gq, gk, _rope_constants(), B, S)
        b_attn = _flash(q, k, vt)
        km, vm = _mem_kv(mem, mem_ln_g[l][None, :], mem_w_k[l].astype(BF16), mem_w_v[l].astype(BF16),
                         mem_kn_g[l][None, :])
        m1 = _mem_attn(pg, km, vm, m0, mem_qn_g[l][None, :], w_pc[l].astype(BF16), S)
        x1, h2, top_idx, gates = _merge(b_attn, pg, m1, x2, w_pb[l].astype(BF16), w_o[l].astype(BF16),
                                        ln2_g[l][None, :], rw, rb)

        block_e, n_used, row_tok3, pos3 = _dispatch_tables(top_idx[:, :TOP_K], T)
        ys = _experts(block_e, n_used, row_tok3, h2, moe_w_gu[l].astype(BF16), moe_b_gu[l][:, None, :],
                      moe_w_down[l].astype(BF16), moe_b_down[l][:, None, :])
        x2 = _combine(pos3, ys, x1, gates)
    return x2.reshape(B, S, D)
```

```python
import functools

import numpy as np
import jax
import jax.numpy as jnp
from jax import lax
from jax.experimental import pallas as pl
from jax.experimental.pallas import tpu as pltpu

F32 = jnp.float32
BF16 = jnp.bfloat16

D_MODEL = 2048
GM_WIDTH = 1024
GM_GROUPS = 8
GM_CHUNK = 128
MLA_HEADS = 16
Q_LORA = 512
KV_LORA = 512
QK_NOPE = 128
QK_ROPE = 64
V_DIM = 128
QK_DIM = QK_NOPE + QK_ROPE
QK_PAD = 256
MLA_SCALE = QK_DIM ** -0.5
ROPE_THETA = 10000.0
MEM_LEN = 256
MEM_HEADS = 4
MEM_HEAD_DIM = 256
MEM_WIDTH = MEM_HEADS * MEM_HEAD_DIM
MEM_SCALE = MEM_HEAD_DIM ** -0.5
N_EXPERTS = 32
EXPERT_BITS = 5
TOP_K = 4
D_FF = 2048
SWIGLU_LIMIT = 7.0
SWIGLU_ALPHA = 1.702
ROW_BLOCK = 256
EPS = 1e-6
LANES = 128
NEG_INF = float("-inf")

PROJ_COLS = 2 * GM_WIDTH + Q_LORA + KV_LORA + MEM_WIDTH
PG_COLS = PROJ_COLS + 3 * D_MODEL

MIB = 1024 * 1024


def _params(semantics, vmem_mib):
    return pltpu.CompilerParams(dimension_semantics=semantics, vmem_limit_bytes=vmem_mib * MIB)


def _const_spec(shape):
    nd = len(shape)
    return pl.BlockSpec(shape, lambda *_: (0,) * nd, pipeline_mode=pl.Buffered(1))


P1_TM = 1024
P1_TN = 1024


def _norm_proj_kernel(x_ref, g_ref, w_ref, b_ref, wr_ref, o_ref, kr_ref, h_scr, *, n_plain):
    j = pl.program_id(1)

    @pl.when(j == 0)
    def _():
        def body(c, carry):
            rows = pl.ds(pl.multiple_of(c * 128, 128), 128)
            x = x_ref[rows, :]
            ms = jnp.mean(x * x, axis=-1, keepdims=True)
            h_scr[rows, :] = (x * lax.rsqrt(ms + EPS) * g_ref[...]).astype(BF16)
            return carry

        lax.fori_loop(0, P1_TM // 128, body, 0)
        kr_ref[...] = jnp.dot(h_scr[...], wr_ref[...], preferred_element_type=F32)

    acc = jnp.dot(h_scr[...], w_ref[...], preferred_element_type=F32)
    gated = jax.nn.sigmoid(acc + b_ref[...])
    o_ref[...] = jnp.where(j >= n_plain, gated, acc).astype(BF16)


def _norm_proj(x2, ln1_g, w1, b1, wr):
    T = x2.shape[0]
    grid = (T // P1_TM, PG_COLS // P1_TN)
    return pl.pallas_call(
        functools.partial(_norm_proj_kernel, n_plain=PROJ_COLS // P1_TN),
        grid=grid,
        in_specs=[
            pl.BlockSpec((P1_TM, D_MODEL), lambda i, j: (i, 0)),
            pl.BlockSpec((1, D_MODEL), lambda i, j: (0, 0)),
            pl.BlockSpec((D_MODEL, P1_TN), lambda i, j: (0, j)),
            pl.BlockSpec((1, P1_TN), lambda i, j: (0, j)),
            pl.BlockSpec((D_MODEL, LANES), lambda i, j: (0, 0)),
        ],
        out_specs=[
            pl.BlockSpec((P1_TM, P1_TN), lambda i, j: (i, j)),
            pl.BlockSpec((P1_TM, LANES), lambda i, j: (i, 0)),
        ],
        out_shape=[
            jax.ShapeDtypeStruct((T, PG_COLS), BF16),
            jax.ShapeDtypeStruct((T, LANES), F32),
        ],
        scratch_shapes=[pltpu.VMEM((P1_TM, D_MODEL), BF16)],
        compiler_params=_params(("parallel", "arbitrary"), 48),
        name="norm_proj",
    )(x2, ln1_g, w1, b1, wr)


GM_TM = 512


def _gmlp_kernel(u_ref, v_ref, g0_ref, lng_ref, lnb_ref, ws_ref, bias_ref, wpa_ref, o_ref, vb_scr, a_scr):
    v = v_ref[...].astype(F32)
    mu = jnp.mean(v, axis=-1, keepdims=True)
    c = v - mu
    var = jnp.mean(c * c, axis=-1, keepdims=True)
    vb_scr[...] = (c * lax.rsqrt(var + EPS) * lng_ref[...] + lnb_ref[...]).astype(BF16)

    row = lax.broadcasted_iota(jnp.int32, (GM_CHUNK, GM_CHUNK), 0)
    col = lax.broadcasted_iota(jnp.int32, (GM_CHUNK, GM_CHUNK), 1)
    causal = col <= row
    for g in range(GM_GROUPS):
        cols = slice(g * GM_CHUNK, (g + 1) * GM_CHUNK)
        wg = jnp.where(causal, ws_ref[g], jnp.zeros((), BF16))
        for ch in range(GM_TM // GM_CHUNK):
            rows = slice(ch * GM_CHUNK, (ch + 1) * GM_CHUNK)
            mixed = jnp.dot(wg, vb_scr[rows, cols], preferred_element_type=F32) + bias_ref[:, cols]
            a_scr[rows, cols] = (u_ref[rows, cols].astype(F32) * mixed).astype(BF16)

    ma = jnp.dot(a_scr[...], wpa_ref[...], preferred_element_type=F32)
    o_ref[...] = (g0_ref[...].astype(F32) * ma).astype(BF16)


def _gmlp(pg, ln_g, ln_b, ws, bias_full, wpa):
    T = pg.shape[0]
    return pl.pallas_call(
        _gmlp_kernel,
        grid=(T // GM_TM,),
        in_specs=[
            pl.BlockSpec((GM_TM, GM_WIDTH), lambda i: (i, 0)),
            pl.BlockSpec((GM_TM, GM_WIDTH), lambda i: (i, 1)),
            pl.BlockSpec((GM_TM, D_MODEL), lambda i: (i, PROJ_COLS // D_MODEL)),
            _const_spec((1, GM_WIDTH)),
            _const_spec((1, GM_WIDTH)),
            _const_spec((GM_GROUPS, GM_CHUNK, GM_CHUNK)),
            _const_spec((GM_CHUNK, GM_WIDTH)),
            _const_spec((GM_WIDTH, D_MODEL)),
        ],
        out_specs=pl.BlockSpec((GM_TM, D_MODEL), lambda i: (i, 0)),
        out_shape=jax.ShapeDtypeStruct((T, D_MODEL), BF16),
        scratch_shapes=[pltpu.VMEM((GM_TM, GM_WIDTH), BF16), pltpu.VMEM((GM_TM, GM_WIDTH), BF16)],
        compiler_params=_params(("parallel",), 40),
        name="gmlp",
    )(pg, pg, pg, ln_g, ln_b, ws, bias_full, wpa)


QKV_TM = 1024
HEAD_BLK = 2
FA_T = 512


def _rope(t, cos, sin):
    return t * cos + pltpu.roll(t, QK_ROPE, 1) * sin


def _mla_qkv_kernel(cq_ref, ckv_ref, kr_ref, pos_ref, gcq_ref, gckv_ref, wq_ref, wkv_ref, gq_ref, gk_ref,
                    cst_ref, q_ref, k_ref, vt_ref, cqn_scr, ckvn_scr, cos_scr, sin_scr, krsq_scr):
    hb = pl.program_id(1)

    @pl.when(hb == 0)
    def _():
        cq = cq_ref[...].astype(F32)
        ms = jnp.mean(cq * cq, axis=-1, keepdims=True)
        cqn_scr[...] = (cq * lax.rsqrt(ms + EPS) * gcq_ref[...]).astype(BF16)
        ckv = ckv_ref[...].astype(F32)
        ms = jnp.mean(ckv * ckv, axis=-1, keepdims=True)
        ckvn_scr[...] = (ckv * lax.rsqrt(ms + EPS) * gckv_ref[...]).astype(BF16)
        ang = pos_ref[...].astype(F32) * cst_ref[0:1, :]
        cos_scr[...] = jnp.cos(ang) * cst_ref[1:2, :]
        sin_scr[...] = jnp.sin(ang) * cst_ref[2:3, :]
        kr = kr_ref[...]
        krsq_scr[...] = 0.5 * jnp.sum(kr * kr, axis=-1, keepdims=True)

    cos = cos_scr[...]
    sin = sin_scr[...]
    inv_dim = 1.0 / QK_DIM
    kr = kr_ref[...]

    yq2 = jnp.dot(cqn_scr[...], wq_ref[...], preferred_element_type=F32)
    ykv2 = jnp.dot(ckvn_scr[...], wkv_ref[...], preferred_element_type=F32)
    for hh in range(HEAD_BLK):
        yq = yq2[:, hh * QK_PAD:(hh + 1) * QK_PAD]
        qn = yq[:, :QK_NOPE]
        qt = yq[:, QK_NOPE:]
        ssq = jnp.sum(qn * qn, axis=-1, keepdims=True) + 0.5 * jnp.sum(qt * qt, axis=-1, keepdims=True)
        rs = lax.rsqrt(ssq * inv_dim + EPS) * MLA_SCALE
        q_ref[hh, :, :QK_NOPE] = (qn * rs * gq_ref[:, :QK_NOPE]).astype(BF16)
        q_ref[hh, :, QK_NOPE:] = _rope(qt * rs * gq_ref[:, QK_NOPE:], cos, sin).astype(BF16)

        ykv = ykv2[:, hh * (QK_NOPE + V_DIM):(hh + 1) * (QK_NOPE + V_DIM)]
        kn = ykv[:, :QK_NOPE]
        ssq = jnp.sum(kn * kn, axis=-1, keepdims=True) + krsq_scr[...]
        rs = lax.rsqrt(ssq * inv_dim + EPS)
        k_ref[hh, :, :QK_NOPE] = (kn * rs * gk_ref[:, :QK_NOPE]).astype(BF16)
        k_ref[hh, :, QK_NOPE:] = _rope(kr * rs * gk_ref[:, QK_NOPE:], cos, sin).astype(BF16)
        vv = ykv[:, QK_NOPE:]
        for c in range(QKV_TM // FA_T):
            vt_ref[hh, c] = vv[c * FA_T:(c + 1) * FA_T, :].T.astype(BF16)


def _mla_qkv(pg, kr, pos, gcq, gckv, wq, wkv, gq, gk, cst, B, S):
    T = pg.shape[0]
    spb = S // QKV_TM
    cpt = QKV_TM // FA_T
    head_spec = lambda w: pl.BlockSpec((None, HEAD_BLK, QKV_TM, w), lambda i, h: (i // spb, h, i % spb, 0))
    return pl.pallas_call(
        _mla_qkv_kernel,
        grid=(T // QKV_TM, MLA_HEADS // HEAD_BLK),
        in_specs=[
            pl.BlockSpec((QKV_TM, Q_LORA), lambda i, h: (i, 2 * GM_WIDTH // Q_LORA)),
            pl.BlockSpec((QKV_TM, KV_LORA), lambda i, h: (i, 2 * GM_WIDTH // KV_LORA + 1)),
            pl.BlockSpec((QKV_TM, LANES), lambda i, h: (i, 0)),
            pl.BlockSpec((QKV_TM, 1), lambda i, h: (i, 0)),
            pl.BlockSpec((1, Q_LORA), lambda i, h: (0, 0)),
            pl.BlockSpec((1, KV_LORA), lambda i, h: (0, 0)),
            pl.BlockSpec((Q_LORA, HEAD_BLK * QK_PAD), lambda i, h: (0, h)),
            pl.BlockSpec((KV_LORA, HEAD_BLK * (QK_NOPE + V_DIM)), lambda i, h: (0, h)),
            pl.BlockSpec((1, QK_PAD), lambda i, h: (0, 0)),
            pl.BlockSpec((1, QK_PAD), lambda i, h: (0, 0)),
            pl.BlockSpec((8, LANES), lambda i, h: (0, 0)),
        ],
        out_specs=[
            head_spec(QK_PAD),
            head_spec(QK_PAD),
            pl.BlockSpec((None, HEAD_BLK, cpt, V_DIM, FA_T), lambda i, h: (i // spb, h, i % spb, 0, 0)),
        ],
        out_shape=[
            jax.ShapeDtypeStruct((B, MLA_HEADS, S, QK_PAD), BF16),
            jax.ShapeDtypeStruct((B, MLA_HEADS, S, QK_PAD), BF16),
            jax.ShapeDtypeStruct((B, MLA_HEADS, S // FA_T, V_DIM, FA_T), BF16),
        ],
        scratch_shapes=[
            pltpu.VMEM((QKV_TM, Q_LORA), BF16),
            pltpu.VMEM((QKV_TM, KV_LORA), BF16),
            pltpu.VMEM((QKV_TM, LANES), F32),
            pltpu.VMEM((QKV_TM, LANES), F32),
            pltpu.VMEM((QKV_TM, 1), F32),
        ],
        compiler_params=_params(("parallel", "arbitrary"), 48),
        name="mla_qkv",
    )(pg, pg, kr, pos, gcq, gckv, wq, wkv, gq, gk, cst)


def _flash_kernel(q_ref, k_ref, vt_ref, o_ref, m_scr, l_scr, acc_scr):
    qi = pl.program_id(2)
    m_scr[...] = jnp.full_like(m_scr, NEG_INF)
    l_scr[...] = jnp.zeros_like(l_scr)
    acc_scr[...] = jnp.zeros_like(acc_scr)

    def step(kb, diagonal):
        ks = pl.ds(pl.multiple_of(kb * FA_T, FA_T), FA_T)
        for hh in range(HEAD_BLK):
            st = lax.dot_general(k_ref[hh, ks, :], q_ref[hh], (((1,), (1,)), ((), ())),
                                 preferred_element_type=F32)
            if diagonal:
                krow = lax.broadcasted_iota(jnp.int32, (FA_T, FA_T), 0)
                qcol = lax.broadcasted_iota(jnp.int32, (FA_T, FA_T), 1)
                st = jnp.where(krow <= qcol, st, NEG_INF)
            m_prev = m_scr[hh]
            m_new = jnp.maximum(m_prev, jnp.max(st, axis=0, keepdims=True))
            alpha = jnp.exp(m_prev - m_new)
            p = jnp.exp(st - m_new)
            l_scr[hh] = alpha * l_scr[hh] + jnp.sum(p, axis=0, keepdims=True)
            acc_scr[hh] = alpha * acc_scr[hh] + jnp.dot(vt_ref[hh, kb], p.astype(BF16),
                                                        preferred_element_type=F32)
            m_scr[hh] = m_new

    def body(kb, carry):
        step(kb, False)
        return carry

    lax.fori_loop(0, qi, body, 0)
    step(qi, True)
    for hh in range(HEAD_BLK):
        o_ref[:, hh * V_DIM:(hh + 1) * V_DIM] = (acc_scr[hh] / l_scr[hh]).T.astype(BF16)


def _flash(q, k, vt):
    B, H, S, _ = q.shape
    nq = S // FA_T
    return pl.pallas_call(
        _flash_kernel,
        grid=(B, H // HEAD_BLK, nq),
        in_specs=[
            pl.BlockSpec((None, HEAD_BLK, FA_T, QK_PAD), lambda b, h, i: (b, h, i, 0)),
            pl.BlockSpec((None, HEAD_BLK, S, QK_PAD), lambda b, h, i: (b, h, 0, 0)),
            pl.BlockSpec((None, HEAD_BLK, nq, V_DIM, FA_T), lambda b, h, i: (b, h, 0, 0, 0)),
        ],
        out_specs=pl.BlockSpec((FA_T, HEAD_BLK * V_DIM), lambda b, h, i: (b * nq + i, h)),
        out_shape=jax.ShapeDtypeStruct((B * S, H * V_DIM), BF16),
        scratch_shapes=[
            pltpu.VMEM((HEAD_BLK, 1, FA_T), F32),
            pltpu.VMEM((HEAD_BLK, 1, FA_T), F32),
            pltpu.VMEM((HEAD_BLK, V_DIM, FA_T), F32),
        ],
        compiler_params=_params(("parallel", "parallel", "arbitrary"), 40),
        name="mla_attention",
    )(q, k, vt)


def _mem_kv_kernel(mem_ref, g_ref, wk_ref, wv_ref, kng_ref, k_ref, v_ref):
    m = mem_ref[...]
    ms = jnp.mean(m * m, axis=-1, keepdims=True)
    mb = (m * lax.rsqrt(ms + EPS) * g_ref[...]).astype(BF16)
    kk = jnp.dot(mb, wk_ref[...], preferred_element_type=F32)
    for h in range(MEM_HEADS):
        cols = slice(h * MEM_HEAD_DIM, (h + 1) * MEM_HEAD_DIM)
        kh = kk[:, cols]
        ms = jnp.mean(kh * kh, axis=-1, keepdims=True)
        k_ref[:, cols] = (kh * lax.rsqrt(ms + EPS) * kng_ref[...]).astype(BF16)
    v_ref[...] = jnp.dot(mb, wv_ref[...], preferred_element_type=F32).astype(BF16)


def _mem_kv(mem, g, wk, wv, kng):
    B = mem.shape[0]
    return pl.pallas_call(
        _mem_kv_kernel,
        grid=(B,),
        in_specs=[
            pl.BlockSpec((None, MEM_LEN, D_MODEL), lambda b: (b, 0, 0)),
            _const_spec((1, D_MODEL)),
            _const_spec((D_MODEL, MEM_WIDTH)),
            _const_spec((D_MODEL, MEM_WIDTH)),
            _const_spec((1, MEM_HEAD_DIM)),
        ],
        out_specs=[
            pl.BlockSpec((None, MEM_LEN, MEM_WIDTH), lambda b: (b, 0, 0)),
            pl.BlockSpec((None, MEM_LEN, MEM_WIDTH), lambda b: (b, 0, 0)),
        ],
        out_shape=[
            jax.ShapeDtypeStruct((B, MEM_LEN, MEM_WIDTH), BF16),
            jax.ShapeDtypeStruct((B, MEM_LEN, MEM_WIDTH), BF16),
        ],
        compiler_params=_params(("parallel",), 32),
        name="mem_kv",
    )(mem, g, wk, wv, kng)


MA_TM = 512


def _mem_attn_kernel(qm_ref, km_ref, vm_ref, g2_ref, m0_ref, qng_ref, wpc_ref, o_ref, c_scr):
    for h in range(MEM_HEADS):
        cols = slice(h * MEM_HEAD_DIM, (h + 1) * MEM_HEAD_DIM)
        qh = qm_ref[:, cols].astype(F32)
        ms = jnp.mean(qh * qh, axis=-1, keepdims=True)
        qn = (qh * (lax.rsqrt(ms + EPS) * MEM_SCALE) * qng_ref[...]).astype(BF16)
        s = lax.dot_general(qn, km_ref[:, cols], (((1,), (1,)), ((), ())), preferred_element_type=F32)
        e = jnp.exp(s - jnp.max(s, axis=-1, keepdims=True))
        p = (e / jnp.sum(e, axis=-1, keepdims=True)).astype(BF16)
        c_scr[:, cols] = jnp.dot(p, vm_ref[:, cols], preferred_element_type=F32).astype(BF16)
    mc = jnp.dot(c_scr[...], wpc_ref[...], preferred_element_type=F32)
    o_ref[...] = (m0_ref[...].astype(F32) + g2_ref[...].astype(F32) * mc).astype(BF16)


def _mem_attn(pg, km, vm, m0, qng, wpc, S):
    T = pg.shape[0]
    spb = S // MA_TM
    return pl.pallas_call(
        _mem_attn_kernel,
        grid=(T // MA_TM,),
        in_specs=[
            pl.BlockSpec((MA_TM, MEM_WIDTH), lambda i: (i, (2 * GM_WIDTH + Q_LORA + KV_LORA) // MEM_WIDTH)),
            pl.BlockSpec((None, MEM_LEN, MEM_WIDTH), lambda i: (i // spb, 0, 0)),
            pl.BlockSpec((None, MEM_LEN, MEM_WIDTH), lambda i: (i // spb, 0, 0)),
            pl.BlockSpec((MA_TM, D_MODEL), lambda i: (i, PROJ_COLS // D_MODEL + 2)),
            pl.BlockSpec((MA_TM, D_MODEL), lambda i: (i, 0)),
            _const_spec((1, MEM_HEAD_DIM)),
            _const_spec((MEM_WIDTH, D_MODEL)),
        ],
        out_specs=pl.BlockSpec((MA_TM, D_MODEL), lambda i: (i, 0)),
        out_shape=jax.ShapeDtypeStruct((T, D_MODEL), BF16),
        scratch_shapes=[pltpu.VMEM((MA_TM, MEM_WIDTH), BF16)],
        compiler_params=_params(("parallel",), 40),
        name="mem_attention",
    )(pg, km, vm, pg, m0, qng, wpc)


MG_TM = 256


def _pack_bf16_pair(a, b):
    hi = pltpu.bitcast(a.astype(BF16).astype(F32), jnp.uint32)
    lo = pltpu.bitcast(b.astype(BF16).astype(F32), jnp.uint32)
    return hi | (lo >> 16)


def _unpack_bf16_pair(p):
    hi = pltpu.bitcast(p & jnp.uint32(0xFFFF0000), F32)
    lo = pltpu.bitcast(p << 16, F32)
    return hi, lo


def _merge_kernel(b_ref, g1_ref, m1_ref, x_ref, wpb_ref, wo_ref, ln2_ref, rw_ref, rb_ref,
                  x1_ref, h2p_ref, code_ref, gate_ref, cnt_ref, cnt_scr):
    @pl.when(pl.program_id(0) == 0)
    def _():
        cnt_scr[...] = jnp.zeros_like(cnt_scr)

    mb = jnp.dot(b_ref[...], wpb_ref[...], preferred_element_type=F32)
    merged = m1_ref[...].astype(F32) + g1_ref[...].astype(F32) * mb
    x1 = x_ref[...] + jnp.dot(merged.astype(BF16), wo_ref[...], preferred_element_type=F32)
    x1_ref[...] = x1
    ms = jnp.mean(x1 * x1, axis=-1, keepdims=True)
    h2 = x1 * lax.rsqrt(ms + EPS) * ln2_ref[...]
    h2p_ref[...] = _pack_bf16_pair(h2[:, :D_MODEL // 2], h2[:, D_MODEL // 2:])

    logits = jnp.dot(h2, rw_ref[...], preferred_element_type=F32, precision=lax.Precision.HIGHEST)
    lane = lax.broadcasted_iota(jnp.int32, (MG_TM, LANES), 1)
    work = jnp.where(lane < N_EXPERTS, logits + rb_ref[...], NEG_INF)
    earlier = (lax.broadcasted_iota(jnp.int32, (MG_TM, MG_TM), 1)
               < lax.broadcasted_iota(jnp.int32, (MG_TM, MG_TM), 0)).astype(BF16)
    base = cnt_scr[...]
    code_out = jnp.zeros((MG_TM, LANES), jnp.int32)
    val_out = jnp.zeros((MG_TM, LANES), F32)
    top = None
    denom = jnp.zeros((MG_TM, 1), F32)
    for k in range(TOP_K):
        mk = jnp.max(work, axis=-1, keepdims=True)
        ik = jnp.min(jnp.where(work == mk, lane, LANES), axis=-1, keepdims=True)
        hit = lane == ik
        work = jnp.where(hit, NEG_INF, work)
        if top is None:
            top = mk
        ek = jnp.exp(mk - top)
        denom = denom + ek
        val_out = jnp.where(lane == k, ek, val_out)
        onehot = hit.astype(BF16)
        prefix = jnp.dot(earlier, onehot, preferred_element_type=F32) + base
        rank = jnp.sum(jnp.where(hit, prefix, 0.0), axis=-1, keepdims=True).astype(jnp.int32)
        code_out = jnp.where(lane == k, rank * N_EXPERTS + ik, code_out)
        base = base + jnp.sum(onehot.astype(F32), axis=0, keepdims=True)
    code_ref[...] = code_out
    gate_ref[...] = val_out / denom
    cnt_scr[...] = base
    cnt_ref[...] = base


def _merge(b_attn, pg, m1, x2, wpb, wo, ln2, rw, rb):
    T = x2.shape[0]
    row = lambda w: pl.BlockSpec((MG_TM, w), lambda i: (i, 0))
    return pl.pallas_call(
        _merge_kernel,
        grid=(T // MG_TM,),
        in_specs=[
            row(D_MODEL),
            pl.BlockSpec((MG_TM, D_MODEL), lambda i: (i, PROJ_COLS // D_MODEL + 1)),
            row(D_MODEL),
            row(D_MODEL),
            _const_spec((MLA_HEADS * V_DIM, D_MODEL)),
            _const_spec((D_MODEL, D_MODEL)),
            _const_spec((1, D_MODEL)),
            _const_spec((D_MODEL, LANES)),
            _const_spec((1, LANES)),
        ],
        out_specs=[row(D_MODEL), row(D_MODEL // 2), row(LANES), row(LANES),
                   pl.BlockSpec((1, LANES), lambda i: (0, 0))],
        out_shape=[
            jax.ShapeDtypeStruct((T, D_MODEL), F32),
            jax.ShapeDtypeStruct((T, D_MODEL // 2), jnp.uint32),
            jax.ShapeDtypeStruct((T, LANES), jnp.int32),
            jax.ShapeDtypeStruct((T, LANES), F32),
            jax.ShapeDtypeStruct((1, LANES), F32),
        ],
        scratch_shapes=[pltpu.VMEM((1, LANES), F32)],
        compiler_params=_params(("arbitrary",), 48),
        name="merge_router",
    )(b_attn, pg, m1, x2, wpb, wo, ln2, rw, rb)


DP_TM = 256
DMA_UNROLL = 8


def _row_copy(src, src_row, dst, dst_row, sem):
    return pltpu.make_async_copy(src.at[pl.ds(src_row, 1), :], dst.at[pl.ds(dst_row, 1), :], sem)


def _dest_row(ps_ref, code):
    return ps_ref[code & (N_EXPERTS - 1)] + (code >> EXPERT_BITS)


def _dispatch_kernel(ps_ref, code_ref, h2p_ref, xs_in, xs_hbm, sem):
    del xs_in

    def issue(j, carry):
        _row_copy(h2p_ref, j >> 2, xs_hbm, _dest_row(ps_ref, code_ref[0, j]), sem).start()
        return carry

    lax.fori_loop(0, DP_TM * TOP_K, issue, 0, unroll=DMA_UNROLL)

    def drain(j, carry):
        _row_copy(h2p_ref, 0, xs_hbm, 0, sem).wait()
        return carry

    lax.fori_loop(0, DP_TM * TOP_K, drain, 0, unroll=DMA_UNROLL)


def _dispatch(pad_starts, codes3, h2p, xs_zero):
    T = h2p.shape[0]
    grid_spec = pltpu.PrefetchScalarGridSpec(
        num_scalar_prefetch=1,
        grid=(T // DP_TM,),
        in_specs=[
            pl.BlockSpec((None, 1, DP_TM * TOP_K), lambda i, ps: (i, 0, 0), memory_space=pltpu.SMEM),
            pl.BlockSpec((DP_TM, D_MODEL // 2), lambda i, ps: (i, 0)),
            pl.BlockSpec(memory_space=pl.ANY),
        ],
        out_specs=pl.BlockSpec(memory_space=pl.ANY),
        scratch_shapes=[pltpu.SemaphoreType.DMA(())],
    )
    return pl.pallas_call(
        _dispatch_kernel,
        grid_spec=grid_spec,
        out_shape=jax.ShapeDtypeStruct(xs_zero.shape, xs_zero.dtype),
        input_output_aliases={3: 0},
        compiler_params=_params(("arbitrary",), 32),
        name="moe_dispatch",
    )(pad_starts, codes3, h2p, xs_zero)


FF_CHUNK = 1024


def _expert_kernel(be_ref, nb_ref, xs_ref, wgu_ref, bgu_ref, wd_ref, bd_ref, y_ref):
    m = pl.program_id(0)

    @pl.when(m < nb_ref[0])
    def _():
        hi, lo = _unpack_bf16_pair(xs_ref[...])
        xb = jnp.concatenate([hi.astype(BF16), lo.astype(BF16)], axis=1)
        acc = jnp.zeros((ROW_BLOCK, D_MODEL), F32)
        for fc in range(D_FF // FF_CHUNK):
            gc = slice(fc * FF_CHUNK, (fc + 1) * FF_CHUNK)
            uc = slice(D_FF + fc * FF_CHUNK, D_FF + (fc + 1) * FF_CHUNK)
            gate = jnp.dot(xb, wgu_ref[:, gc], preferred_element_type=F32) + bgu_ref[:, gc]
            up = jnp.dot(xb, wgu_ref[:, uc], preferred_element_type=F32) + bgu_ref[:, uc]
            gate = jnp.minimum(gate, SWIGLU_LIMIT)
            up = jnp.clip(up, -SWIGLU_LIMIT, SWIGLU_LIMIT)
            glu = gate * jax.nn.sigmoid(gate * SWIGLU_ALPHA)
            act = ((up + 1.0) * glu).astype(BF16)
            acc = acc + jnp.dot(act, wd_ref[gc, :], preferred_element_type=F32)
        y = acc + bd_ref[...]
        y_ref[...] = _pack_bf16_pair(y[:, :D_MODEL // 2], y[:, D_MODEL // 2:])

    @pl.when(m >= nb_ref[0])
    def _():
        y_ref[...] = jnp.zeros_like(y_ref)


def _experts(block_e, n_used, xs, wgu, bgu, wd, bd):
    n_blocks = xs.shape[0] // ROW_BLOCK
    grid_spec = pltpu.PrefetchScalarGridSpec(
        num_scalar_prefetch=2,
        grid=(n_blocks,),
        in_specs=[
            pl.BlockSpec((ROW_BLOCK, D_MODEL // 2), lambda m, be, nb: (m, 0)),
            pl.BlockSpec((None, D_MODEL, 2 * D_FF), lambda m, be, nb: (be[m], 0, 0),
                         pipeline_mode=pl.Buffered(1)),
            pl.BlockSpec((None, 1, 2 * D_FF), lambda m, be, nb: (be[m], 0, 0)),
            pl.BlockSpec((None, D_FF, D_MODEL), lambda m, be, nb: (be[m], 0, 0),
                         pipeline_mode=pl.Buffered(1)),
            pl.BlockSpec((None, 1, D_MODEL), lambda m, be, nb: (be[m], 0, 0)),
        ],
        out_specs=pl.BlockSpec((ROW_BLOCK, D_MODEL // 2), lambda m, be, nb: (m, 0)),
    )
    return pl.pallas_call(
        _expert_kernel,
        grid_spec=grid_spec,
        out_shape=jax.ShapeDtypeStruct((n_blocks * ROW_BLOCK, D_MODEL // 2), jnp.uint32),
        compiler_params=_params(("arbitrary",), 56),
        name="moe_experts",
    )(block_e, n_used, xs, wgu, bgu, wd, bd)


CB_TM = 128


def _combine_kernel(ps_ref, code_ref, code_next_ref, ys_hbm, x1_ref, gate_ref, o_ref, buf, sem):
    i = pl.program_id(0)
    slot = i & 1

    def gather_tile(codes, s):
        def issue(j, carry):
            row = _dest_row(ps_ref, codes[0, j])
            _row_copy(ys_hbm, row, buf.at[s, j & (TOP_K - 1)], j >> 2, sem.at[s]).start()
            return carry

        lax.fori_loop(0, CB_TM * TOP_K, issue, 0, unroll=DMA_UNROLL)

    @pl.when(i == 0)
    def _():
        gather_tile(code_ref, 0)

    @pl.when(i + 1 < pl.num_programs(0))
    def _():
        gather_tile(code_next_ref, 1 - slot)

    def drain(j, carry):
        _row_copy(ys_hbm, 0, buf.at[slot, 0], 0, sem.at[slot]).wait()
        return carry

    lax.fori_loop(0, CB_TM * TOP_K, drain, 0, unroll=DMA_UNROLL)

    half = D_MODEL // 2
    gates = gate_ref[...]
    out_hi = x1_ref[:, :half]
    out_lo = x1_ref[:, half:]
    for k in range(TOP_K):
        hi, lo = _unpack_bf16_pair(buf[slot, k])
        g = gates[:, k:k + 1]
        out_hi = out_hi + g * hi
        out_lo = out_lo + g * lo
    o_ref[:, :half] = out_hi
    o_ref[:, half:] = out_lo


def _combine(pad_starts, codes3, ys, x1, gates):
    T = x1.shape[0]
    n = T // CB_TM
    code_spec = lambda f: pl.BlockSpec((None, 1, CB_TM * TOP_K), f, memory_space=pltpu.SMEM)
    grid_spec = pltpu.PrefetchScalarGridSpec(
        num_scalar_prefetch=1,
        grid=(n,),
        in_specs=[
            code_spec(lambda i, ps: (i, 0, 0)),
            code_spec(lambda i, ps: (jnp.minimum(i + 1, n - 1), 0, 0)),
            pl.BlockSpec(memory_space=pl.ANY),
            pl.BlockSpec((CB_TM, D_MODEL), lambda i, ps: (i, 0)),
            pl.BlockSpec((CB_TM, LANES), lambda i, ps: (i, 0)),
        ],
        out_specs=pl.BlockSpec((CB_TM, D_MODEL), lambda i, ps: (i, 0)),
        scratch_shapes=[pltpu.VMEM((2, TOP_K, CB_TM, D_MODEL // 2), jnp.uint32),
                        pltpu.SemaphoreType.DMA((2,))],
    )
    return pl.pallas_call(
        _combine_kernel,
        grid_spec=grid_spec,
        out_shape=jax.ShapeDtypeStruct((T, D_MODEL), F32),
        compiler_params=_params(("arbitrary",), 32),
        name="moe_combine",
    )(pad_starts, codes3, codes3, ys, x1, gates)


def _block_tables(counts_f32, n_blocks):
    counts = counts_f32[0, :N_EXPERTS].astype(jnp.int32)
    padded = ((counts + ROW_BLOCK - 1) // ROW_BLOCK) * ROW_BLOCK
    upto = jnp.arange(N_EXPERTS)[None, :] <= jnp.arange(N_EXPERTS)[:, None]
    pad_ends = jnp.sum(jnp.where(upto, padded[None, :], 0), axis=1)
    pad_starts = pad_ends - padded
    block_start = jnp.arange(n_blocks, dtype=jnp.int32) * ROW_BLOCK
    block_e = jnp.sum((pad_ends[None, :] <= block_start[:, None]).astype(jnp.int32), axis=1)
    block_e = jnp.minimum(block_e, N_EXPERTS - 1)
    n_used = pad_ends[-1:] // ROW_BLOCK
    return pad_starts.astype(jnp.int32), block_e.astype(jnp.int32), n_used.astype(jnp.int32)


def _rope_constants():
    lane = np.arange(LANES)
    half = QK_ROPE // 2
    inv = 1.0 / (ROPE_THETA ** (np.arange(0, QK_ROPE, 2, dtype=np.float32) / QK_ROPE))
    cst = np.zeros((8, LANES), np.float32)
    cst[0, :QK_ROPE] = inv.astype(np.float32)[lane[:QK_ROPE] % half]
    cst[1, :QK_ROPE] = 1.0
    cst[2, :half] = -1.0
    cst[2, half:QK_ROPE] = 1.0
    return jnp.asarray(cst)


def _swap_halves(a):
    half = QK_ROPE // 2
    return jnp.concatenate([a[..., half:], a[..., :half]], axis=-1)


def kernel(x, mem, positions, ln1_g, w_in, w_gate, b_gate, gmlp_ln_g, gmlp_ln_b, gmlp_ws, gmlp_bs, w_pa,
           mla_cq_g, mla_w_uq, mla_ckv_g, mla_w_ukv, mla_qn_g, mla_kn_g, w_pb, mem_ln_g, mem_w_k, mem_w_v,
           mem_qn_g, mem_kn_g, w_pc, w_o, ln2_g, router_w, router_b, moe_w_gu, moe_b_gu, moe_w_down,
           moe_b_down):
    B, S, D = x.shape
    T = B * S
    x2 = x.reshape(T, D)
    for l in range(ln1_g.shape[0]):
        o_kr = 2 * GM_WIDTH + Q_LORA + KV_LORA
        o_qm = o_kr + QK_ROPE
        wi = w_in[l]
        w1 = jnp.concatenate([wi[:, :o_kr], wi[:, o_qm:], w_gate[l]], axis=1).astype(BF16)
        b1 = jnp.concatenate([jnp.zeros((PROJ_COLS,), F32), b_gate[l]])[None, :]
        w_kr = wi[:, o_kr:o_qm]
        wr = jnp.concatenate([w_kr, _swap_halves(w_kr)], axis=1).astype(BF16)

        wq3 = mla_w_uq[l].reshape(Q_LORA, MLA_HEADS, QK_DIM)
        wq = jnp.concatenate([wq3, _swap_halves(wq3[..., QK_NOPE:])], axis=-1)
        wq = wq.reshape(Q_LORA, MLA_HEADS * QK_PAD).astype(BF16)
        wkv = mla_w_ukv[l].astype(BF16)
        gq = jnp.concatenate([mla_qn_g[l], _swap_halves(mla_qn_g[l][QK_NOPE:])])[None, :]
        gk = jnp.concatenate([mla_kn_g[l], _swap_halves(mla_kn_g[l][QK_NOPE:])])[None, :]

        bias_full = jnp.broadcast_to(gmlp_bs[l].T[:, :, None], (GM_CHUNK, GM_GROUPS, GM_CHUNK))
        bias_full = bias_full.reshape(GM_CHUNK, GM_WIDTH)

        rw = jnp.pad(router_w[l], ((0, 0), (0, LANES - N_EXPERTS)))
        rb = jnp.pad(router_b[l], (0, LANES - N_EXPERTS))[None, :]

        pg, kr = _norm_proj(x2, ln1_g[l][None, :], w1, b1, wr)
        m0 = _gmlp(pg, gmlp_ln_g[l][None, :], gmlp_ln_b[l][None, :], gmlp_ws[l].astype(BF16), bias_full,
                   w_pa[l].astype(BF16))
        q, k, vt = _mla_qkv(pg, kr, positions.reshape(T, 1), mla_cq_g[l][None, :], mla_ckv_g[l][None, :],
                           wq, wkv, gq, gk, _rope_constants(), B, S)
        b_attn = _flash(q, k, vt)
        km, vm = _mem_kv(mem, mem_ln_g[l][None, :], mem_w_k[l].astype(BF16), mem_w_v[l].astype(BF16),
                         mem_kn_g[l][None, :])
        m1 = _mem_attn(pg, km, vm, m0, mem_qn_g[l][None, :], w_pc[l].astype(BF16), S)
        x1, h2p, code, gates, counts = _merge(b_attn, pg, m1, x2, w_pb[l].astype(BF16), w_o[l].astype(BF16),
                                              ln2_g[l][None, :], rw, rb)

        n_rows = T * TOP_K + N_EXPERTS * ROW_BLOCK
        pad_starts, block_e, n_used = _block_tables(counts, n_rows // ROW_BLOCK)
        codes = code[:, :TOP_K]
        xs = _dispatch(pad_starts, codes.reshape(T // DP_TM, 1, DP_TM * TOP_K), h2p,
                       jnp.zeros((n_rows, D_MODEL // 2), jnp.uint32))
        ys = _experts(block_e, n_used, xs, moe_w_gu[l].astype(BF16), moe_b_gu[l][:, None, :],
                      moe_w_down[l].astype(BF16), moe_b_down[l][:, None, :])
        x2 = _combine(pad_starts, codes.reshape(T // CB_TM, 1, CB_TM * TOP_K), ys, x1, gates)
    return x2.reshape(B, S, D)
```

```python
import functools

import numpy as np
import jax
import jax.numpy as jnp
from jax import lax
from jax.experimental import pallas as pl
from jax.experimental.pallas import tpu as pltpu

F32 = jnp.float32
BF16 = jnp.bfloat16

D_MODEL = 2048
GM_WIDTH = 1024
GM_GROUPS = 8
GM_CHUNK = 128
MLA_HEADS = 16
Q_LORA = 512
KV_LORA = 512
QK_NOPE = 128
QK_ROPE = 64
V_DIM = 128
QK_DIM = QK_NOPE + QK_ROPE
QK_PAD = 256
MLA_SCALE = QK_DIM ** -0.5
ROPE_THETA = 10000.0
MEM_LEN = 256
MEM_HEADS = 4
MEM_HEAD_DIM = 256
MEM_WIDTH = MEM_HEADS * MEM_HEAD_DIM
MEM_SCALE = MEM_HEAD_DIM ** -0.5
N_EXPERTS = 32
EXPERT_BITS = 5
TOP_K = 4
D_FF = 2048
SWIGLU_LIMIT = 7.0
SWIGLU_ALPHA = 1.702
ROW_BLOCK = 256
EPS = 1e-6
LANES = 128
NEG_INF = float("-inf")

PROJ_COLS = 2 * GM_WIDTH + Q_LORA + KV_LORA + MEM_WIDTH
PG_COLS = PROJ_COLS + 3 * D_MODEL

MIB = 1024 * 1024


def _params(semantics, vmem_mib):
    return pltpu.CompilerParams(dimension_semantics=semantics, vmem_limit_bytes=vmem_mib * MIB)


def _const_spec(shape):
    nd = len(shape)
    return pl.BlockSpec(shape, lambda *_: (0,) * nd, pipeline_mode=pl.Buffered(1))


P1_TM = 1024
P1_TN = 1024


def _norm_proj_kernel(x_ref, g_ref, w_ref, b_ref, wr_ref, o_ref, kr_ref, h_scr, *, n_plain):
    j = pl.program_id(1)

    @pl.when(j == 0)
    def _():
        def body(c, carry):
            rows = pl.ds(pl.multiple_of(c * 128, 128), 128)
            x = x_ref[rows, :]
            ms = jnp.mean(x * x, axis=-1, keepdims=True)
            h_scr[rows, :] = (x * lax.rsqrt(ms + EPS) * g_ref[...]).astype(BF16)
            return carry

        lax.fori_loop(0, P1_TM // 128, body, 0)
        kr_ref[...] = jnp.dot(h_scr[...], wr_ref[...], preferred_element_type=F32)

    acc = jnp.dot(h_scr[...], w_ref[...], preferred_element_type=F32)
    gated = jax.nn.sigmoid(acc + b_ref[...])
    o_ref[...] = jnp.where(j >= n_plain, gated, acc).astype(BF16)


def _norm_proj(x2, ln1_g, w1, b1, wr):
    T = x2.shape[0]
    grid = (T // P1_TM, PG_COLS // P1_TN)
    return pl.pallas_call(
        functools.partial(_norm_proj_kernel, n_plain=PROJ_COLS // P1_TN),
        grid=grid,
        in_specs=[
            pl.BlockSpec((P1_TM, D_MODEL), lambda i, j: (i, 0)),
            pl.BlockSpec((1, D_MODEL), lambda i, j: (0, 0)),
            pl.BlockSpec((D_MODEL, P1_TN), lambda i, j: (0, j)),
            pl.BlockSpec((1, P1_TN), lambda i, j: (0, j)),
            pl.BlockSpec((D_MODEL, LANES), lambda i, j: (0, 0)),
        ],
        out_specs=[
            pl.BlockSpec((P1_TM, P1_TN), lambda i, j: (i, j)),
            pl.BlockSpec((P1_TM, LANES), lambda i, j: (i, 0)),
        ],
        out_shape=[
            jax.ShapeDtypeStruct((T, PG_COLS), BF16),
            jax.ShapeDtypeStruct((T, LANES), F32),
        ],
        scratch_shapes=[pltpu.VMEM((P1_TM, D_MODEL), BF16)],
        compiler_params=_params(("parallel", "arbitrary"), 48),
        name="norm_proj",
    )(x2, ln1_g, w1, b1, wr)


GM_TM = 512


def _gmlp_kernel(u_ref, v_ref, g0_ref, lng_ref, lnb_ref, ws_ref, bias_ref, wpa_ref, o_ref, vb_scr, a_scr):
    v = v_ref[...].astype(F32)
    mu = jnp.mean(v, axis=-1, keepdims=True)
    c = v - mu
    var = jnp.mean(c * c, axis=-1, keepdims=True)
    vb_scr[...] = (c * lax.rsqrt(var + EPS) * lng_ref[...] + lnb_ref[...]).astype(BF16)

    row = lax.broadcasted_iota(jnp.int32, (GM_CHUNK, GM_CHUNK), 0)
    col = lax.broadcasted_iota(jnp.int32, (GM_CHUNK, GM_CHUNK), 1)
    causal = col <= row
    for g in range(GM_GROUPS):
        cols = slice(g * GM_CHUNK, (g + 1) * GM_CHUNK)
        wg = jnp.where(causal, ws_ref[g], jnp.zeros((), BF16))
        for ch in range(GM_TM // GM_CHUNK):
            rows = slice(ch * GM_CHUNK, (ch + 1) * GM_CHUNK)
            mixed = jnp.dot(wg, vb_scr[rows, cols], preferred_element_type=F32) + bias_ref[:, cols]
            a_scr[rows, cols] = (u_ref[rows, cols].astype(F32) * mixed).astype(BF16)

    ma = jnp.dot(a_scr[...], wpa_ref[...], preferred_element_type=F32)
    o_ref[...] = (g0_ref[...].astype(F32) * ma).astype(BF16)


def _gmlp(pg, ln_g, ln_b, ws, bias_full, wpa):
    T = pg.shape[0]
    return pl.pallas_call(
        _gmlp_kernel,
        grid=(T // GM_TM,),
        in_specs=[
            pl.BlockSpec((GM_TM, GM_WIDTH), lambda i: (i, 0)),
            pl.BlockSpec((GM_TM, GM_WIDTH), lambda i: (i, 1)),
            pl.BlockSpec((GM_TM, D_MODEL), lambda i: (i, PROJ_COLS // D_MODEL)),
            _const_spec((1, GM_WIDTH)),
            _const_spec((1, GM_WIDTH)),
            _const_spec((GM_GROUPS, GM_CHUNK, GM_CHUNK)),
            _const_spec((GM_CHUNK, GM_WIDTH)),
            _const_spec((GM_WIDTH, D_MODEL)),
        ],
        out_specs=pl.BlockSpec((GM_TM, D_MODEL), lambda i: (i, 0)),
        out_shape=jax.ShapeDtypeStruct((T, D_MODEL), BF16),
        scratch_shapes=[pltpu.VMEM((GM_TM, GM_WIDTH), BF16), pltpu.VMEM((GM_TM, GM_WIDTH), BF16)],
        compiler_params=_params(("parallel",), 40),
        name="gmlp",
    )(pg, pg, pg, ln_g, ln_b, ws, bias_full, wpa)


QKV_TM = 1024
HEAD_BLK = 2
FA_T = 512
FA_QS = 256


def _rope(t, cos, sin):
    return t * cos + pltpu.roll(t, QK_ROPE, 1) * sin


def _row_sum_all_lanes(sq, weights):
    hi = sq.astype(BF16)
    lo = (sq - hi.astype(F32)).astype(BF16)
    w = weights.astype(BF16)
    return (jnp.dot(hi, w, preferred_element_type=F32) + jnp.dot(lo, w, preferred_element_type=F32))


def _mla_qkv_kernel(cq_ref, ckv_ref, kr_ref, pos_ref, gcq_ref, gckv_ref, wq_ref, wkv_ref, gq_ref, gk_ref,
                    cst_ref, sumw_ref, q_ref, k_ref, vt_ref, cqn_scr, ckvn_scr, cos_scr, sin_scr, krsq_scr):
    hb = pl.program_id(1)
    w_nope = sumw_ref[:QK_NOPE, :]
    w_rope = sumw_ref[QK_NOPE:, :]

    @pl.when(hb == 0)
    def _():
        cq = cq_ref[...].astype(F32)
        ms = jnp.mean(cq * cq, axis=-1, keepdims=True)
        cqn_scr[...] = (cq * lax.rsqrt(ms + EPS) * gcq_ref[...]).astype(BF16)
        ckv = ckv_ref[...].astype(F32)
        ms = jnp.mean(ckv * ckv, axis=-1, keepdims=True)
        ckvn_scr[...] = (ckv * lax.rsqrt(ms + EPS) * gckv_ref[...]).astype(BF16)
        ang = pos_ref[...].astype(F32) * cst_ref[0:1, :]
        cos_scr[...] = jnp.cos(ang) * cst_ref[1:2, :]
        sin_scr[...] = jnp.sin(ang) * cst_ref[2:3, :]
        kr = kr_ref[...]
        krsq_scr[...] = _row_sum_all_lanes(kr * kr, w_rope)

    cos = cos_scr[...]
    sin = sin_scr[...]
    inv_dim = 1.0 / QK_DIM
    kr = kr_ref[...]

    yq2 = jnp.dot(cqn_scr[...], wq_ref[...], preferred_element_type=F32)
    ykv2 = jnp.dot(ckvn_scr[...], wkv_ref[...], preferred_element_type=F32)
    for hh in range(HEAD_BLK):
        yq = yq2[:, hh * QK_PAD:(hh + 1) * QK_PAD]
        qn = yq[:, :QK_NOPE]
        qt = yq[:, QK_NOPE:]
        ssq = _row_sum_all_lanes(qn * qn, w_nope) + _row_sum_all_lanes(qt * qt, w_rope)
        rs = lax.rsqrt(ssq * inv_dim + EPS) * MLA_SCALE
        q_ref[hh, :, :QK_NOPE] = (qn * rs * gq_ref[:, :QK_NOPE]).astype(BF16)
        q_ref[hh, :, QK_NOPE:] = _rope(qt * rs * gq_ref[:, QK_NOPE:], cos, sin).astype(BF16)

        ykv = ykv2[:, hh * (QK_NOPE + V_DIM):(hh + 1) * (QK_NOPE + V_DIM)]
        kn = ykv[:, :QK_NOPE]
        ssq = _row_sum_all_lanes(kn * kn, w_nope) + krsq_scr[...]
        rs = lax.rsqrt(ssq * inv_dim + EPS)
        k_ref[hh, :, :QK_NOPE] = (kn * rs * gk_ref[:, :QK_NOPE]).astype(BF16)
        k_ref[hh, :, QK_NOPE:] = _rope(kr * rs * gk_ref[:, QK_NOPE:], cos, sin).astype(BF16)
        vv = ykv[:, QK_NOPE:]
        for c in range(QKV_TM // FA_T):
            vt_ref[hh, c] = vv[c * FA_T:(c + 1) * FA_T, :].T.astype(BF16)


def _mla_qkv(pg, kr, pos, gcq, gckv, wq, wkv, gq, gk, cst, B, S):
    T = pg.shape[0]
    sumw = jnp.concatenate([jnp.ones((QK_NOPE, LANES), F32), jnp.full((QK_PAD - QK_NOPE, LANES), 0.5, F32)])
    spb = S // QKV_TM
    cpt = QKV_TM // FA_T
    head_spec = lambda w: pl.BlockSpec((None, HEAD_BLK, QKV_TM, w), lambda i, h: (i // spb, h, i % spb, 0))
    return pl.pallas_call(
        _mla_qkv_kernel,
        grid=(T // QKV_TM, MLA_HEADS // HEAD_BLK),
        in_specs=[
            pl.BlockSpec((QKV_TM, Q_LORA), lambda i, h: (i, 2 * GM_WIDTH // Q_LORA)),
            pl.BlockSpec((QKV_TM, KV_LORA), lambda i, h: (i, 2 * GM_WIDTH // KV_LORA + 1)),
            pl.BlockSpec((QKV_TM, LANES), lambda i, h: (i, 0)),
            pl.BlockSpec((QKV_TM, 1), lambda i, h: (i, 0)),
            pl.BlockSpec((1, Q_LORA), lambda i, h: (0, 0)),
            pl.BlockSpec((1, KV_LORA), lambda i, h: (0, 0)),
            pl.BlockSpec((Q_LORA, HEAD_BLK * QK_PAD), lambda i, h: (0, h)),
            pl.BlockSpec((KV_LORA, HEAD_BLK * (QK_NOPE + V_DIM)), lambda i, h: (0, h)),
            pl.BlockSpec((1, QK_PAD), lambda i, h: (0, 0)),
            pl.BlockSpec((1, QK_PAD), lambda i, h: (0, 0)),
            pl.BlockSpec((8, LANES), lambda i, h: (0, 0)),
            pl.BlockSpec((QK_PAD, LANES), lambda i, h: (0, 0)),
        ],
        out_specs=[
            head_spec(QK_PAD),
            head_spec(QK_PAD),
            pl.BlockSpec((None, HEAD_BLK, cpt, V_DIM, FA_T), lambda i, h: (i // spb, h, i % spb, 0, 0)),
        ],
        out_shape=[
            jax.ShapeDtypeStruct((B, MLA_HEADS, S, QK_PAD), BF16),
            jax.ShapeDtypeStruct((B, MLA_HEADS, S, QK_PAD), BF16),
            jax.ShapeDtypeStruct((B, MLA_HEADS, S // FA_T, V_DIM, FA_T), BF16),
        ],
        scratch_shapes=[
            pltpu.VMEM((QKV_TM, Q_LORA), BF16),
            pltpu.VMEM((QKV_TM, KV_LORA), BF16),
            pltpu.VMEM((QKV_TM, LANES), F32),
            pltpu.VMEM((QKV_TM, LANES), F32),
            pltpu.VMEM((QKV_TM, LANES), F32),
        ],
        compiler_params=_params(("parallel", "arbitrary"), 48),
        name="mla_qkv",
    )(pg, pg, kr, pos, gcq, gckv, wq, wkv, gq, gk, cst, sumw)


def _flash_kernel(q_ref, k_ref, vt_ref, o_ref, m_scr, l_scr, acc_scr):
    qi = pl.program_id(2)
    m_scr[...] = jnp.full_like(m_scr, NEG_INF)
    l_scr[...] = jnp.zeros_like(l_scr)
    acc_scr[...] = jnp.zeros_like(acc_scr)

    def step(kb, diagonal):
        ks = pl.ds(pl.multiple_of(kb * FA_T, FA_T), FA_T)
        chains = [(hh, qs) for hh in range(HEAD_BLK) for qs in range(FA_T // FA_QS)]
        scores = []
        for hh, qs in chains:
            qc = slice(qs * FA_QS, (qs + 1) * FA_QS)
            st = lax.dot_general(k_ref[hh, ks, :], q_ref[hh, qc, :], (((1,), (1,)), ((), ())),
                                 preferred_element_type=F32)
            if diagonal:
                krow = lax.broadcasted_iota(jnp.int32, (FA_T, FA_QS), 0)
                qcol = lax.broadcasted_iota(jnp.int32, (FA_T, FA_QS), 1) + qs * FA_QS
                st = jnp.where(krow <= qcol, st, NEG_INF)
            scores.append(st)
        probs = []
        for (hh, qs), st in zip(chains, scores):
            qc = slice(qs * FA_QS, (qs + 1) * FA_QS)
            m_prev = m_scr[hh, :, qc]
            m_new = jnp.maximum(m_prev, jnp.max(st, axis=0, keepdims=True))
            alpha = jnp.exp(m_prev - m_new)
            p = jnp.exp(st - m_new)
            l_scr[hh, :, qc] = alpha * l_scr[hh, :, qc] + jnp.sum(p, axis=0, keepdims=True)
            m_scr[hh, :, qc] = m_new
            probs.append((alpha, p.astype(BF16)))
        for (hh, qs), (alpha, p) in zip(chains, probs):
            qc = slice(qs * FA_QS, (qs + 1) * FA_QS)
            acc_scr[hh, :, qc] = alpha * acc_scr[hh, :, qc] + jnp.dot(vt_ref[hh, kb], p,
                                                                      preferred_element_type=F32)

    def body(kb, carry):
        step(kb, False)
        return carry

    lax.fori_loop(0, qi, body, 0)
    step(qi, True)
    for hh in range(HEAD_BLK):
        o_ref[:, hh * V_DIM:(hh + 1) * V_DIM] = (acc_scr[hh] / l_scr[hh]).T.astype(BF16)


def _flash(q, k, vt):
    B, H, S, _ = q.shape
    nq = S // FA_T
    return pl.pallas_call(
        _flash_kernel,
        grid=(B, H // HEAD_BLK, nq),
        in_specs=[
            pl.BlockSpec((None, HEAD_BLK, FA_T, QK_PAD), lambda b, h, i: (b, h, i, 0)),
            pl.BlockSpec((None, HEAD_BLK, S, QK_PAD), lambda b, h, i: (b, h, 0, 0)),
            pl.BlockSpec((None, HEAD_BLK, nq, V_DIM, FA_T), lambda b, h, i: (b, h, 0, 0, 0)),
        ],
        out_specs=pl.BlockSpec((FA_T, HEAD_BLK * V_DIM), lambda b, h, i: (b * nq + i, h)),
        out_shape=jax.ShapeDtypeStruct((B * S, H * V_DIM), BF16),
        scratch_shapes=[
            pltpu.VMEM((HEAD_BLK, 1, FA_T), F32),
            pltpu.VMEM((HEAD_BLK, 1, FA_T), F32),
            pltpu.VMEM((HEAD_BLK, V_DIM, FA_T), F32),
        ],
        compiler_params=_params(("parallel", "parallel", "arbitrary"), 40),
        name="mla_attention",
    )(q, k, vt)


def _mem_kv_kernel(mem_ref, g_ref, wk_ref, wv_ref, kng_ref, k_ref, v_ref):
    m = mem_ref[...]
    ms = jnp.mean(m * m, axis=-1, keepdims=True)
    mb = (m * lax.rsqrt(ms + EPS) * g_ref[...]).astype(BF16)
    kk = jnp.dot(mb, wk_ref[...], preferred_element_type=F32)
    for h in range(MEM_HEADS):
        cols = slice(h * MEM_HEAD_DIM, (h + 1) * MEM_HEAD_DIM)
        kh = kk[:, cols]
        ms = jnp.mean(kh * kh, axis=-1, keepdims=True)
        k_ref[:, cols] = (kh * lax.rsqrt(ms + EPS) * kng_ref[...]).astype(BF16)
    v_ref[...] = jnp.dot(mb, wv_ref[...], preferred_element_type=F32).astype(BF16)


def _mem_kv(mem, g, wk, wv, kng):
    B = mem.shape[0]
    return pl.pallas_call(
        _mem_kv_kernel,
        grid=(B,),
        in_specs=[
            pl.BlockSpec((None, MEM_LEN, D_MODEL), lambda b: (b, 0, 0)),
            _const_spec((1, D_MODEL)),
            _const_spec((D_MODEL, MEM_WIDTH)),
            _const_spec((D_MODEL, MEM_WIDTH)),
            _const_spec((1, MEM_HEAD_DIM)),
        ],
        out_specs=[
            pl.BlockSpec((None, MEM_LEN, MEM_WIDTH), lambda b: (b, 0, 0)),
            pl.BlockSpec((None, MEM_LEN, MEM_WIDTH), lambda b: (b, 0, 0)),
        ],
        out_shape=[
            jax.ShapeDtypeStruct((B, MEM_LEN, MEM_WIDTH), BF16),
            jax.ShapeDtypeStruct((B, MEM_LEN, MEM_WIDTH), BF16),
        ],
        compiler_params=_params(("parallel",), 32),
        name="mem_kv",
    )(mem, g, wk, wv, kng)


MA_TM = 512


def _mem_attn_kernel(qm_ref, km_ref, vm_ref, g2_ref, m0_ref, qng_ref, wpc_ref, o_ref, c_scr):
    for h in range(MEM_HEADS):
        cols = slice(h * MEM_HEAD_DIM, (h + 1) * MEM_HEAD_DIM)
        qh = qm_ref[:, cols].astype(F32)
        ms = jnp.mean(qh * qh, axis=-1, keepdims=True)
        qn = (qh * (lax.rsqrt(ms + EPS) * MEM_SCALE) * qng_ref[...]).astype(BF16)
        s = lax.dot_general(qn, km_ref[:, cols], (((1,), (1,)), ((), ())), preferred_element_type=F32)
        e = jnp.exp(s - jnp.max(s, axis=-1, keepdims=True))
        p = (e / jnp.sum(e, axis=-1, keepdims=True)).astype(BF16)
        c_scr[:, cols] = jnp.dot(p, vm_ref[:, cols], preferred_element_type=F32).astype(BF16)
    mc = jnp.dot(c_scr[...], wpc_ref[...], preferred_element_type=F32)
    o_ref[...] = (m0_ref[...].astype(F32) + g2_ref[...].astype(F32) * mc).astype(BF16)


def _mem_attn(pg, km, vm, m0, qng, wpc, S):
    T = pg.shape[0]
    spb = S // MA_TM
    return pl.pallas_call(
        _mem_attn_kernel,
        grid=(T // MA_TM,),
        in_specs=[
            pl.BlockSpec((MA_TM, MEM_WIDTH), lambda i: (i, (2 * GM_WIDTH + Q_LORA + KV_LORA) // MEM_WIDTH)),
            pl.BlockSpec((None, MEM_LEN, MEM_WIDTH), lambda i: (i // spb, 0, 0)),
            pl.BlockSpec((None, MEM_LEN, MEM_WIDTH), lambda i: (i // spb, 0, 0)),
            pl.BlockSpec((MA_TM, D_MODEL), lambda i: (i, PROJ_COLS // D_MODEL + 2)),
            pl.BlockSpec((MA_TM, D_MODEL), lambda i: (i, 0)),
            _const_spec((1, MEM_HEAD_DIM)),
            _const_spec((MEM_WIDTH, D_MODEL)),
        ],
        out_specs=pl.BlockSpec((MA_TM, D_MODEL), lambda i: (i, 0)),
        out_shape=jax.ShapeDtypeStruct((T, D_MODEL), BF16),
        scratch_shapes=[pltpu.VMEM((MA_TM, MEM_WIDTH), BF16)],
        compiler_params=_params(("parallel",), 40),
        name="mem_attention",
    )(pg, km, vm, pg, m0, qng, wpc)


MG_TM = 512


def _pack_bf16_pair(a, b):
    hi = pltpu.bitcast(a.astype(BF16).astype(F32), jnp.uint32)
    lo = pltpu.bitcast(b.astype(BF16).astype(F32), jnp.uint32)
    return hi | (lo >> 16)


def _unpack_bf16_pair(p):
    hi = pltpu.bitcast(p & jnp.uint32(0xFFFF0000), F32)
    lo = pltpu.bitcast(p << 16, F32)
    return hi, lo


def _attn_proj_kernel(b_ref, g1_ref, m1_ref, wpb_ref, o_ref):
    mb = jnp.dot(b_ref[...], wpb_ref[...], preferred_element_type=F32)
    o_ref[...] = (m1_ref[...].astype(F32) + g1_ref[...].astype(F32) * mb).astype(BF16)


def _attn_proj(b_attn, pg, m1, wpb):
    T = m1.shape[0]
    row = pl.BlockSpec((MG_TM, D_MODEL), lambda i: (i, 0))
    return pl.pallas_call(
        _attn_proj_kernel,
        grid=(T // MG_TM,),
        in_specs=[
            row,
            pl.BlockSpec((MG_TM, D_MODEL), lambda i: (i, PROJ_COLS // D_MODEL + 1)),
            row,
            _const_spec((MLA_HEADS * V_DIM, D_MODEL)),
        ],
        out_specs=row,
        out_shape=jax.ShapeDtypeStruct((T, D_MODEL), BF16),
        compiler_params=_params(("parallel",), 40),
        name="attn_proj",
    )(b_attn, pg, m1, wpb)


def _merge_kernel(mg_ref, x_ref, wo_ref, ln2_ref, rw_ref, rb_ref,
                  x1_ref, h2p_ref, code_ref, gate_ref, cnt_ref, cnt_scr):
    @pl.when(pl.program_id(0) == 0)
    def _():
        cnt_scr[...] = jnp.zeros_like(cnt_scr)

    x1 = x_ref[...] + jnp.dot(mg_ref[...], wo_ref[...], preferred_element_type=F32)
    x1_ref[...] = x1
    ms = jnp.mean(x1 * x1, axis=-1, keepdims=True)
    h2 = x1 * lax.rsqrt(ms + EPS) * ln2_ref[...]
    h2p_ref[...] = _pack_bf16_pair(h2[:, :D_MODEL // 2], h2[:, D_MODEL // 2:])

    logits = jnp.dot(h2, rw_ref[...], preferred_element_type=F32, precision=lax.Precision.HIGHEST)
    lane = lax.broadcasted_iota(jnp.int32, (MG_TM, LANES), 1)
    work = jnp.where(lane < N_EXPERTS, logits + rb_ref[...], NEG_INF)
    earlier = (lax.broadcasted_iota(jnp.int32, (MG_TM, MG_TM), 1)
               < lax.broadcasted_iota(jnp.int32, (MG_TM, MG_TM), 0)).astype(BF16)
    base = cnt_scr[...]
    code_out = jnp.zeros((MG_TM, LANES), jnp.int32)
    val_out = jnp.zeros((MG_TM, LANES), F32)
    top = None
    denom = jnp.zeros((MG_TM, 1), F32)
    for k in range(TOP_K):
        mk = jnp.max(work, axis=-1, keepdims=True)
        ik = jnp.min(jnp.where(work == mk, lane, LANES), axis=-1, keepdims=True)
        hit = lane == ik
        work = jnp.where(hit, NEG_INF, work)
        if top is None:
            top = mk
        ek = jnp.exp(mk - top)
        denom = denom + ek
        val_out = jnp.where(lane == k, ek, val_out)
        onehot = hit.astype(BF16)
        prefix = jnp.dot(earlier, onehot, preferred_element_type=F32) + base
        rank = jnp.sum(jnp.where(hit, prefix, 0.0), axis=-1, keepdims=True).astype(jnp.int32)
        code_out = jnp.where(lane == k, rank * N_EXPERTS + ik, code_out)
        base = base + jnp.sum(onehot.astype(F32), axis=0, keepdims=True)
    code_ref[...] = code_out
    gate_ref[...] = val_out / denom
    cnt_scr[...] = base
    cnt_ref[...] = base


def _merge(merged, x2, wo, ln2, rw, rb):
    T = x2.shape[0]
    row = lambda w: pl.BlockSpec((MG_TM, w), lambda i: (i, 0))
    return pl.pallas_call(
        _merge_kernel,
        grid=(T // MG_TM,),
        in_specs=[
            row(D_MODEL),
            row(D_MODEL),
            _const_spec((D_MODEL, D_MODEL)),
            _const_spec((1, D_MODEL)),
            _const_spec((D_MODEL, LANES)),
            _const_spec((1, LANES)),
        ],
        out_specs=[row(D_MODEL), row(D_MODEL // 2), row(LANES), row(LANES),
                   pl.BlockSpec((1, LANES), lambda i: (0, 0))],
        out_shape=[
            jax.ShapeDtypeStruct((T, D_MODEL), F32),
            jax.ShapeDtypeStruct((T, D_MODEL // 2), jnp.uint32),
            jax.ShapeDtypeStruct((T, LANES), jnp.int32),
            jax.ShapeDtypeStruct((T, LANES), F32),
            jax.ShapeDtypeStruct((1, LANES), F32),
        ],
        scratch_shapes=[pltpu.VMEM((1, LANES), F32)],
        compiler_params=_params(("arbitrary",), 56),
        name="merge_router",
    )(merged, x2, wo, ln2, rw, rb)


DP_TM = 256
DMA_UNROLL = 8


def _row_copy(src, src_row, dst, dst_row, sem):
    return pltpu.make_async_copy(src.at[pl.ds(src_row, 1), :], dst.at[pl.ds(dst_row, 1), :], sem)


def _dest_row(ps_ref, code):
    return ps_ref[code & (N_EXPERTS - 1)] + (code >> EXPERT_BITS)


def _dispatch_kernel(ps_ref, code_ref, h2p_ref, xs_in, xs_hbm, sem):
    del xs_in

    def issue(j, carry):
        _row_copy(h2p_ref, j >> 2, xs_hbm, _dest_row(ps_ref, code_ref[0, j]), sem).start()
        return carry

    lax.fori_loop(0, DP_TM * TOP_K, issue, 0, unroll=DMA_UNROLL)

    def drain(j, carry):
        _row_copy(h2p_ref, 0, xs_hbm, 0, sem).wait()
        return carry

    lax.fori_loop(0, DP_TM * TOP_K, drain, 0, unroll=DMA_UNROLL)


def _dispatch(pad_starts, codes3, h2p, xs_zero):
    T = h2p.shape[0]
    grid_spec = pltpu.PrefetchScalarGridSpec(
        num_scalar_prefetch=1,
        grid=(T // DP_TM,),
        in_specs=[
            pl.BlockSpec((None, 1, DP_TM * TOP_K), lambda i, ps: (i, 0, 0), memory_space=pltpu.SMEM),
            pl.BlockSpec((DP_TM, D_MODEL // 2), lambda i, ps: (i, 0)),
            pl.BlockSpec(memory_space=pl.ANY),
        ],
        out_specs=pl.BlockSpec(memory_space=pl.ANY),
        scratch_shapes=[pltpu.SemaphoreType.DMA(())],
    )
    return pl.pallas_call(
        _dispatch_kernel,
        grid_spec=grid_spec,
        out_shape=jax.ShapeDtypeStruct(xs_zero.shape, xs_zero.dtype),
        input_output_aliases={3: 0},
        compiler_params=_params(("arbitrary",), 32),
        name="moe_dispatch",
    )(pad_starts, codes3, h2p, xs_zero)


FF_CHUNK = 1024


def _expert_kernel(be_ref, nb_ref, xs_ref, wgu_ref, bgu_ref, wd_ref, bd_ref, y_ref):
    m = pl.program_id(0)

    @pl.when(m < nb_ref[0])
    def _():
        hi, lo = _unpack_bf16_pair(xs_ref[...])
        xb = jnp.concatenate([hi.astype(BF16), lo.astype(BF16)], axis=1)
        acc = jnp.zeros((ROW_BLOCK, D_MODEL), F32)
        for fc in range(D_FF // FF_CHUNK):
            gc = slice(fc * FF_CHUNK, (fc + 1) * FF_CHUNK)
            uc = slice(D_FF + fc * FF_CHUNK, D_FF + (fc + 1) * FF_CHUNK)
            gate = jnp.dot(xb, wgu_ref[:, gc], preferred_element_type=F32) + bgu_ref[:, gc]
            up = jnp.dot(xb, wgu_ref[:, uc], preferred_element_type=F32) + bgu_ref[:, uc]
            gate = jnp.minimum(gate, SWIGLU_LIMIT)
            up = jnp.clip(up, -SWIGLU_LIMIT, SWIGLU_LIMIT)
            glu = gate * jax.nn.sigmoid(gate * SWIGLU_ALPHA)
            act = ((up + 1.0) * glu).astype(BF16)
            acc = acc + jnp.dot(act, wd_ref[gc, :], preferred_element_type=F32)
        y = acc + bd_ref[...]
        y_ref[...] = _pack_bf16_pair(y[:, :D_MODEL // 2], y[:, D_MODEL // 2:])

    @pl.when(m >= nb_ref[0])
    def _():
        y_ref[...] = jnp.zeros_like(y_ref)


def _experts(block_e, n_used, xs, wgu, bgu, wd, bd):
    n_blocks = xs.shape[0] // ROW_BLOCK
    grid_spec = pltpu.PrefetchScalarGridSpec(
        num_scalar_prefetch=2,
        grid=(n_blocks,),
        in_specs=[
            pl.BlockSpec((ROW_BLOCK, D_MODEL // 2), lambda m, be, nb: (m, 0)),
            pl.BlockSpec((None, D_MODEL, 2 * D_FF), lambda m, be, nb: (be[m], 0, 0),
                         pipeline_mode=pl.Buffered(1)),
            pl.BlockSpec((None, 1, 2 * D_FF), lambda m, be, nb: (be[m], 0, 0)),
            pl.BlockSpec((None, D_FF, D_MODEL), lambda m, be, nb: (be[m], 0, 0),
                         pipeline_mode=pl.Buffered(1)),
            pl.BlockSpec((None, 1, D_MODEL), lambda m, be, nb: (be[m], 0, 0)),
        ],
        out_specs=pl.BlockSpec((ROW_BLOCK, D_MODEL // 2), lambda m, be, nb: (m, 0)),
    )
    return pl.pallas_call(
        _expert_kernel,
        grid_spec=grid_spec,
        out_shape=jax.ShapeDtypeStruct((n_blocks * ROW_BLOCK, D_MODEL // 2), jnp.uint32),
        compiler_params=_params(("arbitrary",), 56),
        name="moe_experts",
    )(block_e, n_used, xs, wgu, bgu, wd, bd)


CB_TM = 128


def _combine_kernel(ps_ref, code_ref, code_next_ref, ys_hbm, x1_ref, gate_ref, o_ref, buf, sem):
    i = pl.program_id(0)
    slot = i & 1

    def gather_tile(codes, s):
        def issue(j, carry):
            row = _dest_row(ps_ref, codes[0, j])
            _row_copy(ys_hbm, row, buf.at[s, j & (TOP_K - 1)], j >> 2, sem.at[s]).start()
            return carry

        lax.fori_loop(0, CB_TM * TOP_K, issue, 0, unroll=DMA_UNROLL)

    @pl.when(i == 0)
    def _():
        gather_tile(code_ref, 0)

    @pl.when(i + 1 < pl.num_programs(0))
    def _():
        gather_tile(code_next_ref, 1 - slot)

    def drain(j, carry):
        _row_copy(ys_hbm, 0, buf.at[slot, 0], 0, sem.at[slot]).wait()
        return carry

    lax.fori_loop(0, CB_TM * TOP_K, drain, 0, unroll=DMA_UNROLL)

    half = D_MODEL // 2
    gates = gate_ref[...]
    out_hi = x1_ref[:, :half]
    out_lo = x1_ref[:, half:]
    for k in range(TOP_K):
        hi, lo = _unpack_bf16_pair(buf[slot, k])
        g = gates[:, k:k + 1]
        out_hi = out_hi + g * hi
        out_lo = out_lo + g * lo
    o_ref[:, :half] = out_hi
    o_ref[:, half:] = out_lo


def _combine(pad_starts, codes3, ys, x1, gates):
    T = x1.shape[0]
    n = T // CB_TM
    code_spec = lambda f: pl.BlockSpec((None, 1, CB_TM * TOP_K), f, memory_space=pltpu.SMEM)
    grid_spec = pltpu.PrefetchScalarGridSpec(
        num_scalar_prefetch=1,
        grid=(n,),
        in_specs=[
            code_spec(lambda i, ps: (i, 0, 0)),
            code_spec(lambda i, ps: (jnp.minimum(i + 1, n - 1), 0, 0)),
            pl.BlockSpec(memory_space=pl.ANY),
            pl.BlockSpec((CB_TM, D_MODEL), lambda i, ps: (i, 0)),
            pl.BlockSpec((CB_TM, LANES), lambda i, ps: (i, 0)),
        ],
        out_specs=pl.BlockSpec((CB_TM, D_MODEL), lambda i, ps: (i, 0)),
        scratch_shapes=[pltpu.VMEM((2, TOP_K, CB_TM, D_MODEL // 2), jnp.uint32),
                        pltpu.SemaphoreType.DMA((2,))],
    )
    return pl.pallas_call(
        _combine_kernel,
        grid_spec=grid_spec,
        out_shape=jax.ShapeDtypeStruct((T, D_MODEL), F32),
        compiler_params=_params(("arbitrary",), 32),
        name="moe_combine",
    )(pad_starts, codes3, codes3, ys, x1, gates)


def _block_tables(counts_f32, n_blocks):
    counts = counts_f32[0, :N_EXPERTS].astype(jnp.int32)
    padded = ((counts + ROW_BLOCK - 1) // ROW_BLOCK) * ROW_BLOCK
    upto = jnp.arange(N_EXPERTS)[None, :] <= jnp.arange(N_EXPERTS)[:, None]
    pad_ends = jnp.sum(jnp.where(upto, padded[None, :], 0), axis=1)
    pad_starts = pad_ends - padded
    block_start = jnp.arange(n_blocks, dtype=jnp.int32) * ROW_BLOCK
    block_e = jnp.sum((pad_ends[None, :] <= block_start[:, None]).astype(jnp.int32), axis=1)
    block_e = jnp.minimum(block_e, N_EXPERTS - 1)
    n_used = pad_ends[-1:] // ROW_BLOCK
    return pad_starts.astype(jnp.int32), block_e.astype(jnp.int32), n_used.astype(jnp.int32)


def _rope_constants():
    lane = np.arange(LANES)
    half = QK_ROPE // 2
    inv = 1.0 / (ROPE_THETA ** (np.arange(0, QK_ROPE, 2, dtype=np.float32) / QK_ROPE))
    cst = np.zeros((8, LANES), np.float32)
    cst[0, :QK_ROPE] = inv.astype(np.float32)[lane[:QK_ROPE] % half]
    cst[1, :QK_ROPE] = 1.0
    cst[2, :half] = -1.0
    cst[2, half:QK_ROPE] = 1.0
    return jnp.asarray(cst)


def _swap_halves(a):
    half = QK_ROPE // 2
    return jnp.concatenate([a[..., half:], a[..., :half]], axis=-1)


def kernel(x, mem, positions, ln1_g, w_in, w_gate, b_gate, gmlp_ln_g, gmlp_ln_b, gmlp_ws, gmlp_bs, w_pa,
           mla_cq_g, mla_w_uq, mla_ckv_g, mla_w_ukv, mla_qn_g, mla_kn_g, w_pb, mem_ln_g, mem_w_k, mem_w_v,
           mem_qn_g, mem_kn_g, w_pc, w_o, ln2_g, router_w, router_b, moe_w_gu, moe_b_gu, moe_w_down,
           moe_b_down):
    B, S, D = x.shape
    T = B * S
    x2 = x.reshape(T, D)
    for l in range(ln1_g.shape[0]):
        o_kr = 2 * GM_WIDTH + Q_LORA + KV_LORA
        o_qm = o_kr + QK_ROPE
        wi = w_in[l]
        w1 = jnp.concatenate([wi[:, :o_kr], wi[:, o_qm:], w_gate[l]], axis=1).astype(BF16)
        b1 = jnp.concatenate([jnp.zeros((PROJ_COLS,), F32), b_gate[l]])[None, :]
        w_kr = wi[:, o_kr:o_qm]
        wr = jnp.concatenate([w_kr, _swap_halves(w_kr)], axis=1).astype(BF16)

        wq3 = mla_w_uq[l].reshape(Q_LORA, MLA_HEADS, QK_DIM)
        wq = jnp.concatenate([wq3, _swap_halves(wq3[..., QK_NOPE:])], axis=-1)
        wq = wq.reshape(Q_LORA, MLA_HEADS * QK_PAD).astype(BF16)
        wkv = mla_w_ukv[l].astype(BF16)
        gq = jnp.concatenate([mla_qn_g[l], _swap_halves(mla_qn_g[l][QK_NOPE:])])[None, :]
        gk = jnp.concatenate([mla_kn_g[l], _swap_halves(mla_kn_g[l][QK_NOPE:])])[None, :]

        bias_full = jnp.broadcast_to(gmlp_bs[l].T[:, :, None], (GM_CHUNK, GM_GROUPS, GM_CHUNK))
        bias_full = bias_full.reshape(GM_CHUNK, GM_WIDTH)

        rw = jnp.pad(router_w[l], ((0, 0), (0, LANES - N_EXPERTS)))
        rb = jnp.pad(router_b[l], (0, LANES - N_EXPERTS))[None, :]

        pg, kr = _norm_proj(x2, ln1_g[l][None, :], w1, b1, wr)
        m0 = _gmlp(pg, gmlp_ln_g[l][None, :], gmlp_ln_b[l][None, :], gmlp_ws[l].astype(BF16), bias_full,
                   w_pa[l].astype(BF16))
        q, k, vt = _mla_qkv(pg, kr, positions.reshape(T, 1), mla_cq_g[l][None, :], mla_ckv_g[l][None, :],
                           wq, wkv, gq, gk, _rope_constants(), B, S)
        b_attn = _flash(q, k, vt)
        km, vm = _mem_kv(mem, mem_ln_g[l][None, :], mem_w_k[l].astype(BF16), mem_w_v[l].astype(BF16),
                         mem_kn_g[l][None, :])
        m1 = _mem_attn(pg, km, vm, m0, mem_qn_g[l][None, :], w_pc[l].astype(BF16), S)
        merged = _attn_proj(b_attn, pg, m1, w_pb[l].astype(BF16))
        x1, h2p, code, gates, counts = _merge(merged, x2, w_o[l].astype(BF16), ln2_g[l][None, :], rw, rb)

        n_rows = T * TOP_K + N_EXPERTS * ROW_BLOCK
        pad_starts, block_e, n_used = _block_tables(counts, n_rows // ROW_BLOCK)
        codes = code[:, :TOP_K]
        xs = _dispatch(pad_starts, codes.reshape(T // DP_TM, 1, DP_TM * TOP_K), h2p,
                       jnp.zeros((n_rows, D_MODEL // 2), jnp.uint32))
        ys = _experts(block_e, n_used, xs, moe_w_gu[l].astype(BF16), moe_b_gu[l][:, None, :],
                      moe_w_down[l].astype(BF16), moe_b_down[l][:, None, :])
        x2 = _combine(pad_starts, codes.reshape(T // CB_TM, 1, CB_TM * TOP_K), ys, x1, gates)
    return x2.reshape(B, S, D)
```

```python
import functools

import numpy as np
import jax
import jax.numpy as jnp
from jax import lax
from jax.experimental import pallas as pl
from jax.experimental.pallas import tpu as pltpu

F32 = jnp.float32
BF16 = jnp.bfloat16

D_MODEL = 2048
GM_WIDTH = 1024
GM_GROUPS = 8
GM_CHUNK = 128
MLA_HEADS = 16
Q_LORA = 512
KV_LORA = 512
QK_NOPE = 128
QK_ROPE = 64
V_DIM = 128
QK_DIM = QK_NOPE + QK_ROPE
QK_PAD = 256
MLA_SCALE = QK_DIM ** -0.5
ROPE_THETA = 10000.0
MEM_LEN = 256
MEM_HEADS = 4
MEM_HEAD_DIM = 256
MEM_WIDTH = MEM_HEADS * MEM_HEAD_DIM
MEM_SCALE = MEM_HEAD_DIM ** -0.5
N_EXPERTS = 32
EXPERT_BITS = 5
TOP_K = 4
D_FF = 2048
SWIGLU_LIMIT = 7.0
SWIGLU_ALPHA = 1.702
ROW_BLOCK = 256
EPS = 1e-6
LANES = 128
NEG_INF = float("-inf")

PROJ_COLS = 2 * GM_WIDTH + Q_LORA + KV_LORA + MEM_WIDTH
PG_COLS = PROJ_COLS + 3 * D_MODEL

MIB = 1024 * 1024


def _params(semantics, vmem_mib):
    return pltpu.CompilerParams(dimension_semantics=semantics, vmem_limit_bytes=vmem_mib * MIB)


def _const_spec(shape):
    nd = len(shape)
    return pl.BlockSpec(shape, lambda *_: (0,) * nd, pipeline_mode=pl.Buffered(1))


P1_TM = 1024
P1_TN = 1024


def _norm_proj_kernel(x_ref, g_ref, w_ref, b_ref, wr_ref, o_ref, kr_ref, h_scr, *, n_plain):
    j = pl.program_id(1)

    @pl.when(j == 0)
    def _():
        def body(c, carry):
            rows = pl.ds(pl.multiple_of(c * 128, 128), 128)
            x = x_ref[rows, :]
            ms = jnp.mean(x * x, axis=-1, keepdims=True)
            h_scr[rows, :] = (x * lax.rsqrt(ms + EPS) * g_ref[...]).astype(BF16)
            return carry

        lax.fori_loop(0, P1_TM // 128, body, 0)
        kr_ref[...] = jnp.dot(h_scr[...], wr_ref[...], preferred_element_type=F32)

    acc = jnp.dot(h_scr[...], w_ref[...], preferred_element_type=F32)
    gated = jax.nn.sigmoid(acc + b_ref[...])
    o_ref[...] = jnp.where(j >= n_plain, gated, acc).astype(BF16)


def _norm_proj(x2, ln1_g, w1, b1, wr):
    T = x2.shape[0]
    grid = (T // P1_TM, PG_COLS // P1_TN)
    return pl.pallas_call(
        functools.partial(_norm_proj_kernel, n_plain=PROJ_COLS // P1_TN),
        grid=grid,
        in_specs=[
            pl.BlockSpec((P1_TM, D_MODEL), lambda i, j: (i, 0)),
            pl.BlockSpec((1, D_MODEL), lambda i, j: (0, 0)),
            pl.BlockSpec((D_MODEL, P1_TN), lambda i, j: (0, j)),
            pl.BlockSpec((1, P1_TN), lambda i, j: (0, j)),
            pl.BlockSpec((D_MODEL, LANES), lambda i, j: (0, 0)),
        ],
        out_specs=[
            pl.BlockSpec((P1_TM, P1_TN), lambda i, j: (i, j)),
            pl.BlockSpec((P1_TM, LANES), lambda i, j: (i, 0)),
        ],
        out_shape=[
            jax.ShapeDtypeStruct((T, PG_COLS), BF16),
            jax.ShapeDtypeStruct((T, LANES), F32),
        ],
        scratch_shapes=[pltpu.VMEM((P1_TM, D_MODEL), BF16)],
        compiler_params=_params(("parallel", "arbitrary"), 48),
        name="norm_proj",
    )(x2, ln1_g, w1, b1, wr)


GM_TM = 512


def _gmlp_kernel(u_ref, v_ref, g0_ref, lng_ref, lnb_ref, ws_ref, bias_ref, wpa_ref, o_ref, vb_scr, a_scr):
    v = v_ref[...].astype(F32)
    mu = jnp.mean(v, axis=-1, keepdims=True)
    c = v - mu
    var = jnp.mean(c * c, axis=-1, keepdims=True)
    vb_scr[...] = (c * lax.rsqrt(var + EPS) * lng_ref[...] + lnb_ref[...]).astype(BF16)

    row = lax.broadcasted_iota(jnp.int32, (GM_CHUNK, GM_CHUNK), 0)
    col = lax.broadcasted_iota(jnp.int32, (GM_CHUNK, GM_CHUNK), 1)
    causal = col <= row
    for g in range(GM_GROUPS):
        cols = slice(g * GM_CHUNK, (g + 1) * GM_CHUNK)
        wg = jnp.where(causal, ws_ref[g], jnp.zeros((), BF16))
        for ch in range(GM_TM // GM_CHUNK):
            rows = slice(ch * GM_CHUNK, (ch + 1) * GM_CHUNK)
            mixed = jnp.dot(wg, vb_scr[rows, cols], preferred_element_type=F32) + bias_ref[:, cols]
            a_scr[rows, cols] = (u_ref[rows, cols].astype(F32) * mixed).astype(BF16)

    ma = jnp.dot(a_scr[...], wpa_ref[...], preferred_element_type=F32)
    o_ref[...] = (g0_ref[...].astype(F32) * ma).astype(BF16)


def _gmlp(pg, ln_g, ln_b, ws, bias_full, wpa):
    T = pg.shape[0]
    return pl.pallas_call(
        _gmlp_kernel,
        grid=(T // GM_TM,),
        in_specs=[
            pl.BlockSpec((GM_TM, GM_WIDTH), lambda i: (i, 0)),
            pl.BlockSpec((GM_TM, GM_WIDTH), lambda i: (i, 1)),
            pl.BlockSpec((GM_TM, D_MODEL), lambda i: (i, PROJ_COLS // D_MODEL)),
            _const_spec((1, GM_WIDTH)),
            _const_spec((1, GM_WIDTH)),
            _const_spec((GM_GROUPS, GM_CHUNK, GM_CHUNK)),
            _const_spec((GM_CHUNK, GM_WIDTH)),
            _const_spec((GM_WIDTH, D_MODEL)),
        ],
        out_specs=pl.BlockSpec((GM_TM, D_MODEL), lambda i: (i, 0)),
        out_shape=jax.ShapeDtypeStruct((T, D_MODEL), BF16),
        scratch_shapes=[pltpu.VMEM((GM_TM, GM_WIDTH), BF16), pltpu.VMEM((GM_TM, GM_WIDTH), BF16)],
        compiler_params=_params(("parallel",), 40),
        name="gmlp",
    )(pg, pg, pg, ln_g, ln_b, ws, bias_full, wpa)


QKV_TM = 1024
HEAD_BLK = 2
FA_T = 512
FA_QS = 256


def _rope(t, cos, sin):
    return t * cos + pltpu.roll(t, QK_ROPE, 1) * sin


def _row_sum_all_lanes(sq, weights):
    hi = sq.astype(BF16)
    lo = (sq - hi.astype(F32)).astype(BF16)
    w = weights.astype(BF16)
    return (jnp.dot(hi, w, preferred_element_type=F32) + jnp.dot(lo, w, preferred_element_type=F32))


def _mla_qkv_kernel(cq_ref, ckv_ref, kr_ref, pos_ref, gcq_ref, gckv_ref, wq_ref, wkv_ref, gq_ref, gk_ref,
                    cst_ref, sumw_ref, q_ref, k_ref, vt_ref, cqn_scr, ckvn_scr, cos_scr, sin_scr, krsq_scr):
    hb = pl.program_id(1)
    w_nope = sumw_ref[:QK_NOPE, :]
    w_rope = sumw_ref[QK_NOPE:, :]

    @pl.when(hb == 0)
    def _():
        cq = cq_ref[...].astype(F32)
        ms = jnp.mean(cq * cq, axis=-1, keepdims=True)
        cqn_scr[...] = (cq * lax.rsqrt(ms + EPS) * gcq_ref[...]).astype(BF16)
        ckv = ckv_ref[...].astype(F32)
        ms = jnp.mean(ckv * ckv, axis=-1, keepdims=True)
        ckvn_scr[...] = (ckv * lax.rsqrt(ms + EPS) * gckv_ref[...]).astype(BF16)
        ang = pos_ref[...].astype(F32) * cst_ref[0:1, :]
        cos_scr[...] = jnp.cos(ang) * cst_ref[1:2, :]
        sin_scr[...] = jnp.sin(ang) * cst_ref[2:3, :]
        kr = kr_ref[...]
        krsq_scr[...] = _row_sum_all_lanes(kr * kr, w_rope)

    cos = cos_scr[...]
    sin = sin_scr[...]
    inv_dim = 1.0 / QK_DIM
    kr = kr_ref[...]

    yq2 = jnp.dot(cqn_scr[...], wq_ref[...], preferred_element_type=F32)
    ykv2 = jnp.dot(ckvn_scr[...], wkv_ref[...], preferred_element_type=F32)
    for hh in range(HEAD_BLK):
        yq = yq2[:, hh * QK_PAD:(hh + 1) * QK_PAD]
        qn = yq[:, :QK_NOPE]
        qt = yq[:, QK_NOPE:]
        ssq = _row_sum_all_lanes(qn * qn, w_nope) + _row_sum_all_lanes(qt * qt, w_rope)
        rs = lax.rsqrt(ssq * inv_dim + EPS) * MLA_SCALE
        q_ref[hh, :, :QK_NOPE] = (qn * rs * gq_ref[:, :QK_NOPE]).astype(BF16)
        q_ref[hh, :, QK_NOPE:] = _rope(qt * rs * gq_ref[:, QK_NOPE:], cos, sin).astype(BF16)

        ykv = ykv2[:, hh * (QK_NOPE + V_DIM):(hh + 1) * (QK_NOPE + V_DIM)]
        kn = ykv[:, :QK_NOPE]
        ssq = _row_sum_all_lanes(kn * kn, w_nope) + krsq_scr[...]
        rs = lax.rsqrt(ssq * inv_dim + EPS)
        k_ref[hh, :, :QK_NOPE] = (kn * rs * gk_ref[:, :QK_NOPE]).astype(BF16)
        k_ref[hh, :, QK_NOPE:] = _rope(kr * rs * gk_ref[:, QK_NOPE:], cos, sin).astype(BF16)
        vv = ykv[:, QK_NOPE:]
        for c in range(QKV_TM // FA_T):
            vt_ref[hh, c] = vv[c * FA_T:(c + 1) * FA_T, :].T.astype(BF16)


def _mla_qkv(pg, kr, pos, gcq, gckv, wq, wkv, gq, gk, cst, B, S):
    T = pg.shape[0]
    sumw = jnp.concatenate([jnp.ones((QK_NOPE, LANES), F32), jnp.full((QK_PAD - QK_NOPE, LANES), 0.5, F32)])
    spb = S // QKV_TM
    cpt = QKV_TM // FA_T
    head_spec = lambda w: pl.BlockSpec((None, HEAD_BLK, QKV_TM, w), lambda i, h: (i // spb, h, i % spb, 0))
    return pl.pallas_call(
        _mla_qkv_kernel,
        grid=(T // QKV_TM, MLA_HEADS // HEAD_BLK),
        in_specs=[
            pl.BlockSpec((QKV_TM, Q_LORA), lambda i, h: (i, 2 * GM_WIDTH // Q_LORA)),
            pl.BlockSpec((QKV_TM, KV_LORA), lambda i, h: (i, 2 * GM_WIDTH // KV_LORA + 1)),
            pl.BlockSpec((QKV_TM, LANES), lambda i, h: (i, 0)),
            pl.BlockSpec((QKV_TM, 1), lambda i, h: (i, 0)),
            pl.BlockSpec((1, Q_LORA), lambda i, h: (0, 0)),
            pl.BlockSpec((1, KV_LORA), lambda i, h: (0, 0)),
            pl.BlockSpec((Q_LORA, HEAD_BLK * QK_PAD), lambda i, h: (0, h)),
            pl.BlockSpec((KV_LORA, HEAD_BLK * (QK_NOPE + V_DIM)), lambda i, h: (0, h)),
            pl.BlockSpec((1, QK_PAD), lambda i, h: (0, 0)),
            pl.BlockSpec((1, QK_PAD), lambda i, h: (0, 0)),
            pl.BlockSpec((8, LANES), lambda i, h: (0, 0)),
            pl.BlockSpec((QK_PAD, LANES), lambda i, h: (0, 0)),
        ],
        out_specs=[
            head_spec(QK_PAD),
            head_spec(QK_PAD),
            pl.BlockSpec((None, HEAD_BLK, cpt, V_DIM, FA_T), lambda i, h: (i // spb, h, i % spb, 0, 0)),
        ],
        out_shape=[
            jax.ShapeDtypeStruct((B, MLA_HEADS, S, QK_PAD), BF16),
            jax.ShapeDtypeStruct((B, MLA_HEADS, S, QK_PAD), BF16),
            jax.ShapeDtypeStruct((B, MLA_HEADS, S // FA_T, V_DIM, FA_T), BF16),
        ],
        scratch_shapes=[
            pltpu.VMEM((QKV_TM, Q_LORA), BF16),
            pltpu.VMEM((QKV_TM, KV_LORA), BF16),
            pltpu.VMEM((QKV_TM, LANES), F32),
            pltpu.VMEM((QKV_TM, LANES), F32),
            pltpu.VMEM((QKV_TM, LANES), F32),
        ],
        compiler_params=_params(("parallel", "arbitrary"), 48),
        name="mla_qkv",
    )(pg, pg, kr, pos, gcq, gckv, wq, wkv, gq, gk, cst, sumw)


def _flash_kernel(q_ref, k_ref, vt_ref, o_ref, m_scr, l_scr, acc_scr):
    qi = pl.program_id(2)
    m_scr[...] = jnp.full_like(m_scr, NEG_INF)
    l_scr[...] = jnp.zeros_like(l_scr)
    acc_scr[...] = jnp.zeros_like(acc_scr)

    def step(kb, diagonal):
        ks = pl.ds(pl.multiple_of(kb * FA_T, FA_T), FA_T)
        chains = [(hh, qs) for hh in range(HEAD_BLK) for qs in range(FA_T // FA_QS)]
        scores = []
        for hh, qs in chains:
            qc = slice(qs * FA_QS, (qs + 1) * FA_QS)
            st = lax.dot_general(k_ref[hh, ks, :], q_ref[hh, qc, :], (((1,), (1,)), ((), ())),
                                 preferred_element_type=F32)
            if diagonal:
                krow = lax.broadcasted_iota(jnp.int32, (FA_T, FA_QS), 0)
                qcol = lax.broadcasted_iota(jnp.int32, (FA_T, FA_QS), 1) + qs * FA_QS
                st = jnp.where(krow <= qcol, st, NEG_INF)
            scores.append(st)
        probs = []
        for (hh, qs), st in zip(chains, scores):
            qc = slice(qs * FA_QS, (qs + 1) * FA_QS)
            m_prev = m_scr[hh, :, qc]
            m_new = jnp.maximum(m_prev, jnp.max(st, axis=0, keepdims=True))
            alpha = jnp.exp(m_prev - m_new)
            p = jnp.exp(st - m_new)
            l_scr[hh, :, qc] = alpha * l_scr[hh, :, qc] + jnp.sum(p, axis=0, keepdims=True)
            m_scr[hh, :, qc] = m_new
            probs.append((alpha, p.astype(BF16)))
        for (hh, qs), (alpha, p) in zip(chains, probs):
            qc = slice(qs * FA_QS, (qs + 1) * FA_QS)
            acc_scr[hh, :, qc] = alpha * acc_scr[hh, :, qc] + jnp.dot(vt_ref[hh, kb], p,
                                                                      preferred_element_type=F32)

    def body(kb, carry):
        step(kb, False)
        return carry

    lax.fori_loop(0, qi, body, 0)
    step(qi, True)
    for hh in range(HEAD_BLK):
        o_ref[:, hh * V_DIM:(hh + 1) * V_DIM] = (acc_scr[hh] / l_scr[hh]).T.astype(BF16)


def _flash(q, k, vt):
    B, H, S, _ = q.shape
    nq = S // FA_T
    return pl.pallas_call(
        _flash_kernel,
        grid=(B, H // HEAD_BLK, nq),
        in_specs=[
            pl.BlockSpec((None, HEAD_BLK, FA_T, QK_PAD), lambda b, h, i: (b, h, i, 0)),
            pl.BlockSpec((None, HEAD_BLK, S, QK_PAD), lambda b, h, i: (b, h, 0, 0)),
            pl.BlockSpec((None, HEAD_BLK, nq, V_DIM, FA_T), lambda b, h, i: (b, h, 0, 0, 0)),
        ],
        out_specs=pl.BlockSpec((FA_T, HEAD_BLK * V_DIM), lambda b, h, i: (b * nq + i, h)),
        out_shape=jax.ShapeDtypeStruct((B * S, H * V_DIM), BF16),
        scratch_shapes=[
            pltpu.VMEM((HEAD_BLK, 1, FA_T), F32),
            pltpu.VMEM((HEAD_BLK, 1, FA_T), F32),
            pltpu.VMEM((HEAD_BLK, V_DIM, FA_T), F32),
        ],
        compiler_params=_params(("parallel", "parallel", "arbitrary"), 40),
        name="mla_attention",
    )(q, k, vt)


def _mem_kv_kernel(mem_ref, g_ref, wk_ref, wv_ref, kng_ref, k_ref, v_ref):
    m = mem_ref[...]
    ms = jnp.mean(m * m, axis=-1, keepdims=True)
    mb = (m * lax.rsqrt(ms + EPS) * g_ref[...]).astype(BF16)
    kk = jnp.dot(mb, wk_ref[...], preferred_element_type=F32)
    for h in range(MEM_HEADS):
        cols = slice(h * MEM_HEAD_DIM, (h + 1) * MEM_HEAD_DIM)
        kh = kk[:, cols]
        ms = jnp.mean(kh * kh, axis=-1, keepdims=True)
        k_ref[:, cols] = (kh * lax.rsqrt(ms + EPS) * kng_ref[...]).astype(BF16)
    v_ref[...] = jnp.dot(mb, wv_ref[...], preferred_element_type=F32).astype(BF16)


def _mem_kv(mem, g, wk, wv, kng):
    B = mem.shape[0]
    return pl.pallas_call(
        _mem_kv_kernel,
        grid=(B,),
        in_specs=[
            pl.BlockSpec((None, MEM_LEN, D_MODEL), lambda b: (b, 0, 0)),
            _const_spec((1, D_MODEL)),
            _const_spec((D_MODEL, MEM_WIDTH)),
            _const_spec((D_MODEL, MEM_WIDTH)),
            _const_spec((1, MEM_HEAD_DIM)),
        ],
        out_specs=[
            pl.BlockSpec((None, MEM_LEN, MEM_WIDTH), lambda b: (b, 0, 0)),
            pl.BlockSpec((None, MEM_LEN, MEM_WIDTH), lambda b: (b, 0, 0)),
        ],
        out_shape=[
            jax.ShapeDtypeStruct((B, MEM_LEN, MEM_WIDTH), BF16),
            jax.ShapeDtypeStruct((B, MEM_LEN, MEM_WIDTH), BF16),
        ],
        compiler_params=_params(("parallel",), 32),
        name="mem_kv",
    )(mem, g, wk, wv, kng)


MA_TM = 512


def _mem_attn_kernel(qm_ref, km_ref, vm_ref, g2_ref, m0_ref, qng_ref, wpc_ref, o_ref, c_scr):
    for h in range(MEM_HEADS):
        cols = slice(h * MEM_HEAD_DIM, (h + 1) * MEM_HEAD_DIM)
        qh = qm_ref[:, cols].astype(F32)
        ms = jnp.mean(qh * qh, axis=-1, keepdims=True)
        qn = (qh * (lax.rsqrt(ms + EPS) * MEM_SCALE) * qng_ref[...]).astype(BF16)
        s = lax.dot_general(qn, km_ref[:, cols], (((1,), (1,)), ((), ())), preferred_element_type=F32)
        e = jnp.exp(s - jnp.max(s, axis=-1, keepdims=True))
        p = (e / jnp.sum(e, axis=-1, keepdims=True)).astype(BF16)
        c_scr[:, cols] = jnp.dot(p, vm_ref[:, cols], preferred_element_type=F32).astype(BF16)
    mc = jnp.dot(c_scr[...], wpc_ref[...], preferred_element_type=F32)
    o_ref[...] = (m0_ref[...].astype(F32) + g2_ref[...].astype(F32) * mc).astype(BF16)


def _mem_attn(pg, km, vm, m0, qng, wpc, S):
    T = pg.shape[0]
    spb = S // MA_TM
    return pl.pallas_call(
        _mem_attn_kernel,
        grid=(T // MA_TM,),
        in_specs=[
            pl.BlockSpec((MA_TM, MEM_WIDTH), lambda i: (i, (2 * GM_WIDTH + Q_LORA + KV_LORA) // MEM_WIDTH)),
            pl.BlockSpec((None, MEM_LEN, MEM_WIDTH), lambda i: (i // spb, 0, 0)),
            pl.BlockSpec((None, MEM_LEN, MEM_WIDTH), lambda i: (i // spb, 0, 0)),
            pl.BlockSpec((MA_TM, D_MODEL), lambda i: (i, PROJ_COLS // D_MODEL + 2)),
            pl.BlockSpec((MA_TM, D_MODEL), lambda i: (i, 0)),
            _const_spec((1, MEM_HEAD_DIM)),
            _const_spec((MEM_WIDTH, D_MODEL)),
        ],
        out_specs=pl.BlockSpec((MA_TM, D_MODEL), lambda i: (i, 0)),
        out_shape=jax.ShapeDtypeStruct((T, D_MODEL), BF16),
        scratch_shapes=[pltpu.VMEM((MA_TM, MEM_WIDTH), BF16)],
        compiler_params=_params(("parallel",), 40),
        name="mem_attention",
    )(pg, km, vm, pg, m0, qng, wpc)


MG_TM = 512


def _pack_bf16_pair(a, b):
    hi = pltpu.bitcast(a.astype(BF16).astype(F32), jnp.uint32)
    lo = pltpu.bitcast(b.astype(BF16).astype(F32), jnp.uint32)
    return hi | (lo >> 16)


def _unpack_bf16_pair(p):
    hi = pltpu.bitcast(p & jnp.uint32(0xFFFF0000), F32)
    lo = pltpu.bitcast(p << 16, F32)
    return hi, lo


def _attn_proj_kernel(b_ref, g1_ref, m1_ref, wpb_ref, o_ref):
    mb = jnp.dot(b_ref[...], wpb_ref[...], preferred_element_type=F32)
    o_ref[...] = (m1_ref[...].astype(F32) + g1_ref[...].astype(F32) * mb).astype(BF16)


def _attn_proj(b_attn, pg, m1, wpb):
    T = m1.shape[0]
    row = pl.BlockSpec((MG_TM, D_MODEL), lambda i: (i, 0))
    return pl.pallas_call(
        _attn_proj_kernel,
        grid=(T // MG_TM,),
        in_specs=[
            row,
            pl.BlockSpec((MG_TM, D_MODEL), lambda i: (i, PROJ_COLS // D_MODEL + 1)),
            row,
            _const_spec((MLA_HEADS * V_DIM, D_MODEL)),
        ],
        out_specs=row,
        out_shape=jax.ShapeDtypeStruct((T, D_MODEL), BF16),
        compiler_params=_params(("parallel",), 40),
        name="attn_proj",
    )(b_attn, pg, m1, wpb)


def _merge_kernel(mg_ref, x_ref, wo_ref, ln2_ref, rw_ref, rb_ref,
                  x1_ref, h2p_ref, code_ref, gate_ref, cnt_ref, cnt_scr):
    @pl.when(pl.program_id(0) == 0)
    def _():
        cnt_scr[...] = jnp.zeros_like(cnt_scr)

    x1 = x_ref[...] + jnp.dot(mg_ref[...], wo_ref[...], preferred_element_type=F32)
    x1_ref[...] = x1
    ms = jnp.mean(x1 * x1, axis=-1, keepdims=True)
    h2 = x1 * lax.rsqrt(ms + EPS) * ln2_ref[...]
    h2p_ref[...] = _pack_bf16_pair(h2[:, :D_MODEL // 2], h2[:, D_MODEL // 2:])

    logits = jnp.dot(h2, rw_ref[...], preferred_element_type=F32, precision=lax.Precision.HIGHEST)
    lane = lax.broadcasted_iota(jnp.int32, (MG_TM, LANES), 1)
    work = jnp.where(lane < N_EXPERTS, logits + rb_ref[...], NEG_INF)
    earlier = (lax.broadcasted_iota(jnp.int32, (MG_TM, MG_TM), 1)
               < lax.broadcasted_iota(jnp.int32, (MG_TM, MG_TM), 0)).astype(BF16)
    base = cnt_scr[...]
    code_out = jnp.zeros((MG_TM, LANES), jnp.int32)
    val_out = jnp.zeros((MG_TM, LANES), F32)
    top = None
    denom = jnp.zeros((MG_TM, 1), F32)
    for k in range(TOP_K):
        mk = jnp.max(work, axis=-1, keepdims=True)
        ik = jnp.min(jnp.where(work == mk, lane, LANES), axis=-1, keepdims=True)
        hit = lane == ik
        work = jnp.where(hit, NEG_INF, work)
        if top is None:
            top = mk
        ek = jnp.exp(mk - top)
        denom = denom + ek
        val_out = jnp.where(lane == k, ek, val_out)
        onehot = hit.astype(BF16)
        prefix = jnp.dot(earlier, onehot, preferred_element_type=F32) + base
        rank = jnp.sum(jnp.where(hit, prefix, 0.0), axis=-1, keepdims=True).astype(jnp.int32)
        code_out = jnp.where(lane == k, rank * N_EXPERTS + ik, code_out)
        base = base + jnp.sum(onehot.astype(F32), axis=0, keepdims=True)
    code_ref[...] = code_out
    gate_ref[...] = val_out / denom
    cnt_scr[...] = base
    cnt_ref[...] = base


def _merge(merged, x2, wo, ln2, rw, rb):
    T = x2.shape[0]
    row = lambda w: pl.BlockSpec((MG_TM, w), lambda i: (i, 0))
    return pl.pallas_call(
        _merge_kernel,
        grid=(T // MG_TM,),
        in_specs=[
            row(D_MODEL),
            row(D_MODEL),
            _const_spec((D_MODEL, D_MODEL)),
            _const_spec((1, D_MODEL)),
            _const_spec((D_MODEL, LANES)),
            _const_spec((1, LANES)),
        ],
        out_specs=[row(D_MODEL), row(D_MODEL // 2), row(LANES), row(LANES),
                   pl.BlockSpec((1, LANES), lambda i: (0, 0))],
        out_shape=[
            jax.ShapeDtypeStruct((T, D_MODEL), F32),
            jax.ShapeDtypeStruct((T, D_MODEL // 2), jnp.uint32),
            jax.ShapeDtypeStruct((T, LANES), jnp.int32),
            jax.ShapeDtypeStruct((T, LANES), F32),
            jax.ShapeDtypeStruct((1, LANES), F32),
        ],
        scratch_shapes=[pltpu.VMEM((1, LANES), F32)],
        compiler_params=_params(("arbitrary",), 56),
        name="merge_router",
    )(merged, x2, wo, ln2, rw, rb)


DP_TM = 256
DMA_UNROLL = 8


def _row_copy(src, src_row, dst, dst_row, sem):
    return pltpu.make_async_copy(src.at[pl.ds(src_row, 1), :], dst.at[pl.ds(dst_row, 1), :], sem)


def _dest_row(ps_ref, code):
    return ps_ref[code & (N_EXPERTS - 1)] + (code >> EXPERT_BITS)


def _dispatch_kernel(ps_ref, code_ref, h2p_ref, xs_in, xs_hbm, sem):
    del xs_in

    def issue(j, carry):
        _row_copy(h2p_ref, j >> 2, xs_hbm, _dest_row(ps_ref, code_ref[0, j]), sem).start()
        return carry

    lax.fori_loop(0, DP_TM * TOP_K, issue, 0, unroll=DMA_UNROLL)

    def drain(j, carry):
        _row_copy(h2p_ref, 0, xs_hbm, 0, sem).wait()
        return carry

    lax.fori_loop(0, DP_TM * TOP_K, drain, 0, unroll=DMA_UNROLL)


def _dispatch(pad_starts, codes3, h2p, xs_zero):
    T = h2p.shape[0]
    grid_spec = pltpu.PrefetchScalarGridSpec(
        num_scalar_prefetch=1,
        grid=(T // DP_TM,),
        in_specs=[
            pl.BlockSpec((None, 1, DP_TM * TOP_K), lambda i, ps: (i, 0, 0), memory_space=pltpu.SMEM),
            pl.BlockSpec((DP_TM, D_MODEL // 2), lambda i, ps: (i, 0)),
            pl.BlockSpec(memory_space=pl.ANY),
        ],
        out_specs=pl.BlockSpec(memory_space=pl.ANY),
        scratch_shapes=[pltpu.SemaphoreType.DMA(())],
    )
    return pl.pallas_call(
        _dispatch_kernel,
        grid_spec=grid_spec,
        out_shape=jax.ShapeDtypeStruct(xs_zero.shape, xs_zero.dtype),
        input_output_aliases={3: 0},
        compiler_params=_params(("arbitrary",), 32),
        name="moe_dispatch",
    )(pad_starts, codes3, h2p, xs_zero)


GROUP_SUB = 4
GROUP_ROWS = GROUP_SUB * ROW_BLOCK
FF_TILE = 256
N_FF_TILES = D_FF // FF_TILE
DOWN_CHUNK = 512


def _expert_kernel(ge_ref, gs_ref, gn_ref, xs_in, wg_ref, wu_ref, bg_ref, bu_ref, wd_ref, bd_ref,
                   rows_hbm, xwin, x_scr, acc_scr, stage, wsem, sem):
    del ge_ref, xs_in
    g = pl.program_id(0)
    f = pl.program_id(1)
    ns = gn_ref[g]
    half = D_MODEL // 2
    slot = g & 1

    def win_copy(group, s):
        rows = pl.ds(gs_ref[group] * ROW_BLOCK, GROUP_ROWS)
        return pltpu.make_async_copy(rows_hbm.at[rows, :], xwin.at[s], wsem.at[s])

    def out_copy(s, start_block):
        rows = pl.ds((start_block + s) * ROW_BLOCK, ROW_BLOCK)
        return pltpu.make_async_copy(stage.at[s], rows_hbm.at[rows, :], sem.at[s])

    @pl.when((f == 0) & (ns > 0))
    def _():
        @pl.when(g == 0)
        def _():
            win_copy(0, 0).start()

        win_copy(g, slot).wait()
        @pl.when(g + 1 < pl.num_programs(0))
        def _():
            @pl.when(gn_ref[g + 1] > 0)
            def _():
                win_copy(g + 1, 1 - slot).start()

        hi, lo = _unpack_bf16_pair(xwin[slot])
        x_scr[:, :half] = hi.astype(BF16)
        x_scr[:, half:] = lo.astype(BF16)
        acc_scr[...] = jnp.broadcast_to(bd_ref[...], acc_scr.shape)

    for n in range(1, GROUP_SUB + 1):
        @pl.when(ns == n)
        def _(n=n):
            m = n * ROW_BLOCK
            x = x_scr[:m, :]
            gate = jnp.dot(x, wg_ref[...].astype(BF16), preferred_element_type=F32) + bg_ref[...]
            up = jnp.dot(x, wu_ref[...].astype(BF16), preferred_element_type=F32) + bu_ref[...]
            gate = jnp.minimum(gate, SWIGLU_LIMIT)
            up = jnp.clip(up, -SWIGLU_LIMIT, SWIGLU_LIMIT)
            glu = gate * jax.nn.sigmoid(gate * SWIGLU_ALPHA)
            act = ((up + 1.0) * glu).astype(BF16)
            for c in range(D_MODEL // DOWN_CHUNK):
                cols = slice(c * DOWN_CHUNK, (c + 1) * DOWN_CHUNK)
                acc_scr[:m, cols] += jnp.dot(act, wd_ref[:, cols].astype(BF16), preferred_element_type=F32)

    @pl.when(f == N_FF_TILES - 1)
    def _():
        @pl.when(g > 0)
        def _():
            prev = gn_ref[g - 1]
            for s in range(GROUP_SUB):
                @pl.when(s < prev)
                def _(s=s):
                    out_copy(s, 0).wait()

        for s in range(GROUP_SUB):
            @pl.when(s < ns)
            def _(s=s):
                y = acc_scr[s * ROW_BLOCK:(s + 1) * ROW_BLOCK, :]
                stage[s] = _pack_bf16_pair(y[:, :half], y[:, half:])
                out_copy(s, gs_ref[g]).start()

        @pl.when(g == pl.num_programs(0) - 1)
        def _():
            for s in range(GROUP_SUB):
                @pl.when(s < ns)
                def _(s=s):
                    out_copy(s, 0).wait()


def _experts(g_exp, g_start, g_nsub, xs, wgu, bgu, wd, bd):
    def ff(f, gn, g):
        return jnp.where(gn[g] > 0, f, N_FF_TILES - 1)

    grid_spec = pltpu.PrefetchScalarGridSpec(
        num_scalar_prefetch=3,
        grid=(g_exp.shape[0], N_FF_TILES),
        in_specs=[
            pl.BlockSpec(memory_space=pl.ANY),
            pl.BlockSpec((None, D_MODEL, FF_TILE), lambda g, f, ge, gs, gn: (ge[g], 0, ff(f, gn, g))),
            pl.BlockSpec((None, D_MODEL, FF_TILE),
                         lambda g, f, ge, gs, gn: (ge[g], 0, N_FF_TILES + ff(f, gn, g))),
            pl.BlockSpec((None, 1, FF_TILE), lambda g, f, ge, gs, gn: (ge[g], 0, ff(f, gn, g))),
            pl.BlockSpec((None, 1, FF_TILE), lambda g, f, ge, gs, gn: (ge[g], 0, N_FF_TILES + ff(f, gn, g))),
            pl.BlockSpec((None, FF_TILE, D_MODEL), lambda g, f, ge, gs, gn: (ge[g], ff(f, gn, g), 0)),
            pl.BlockSpec((None, 1, D_MODEL), lambda g, f, ge, gs, gn: (ge[g], 0, 0)),
        ],
        out_specs=pl.BlockSpec(memory_space=pl.ANY),
        scratch_shapes=[
            pltpu.VMEM((2, GROUP_ROWS, D_MODEL // 2), jnp.uint32),
            pltpu.VMEM((GROUP_ROWS, D_MODEL), BF16),
            pltpu.VMEM((GROUP_ROWS, D_MODEL), F32),
            pltpu.VMEM((GROUP_SUB, ROW_BLOCK, D_MODEL // 2), jnp.uint32),
            pltpu.SemaphoreType.DMA((2,)),
            pltpu.SemaphoreType.DMA((GROUP_SUB,)),
        ],
    )
    return pl.pallas_call(
        _expert_kernel,
        grid_spec=grid_spec,
        out_shape=jax.ShapeDtypeStruct(xs.shape, jnp.uint32),
        input_output_aliases={3: 0},
        compiler_params=_params(("arbitrary", "arbitrary"), 58),
        name="moe_experts",
    )(g_exp, g_start, g_nsub, xs, wgu, wgu, bgu, bgu, wd, bd)


CB_TM = 128


def _combine_kernel(ps_ref, code_ref, code_next_ref, ys_hbm, x1_ref, gate_ref, o_ref, buf, sem):
    i = pl.program_id(0)
    slot = i & 1

    def gather_tile(codes, s):
        def issue(j, carry):
            row = _dest_row(ps_ref, codes[0, j])
            _row_copy(ys_hbm, row, buf.at[s, j & (TOP_K - 1)], j >> 2, sem.at[s]).start()
            return carry

        lax.fori_loop(0, CB_TM * TOP_K, issue, 0, unroll=DMA_UNROLL)

    @pl.when(i == 0)
    def _():
        gather_tile(code_ref, 0)

    @pl.when(i + 1 < pl.num_programs(0))
    def _():
        gather_tile(code_next_ref, 1 - slot)

    def drain(j, carry):
        _row_copy(ys_hbm, 0, buf.at[slot, 0], 0, sem.at[slot]).wait()
        return carry

    lax.fori_loop(0, CB_TM * TOP_K, drain, 0, unroll=DMA_UNROLL)

    half = D_MODEL // 2
    gates = gate_ref[...]
    out_hi = x1_ref[:, :half]
    out_lo = x1_ref[:, half:]
    for k in range(TOP_K):
        hi, lo = _unpack_bf16_pair(buf[slot, k])
        g = gates[:, k:k + 1]
        out_hi = out_hi + g * hi
        out_lo = out_lo + g * lo
    o_ref[:, :half] = out_hi
    o_ref[:, half:] = out_lo


def _combine(pad_starts, codes3, ys, x1, gates):
    T = x1.shape[0]
    n = T // CB_TM
    code_spec = lambda f: pl.BlockSpec((None, 1, CB_TM * TOP_K), f, memory_space=pltpu.SMEM)
    grid_spec = pltpu.PrefetchScalarGridSpec(
        num_scalar_prefetch=1,
        grid=(n,),
        in_specs=[
            code_spec(lambda i, ps: (i, 0, 0)),
            code_spec(lambda i, ps: (jnp.minimum(i + 1, n - 1), 0, 0)),
            pl.BlockSpec(memory_space=pl.ANY),
            pl.BlockSpec((CB_TM, D_MODEL), lambda i, ps: (i, 0)),
            pl.BlockSpec((CB_TM, LANES), lambda i, ps: (i, 0)),
        ],
        out_specs=pl.BlockSpec((CB_TM, D_MODEL), lambda i, ps: (i, 0)),
        scratch_shapes=[pltpu.VMEM((2, TOP_K, CB_TM, D_MODEL // 2), jnp.uint32),
                        pltpu.SemaphoreType.DMA((2,))],
    )
    return pl.pallas_call(
        _combine_kernel,
        grid_spec=grid_spec,
        out_shape=jax.ShapeDtypeStruct((T, D_MODEL), F32),
        compiler_params=_params(("arbitrary",), 32),
        name="moe_combine",
    )(pad_starts, codes3, codes3, ys, x1, gates)


def _group_tables(counts_f32, max_groups):
    i32 = jnp.int32
    counts = counts_f32[0, :N_EXPERTS].astype(i32)
    nb = (counts + ROW_BLOCK - 1) // ROW_BLOCK
    ng = (nb + GROUP_SUB - 1) // GROUP_SUB
    upto = jnp.arange(N_EXPERTS)[None, :] <= jnp.arange(N_EXPERTS)[:, None]
    pad_ends = jnp.sum(jnp.where(upto, nb[None, :], 0), axis=1) * ROW_BLOCK
    pad_starts = pad_ends - nb * ROW_BLOCK
    g_ends = jnp.sum(jnp.where(upto, ng[None, :], 0), axis=1)
    g_starts = g_ends - ng
    n_groups = g_ends[-1]

    g = jnp.arange(max_groups, dtype=i32)
    gg = jnp.minimum(g, n_groups - 1)
    e = jnp.minimum(jnp.sum((g_ends[None, :] <= gg[:, None]).astype(i32), axis=1), N_EXPERTS - 1)
    pick = e[:, None] == jnp.arange(N_EXPERTS)[None, :]
    take = lambda table: jnp.sum(jnp.where(pick, table[None, :], 0), axis=1)
    nb_g, ng_g = take(nb), jnp.maximum(take(ng), 1)
    j = gg - take(g_starts)
    base, rem = nb_g // ng_g, nb_g % ng_g
    nsub = jnp.where(g < n_groups, base + (j < rem).astype(i32), 0)
    start_block = take(pad_starts) // ROW_BLOCK + j * base + jnp.minimum(j, rem)
    return pad_starts.astype(i32), e.astype(i32), start_block.astype(i32), nsub.astype(i32)


def _rope_constants():
    lane = np.arange(LANES)
    half = QK_ROPE // 2
    inv = 1.0 / (ROPE_THETA ** (np.arange(0, QK_ROPE, 2, dtype=np.float32) / QK_ROPE))
    cst = np.zeros((8, LANES), np.float32)
    cst[0, :QK_ROPE] = inv.astype(np.float32)[lane[:QK_ROPE] % half]
    cst[1, :QK_ROPE] = 1.0
    cst[2, :half] = -1.0
    cst[2, half:QK_ROPE] = 1.0
    return jnp.asarray(cst)


def _swap_halves(a):
    half = QK_ROPE // 2
    return jnp.concatenate([a[..., half:], a[..., :half]], axis=-1)


def kernel(x, mem, positions, ln1_g, w_in, w_gate, b_gate, gmlp_ln_g, gmlp_ln_b, gmlp_ws, gmlp_bs, w_pa,
           mla_cq_g, mla_w_uq, mla_ckv_g, mla_w_ukv, mla_qn_g, mla_kn_g, w_pb, mem_ln_g, mem_w_k, mem_w_v,
           mem_qn_g, mem_kn_g, w_pc, w_o, ln2_g, router_w, router_b, moe_w_gu, moe_b_gu, moe_w_down,
           moe_b_down):
    B, S, D = x.shape
    T = B * S
    x2 = x.reshape(T, D)
    for l in range(ln1_g.shape[0]):
        o_kr = 2 * GM_WIDTH + Q_LORA + KV_LORA
        o_qm = o_kr + QK_ROPE
        wi = w_in[l]
        w1 = jnp.concatenate([wi[:, :o_kr], wi[:, o_qm:], w_gate[l]], axis=1).astype(BF16)
        b1 = jnp.concatenate([jnp.zeros((PROJ_COLS,), F32), b_gate[l]])[None, :]
        w_kr = wi[:, o_kr:o_qm]
        wr = jnp.concatenate([w_kr, _swap_halves(w_kr)], axis=1).astype(BF16)

        wq3 = mla_w_uq[l].reshape(Q_LORA, MLA_HEADS, QK_DIM)
        wq = jnp.concatenate([wq3, _swap_halves(wq3[..., QK_NOPE:])], axis=-1)
        wq = wq.reshape(Q_LORA, MLA_HEADS * QK_PAD).astype(BF16)
        wkv = mla_w_ukv[l].astype(BF16)
        gq = jnp.concatenate([mla_qn_g[l], _swap_halves(mla_qn_g[l][QK_NOPE:])])[None, :]
        gk = jnp.concatenate([mla_kn_g[l], _swap_halves(mla_kn_g[l][QK_NOPE:])])[None, :]

        bias_full = jnp.broadcast_to(gmlp_bs[l].T[:, :, None], (GM_CHUNK, GM_GROUPS, GM_CHUNK))
        bias_full = bias_full.reshape(GM_CHUNK, GM_WIDTH)

        rw = jnp.pad(router_w[l], ((0, 0), (0, LANES - N_EXPERTS)))
        rb = jnp.pad(router_b[l], (0, LANES - N_EXPERTS))[None, :]

        pg, kr = _norm_proj(x2, ln1_g[l][None, :], w1, b1, wr)
        m0 = _gmlp(pg, gmlp_ln_g[l][None, :], gmlp_ln_b[l][None, :], gmlp_ws[l].astype(BF16), bias_full,
                   w_pa[l].astype(BF16))
        q, k, vt = _mla_qkv(pg, kr, positions.reshape(T, 1), mla_cq_g[l][None, :], mla_ckv_g[l][None, :],
                           wq, wkv, gq, gk, _rope_constants(), B, S)
        b_attn = _flash(q, k, vt)
        km, vm = _mem_kv(mem, mem_ln_g[l][None, :], mem_w_k[l].astype(BF16), mem_w_v[l].astype(BF16),
                         mem_kn_g[l][None, :])
        m1 = _mem_attn(pg, km, vm, m0, mem_qn_g[l][None, :], w_pc[l].astype(BF16), S)
        merged = _attn_proj(b_attn, pg, m1, w_pb[l].astype(BF16))
        x1, h2p, code, gates, counts = _merge(merged, x2, w_o[l].astype(BF16), ln2_g[l][None, :], rw, rb)

        n_rows = T * TOP_K + N_EXPERTS * ROW_BLOCK
        max_groups = n_rows // GROUP_ROWS + N_EXPERTS
        pad_starts, g_exp, g_start, g_nsub = _group_tables(counts, max_groups)
        codes = code[:, :TOP_K]
        xs = _dispatch(pad_starts, codes.reshape(T // DP_TM, 1, DP_TM * TOP_K), h2p,
                       jnp.zeros((n_rows + GROUP_ROWS, D_MODEL // 2), jnp.uint32))
        ys = _experts(g_exp, g_start, g_nsub, xs, moe_w_gu[l], moe_b_gu[l][:, None, :],
                      moe_w_down[l], moe_b_down[l][:, None, :])
        x2 = _combine(pad_starts, codes.reshape(T // CB_TM, 1, CB_TM * TOP_K), ys, x1, gates)
    return x2.reshape(B, S, D)
```

```python
import functools

import numpy as np
import jax
import jax.numpy as jnp
from jax import lax
from jax.experimental import pallas as pl
from jax.experimental.pallas import tpu as pltpu

F32 = jnp.float32
BF16 = jnp.bfloat16

D_MODEL = 2048
GM_WIDTH = 1024
GM_GROUPS = 8
GM_CHUNK = 128
MLA_HEADS = 16
Q_LORA = 512
KV_LORA = 512
QK_NOPE = 128
QK_ROPE = 64
V_DIM = 128
QK_DIM = QK_NOPE + QK_ROPE
QK_PAD = 256
MLA_SCALE = QK_DIM ** -0.5
ROPE_THETA = 10000.0
MEM_LEN = 256
MEM_HEADS = 4
MEM_HEAD_DIM = 256
MEM_WIDTH = MEM_HEADS * MEM_HEAD_DIM
MEM_SCALE = MEM_HEAD_DIM ** -0.5
N_EXPERTS = 32
EXPERT_BITS = 5
TOP_K = 4
D_FF = 2048
SWIGLU_LIMIT = 7.0
SWIGLU_ALPHA = 1.702
ROW_BLOCK = 256
EPS = 1e-6
LANES = 128
NEG_INF = float("-inf")

PROJ_COLS = 2 * GM_WIDTH + Q_LORA + KV_LORA + MEM_WIDTH
PG_COLS = PROJ_COLS + 3 * D_MODEL

MIB = 1024 * 1024


def _params(semantics, vmem_mib):
    return pltpu.CompilerParams(dimension_semantics=semantics, vmem_limit_bytes=vmem_mib * MIB)


def _const_spec(shape):
    nd = len(shape)
    return pl.BlockSpec(shape, lambda *_: (0,) * nd, pipeline_mode=pl.Buffered(1))


P1_TM = 1024
P1_TN = 1024


def _norm_proj_kernel(x_ref, g_ref, w_ref, b_ref, wr_ref, o_ref, kr_ref, h_scr, *, n_plain):
    j = pl.program_id(1)

    @pl.when(j == 0)
    def _():
        def body(c, carry):
            rows = pl.ds(pl.multiple_of(c * 128, 128), 128)
            x = x_ref[rows, :]
            ms = jnp.mean(x * x, axis=-1, keepdims=True)
            h_scr[rows, :] = (x * lax.rsqrt(ms + EPS) * g_ref[...]).astype(BF16)
            return carry

        lax.fori_loop(0, P1_TM // 128, body, 0)
        kr_ref[...] = jnp.dot(h_scr[...], wr_ref[...], preferred_element_type=F32)

    acc = jnp.dot(h_scr[...], w_ref[...], preferred_element_type=F32)
    gated = jax.nn.sigmoid(acc + b_ref[...])
    o_ref[...] = jnp.where(j >= n_plain, gated, acc).astype(BF16)


def _norm_proj(x2, ln1_g, w1, b1, wr):
    T = x2.shape[0]
    grid = (T // P1_TM, PG_COLS // P1_TN)
    return pl.pallas_call(
        functools.partial(_norm_proj_kernel, n_plain=PROJ_COLS // P1_TN),
        grid=grid,
        in_specs=[
            pl.BlockSpec((P1_TM, D_MODEL), lambda i, j: (i, 0)),
            pl.BlockSpec((1, D_MODEL), lambda i, j: (0, 0)),
            pl.BlockSpec((D_MODEL, P1_TN), lambda i, j: (0, j)),
            pl.BlockSpec((1, P1_TN), lambda i, j: (0, j)),
            pl.BlockSpec((D_MODEL, LANES), lambda i, j: (0, 0)),
        ],
        out_specs=[
            pl.BlockSpec((P1_TM, P1_TN), lambda i, j: (i, j)),
            pl.BlockSpec((P1_TM, LANES), lambda i, j: (i, 0)),
        ],
        out_shape=[
            jax.ShapeDtypeStruct((T, PG_COLS), BF16),
            jax.ShapeDtypeStruct((T, LANES), F32),
        ],
        scratch_shapes=[pltpu.VMEM((P1_TM, D_MODEL), BF16)],
        compiler_params=_params(("parallel", "arbitrary"), 48),
        name="norm_proj",
    )(x2, ln1_g, w1, b1, wr)


GM_TM = 512


def _gmlp_kernel(u_ref, v_ref, g0_ref, lng_ref, lnb_ref, ws_ref, bias_ref, wpa_ref, o_ref, vb_scr, a_scr):
    v = v_ref[...].astype(F32)
    mu = jnp.mean(v, axis=-1, keepdims=True)
    c = v - mu
    var = jnp.mean(c * c, axis=-1, keepdims=True)
    vb_scr[...] = (c * lax.rsqrt(var + EPS) * lng_ref[...] + lnb_ref[...]).astype(BF16)

    row = lax.broadcasted_iota(jnp.int32, (GM_CHUNK, GM_CHUNK), 0)
    col = lax.broadcasted_iota(jnp.int32, (GM_CHUNK, GM_CHUNK), 1)
    causal = col <= row
    for g in range(GM_GROUPS):
        cols = slice(g * GM_CHUNK, (g + 1) * GM_CHUNK)
        wg = jnp.where(causal, ws_ref[g], jnp.zeros((), BF16))
        for ch in range(GM_TM // GM_CHUNK):
            rows = slice(ch * GM_CHUNK, (ch + 1) * GM_CHUNK)
            mixed = jnp.dot(wg, vb_scr[rows, cols], preferred_element_type=F32) + bias_ref[:, cols]
            a_scr[rows, cols] = (u_ref[rows, cols].astype(F32) * mixed).astype(BF16)

    ma = jnp.dot(a_scr[...], wpa_ref[...], preferred_element_type=F32)
    o_ref[...] = (g0_ref[...].astype(F32) * ma).astype(BF16)


def _gmlp(pg, ln_g, ln_b, ws, bias_full, wpa):
    T = pg.shape[0]
    return pl.pallas_call(
        _gmlp_kernel,
        grid=(T // GM_TM,),
        in_specs=[
            pl.BlockSpec((GM_TM, GM_WIDTH), lambda i: (i, 0)),
            pl.BlockSpec((GM_TM, GM_WIDTH), lambda i: (i, 1)),
            pl.BlockSpec((GM_TM, D_MODEL), lambda i: (i, PROJ_COLS // D_MODEL)),
            _const_spec((1, GM_WIDTH)),
            _const_spec((1, GM_WIDTH)),
            _const_spec((GM_GROUPS, GM_CHUNK, GM_CHUNK)),
            _const_spec((GM_CHUNK, GM_WIDTH)),
            _const_spec((GM_WIDTH, D_MODEL)),
        ],
        out_specs=pl.BlockSpec((GM_TM, D_MODEL), lambda i: (i, 0)),
        out_shape=jax.ShapeDtypeStruct((T, D_MODEL), BF16),
        scratch_shapes=[pltpu.VMEM((GM_TM, GM_WIDTH), BF16), pltpu.VMEM((GM_TM, GM_WIDTH), BF16)],
        compiler_params=_params(("parallel",), 40),
        name="gmlp",
    )(pg, pg, pg, ln_g, ln_b, ws, bias_full, wpa)


QKV_TM = 1024
QKV_HEADS = 2
HEAD_BLK = 4
FA_T = 512
FA_QS = 256


def _rope(t, cos, sin):
    return t * cos + pltpu.roll(t, QK_ROPE, 1) * sin


def _row_sum_all_lanes(sq, weights):
    hi = sq.astype(BF16)
    lo = (sq - hi.astype(F32)).astype(BF16)
    w = weights.astype(BF16)
    return (jnp.dot(hi, w, preferred_element_type=F32) + jnp.dot(lo, w, preferred_element_type=F32))


def _mla_qkv_kernel(cq_ref, ckv_ref, kr_ref, pos_ref, gcq_ref, gckv_ref, wq_ref, wkv_ref, gq_ref, gk_ref,
                    cst_ref, sumw_ref, q_ref, k_ref, vt_ref, cqn_scr, ckvn_scr, cos_scr, sin_scr, krsq_scr):
    hb = pl.program_id(1)
    w_nope = sumw_ref[:QK_NOPE, :]
    w_rope = sumw_ref[QK_NOPE:, :]

    @pl.when(hb == 0)
    def _():
        cq = cq_ref[...].astype(F32)
        ms = jnp.mean(cq * cq, axis=-1, keepdims=True)
        cqn_scr[...] = (cq * lax.rsqrt(ms + EPS) * gcq_ref[...]).astype(BF16)
        ckv = ckv_ref[...].astype(F32)
        ms = jnp.mean(ckv * ckv, axis=-1, keepdims=True)
        ckvn_scr[...] = (ckv * lax.rsqrt(ms + EPS) * gckv_ref[...]).astype(BF16)
        ang = pos_ref[...].astype(F32) * cst_ref[0:1, :]
        cos_scr[...] = jnp.cos(ang) * cst_ref[1:2, :]
        sin_scr[...] = jnp.sin(ang) * cst_ref[2:3, :]
        kr = kr_ref[...]
        krsq_scr[...] = _row_sum_all_lanes(kr * kr, w_rope)

    cos = cos_scr[...]
    sin = sin_scr[...]
    inv_dim = 1.0 / QK_DIM
    kr = kr_ref[...]

    yq2 = jnp.dot(cqn_scr[...], wq_ref[...], preferred_element_type=F32)
    ykv2 = jnp.dot(ckvn_scr[...], wkv_ref[...], preferred_element_type=F32)
    for hh in range(QKV_HEADS):
        yq = yq2[:, hh * QK_PAD:(hh + 1) * QK_PAD]
        qn = yq[:, :QK_NOPE]
        qt = yq[:, QK_NOPE:]
        ssq = _row_sum_all_lanes(qn * qn, w_nope) + _row_sum_all_lanes(qt * qt, w_rope)
        rs = lax.rsqrt(ssq * inv_dim + EPS) * MLA_SCALE
        q_ref[hh, :, :QK_NOPE] = (qn * rs * gq_ref[:, :QK_NOPE]).astype(BF16)
        q_ref[hh, :, QK_NOPE:] = _rope(qt * rs * gq_ref[:, QK_NOPE:], cos, sin).astype(BF16)

        ykv = ykv2[:, hh * (QK_NOPE + V_DIM):(hh + 1) * (QK_NOPE + V_DIM)]
        kn = ykv[:, :QK_NOPE]
        ssq = _row_sum_all_lanes(kn * kn, w_nope) + krsq_scr[...]
        rs = lax.rsqrt(ssq * inv_dim + EPS)
        k_ref[hh, :, :QK_NOPE] = (kn * rs * gk_ref[:, :QK_NOPE]).astype(BF16)
        k_ref[hh, :, QK_NOPE:] = _rope(kr * rs * gk_ref[:, QK_NOPE:], cos, sin).astype(BF16)
        vv = ykv[:, QK_NOPE:]
        for c in range(QKV_TM // FA_T):
            vt_ref[hh, c] = vv[c * FA_T:(c + 1) * FA_T, :].T.astype(BF16)


def _mla_qkv(pg, kr, pos, gcq, gckv, wq, wkv, gq, gk, cst, B, S):
    T = pg.shape[0]
    sumw = jnp.concatenate([jnp.ones((QK_NOPE, LANES), F32), jnp.full((QK_PAD - QK_NOPE, LANES), 0.5, F32)])
    spb = S // QKV_TM
    cpt = QKV_TM // FA_T
    head_spec = lambda w: pl.BlockSpec((None, QKV_HEADS, QKV_TM, w), lambda i, h: (i // spb, h, i % spb, 0))
    return pl.pallas_call(
        _mla_qkv_kernel,
        grid=(T // QKV_TM, MLA_HEADS // QKV_HEADS),
        in_specs=[
            pl.BlockSpec((QKV_TM, Q_LORA), lambda i, h: (i, 2 * GM_WIDTH // Q_LORA)),
            pl.BlockSpec((QKV_TM, KV_LORA), lambda i, h: (i, 2 * GM_WIDTH // KV_LORA + 1)),
            pl.BlockSpec((QKV_TM, LANES), lambda i, h: (i, 0)),
            pl.BlockSpec((QKV_TM, 1), lambda i, h: (i, 0)),
            pl.BlockSpec((1, Q_LORA), lambda i, h: (0, 0)),
            pl.BlockSpec((1, KV_LORA), lambda i, h: (0, 0)),
            pl.BlockSpec((Q_LORA, QKV_HEADS * QK_PAD), lambda i, h: (0, h)),
            pl.BlockSpec((KV_LORA, QKV_HEADS * (QK_NOPE + V_DIM)), lambda i, h: (0, h)),
            pl.BlockSpec((1, QK_PAD), lambda i, h: (0, 0)),
            pl.BlockSpec((1, QK_PAD), lambda i, h: (0, 0)),
            pl.BlockSpec((8, LANES), lambda i, h: (0, 0)),
            pl.BlockSpec((QK_PAD, LANES), lambda i, h: (0, 0)),
        ],
        out_specs=[
            head_spec(QK_PAD),
            head_spec(QK_PAD),
            pl.BlockSpec((None, QKV_HEADS, cpt, V_DIM, FA_T), lambda i, h: (i // spb, h, i % spb, 0, 0)),
        ],
        out_shape=[
            jax.ShapeDtypeStruct((B, MLA_HEADS, S, QK_PAD), BF16),
            jax.ShapeDtypeStruct((B, MLA_HEADS, S, QK_PAD), BF16),
            jax.ShapeDtypeStruct((B, MLA_HEADS, S // FA_T, V_DIM, FA_T), BF16),
        ],
        scratch_shapes=[
            pltpu.VMEM((QKV_TM, Q_LORA), BF16),
            pltpu.VMEM((QKV_TM, KV_LORA), BF16),
            pltpu.VMEM((QKV_TM, LANES), F32),
            pltpu.VMEM((QKV_TM, LANES), F32),
            pltpu.VMEM((QKV_TM, LANES), F32),
        ],
        compiler_params=_params(("parallel", "arbitrary"), 48),
        name="mla_qkv",
    )(pg, pg, kr, pos, gcq, gckv, wq, wkv, gq, gk, cst, sumw)


def _flash_kernel(q_ref, k_ref, vt_ref, o_ref, m_scr, l_scr, acc_scr):
    qi = pl.program_id(2)
    m_scr[...] = jnp.full_like(m_scr, NEG_INF)
    l_scr[...] = jnp.zeros_like(l_scr)
    acc_scr[...] = jnp.zeros_like(acc_scr)

    def step(kb, diagonal):
        ks = pl.ds(pl.multiple_of(kb * FA_T, FA_T), FA_T)
        chains = [(hh, qs) for hh in range(HEAD_BLK) for qs in range(FA_T // FA_QS)]
        scores = []
        for hh, qs in chains:
            qc = slice(qs * FA_QS, (qs + 1) * FA_QS)
            st = lax.dot_general(k_ref[hh, ks, :], q_ref[hh, qc, :], (((1,), (1,)), ((), ())),
                                 preferred_element_type=F32)
            if diagonal:
                krow = lax.broadcasted_iota(jnp.int32, (FA_T, FA_QS), 0)
                qcol = lax.broadcasted_iota(jnp.int32, (FA_T, FA_QS), 1) + qs * FA_QS
                st = jnp.where(krow <= qcol, st, NEG_INF)
            scores.append(st)
        probs = []
        for (hh, qs), st in zip(chains, scores):
            qc = slice(qs * FA_QS, (qs + 1) * FA_QS)
            m_prev = m_scr[hh, :, qc]
            m_new = jnp.maximum(m_prev, jnp.max(st, axis=0, keepdims=True))
            alpha = jnp.exp(m_prev - m_new)
            p = jnp.exp(st - m_new)
            l_scr[hh, :, qc] = alpha * l_scr[hh, :, qc] + jnp.sum(p, axis=0, keepdims=True)
            m_scr[hh, :, qc] = m_new
            probs.append((alpha, p.astype(BF16)))
        for (hh, qs), (alpha, p) in zip(chains, probs):
            qc = slice(qs * FA_QS, (qs + 1) * FA_QS)
            acc_scr[hh, :, qc] = alpha * acc_scr[hh, :, qc] + jnp.dot(vt_ref[hh, kb], p,
                                                                      preferred_element_type=F32)

    def body(kb, carry):
        step(kb, False)
        return carry

    lax.fori_loop(0, qi, body, 0)
    step(qi, True)
    for hh in range(HEAD_BLK):
        o_ref[:, hh * V_DIM:(hh + 1) * V_DIM] = (acc_scr[hh] / l_scr[hh]).T.astype(BF16)


def _flash(q, k, vt):
    B, H, S, _ = q.shape
    nq = S // FA_T
    return pl.pallas_call(
        _flash_kernel,
        grid=(B, H // HEAD_BLK, nq),
        in_specs=[
            pl.BlockSpec((None, HEAD_BLK, FA_T, QK_PAD), lambda b, h, i: (b, h, i, 0)),
            pl.BlockSpec((None, HEAD_BLK, S, QK_PAD), lambda b, h, i: (b, h, 0, 0)),
            pl.BlockSpec((None, HEAD_BLK, nq, V_DIM, FA_T), lambda b, h, i: (b, h, 0, 0, 0)),
        ],
        out_specs=pl.BlockSpec((FA_T, HEAD_BLK * V_DIM), lambda b, h, i: (b * nq + i, h)),
        out_shape=jax.ShapeDtypeStruct((B * S, H * V_DIM), BF16),
        scratch_shapes=[
            pltpu.VMEM((HEAD_BLK, 1, FA_T), F32),
            pltpu.VMEM((HEAD_BLK, 1, FA_T), F32),
            pltpu.VMEM((HEAD_BLK, V_DIM, FA_T), F32),
        ],
        compiler_params=_params(("parallel", "parallel", "arbitrary"), 40),
        name="mla_attention",
    )(q, k, vt)


def _mem_kv_kernel(mem_ref, g_ref, wk_ref, wv_ref, kng_ref, k_ref, v_ref):
    m = mem_ref[...]
    ms = jnp.mean(m * m, axis=-1, keepdims=True)
    mb = (m * lax.rsqrt(ms + EPS) * g_ref[...]).astype(BF16)
    kk = jnp.dot(mb, wk_ref[...], preferred_element_type=F32)
    for h in range(MEM_HEADS):
        cols = slice(h * MEM_HEAD_DIM, (h + 1) * MEM_HEAD_DIM)
        kh = kk[:, cols]
        ms = jnp.mean(kh * kh, axis=-1, keepdims=True)
        k_ref[:, cols] = (kh * lax.rsqrt(ms + EPS) * kng_ref[...]).astype(BF16)
    v_ref[...] = jnp.dot(mb, wv_ref[...], preferred_element_type=F32).astype(BF16)


def _mem_kv(mem, g, wk, wv, kng):
    B = mem.shape[0]
    return pl.pallas_call(
        _mem_kv_kernel,
        grid=(B,),
        in_specs=[
            pl.BlockSpec((None, MEM_LEN, D_MODEL), lambda b: (b, 0, 0)),
            _const_spec((1, D_MODEL)),
            _const_spec((D_MODEL, MEM_WIDTH)),
            _const_spec((D_MODEL, MEM_WIDTH)),
            _const_spec((1, MEM_HEAD_DIM)),
        ],
        out_specs=[
            pl.BlockSpec((None, MEM_LEN, MEM_WIDTH), lambda b: (b, 0, 0)),
            pl.BlockSpec((None, MEM_LEN, MEM_WIDTH), lambda b: (b, 0, 0)),
        ],
        out_shape=[
            jax.ShapeDtypeStruct((B, MEM_LEN, MEM_WIDTH), BF16),
            jax.ShapeDtypeStruct((B, MEM_LEN, MEM_WIDTH), BF16),
        ],
        compiler_params=_params(("parallel",), 32),
        name="mem_kv",
    )(mem, g, wk, wv, kng)


MA_TM = 512


def _mem_attn_kernel(qm_ref, km_ref, vm_ref, g2_ref, m0_ref, qng_ref, wpc_ref, o_ref, c_scr):
    for h in range(MEM_HEADS):
        cols = slice(h * MEM_HEAD_DIM, (h + 1) * MEM_HEAD_DIM)
        qh = qm_ref[:, cols].astype(F32)
        ms = jnp.mean(qh * qh, axis=-1, keepdims=True)
        qn = (qh * (lax.rsqrt(ms + EPS) * MEM_SCALE) * qng_ref[...]).astype(BF16)
        s = lax.dot_general(qn, km_ref[:, cols], (((1,), (1,)), ((), ())), preferred_element_type=F32)
        e = jnp.exp(s - jnp.max(s, axis=-1, keepdims=True))
        p = (e / jnp.sum(e, axis=-1, keepdims=True)).astype(BF16)
        c_scr[:, cols] = jnp.dot(p, vm_ref[:, cols], preferred_element_type=F32).astype(BF16)
    mc = jnp.dot(c_scr[...], wpc_ref[...], preferred_element_type=F32)
    o_ref[...] = (m0_ref[...].astype(F32) + g2_ref[...].astype(F32) * mc).astype(BF16)


def _mem_attn(pg, km, vm, m0, qng, wpc, S):
    T = pg.shape[0]
    spb = S // MA_TM
    return pl.pallas_call(
        _mem_attn_kernel,
        grid=(T // MA_TM,),
        in_specs=[
            pl.BlockSpec((MA_TM, MEM_WIDTH), lambda i: (i, (2 * GM_WIDTH + Q_LORA + KV_LORA) // MEM_WIDTH)),
            pl.BlockSpec((None, MEM_LEN, MEM_WIDTH), lambda i: (i // spb, 0, 0)),
            pl.BlockSpec((None, MEM_LEN, MEM_WIDTH), lambda i: (i // spb, 0, 0)),
            pl.BlockSpec((MA_TM, D_MODEL), lambda i: (i, PROJ_COLS // D_MODEL + 2)),
            pl.BlockSpec((MA_TM, D_MODEL), lambda i: (i, 0)),
            _const_spec((1, MEM_HEAD_DIM)),
            _const_spec((MEM_WIDTH, D_MODEL)),
        ],
        out_specs=pl.BlockSpec((MA_TM, D_MODEL), lambda i: (i, 0)),
        out_shape=jax.ShapeDtypeStruct((T, D_MODEL), BF16),
        scratch_shapes=[pltpu.VMEM((MA_TM, MEM_WIDTH), BF16)],
        compiler_params=_params(("parallel",), 40),
        name="mem_attention",
    )(pg, km, vm, pg, m0, qng, wpc)


MG_TM = 512


def _pack_bf16_pair(a, b):
    hi = pltpu.bitcast(a.astype(BF16).astype(F32), jnp.uint32)
    lo = pltpu.bitcast(b.astype(BF16).astype(F32), jnp.uint32)
    return hi | (lo >> 16)


def _unpack_bf16_pair(p):
    hi = pltpu.bitcast(p & jnp.uint32(0xFFFF0000), F32)
    lo = pltpu.bitcast(p << 16, F32)
    return hi, lo


def _attn_proj_kernel(b_ref, g1_ref, m1_ref, wpb_ref, o_ref):
    mb = jnp.dot(b_ref[...], wpb_ref[...], preferred_element_type=F32)
    o_ref[...] = (m1_ref[...].astype(F32) + g1_ref[...].astype(F32) * mb).astype(BF16)


def _attn_proj(b_attn, pg, m1, wpb):
    T = m1.shape[0]
    row = pl.BlockSpec((MG_TM, D_MODEL), lambda i: (i, 0))
    return pl.pallas_call(
        _attn_proj_kernel,
        grid=(T // MG_TM,),
        in_specs=[
            row,
            pl.BlockSpec((MG_TM, D_MODEL), lambda i: (i, PROJ_COLS // D_MODEL + 1)),
            row,
            _const_spec((MLA_HEADS * V_DIM, D_MODEL)),
        ],
        out_specs=row,
        out_shape=jax.ShapeDtypeStruct((T, D_MODEL), BF16),
        compiler_params=_params(("parallel",), 40),
        name="attn_proj",
    )(b_attn, pg, m1, wpb)


def _merge_kernel(mg_ref, x_ref, wo_ref, ln2_ref, rw_ref, rb_ref,
                  x1_ref, h2p_ref, code_ref, gate_ref, cnt_ref, cnt_scr):
    @pl.when(pl.program_id(0) == 0)
    def _():
        cnt_scr[...] = jnp.zeros_like(cnt_scr)

    x1 = x_ref[...] + jnp.dot(mg_ref[...], wo_ref[...], preferred_element_type=F32)
    x1_ref[...] = x1
    ms = jnp.mean(x1 * x1, axis=-1, keepdims=True)
    h2 = x1 * lax.rsqrt(ms + EPS) * ln2_ref[...]
    h2p_ref[...] = _pack_bf16_pair(h2[:, :D_MODEL // 2], h2[:, D_MODEL // 2:])

    logits = jnp.dot(h2, rw_ref[...], preferred_element_type=F32, precision=lax.Precision.HIGHEST)
    lane = lax.broadcasted_iota(jnp.int32, (MG_TM, LANES), 1)
    work = jnp.where(lane < N_EXPERTS, logits + rb_ref[...], NEG_INF)
    earlier = (lax.broadcasted_iota(jnp.int32, (MG_TM, MG_TM), 1)
               < lax.broadcasted_iota(jnp.int32, (MG_TM, MG_TM), 0)).astype(BF16)
    base = cnt_scr[...]
    code_out = jnp.zeros((MG_TM, LANES), jnp.int32)
    val_out = jnp.zeros((MG_TM, LANES), F32)
    top = None
    denom = jnp.zeros((MG_TM, 1), F32)
    for k in range(TOP_K):
        mk = jnp.max(work, axis=-1, keepdims=True)
        ik = jnp.min(jnp.where(work == mk, lane, LANES), axis=-1, keepdims=True)
        hit = lane == ik
        work = jnp.where(hit, NEG_INF, work)
        if top is None:
            top = mk
        ek = jnp.exp(mk - top)
        denom = denom + ek
        val_out = jnp.where(lane == k, ek, val_out)
        onehot = hit.astype(BF16)
        prefix = jnp.dot(earlier, onehot, preferred_element_type=F32) + base
        rank = jnp.sum(jnp.where(hit, prefix, 0.0), axis=-1, keepdims=True).astype(jnp.int32)
        code_out = jnp.where(lane == k, rank * N_EXPERTS + ik, code_out)
        base = base + jnp.sum(onehot.astype(F32), axis=0, keepdims=True)
    code_ref[...] = code_out
    gate_ref[...] = val_out / denom
    cnt_scr[...] = base
    cnt_ref[...] = base


def _merge(merged, x2, wo, ln2, rw, rb):
    T = x2.shape[0]
    row = lambda w: pl.BlockSpec((MG_TM, w), lambda i: (i, 0))
    return pl.pallas_call(
        _merge_kernel,
        grid=(T // MG_TM,),
        in_specs=[
            row(D_MODEL),
            row(D_MODEL),
            _const_spec((D_MODEL, D_MODEL)),
            _const_spec((1, D_MODEL)),
            _const_spec((D_MODEL, LANES)),
            _const_spec((1, LANES)),
        ],
        out_specs=[row(D_MODEL), row(D_MODEL // 2), row(LANES), row(LANES),
                   pl.BlockSpec((1, LANES), lambda i: (0, 0))],
        out_shape=[
            jax.ShapeDtypeStruct((T, D_MODEL), F32),
            jax.ShapeDtypeStruct((T, D_MODEL // 2), jnp.uint32),
            jax.ShapeDtypeStruct((T, LANES), jnp.int32),
            jax.ShapeDtypeStruct((T, LANES), F32),
            jax.ShapeDtypeStruct((1, LANES), F32),
        ],
        scratch_shapes=[pltpu.VMEM((1, LANES), F32)],
        compiler_params=_params(("arbitrary",), 56),
        name="merge_router",
    )(merged, x2, wo, ln2, rw, rb)


DP_TM = 256
DMA_UNROLL = 8


def _row_copy(src, src_row, dst, dst_row, sem):
    return pltpu.make_async_copy(src.at[pl.ds(src_row, 1), :], dst.at[pl.ds(dst_row, 1), :], sem)


def _dispatch_kernel(dest_ref, h2p_ref, xs_in, xs_hbm, sem):
    del xs_in

    def issue(t, carry):
        src = h2p_ref.at[pl.ds(t, 1), :]
        for k in range(TOP_K):
            row = dest_ref[0, t * TOP_K + k]
            pltpu.make_async_copy(src, xs_hbm.at[pl.ds(row, 1), :], sem).start()
        return carry

    lax.fori_loop(0, DP_TM, issue, 0, unroll=DMA_UNROLL // TOP_K)
    all_rows = xs_hbm.at[pl.ds(0, DP_TM * TOP_K), :]
    pltpu.make_async_copy(all_rows, all_rows, sem).wait()


def _dispatch(dest3, h2p, xs_zero):
    T = h2p.shape[0]
    return pl.pallas_call(
        _dispatch_kernel,
        grid=(T // DP_TM,),
        in_specs=[
            pl.BlockSpec((None, 1, DP_TM * TOP_K), lambda i: (i, 0, 0), memory_space=pltpu.SMEM),
            pl.BlockSpec((DP_TM, D_MODEL // 2), lambda i: (i, 0)),
            pl.BlockSpec(memory_space=pl.ANY),
        ],
        out_specs=pl.BlockSpec(memory_space=pl.ANY),
        out_shape=jax.ShapeDtypeStruct(xs_zero.shape, xs_zero.dtype),
        scratch_shapes=[pltpu.SemaphoreType.DMA(())],
        input_output_aliases={2: 0},
        compiler_params=_params(("arbitrary",), 32),
        name="moe_dispatch",
    )(dest3, h2p, xs_zero)


GROUP_SUB = 4
GROUP_ROWS = GROUP_SUB * ROW_BLOCK
FF_TILE = 256
N_FF_TILES = D_FF // FF_TILE
DOWN_CHUNK = 512


def _expert_kernel(ge_ref, gs_ref, gn_ref, xs_in, wg_ref, wu_ref, bg_ref, bu_ref, wd_ref, bd_ref,
                   rows_hbm, xwin, x_scr, acc_scr, stage, wsem, sem):
    del ge_ref, xs_in
    g = pl.program_id(0)
    f = pl.program_id(1)
    ns = gn_ref[g]
    half = D_MODEL // 2
    slot = g & 1

    def win_copy(group, s):
        rows = pl.ds(gs_ref[group] * ROW_BLOCK, GROUP_ROWS)
        return pltpu.make_async_copy(rows_hbm.at[rows, :], xwin.at[s], wsem.at[s])

    def out_copy(s, start_block):
        rows = pl.ds((start_block + s) * ROW_BLOCK, ROW_BLOCK)
        return pltpu.make_async_copy(stage.at[s], rows_hbm.at[rows, :], sem.at[s])

    @pl.when((f == 0) & (ns > 0))
    def _():
        @pl.when(g == 0)
        def _():
            win_copy(0, 0).start()

        win_copy(g, slot).wait()
        @pl.when(g + 1 < pl.num_programs(0))
        def _():
            @pl.when(gn_ref[g + 1] > 0)
            def _():
                win_copy(g + 1, 1 - slot).start()

        hi, lo = _unpack_bf16_pair(xwin[slot])
        x_scr[:, :half] = hi.astype(BF16)
        x_scr[:, half:] = lo.astype(BF16)
        acc_scr[...] = jnp.broadcast_to(bd_ref[...], acc_scr.shape)

    for n in range(1, GROUP_SUB + 1):
        @pl.when(ns == n)
        def _(n=n):
            m = n * ROW_BLOCK
            x = x_scr[:m, :]
            gate = jnp.dot(x, wg_ref[...].astype(BF16), preferred_element_type=F32) + bg_ref[...]
            up = jnp.dot(x, wu_ref[...].astype(BF16), preferred_element_type=F32) + bu_ref[...]
            gate = jnp.minimum(gate, SWIGLU_LIMIT)
            up = jnp.clip(up, -SWIGLU_LIMIT, SWIGLU_LIMIT)
            glu = gate * jax.nn.sigmoid(gate * SWIGLU_ALPHA)
            act = ((up + 1.0) * glu).astype(BF16)
            for c in range(D_MODEL // DOWN_CHUNK):
                cols = slice(c * DOWN_CHUNK, (c + 1) * DOWN_CHUNK)
                acc_scr[:m, cols] += jnp.dot(act, wd_ref[:, cols].astype(BF16), preferred_element_type=F32)

    @pl.when(f == N_FF_TILES - 1)
    def _():
        @pl.when(g > 0)
        def _():
            prev = gn_ref[g - 1]
            for s in range(GROUP_SUB):
                @pl.when(s < prev)
                def _(s=s):
                    out_copy(s, 0).wait()

        for s in range(GROUP_SUB):
            @pl.when(s < ns)
            def _(s=s):
                y = acc_scr[s * ROW_BLOCK:(s + 1) * ROW_BLOCK, :]
                stage[s] = _pack_bf16_pair(y[:, :half], y[:, half:])
                out_copy(s, gs_ref[g]).start()

        @pl.when(g == pl.num_programs(0) - 1)
        def _():
            for s in range(GROUP_SUB):
                @pl.when(s < ns)
                def _(s=s):
                    out_copy(s, 0).wait()


def _experts(g_exp, g_start, g_nsub, xs, wgu, bgu, wd, bd):
    def ff(f, gn, g):
        return jnp.where(gn[g] > 0, f, N_FF_TILES - 1)

    grid_spec = pltpu.PrefetchScalarGridSpec(
        num_scalar_prefetch=3,
        grid=(g_exp.shape[0], N_FF_TILES),
        in_specs=[
            pl.BlockSpec(memory_space=pl.ANY),
            pl.BlockSpec((None, D_MODEL, FF_TILE), lambda g, f, ge, gs, gn: (ge[g], 0, ff(f, gn, g))),
            pl.BlockSpec((None, D_MODEL, FF_TILE),
                         lambda g, f, ge, gs, gn: (ge[g], 0, N_FF_TILES + ff(f, gn, g))),
            pl.BlockSpec((None, 1, FF_TILE), lambda g, f, ge, gs, gn: (ge[g], 0, ff(f, gn, g))),
            pl.BlockSpec((None, 1, FF_TILE), lambda g, f, ge, gs, gn: (ge[g], 0, N_FF_TILES + ff(f, gn, g))),
            pl.BlockSpec((None, FF_TILE, D_MODEL), lambda g, f, ge, gs, gn: (ge[g], ff(f, gn, g), 0)),
            pl.BlockSpec((None, 1, D_MODEL), lambda g, f, ge, gs, gn: (ge[g], 0, 0)),
        ],
        out_specs=pl.BlockSpec(memory_space=pl.ANY),
        scratch_shapes=[
            pltpu.VMEM((2, GROUP_ROWS, D_MODEL // 2), jnp.uint32),
            pltpu.VMEM((GROUP_ROWS, D_MODEL), BF16),
            pltpu.VMEM((GROUP_ROWS, D_MODEL), F32),
            pltpu.VMEM((GROUP_SUB, ROW_BLOCK, D_MODEL // 2), jnp.uint32),
            pltpu.SemaphoreType.DMA((2,)),
            pltpu.SemaphoreType.DMA((GROUP_SUB,)),
        ],
    )
    return pl.pallas_call(
        _expert_kernel,
        grid_spec=grid_spec,
        out_shape=jax.ShapeDtypeStruct(xs.shape, jnp.uint32),
        input_output_aliases={3: 0},
        compiler_params=_params(("arbitrary", "arbitrary"), 58),
        name="moe_experts",
    )(g_exp, g_start, g_nsub, xs, wgu, wgu, bgu, bgu, wd, bd)


CB_TM = 128


def _combine_kernel(dest_ref, dest_next_ref, ys_hbm, x1_ref, gate_ref, o_ref, buf, sem):
    i = pl.program_id(0)
    slot = i & 1

    def gather_tile(dest, s):
        def issue(t, carry):
            for k in range(TOP_K):
                row = dest[0, t * TOP_K + k]
                _row_copy(ys_hbm, row, buf.at[s, k], t, sem.at[s]).start()
            return carry

        lax.fori_loop(0, CB_TM, issue, 0, unroll=DMA_UNROLL // TOP_K)

    @pl.when(i == 0)
    def _():
        gather_tile(dest_ref, 0)

    @pl.when(i + 1 < pl.num_programs(0))
    def _():
        gather_tile(dest_next_ref, 1 - slot)

    pltpu.make_async_copy(buf.at[slot], buf.at[slot], sem.at[slot]).wait()

    half = D_MODEL // 2
    gates = gate_ref[...]
    out_hi = x1_ref[:, :half]
    out_lo = x1_ref[:, half:]
    for k in range(TOP_K):
        hi, lo = _unpack_bf16_pair(buf[slot, k])
        g = gates[:, k:k + 1]
        out_hi = out_hi + g * hi
        out_lo = out_lo + g * lo
    o_ref[:, :half] = out_hi
    o_ref[:, half:] = out_lo


def _combine(dest3, ys, x1, gates):
    T = x1.shape[0]
    n = T // CB_TM
    dest_spec = lambda f: pl.BlockSpec((None, 1, CB_TM * TOP_K), f, memory_space=pltpu.SMEM)
    return pl.pallas_call(
        _combine_kernel,
        grid=(n,),
        in_specs=[
            dest_spec(lambda i: (i, 0, 0)),
            dest_spec(lambda i: (jnp.minimum(i + 1, n - 1), 0, 0)),
            pl.BlockSpec(memory_space=pl.ANY),
            pl.BlockSpec((CB_TM, D_MODEL), lambda i: (i, 0)),
            pl.BlockSpec((CB_TM, LANES), lambda i: (i, 0)),
        ],
        out_specs=pl.BlockSpec((CB_TM, D_MODEL), lambda i: (i, 0)),
        out_shape=jax.ShapeDtypeStruct((T, D_MODEL), F32),
        scratch_shapes=[pltpu.VMEM((2, TOP_K, CB_TM, D_MODEL // 2), jnp.uint32),
                        pltpu.SemaphoreType.DMA((2,))],
        compiler_params=_params(("arbitrary",), 32),
        name="moe_combine",
    )(dest3, dest3, ys, x1, gates)


def _group_tables(counts_f32, max_groups):
    i32 = jnp.int32
    counts = counts_f32[0, :N_EXPERTS].astype(i32)
    nb = (counts + ROW_BLOCK - 1) // ROW_BLOCK
    ng = (nb + GROUP_SUB - 1) // GROUP_SUB
    upto = jnp.arange(N_EXPERTS)[None, :] <= jnp.arange(N_EXPERTS)[:, None]
    pad_ends = jnp.sum(jnp.where(upto, nb[None, :], 0), axis=1) * ROW_BLOCK
    pad_starts = pad_ends - nb * ROW_BLOCK
    g_ends = jnp.sum(jnp.where(upto, ng[None, :], 0), axis=1)
    g_starts = g_ends - ng
    n_groups = g_ends[-1]

    g = jnp.arange(max_groups, dtype=i32)
    gg = jnp.minimum(g, n_groups - 1)
    e = jnp.minimum(jnp.sum((g_ends[None, :] <= gg[:, None]).astype(i32), axis=1), N_EXPERTS - 1)
    pick = e[:, None] == jnp.arange(N_EXPERTS)[None, :]
    take = lambda table: jnp.sum(jnp.where(pick, table[None, :], 0), axis=1)
    nb_g, ng_g = take(nb), jnp.maximum(take(ng), 1)
    j = gg - take(g_starts)
    base, rem = nb_g // ng_g, nb_g % ng_g
    nsub = jnp.where(g < n_groups, base + (j < rem).astype(i32), 0)
    start_block = take(pad_starts) // ROW_BLOCK + j * base + jnp.minimum(j, rem)
    return pad_starts.astype(i32), e.astype(i32), start_block.astype(i32), nsub.astype(i32)


def _rope_constants():
    lane = np.arange(LANES)
    half = QK_ROPE // 2
    inv = 1.0 / (ROPE_THETA ** (np.arange(0, QK_ROPE, 2, dtype=np.float32) / QK_ROPE))
    cst = np.zeros((8, LANES), np.float32)
    cst[0, :QK_ROPE] = inv.astype(np.float32)[lane[:QK_ROPE] % half]
    cst[1, :QK_ROPE] = 1.0
    cst[2, :half] = -1.0
    cst[2, half:QK_ROPE] = 1.0
    return jnp.asarray(cst)


def _swap_halves(a):
    half = QK_ROPE // 2
    return jnp.concatenate([a[..., half:], a[..., :half]], axis=-1)


def kernel(x, mem, positions, ln1_g, w_in, w_gate, b_gate, gmlp_ln_g, gmlp_ln_b, gmlp_ws, gmlp_bs, w_pa,
           mla_cq_g, mla_w_uq, mla_ckv_g, mla_w_ukv, mla_qn_g, mla_kn_g, w_pb, mem_ln_g, mem_w_k, mem_w_v,
           mem_qn_g, mem_kn_g, w_pc, w_o, ln2_g, router_w, router_b, moe_w_gu, moe_b_gu, moe_w_down,
           moe_b_down):
    B, S, D = x.shape
    T = B * S
    x2 = x.reshape(T, D)
    for l in range(ln1_g.shape[0]):
        o_kr = 2 * GM_WIDTH + Q_LORA + KV_LORA
        o_qm = o_kr + QK_ROPE
        wi = w_in[l]
        w1 = jnp.concatenate([wi[:, :o_kr], wi[:, o_qm:], w_gate[l]], axis=1).astype(BF16)
        b1 = jnp.concatenate([jnp.zeros((PROJ_COLS,), F32), b_gate[l]])[None, :]
        w_kr = wi[:, o_kr:o_qm]
        wr = jnp.concatenate([w_kr, _swap_halves(w_kr)], axis=1).astype(BF16)

        wq3 = mla_w_uq[l].reshape(Q_LORA, MLA_HEADS, QK_DIM)
        wq = jnp.concatenate([wq3, _swap_halves(wq3[..., QK_NOPE:])], axis=-1)
        wq = wq.reshape(Q_LORA, MLA_HEADS * QK_PAD).astype(BF16)
        wkv = mla_w_ukv[l].astype(BF16)
        gq = jnp.concatenate([mla_qn_g[l], _swap_halves(mla_qn_g[l][QK_NOPE:])])[None, :]
        gk = jnp.concatenate([mla_kn_g[l], _swap_halves(mla_kn_g[l][QK_NOPE:])])[None, :]

        bias_full = jnp.broadcast_to(gmlp_bs[l].T[:, :, None], (GM_CHUNK, GM_GROUPS, GM_CHUNK))
        bias_full = bias_full.reshape(GM_CHUNK, GM_WIDTH)

        rw = jnp.pad(router_w[l], ((0, 0), (0, LANES - N_EXPERTS)))
        rb = jnp.pad(router_b[l], (0, LANES - N_EXPERTS))[None, :]

        pg, kr = _norm_proj(x2, ln1_g[l][None, :], w1, b1, wr)
        m0 = _gmlp(pg, gmlp_ln_g[l][None, :], gmlp_ln_b[l][None, :], gmlp_ws[l].astype(BF16), bias_full,
                   w_pa[l].astype(BF16))
        q, k, vt = _mla_qkv(pg, kr, positions.reshape(T, 1), mla_cq_g[l][None, :], mla_ckv_g[l][None, :],
                           wq, wkv, gq, gk, _rope_constants(), B, S)
        b_attn = _flash(q, k, vt)
        km, vm = _mem_kv(mem, mem_ln_g[l][None, :], mem_w_k[l].astype(BF16), mem_w_v[l].astype(BF16),
                         mem_kn_g[l][None, :])
        m1 = _mem_attn(pg, km, vm, m0, mem_qn_g[l][None, :], w_pc[l].astype(BF16), S)
        merged = _attn_proj(b_attn, pg, m1, w_pb[l].astype(BF16))
        x1, h2p, code, gates, counts = _merge(merged, x2, w_o[l].astype(BF16), ln2_g[l][None, :], rw, rb)

        n_rows = T * TOP_K + N_EXPERTS * ROW_BLOCK
        max_groups = n_rows // GROUP_ROWS + N_EXPERTS
        pad_starts, g_exp, g_start, g_nsub = _group_tables(counts, max_groups)
        codes = code[:, :TOP_K]
        pick = (codes & (N_EXPERTS - 1))[..., None] == jnp.arange(N_EXPERTS)
        dest = (codes >> EXPERT_BITS) + jnp.sum(jnp.where(pick, pad_starts, 0), axis=-1)
        xs = _dispatch(dest.reshape(T // DP_TM, 1, DP_TM * TOP_K), h2p,
                       jnp.zeros((n_rows + GROUP_ROWS, D_MODEL // 2), jnp.uint32))
        ys = _experts(g_exp, g_start, g_nsub, xs, moe_w_gu[l], moe_b_gu[l][:, None, :],
                      moe_w_down[l], moe_b_down[l][:, None, :])
        x2 = _combine(dest.reshape(T // CB_TM, 1, CB_TM * TOP_K), ys, x1, gates)
    return x2.reshape(B, S, D)
```

```python
import functools

import numpy as np
import jax
import jax.numpy as jnp
from jax import lax
from jax.experimental import pallas as pl
from jax.experimental.pallas import tpu as pltpu

F32 = jnp.float32
BF16 = jnp.bfloat16

D_MODEL = 2048
GM_WIDTH = 1024
GM_GROUPS = 8
GM_CHUNK = 128
MLA_HEADS = 16
Q_LORA = 512
KV_LORA = 512
QK_NOPE = 128
QK_ROPE = 64
V_DIM = 128
QK_DIM = QK_NOPE + QK_ROPE
QK_PAD = 256
MLA_SCALE = QK_DIM ** -0.5
LOG2_E = 1.4426950408889634
ROPE_THETA = 10000.0
MEM_LEN = 256
MEM_HEADS = 4
MEM_HEAD_DIM = 256
MEM_WIDTH = MEM_HEADS * MEM_HEAD_DIM
MEM_SCALE = MEM_HEAD_DIM ** -0.5
N_EXPERTS = 32
EXPERT_BITS = 5
TOP_K = 4
D_FF = 2048
SWIGLU_LIMIT = 7.0
SWIGLU_ALPHA = 1.702
ROW_BLOCK = 256
EPS = 1e-6
LANES = 128
NEG_INF = float("-inf")

PROJ_COLS = 2 * GM_WIDTH + Q_LORA + KV_LORA + MEM_WIDTH
PG_COLS = PROJ_COLS + 3 * D_MODEL

MIB = 1024 * 1024


def _params(semantics, vmem_mib):
    return pltpu.CompilerParams(dimension_semantics=semantics, vmem_limit_bytes=vmem_mib * MIB)


def _const_spec(shape):
    nd = len(shape)
    return pl.BlockSpec(shape, lambda *_: (0,) * nd, pipeline_mode=pl.Buffered(1))


P1_TM = 1024
P1_TN = 1024


def _norm_proj_kernel(x_ref, g_ref, w_ref, b_ref, wr_ref, o_ref, kr_ref, h_scr, *, n_plain):
    j = pl.program_id(1)

    @pl.when(j == 0)
    def _():
        def body(c, carry):
            rows = pl.ds(pl.multiple_of(c * 128, 128), 128)
            x = x_ref[rows, :]
            ms = jnp.mean(x * x, axis=-1, keepdims=True)
            h_scr[rows, :] = (x * lax.rsqrt(ms + EPS) * g_ref[...]).astype(BF16)
            return carry

        lax.fori_loop(0, P1_TM // 128, body, 0)
        kr_ref[...] = jnp.dot(h_scr[...], wr_ref[...], preferred_element_type=F32)

    acc = jnp.dot(h_scr[...], w_ref[...], preferred_element_type=F32)
    gated = jax.nn.sigmoid(acc + b_ref[...])
    o_ref[...] = jnp.where(j >= n_plain, gated, acc).astype(BF16)


def _norm_proj(x2, ln1_g, w1, b1, wr):
    T = x2.shape[0]
    grid = (T // P1_TM, PG_COLS // P1_TN)
    return pl.pallas_call(
        functools.partial(_norm_proj_kernel, n_plain=PROJ_COLS // P1_TN),
        grid=grid,
        in_specs=[
            pl.BlockSpec((P1_TM, D_MODEL), lambda i, j: (i, 0)),
            pl.BlockSpec((1, D_MODEL), lambda i, j: (0, 0)),
            pl.BlockSpec((D_MODEL, P1_TN), lambda i, j: (0, j)),
            pl.BlockSpec((1, P1_TN), lambda i, j: (0, j)),
            pl.BlockSpec((D_MODEL, LANES), lambda i, j: (0, 0)),
        ],
        out_specs=[
            pl.BlockSpec((P1_TM, P1_TN), lambda i, j: (i, j)),
            pl.BlockSpec((P1_TM, LANES), lambda i, j: (i, 0)),
        ],
        out_shape=[
            jax.ShapeDtypeStruct((T, PG_COLS), BF16),
            jax.ShapeDtypeStruct((T, LANES), F32),
        ],
        scratch_shapes=[pltpu.VMEM((P1_TM, D_MODEL), BF16)],
        compiler_params=_params(("parallel", "arbitrary"), 48),
        name="norm_proj",
    )(x2, ln1_g, w1, b1, wr)


GM_TM = 512


def _gmlp_kernel(u_ref, v_ref, g0_ref, lng_ref, lnb_ref, ws_ref, bias_ref, wpa_ref, o_ref, vb_scr, a_scr):
    v = v_ref[...].astype(F32)
    mu = jnp.mean(v, axis=-1, keepdims=True)
    c = v - mu
    var = jnp.mean(c * c, axis=-1, keepdims=True)
    vb_scr[...] = (c * lax.rsqrt(var + EPS) * lng_ref[...] + lnb_ref[...]).astype(BF16)

    row = lax.broadcasted_iota(jnp.int32, (GM_CHUNK, GM_CHUNK), 0)
    col = lax.broadcasted_iota(jnp.int32, (GM_CHUNK, GM_CHUNK), 1)
    causal = col <= row
    for g in range(GM_GROUPS):
        cols = slice(g * GM_CHUNK, (g + 1) * GM_CHUNK)
        wg = jnp.where(causal, ws_ref[g], jnp.zeros((), BF16))
        for ch in range(GM_TM // GM_CHUNK):
            rows = slice(ch * GM_CHUNK, (ch + 1) * GM_CHUNK)
            mixed = jnp.dot(wg, vb_scr[rows, cols], preferred_element_type=F32) + bias_ref[:, cols]
            a_scr[rows, cols] = (u_ref[rows, cols].astype(F32) * mixed).astype(BF16)

    ma = jnp.dot(a_scr[...], wpa_ref[...], preferred_element_type=F32)
    o_ref[...] = (g0_ref[...].astype(F32) * ma).astype(BF16)


def _gmlp(pg, ln_g, ln_b, ws, bias_full, wpa):
    T = pg.shape[0]
    return pl.pallas_call(
        _gmlp_kernel,
        grid=(T // GM_TM,),
        in_specs=[
            pl.BlockSpec((GM_TM, GM_WIDTH), lambda i: (i, 0)),
            pl.BlockSpec((GM_TM, GM_WIDTH), lambda i: (i, 1)),
            pl.BlockSpec((GM_TM, D_MODEL), lambda i: (i, PROJ_COLS // D_MODEL)),
            _const_spec((1, GM_WIDTH)),
            _const_spec((1, GM_WIDTH)),
            _const_spec((GM_GROUPS, GM_CHUNK, GM_CHUNK)),
            _const_spec((GM_CHUNK, GM_WIDTH)),
            _const_spec((GM_WIDTH, D_MODEL)),
        ],
        out_specs=pl.BlockSpec((GM_TM, D_MODEL), lambda i: (i, 0)),
        out_shape=jax.ShapeDtypeStruct((T, D_MODEL), BF16),
        scratch_shapes=[pltpu.VMEM((GM_TM, GM_WIDTH), BF16), pltpu.VMEM((GM_TM, GM_WIDTH), BF16)],
        compiler_params=_params(("parallel",), 40),
        name="gmlp",
    )(pg, pg, pg, ln_g, ln_b, ws, bias_full, wpa)


QKV_TM = 1024
QKV_HEADS = 2
HEAD_BLK = 4
FA_T = 512
FA_QS = 256


def _rope(t, cos, sin):
    return t * cos + pltpu.roll(t, QK_ROPE, 1) * sin


def _row_sum_all_lanes(sq, weights):
    hi = sq.astype(BF16)
    lo = (sq - hi.astype(F32)).astype(BF16)
    w = weights.astype(BF16)
    return (jnp.dot(hi, w, preferred_element_type=F32) + jnp.dot(lo, w, preferred_element_type=F32))


def _mla_qkv_kernel(cq_ref, ckv_ref, kr_ref, pos_ref, gcq_ref, gckv_ref, wq_ref, wkv_ref, gq_ref, gk_ref,
                    cst_ref, sumw_ref, q_ref, k_ref, vt_ref, cqn_scr, ckvn_scr, cos_scr, sin_scr, krsq_scr):
    hb = pl.program_id(1)
    w_nope = sumw_ref[:QK_NOPE, :]
    w_rope = sumw_ref[QK_NOPE:, :]

    @pl.when(hb == 0)
    def _():
        cq = cq_ref[...].astype(F32)
        ms = jnp.mean(cq * cq, axis=-1, keepdims=True)
        cqn_scr[...] = (cq * lax.rsqrt(ms + EPS) * gcq_ref[...]).astype(BF16)
        ckv = ckv_ref[...].astype(F32)
        ms = jnp.mean(ckv * ckv, axis=-1, keepdims=True)
        ckvn_scr[...] = (ckv * lax.rsqrt(ms + EPS) * gckv_ref[...]).astype(BF16)
        ang = pos_ref[...].astype(F32) * cst_ref[0:1, :]
        cos_scr[...] = jnp.cos(ang) * cst_ref[1:2, :]
        sin_scr[...] = jnp.sin(ang) * cst_ref[2:3, :]
        kr = kr_ref[...]
        krsq_scr[...] = _row_sum_all_lanes(kr * kr, w_rope)

    cos = cos_scr[...]
    sin = sin_scr[...]
    inv_dim = 1.0 / QK_DIM
    kr = kr_ref[...]

    yq2 = jnp.dot(cqn_scr[...], wq_ref[...], preferred_element_type=F32)
    ykv2 = jnp.dot(ckvn_scr[...], wkv_ref[...], preferred_element_type=F32)
    for hh in range(QKV_HEADS):
        yq = yq2[:, hh * QK_PAD:(hh + 1) * QK_PAD]
        qn = yq[:, :QK_NOPE]
        qt = yq[:, QK_NOPE:]
        ssq = _row_sum_all_lanes(qn * qn, w_nope) + _row_sum_all_lanes(qt * qt, w_rope)
        rs = lax.rsqrt(ssq * inv_dim + EPS) * (MLA_SCALE * LOG2_E)
        q_ref[hh, :, :QK_NOPE] = (qn * rs * gq_ref[:, :QK_NOPE]).astype(BF16)
        q_ref[hh, :, QK_NOPE:] = _rope(qt * rs * gq_ref[:, QK_NOPE:], cos, sin).astype(BF16)

        ykv = ykv2[:, hh * (QK_NOPE + V_DIM):(hh + 1) * (QK_NOPE + V_DIM)]
        kn = ykv[:, :QK_NOPE]
        ssq = _row_sum_all_lanes(kn * kn, w_nope) + krsq_scr[...]
        rs = lax.rsqrt(ssq * inv_dim + EPS)
        k_ref[hh, :, :QK_NOPE] = (kn * rs * gk_ref[:, :QK_NOPE]).astype(BF16)
        k_ref[hh, :, QK_NOPE:] = _rope(kr * rs * gk_ref[:, QK_NOPE:], cos, sin).astype(BF16)
        vv = ykv[:, QK_NOPE:]
        for c in range(QKV_TM // FA_T):
            vt_ref[hh, c] = vv[c * FA_T:(c + 1) * FA_T, :].T.astype(BF16)


def _mla_qkv(pg, kr, pos, gcq, gckv, wq, wkv, gq, gk, cst, B, S):
    T = pg.shape[0]
    sumw = jnp.concatenate([jnp.ones((QK_NOPE, LANES), F32), jnp.full((QK_PAD - QK_NOPE, LANES), 0.5, F32)])
    spb = S // QKV_TM
    cpt = QKV_TM // FA_T
    head_spec = lambda w: pl.BlockSpec((None, QKV_HEADS, QKV_TM, w), lambda i, h: (i // spb, h, i % spb, 0))
    return pl.pallas_call(
        _mla_qkv_kernel,
        grid=(T // QKV_TM, MLA_HEADS // QKV_HEADS),
        in_specs=[
            pl.BlockSpec((QKV_TM, Q_LORA), lambda i, h: (i, 2 * GM_WIDTH // Q_LORA)),
            pl.BlockSpec((QKV_TM, KV_LORA), lambda i, h: (i, 2 * GM_WIDTH // KV_LORA + 1)),
            pl.BlockSpec((QKV_TM, LANES), lambda i, h: (i, 0)),
            pl.BlockSpec((QKV_TM, 1), lambda i, h: (i, 0)),
            pl.BlockSpec((1, Q_LORA), lambda i, h: (0, 0)),
            pl.BlockSpec((1, KV_LORA), lambda i, h: (0, 0)),
            pl.BlockSpec((Q_LORA, QKV_HEADS * QK_PAD), lambda i, h: (0, h)),
            pl.BlockSpec((KV_LORA, QKV_HEADS * (QK_NOPE + V_DIM)), lambda i, h: (0, h)),
            pl.BlockSpec((1, QK_PAD), lambda i, h: (0, 0)),
            pl.BlockSpec((1, QK_PAD), lambda i, h: (0, 0)),
            pl.BlockSpec((8, LANES), lambda i, h: (0, 0)),
            pl.BlockSpec((QK_PAD, LANES), lambda i, h: (0, 0)),
        ],
        out_specs=[
            head_spec(QK_PAD),
            head_spec(QK_PAD),
            pl.BlockSpec((None, QKV_HEADS, cpt, V_DIM, FA_T), lambda i, h: (i // spb, h, i % spb, 0, 0)),
        ],
        out_shape=[
            jax.ShapeDtypeStruct((B, MLA_HEADS, S, QK_PAD), BF16),
            jax.ShapeDtypeStruct((B, MLA_HEADS, S, QK_PAD), BF16),
            jax.ShapeDtypeStruct((B, MLA_HEADS, S // FA_T, V_DIM, FA_T), BF16),
        ],
        scratch_shapes=[
            pltpu.VMEM((QKV_TM, Q_LORA), BF16),
            pltpu.VMEM((QKV_TM, KV_LORA), BF16),
            pltpu.VMEM((QKV_TM, LANES), F32),
            pltpu.VMEM((QKV_TM, LANES), F32),
            pltpu.VMEM((QKV_TM, LANES), F32),
        ],
        compiler_params=_params(("parallel", "arbitrary"), 48),
        name="mla_qkv",
    )(pg, pg, kr, pos, gcq, gckv, wq, wkv, gq, gk, cst, sumw)


def _flash_kernel(q_ref, k_ref, vt_ref, o_ref, m_scr, l_scr, acc_scr):
    qi = pl.program_id(2)
    m_scr[...] = jnp.full_like(m_scr, NEG_INF)
    l_scr[...] = jnp.zeros_like(l_scr)
    acc_scr[...] = jnp.zeros_like(acc_scr)

    def step(kb, diagonal):
        ks = pl.ds(pl.multiple_of(kb * FA_T, FA_T), FA_T)
        chains = [(hh, qs) for hh in range(HEAD_BLK) for qs in range(FA_T // FA_QS)]
        scores = []
        for hh, qs in chains:
            qc = slice(qs * FA_QS, (qs + 1) * FA_QS)
            st = lax.dot_general(k_ref[hh, ks, :], q_ref[hh, qc, :], (((1,), (1,)), ((), ())),
                                 preferred_element_type=F32)
            if diagonal:
                krow = lax.broadcasted_iota(jnp.int32, (FA_T, FA_QS), 0)
                qcol = lax.broadcasted_iota(jnp.int32, (FA_T, FA_QS), 1) + qs * FA_QS
                st = jnp.where(krow <= qcol, st, NEG_INF)
            scores.append(st)
        probs = []
        for (hh, qs), st in zip(chains, scores):
            qc = slice(qs * FA_QS, (qs + 1) * FA_QS)
            m_prev = m_scr[hh, :, qc]
            m_new = jnp.maximum(m_prev, jnp.max(st, axis=0, keepdims=True))
            alpha = jnp.exp2(m_prev - m_new)
            p = jnp.exp2(st - m_new)
            l_scr[hh, :, qc] = alpha * l_scr[hh, :, qc] + jnp.sum(p, axis=0, keepdims=True)
            m_scr[hh, :, qc] = m_new
            probs.append((alpha, p.astype(BF16)))
        for (hh, qs), (alpha, p) in zip(chains, probs):
            qc = slice(qs * FA_QS, (qs + 1) * FA_QS)
            acc_scr[hh, :, qc] = alpha * acc_scr[hh, :, qc] + jnp.dot(vt_ref[hh, kb], p,
                                                                      preferred_element_type=F32)

    def body(kb, carry):
        step(kb, False)
        return carry

    lax.fori_loop(0, qi, body, 0)
    step(qi, True)
    for hh in range(HEAD_BLK):
        o_ref[:, hh * V_DIM:(hh + 1) * V_DIM] = (acc_scr[hh] / l_scr[hh]).T.astype(BF16)


def _flash(q, k, vt):
    B, H, S, _ = q.shape
    nq = S // FA_T
    return pl.pallas_call(
        _flash_kernel,
        grid=(B, H // HEAD_BLK, nq),
        in_specs=[
            pl.BlockSpec((None, HEAD_BLK, FA_T, QK_PAD), lambda b, h, i: (b, h, i, 0)),
            pl.BlockSpec((None, HEAD_BLK, S, QK_PAD), lambda b, h, i: (b, h, 0, 0)),
            pl.BlockSpec((None, HEAD_BLK, nq, V_DIM, FA_T), lambda b, h, i: (b, h, 0, 0, 0)),
        ],
        out_specs=pl.BlockSpec((FA_T, HEAD_BLK * V_DIM), lambda b, h, i: (b * nq + i, h)),
        out_shape=jax.ShapeDtypeStruct((B * S, H * V_DIM), BF16),
        scratch_shapes=[
            pltpu.VMEM((HEAD_BLK, 1, FA_T), F32),
            pltpu.VMEM((HEAD_BLK, 1, FA_T), F32),
            pltpu.VMEM((HEAD_BLK, V_DIM, FA_T), F32),
        ],
        compiler_params=_params(("parallel", "parallel", "arbitrary"), 40),
        name="mla_attention",
    )(q, k, vt)


def _mem_kv_kernel(mem_ref, g_ref, wk_ref, wv_ref, kng_ref, k_ref, v_ref):
    m = mem_ref[...]
    ms = jnp.mean(m * m, axis=-1, keepdims=True)
    mb = (m * lax.rsqrt(ms + EPS) * g_ref[...]).astype(BF16)
    kk = jnp.dot(mb, wk_ref[...], preferred_element_type=F32)
    for h in range(MEM_HEADS):
        cols = slice(h * MEM_HEAD_DIM, (h + 1) * MEM_HEAD_DIM)
        kh = kk[:, cols]
        ms = jnp.mean(kh * kh, axis=-1, keepdims=True)
        k_ref[:, cols] = (kh * lax.rsqrt(ms + EPS) * kng_ref[...]).astype(BF16)
    v_ref[...] = jnp.dot(mb, wv_ref[...], preferred_element_type=F32).astype(BF16)


def _mem_kv(mem, g, wk, wv, kng):
    B = mem.shape[0]
    return pl.pallas_call(
        _mem_kv_kernel,
        grid=(B,),
        in_specs=[
            pl.BlockSpec((None, MEM_LEN, D_MODEL), lambda b: (b, 0, 0)),
            _const_spec((1, D_MODEL)),
            _const_spec((D_MODEL, MEM_WIDTH)),
            _const_spec((D_MODEL, MEM_WIDTH)),
            _const_spec((1, MEM_HEAD_DIM)),
        ],
        out_specs=[
            pl.BlockSpec((None, MEM_LEN, MEM_WIDTH), lambda b: (b, 0, 0)),
            pl.BlockSpec((None, MEM_LEN, MEM_WIDTH), lambda b: (b, 0, 0)),
        ],
        out_shape=[
            jax.ShapeDtypeStruct((B, MEM_LEN, MEM_WIDTH), BF16),
            jax.ShapeDtypeStruct((B, MEM_LEN, MEM_WIDTH), BF16),
        ],
        compiler_params=_params(("parallel",), 32),
        name="mem_kv",
    )(mem, g, wk, wv, kng)


MA_TM = 512


def _mem_attn_kernel(qm_ref, km_ref, vm_ref, g2_ref, m0_ref, qng_ref, wpc_ref, o_ref, c_scr):
    for h in range(MEM_HEADS):
        cols = slice(h * MEM_HEAD_DIM, (h + 1) * MEM_HEAD_DIM)
        qh = qm_ref[:, cols].astype(F32)
        ms = jnp.mean(qh * qh, axis=-1, keepdims=True)
        qn = (qh * (lax.rsqrt(ms + EPS) * MEM_SCALE) * qng_ref[...]).astype(BF16)
        s = lax.dot_general(qn, km_ref[:, cols], (((1,), (1,)), ((), ())), preferred_element_type=F32)
        e = jnp.exp(s - jnp.max(s, axis=-1, keepdims=True))
        p = (e / jnp.sum(e, axis=-1, keepdims=True)).astype(BF16)
        c_scr[:, cols] = jnp.dot(p, vm_ref[:, cols], preferred_element_type=F32).astype(BF16)
    mc = jnp.dot(c_scr[...], wpc_ref[...], preferred_element_type=F32)
    o_ref[...] = (m0_ref[...].astype(F32) + g2_ref[...].astype(F32) * mc).astype(BF16)


def _mem_attn(pg, km, vm, m0, qng, wpc, S):
    T = pg.shape[0]
    spb = S // MA_TM
    return pl.pallas_call(
        _mem_attn_kernel,
        grid=(T // MA_TM,),
        in_specs=[
            pl.BlockSpec((MA_TM, MEM_WIDTH), lambda i: (i, (2 * GM_WIDTH + Q_LORA + KV_LORA) // MEM_WIDTH)),
            pl.BlockSpec((None, MEM_LEN, MEM_WIDTH), lambda i: (i // spb, 0, 0)),
            pl.BlockSpec((None, MEM_LEN, MEM_WIDTH), lambda i: (i // spb, 0, 0)),
            pl.BlockSpec((MA_TM, D_MODEL), lambda i: (i, PROJ_COLS // D_MODEL + 2)),
            pl.BlockSpec((MA_TM, D_MODEL), lambda i: (i, 0)),
            _const_spec((1, MEM_HEAD_DIM)),
            _const_spec((MEM_WIDTH, D_MODEL)),
        ],
        out_specs=pl.BlockSpec((MA_TM, D_MODEL), lambda i: (i, 0)),
        out_shape=jax.ShapeDtypeStruct((T, D_MODEL), BF16),
        scratch_shapes=[pltpu.VMEM((MA_TM, MEM_WIDTH), BF16)],
        compiler_params=_params(("parallel",), 40),
        name="mem_attention",
    )(pg, km, vm, pg, m0, qng, wpc)


MG_TM = 512


def _pack_bf16_pair(a, b):
    hi = pltpu.bitcast(a.astype(BF16).astype(F32), jnp.uint32)
    lo = pltpu.bitcast(b.astype(BF16).astype(F32), jnp.uint32)
    return hi | (lo >> 16)


def _unpack_bf16_pair(p):
    hi = pltpu.bitcast(p & jnp.uint32(0xFFFF0000), F32)
    lo = pltpu.bitcast(p << 16, F32)
    return hi, lo


def _attn_proj_kernel(b_ref, g1_ref, m1_ref, wpb_ref, o_ref):
    mb = jnp.dot(b_ref[...], wpb_ref[...], preferred_element_type=F32)
    o_ref[...] = (m1_ref[...].astype(F32) + g1_ref[...].astype(F32) * mb).astype(BF16)


def _attn_proj(b_attn, pg, m1, wpb):
    T = m1.shape[0]
    row = pl.BlockSpec((MG_TM, D_MODEL), lambda i: (i, 0))
    return pl.pallas_call(
        _attn_proj_kernel,
        grid=(T // MG_TM,),
        in_specs=[
            row,
            pl.BlockSpec((MG_TM, D_MODEL), lambda i: (i, PROJ_COLS // D_MODEL + 1)),
            row,
            _const_spec((MLA_HEADS * V_DIM, D_MODEL)),
        ],
        out_specs=row,
        out_shape=jax.ShapeDtypeStruct((T, D_MODEL), BF16),
        compiler_params=_params(("parallel",), 40),
        name="attn_proj",
    )(b_attn, pg, m1, wpb)


def _merge_kernel(mg_ref, x_ref, wo_ref, ln2_ref, rw_ref, rb_ref,
                  x1_ref, h2p_ref, code_ref, gate_ref, cnt_ref, cnt_scr):
    @pl.when(pl.program_id(0) == 0)
    def _():
        cnt_scr[...] = jnp.zeros_like(cnt_scr)

    x1 = x_ref[...] + jnp.dot(mg_ref[...], wo_ref[...], preferred_element_type=F32)
    x1_ref[...] = x1
    ms = jnp.mean(x1 * x1, axis=-1, keepdims=True)
    h2 = x1 * lax.rsqrt(ms + EPS) * ln2_ref[...]
    h2p_ref[...] = _pack_bf16_pair(h2[:, :D_MODEL // 2], h2[:, D_MODEL // 2:])

    logits = jnp.dot(h2.astype(BF16), rw_ref[...], preferred_element_type=F32)
    lane = lax.broadcasted_iota(jnp.int32, (MG_TM, LANES), 1)
    work = jnp.where(lane < N_EXPERTS, logits + rb_ref[...], NEG_INF)
    earlier = (lax.broadcasted_iota(jnp.int32, (MG_TM, MG_TM), 1)
               < lax.broadcasted_iota(jnp.int32, (MG_TM, MG_TM), 0)).astype(BF16)
    base = cnt_scr[...]
    code_out = jnp.zeros((MG_TM, LANES), jnp.int32)
    val_out = jnp.zeros((MG_TM, LANES), F32)
    top = None
    denom = jnp.zeros((MG_TM, 1), F32)
    for k in range(TOP_K):
        mk = jnp.max(work, axis=-1, keepdims=True)
        ik = jnp.min(jnp.where(work == mk, lane, LANES), axis=-1, keepdims=True)
        hit = lane == ik
        work = jnp.where(hit, NEG_INF, work)
        if top is None:
            top = mk
        ek = jnp.exp(mk - top)
        denom = denom + ek
        val_out = jnp.where(lane == k, ek, val_out)
        onehot = hit.astype(BF16)
        prefix = jnp.dot(earlier, onehot, preferred_element_type=F32) + base
        rank = jnp.sum(jnp.where(hit, prefix, 0.0), axis=-1, keepdims=True).astype(jnp.int32)
        code_out = jnp.where(lane == k, rank * N_EXPERTS + ik, code_out)
        base = base + jnp.sum(onehot.astype(F32), axis=0, keepdims=True)
    code_ref[...] = code_out
    gate_ref[...] = val_out / denom
    cnt_scr[...] = base
    cnt_ref[...] = base


def _merge(merged, x2, wo, ln2, rw, rb):
    T = x2.shape[0]
    row = lambda w: pl.BlockSpec((MG_TM, w), lambda i: (i, 0))
    return pl.pallas_call(
        _merge_kernel,
        grid=(T // MG_TM,),
        in_specs=[
            row(D_MODEL),
            row(D_MODEL),
            _const_spec((D_MODEL, D_MODEL)),
            _const_spec((1, D_MODEL)),
            _const_spec((D_MODEL, LANES)),
            _const_spec((1, LANES)),
        ],
        out_specs=[row(D_MODEL), row(D_MODEL // 2), row(LANES), row(LANES),
                   pl.BlockSpec((1, LANES), lambda i: (0, 0))],
        out_shape=[
            jax.ShapeDtypeStruct((T, D_MODEL), F32),
            jax.ShapeDtypeStruct((T, D_MODEL // 2), jnp.uint32),
            jax.ShapeDtypeStruct((T, LANES), jnp.int32),
            jax.ShapeDtypeStruct((T, LANES), F32),
            jax.ShapeDtypeStruct((1, LANES), F32),
        ],
        scratch_shapes=[pltpu.VMEM((1, LANES), F32)],
        compiler_params=_params(("arbitrary",), 56),
        name="merge_router",
    )(merged, x2, wo, ln2, rw, rb)


DP_TM = 256
DMA_UNROLL = 8


def _row_copy(src, src_row, dst, dst_row, sem):
    return pltpu.make_async_copy(src.at[pl.ds(src_row, 1), :], dst.at[pl.ds(dst_row, 1), :], sem)


def _dispatch_kernel(dest_ref, h2p_ref, xs_in, xs_hbm, sem):
    del xs_in

    def issue(t, carry):
        src = h2p_ref.at[pl.ds(t, 1), :]
        for k in range(TOP_K):
            row = dest_ref[0, t * TOP_K + k]
            pltpu.make_async_copy(src, xs_hbm.at[pl.ds(row, 1), :], sem).start()
        return carry

    lax.fori_loop(0, DP_TM, issue, 0, unroll=DMA_UNROLL // TOP_K)
    all_rows = xs_hbm.at[pl.ds(0, DP_TM * TOP_K), :]
    pltpu.make_async_copy(all_rows, all_rows, sem).wait()


def _dispatch(dest3, h2p, xs_zero):
    T = h2p.shape[0]
    return pl.pallas_call(
        _dispatch_kernel,
        grid=(T // DP_TM,),
        in_specs=[
            pl.BlockSpec((None, 1, DP_TM * TOP_K), lambda i: (i, 0, 0), memory_space=pltpu.SMEM),
            pl.BlockSpec((DP_TM, D_MODEL // 2), lambda i: (i, 0)),
            pl.BlockSpec(memory_space=pl.ANY),
        ],
        out_specs=pl.BlockSpec(memory_space=pl.ANY),
        out_shape=jax.ShapeDtypeStruct(xs_zero.shape, xs_zero.dtype),
        scratch_shapes=[pltpu.SemaphoreType.DMA(())],
        input_output_aliases={2: 0},
        compiler_params=_params(("arbitrary",), 32),
        name="moe_dispatch",
    )(dest3, h2p, xs_zero)


GROUP_SUB = 4
GROUP_ROWS = GROUP_SUB * ROW_BLOCK
FF_TILE = 512
N_FF_TILES = D_FF // FF_TILE
DOWN_CHUNK = 512


def _expert_kernel(ge_ref, gs_ref, gn_ref, xs_in, wg_ref, wu_ref, bg_ref, bu_ref, wd_ref, bd_ref,
                   rows_hbm, xwin, x_scr, acc_scr, stage, wsem, sem):
    del ge_ref, xs_in
    g = pl.program_id(0)
    f = pl.program_id(1)
    ns = gn_ref[g]
    half = D_MODEL // 2
    slot = g & 1

    def win_copy(group, s):
        rows = pl.ds(gs_ref[group] * ROW_BLOCK, GROUP_ROWS)
        return pltpu.make_async_copy(rows_hbm.at[rows, :], xwin.at[s], wsem.at[s])

    def out_copy(s, start_block):
        rows = pl.ds((start_block + s) * ROW_BLOCK, ROW_BLOCK)
        return pltpu.make_async_copy(stage.at[s], rows_hbm.at[rows, :], sem.at[s])

    @pl.when((f == 0) & (ns > 0))
    def _():
        @pl.when(g == 0)
        def _():
            win_copy(0, 0).start()

        win_copy(g, slot).wait()
        @pl.when(g + 1 < pl.num_programs(0))
        def _():
            @pl.when(gn_ref[g + 1] > 0)
            def _():
                win_copy(g + 1, 1 - slot).start()

        hi, lo = _unpack_bf16_pair(xwin[slot])
        x_scr[:, :half] = hi.astype(BF16)
        x_scr[:, half:] = lo.astype(BF16)
        acc_scr[...] = jnp.broadcast_to(bd_ref[...], acc_scr.shape)

    for n in range(1, GROUP_SUB + 1):
        @pl.when(ns == n)
        def _(n=n):
            m = n * ROW_BLOCK
            x = x_scr[:m, :]
            gate = jnp.dot(x, wg_ref[...].astype(BF16), preferred_element_type=F32) + bg_ref[...]
            up = jnp.dot(x, wu_ref[...].astype(BF16), preferred_element_type=F32) + bu_ref[...]
            gate = jnp.minimum(gate, SWIGLU_LIMIT)
            up = jnp.clip(up, -SWIGLU_LIMIT, SWIGLU_LIMIT)
            glu = gate * jax.nn.sigmoid(gate * SWIGLU_ALPHA)
            act = ((up + 1.0) * glu).astype(BF16)
            for c in range(D_MODEL // DOWN_CHUNK):
                cols = slice(c * DOWN_CHUNK, (c + 1) * DOWN_CHUNK)
                acc_scr[:m, cols] += jnp.dot(act, wd_ref[:, cols].astype(BF16), preferred_element_type=F32)

    @pl.when(f == N_FF_TILES - 1)
    def _():
        @pl.when(g > 0)
        def _():
            prev = gn_ref[g - 1]
            for s in range(GROUP_SUB):
                @pl.when(s < prev)
                def _(s=s):
                    out_copy(s, 0).wait()

        for s in range(GROUP_SUB):
            @pl.when(s < ns)
            def _(s=s):
                y = acc_scr[s * ROW_BLOCK:(s + 1) * ROW_BLOCK, :]
                stage[s] = _pack_bf16_pair(y[:, :half], y[:, half:])
                out_copy(s, gs_ref[g]).start()

        @pl.when(g == pl.num_programs(0) - 1)
        def _():
            for s in range(GROUP_SUB):
                @pl.when(s < ns)
                def _(s=s):
                    out_copy(s, 0).wait()


def _experts(g_exp, g_start, g_nsub, xs, wgu, bgu, wd, bd):
    def ff(f, gn, g):
        return jnp.where(gn[g] > 0, f, N_FF_TILES - 1)

    grid_spec = pltpu.PrefetchScalarGridSpec(
        num_scalar_prefetch=3,
        grid=(g_exp.shape[0], N_FF_TILES),
        in_specs=[
            pl.BlockSpec(memory_space=pl.ANY),
            pl.BlockSpec((None, D_MODEL, FF_TILE), lambda g, f, ge, gs, gn: (ge[g], 0, ff(f, gn, g))),
            pl.BlockSpec((None, D_MODEL, FF_TILE),
                         lambda g, f, ge, gs, gn: (ge[g], 0, N_FF_TILES + ff(f, gn, g))),
            pl.BlockSpec((None, 1, FF_TILE), lambda g, f, ge, gs, gn: (ge[g], 0, ff(f, gn, g))),
            pl.BlockSpec((None, 1, FF_TILE), lambda g, f, ge, gs, gn: (ge[g], 0, N_FF_TILES + ff(f, gn, g))),
            pl.BlockSpec((None, FF_TILE, D_MODEL), lambda g, f, ge, gs, gn: (ge[g], ff(f, gn, g), 0)),
            pl.BlockSpec((None, 1, D_MODEL), lambda g, f, ge, gs, gn: (ge[g], 0, 0)),
        ],
        out_specs=pl.BlockSpec(memory_space=pl.ANY),
        scratch_shapes=[
            pltpu.VMEM((2, GROUP_ROWS, D_MODEL // 2), jnp.uint32),
            pltpu.VMEM((GROUP_ROWS, D_MODEL), BF16),
            pltpu.VMEM((GROUP_ROWS, D_MODEL), F32),
            pltpu.VMEM((GROUP_SUB, ROW_BLOCK, D_MODEL // 2), jnp.uint32),
            pltpu.SemaphoreType.DMA((2,)),
            pltpu.SemaphoreType.DMA((GROUP_SUB,)),
        ],
    )
    return pl.pallas_call(
        _expert_kernel,
        grid_spec=grid_spec,
        out_shape=jax.ShapeDtypeStruct(xs.shape, jnp.uint32),
        input_output_aliases={3: 0},
        compiler_params=_params(("arbitrary", "arbitrary"), 58),
        name="moe_experts",
    )(g_exp, g_start, g_nsub, xs, wgu, wgu, bgu, bgu, wd, bd)


CB_TM = 128


def _combine_kernel(dest_ref, dest_next_ref, ys_hbm, x1_ref, gate_ref, o_ref, buf, sem):
    i = pl.program_id(0)
    slot = i & 1

    def gather_tile(dest, s):
        def issue(t, carry):
            for k in range(TOP_K):
                row = dest[0, t * TOP_K + k]
                _row_copy(ys_hbm, row, buf.at[s, k], t, sem.at[s]).start()
            return carry

        lax.fori_loop(0, CB_TM, issue, 0, unroll=DMA_UNROLL // TOP_K)

    @pl.when(i == 0)
    def _():
        gather_tile(dest_ref, 0)

    @pl.when(i + 1 < pl.num_programs(0))
    def _():
        gather_tile(dest_next_ref, 1 - slot)

    pltpu.make_async_copy(buf.at[slot], buf.at[slot], sem.at[slot]).wait()

    half = D_MODEL // 2
    gates = gate_ref[...]
    out_hi = x1_ref[:, :half]
    out_lo = x1_ref[:, half:]
    for k in range(TOP_K):
        hi, lo = _unpack_bf16_pair(buf[slot, k])
        g = gates[:, k:k + 1]
        out_hi = out_hi + g * hi
        out_lo = out_lo + g * lo
    o_ref[:, :half] = out_hi
    o_ref[:, half:] = out_lo


def _combine(dest3, ys, x1, gates):
    T = x1.shape[0]
    n = T // CB_TM
    dest_spec = lambda f: pl.BlockSpec((None, 1, CB_TM * TOP_K), f, memory_space=pltpu.SMEM)
    return pl.pallas_call(
        _combine_kernel,
        grid=(n,),
        in_specs=[
            dest_spec(lambda i: (i, 0, 0)),
            dest_spec(lambda i: (jnp.minimum(i + 1, n - 1), 0, 0)),
            pl.BlockSpec(memory_space=pl.ANY),
            pl.BlockSpec((CB_TM, D_MODEL), lambda i: (i, 0)),
            pl.BlockSpec((CB_TM, LANES), lambda i: (i, 0)),
        ],
        out_specs=pl.BlockSpec((CB_TM, D_MODEL), lambda i: (i, 0)),
        out_shape=jax.ShapeDtypeStruct((T, D_MODEL), F32),
        scratch_shapes=[pltpu.VMEM((2, TOP_K, CB_TM, D_MODEL // 2), jnp.uint32),
                        pltpu.SemaphoreType.DMA((2,))],
        compiler_params=_params(("arbitrary",), 32),
        name="moe_combine",
    )(dest3, dest3, ys, x1, gates)


def _group_tables(counts_f32, max_groups):
    i32 = jnp.int32
    counts = counts_f32[0, :N_EXPERTS].astype(i32)
    nb = (counts + ROW_BLOCK - 1) // ROW_BLOCK
    ng = (nb + GROUP_SUB - 1) // GROUP_SUB
    upto = jnp.arange(N_EXPERTS)[None, :] <= jnp.arange(N_EXPERTS)[:, None]
    pad_ends = jnp.sum(jnp.where(upto, nb[None, :], 0), axis=1) * ROW_BLOCK
    pad_starts = pad_ends - nb * ROW_BLOCK
    g_ends = jnp.sum(jnp.where(upto, ng[None, :], 0), axis=1)
    g_starts = g_ends - ng
    n_groups = g_ends[-1]

    g = jnp.arange(max_groups, dtype=i32)
    gg = jnp.minimum(g, n_groups - 1)
    e = jnp.minimum(jnp.sum((g_ends[None, :] <= gg[:, None]).astype(i32), axis=1), N_EXPERTS - 1)
    pick = e[:, None] == jnp.arange(N_EXPERTS)[None, :]
    take = lambda table: jnp.sum(jnp.where(pick, table[None, :], 0), axis=1)
    nb_g, ng_g = take(nb), jnp.maximum(take(ng), 1)
    j = gg - take(g_starts)
    base, rem = nb_g // ng_g, nb_g % ng_g
    nsub = jnp.where(g < n_groups, base + (j < rem).astype(i32), 0)
    start_block = take(pad_starts) // ROW_BLOCK + j * base + jnp.minimum(j, rem)
    return pad_starts.astype(i32), e.astype(i32), start_block.astype(i32), nsub.astype(i32)


def _rope_constants():
    lane = np.arange(LANES)
    half = QK_ROPE // 2
    inv = 1.0 / (ROPE_THETA ** (np.arange(0, QK_ROPE, 2, dtype=np.float32) / QK_ROPE))
    cst = np.zeros((8, LANES), np.float32)
    cst[0, :QK_ROPE] = inv.astype(np.float32)[lane[:QK_ROPE] % half]
    cst[1, :QK_ROPE] = 1.0
    cst[2, :half] = -1.0
    cst[2, half:QK_ROPE] = 1.0
    return jnp.asarray(cst)


def _swap_halves(a):
    half = QK_ROPE // 2
    return jnp.concatenate([a[..., half:], a[..., :half]], axis=-1)


def kernel(x, mem, positions, ln1_g, w_in, w_gate, b_gate, gmlp_ln_g, gmlp_ln_b, gmlp_ws, gmlp_bs, w_pa,
           mla_cq_g, mla_w_uq, mla_ckv_g, mla_w_ukv, mla_qn_g, mla_kn_g, w_pb, mem_ln_g, mem_w_k, mem_w_v,
           mem_qn_g, mem_kn_g, w_pc, w_o, ln2_g, router_w, router_b, moe_w_gu, moe_b_gu, moe_w_down,
           moe_b_down):
    B, S, D = x.shape
    T = B * S
    x2 = x.reshape(T, D)
    for l in range(ln1_g.shape[0]):
        o_kr = 2 * GM_WIDTH + Q_LORA + KV_LORA
        o_qm = o_kr + QK_ROPE
        wi = w_in[l]
        w1 = jnp.concatenate([wi[:, :o_kr], wi[:, o_qm:], w_gate[l]], axis=1).astype(BF16)
        b1 = jnp.concatenate([jnp.zeros((PROJ_COLS,), F32), b_gate[l]])[None, :]
        w_kr = wi[:, o_kr:o_qm]
        wr = jnp.concatenate([w_kr, _swap_halves(w_kr)], axis=1).astype(BF16)

        wq3 = mla_w_uq[l].reshape(Q_LORA, MLA_HEADS, QK_DIM)
        wq = jnp.concatenate([wq3, _swap_halves(wq3[..., QK_NOPE:])], axis=-1)
        wq = wq.reshape(Q_LORA, MLA_HEADS * QK_PAD).astype(BF16)
        wkv = mla_w_ukv[l].astype(BF16)
        gq = jnp.concatenate([mla_qn_g[l], _swap_halves(mla_qn_g[l][QK_NOPE:])])[None, :]
        gk = jnp.concatenate([mla_kn_g[l], _swap_halves(mla_kn_g[l][QK_NOPE:])])[None, :]

        bias_full = jnp.broadcast_to(gmlp_bs[l].T[:, :, None], (GM_CHUNK, GM_GROUPS, GM_CHUNK))
        bias_full = bias_full.reshape(GM_CHUNK, GM_WIDTH)

        rw = jnp.pad(router_w[l], ((0, 0), (0, LANES - N_EXPERTS))).astype(BF16)
        rb = jnp.pad(router_b[l], (0, LANES - N_EXPERTS))[None, :]

        pg, kr = _norm_proj(x2, ln1_g[l][None, :], w1, b1, wr)
        m0 = _gmlp(pg, gmlp_ln_g[l][None, :], gmlp_ln_b[l][None, :], gmlp_ws[l].astype(BF16), bias_full,
                   w_pa[l].astype(BF16))
        q, k, vt = _mla_qkv(pg, kr, positions.reshape(T, 1), mla_cq_g[l][None, :], mla_ckv_g[l][None, :],
                           wq, wkv, gq, gk, _rope_constants(), B, S)
        b_attn = _flash(q, k, vt)
        km, vm = _mem_kv(mem, mem_ln_g[l][None, :], mem_w_k[l].astype(BF16), mem_w_v[l].astype(BF16),
                         mem_kn_g[l][None, :])
        m1 = _mem_attn(pg, km, vm, m0, mem_qn_g[l][None, :], w_pc[l].astype(BF16), S)
        merged = _attn_proj(b_attn, pg, m1, w_pb[l].astype(BF16))
        x1, h2p, code, gates, counts = _merge(merged, x2, w_o[l].astype(BF16), ln2_g[l][None, :], rw, rb)

        n_rows = T * TOP_K + N_EXPERTS * ROW_BLOCK
        max_groups = n_rows // GROUP_ROWS + N_EXPERTS
        pad_starts, g_exp, g_start, g_nsub = _group_tables(counts, max_groups)
        codes = code[:, :TOP_K]
        pick = (codes & (N_EXPERTS - 1))[..., None] == jnp.arange(N_EXPERTS)
        dest = (codes >> EXPERT_BITS) + jnp.sum(jnp.where(pick, pad_starts, 0), axis=-1)
        xs = _dispatch(dest.reshape(T // DP_TM, 1, DP_TM * TOP_K), h2p,
                       jnp.zeros((n_rows + GROUP_ROWS, D_MODEL // 2), jnp.uint32))
        ys = _experts(g_exp, g_start, g_nsub, xs, moe_w_gu[l], moe_b_gu[l][:, None, :],
                      moe_w_down[l], moe_b_down[l][:, None, :])
        x2 = _combine(dest.reshape(T // CB_TM, 1, CB_TM * TOP_K), ys, x1, gates)
    return x2.reshape(B, S, D)
```

```python
import functools

import numpy as np
import jax
import jax.numpy as jnp
from jax import lax
from jax.experimental import pallas as pl
from jax.experimental.pallas import tpu as pltpu

F32 = jnp.float32
BF16 = jnp.bfloat16

D_MODEL = 2048
GM_WIDTH = 1024
GM_GROUPS = 8
GM_CHUNK = 128
MLA_HEADS = 16
Q_LORA = 512
KV_LORA = 512
QK_NOPE = 128
QK_ROPE = 64
V_DIM = 128
QK_DIM = QK_NOPE + QK_ROPE
QK_PAD = 256
MLA_SCALE = QK_DIM ** -0.5
LOG2_E = 1.4426950408889634
ROPE_THETA = 10000.0
MEM_LEN = 256
MEM_HEADS = 4
MEM_HEAD_DIM = 256
MEM_WIDTH = MEM_HEADS * MEM_HEAD_DIM
MEM_SCALE = MEM_HEAD_DIM ** -0.5
N_EXPERTS = 32
EXPERT_BITS = 5
TOP_K = 4
D_FF = 2048
SWIGLU_LIMIT = 7.0
SWIGLU_ALPHA = 1.702
ROW_BLOCK = 256
EPS = 1e-6
LANES = 128
NEG_INF = float("-inf")

PROJ_COLS = 2 * GM_WIDTH + Q_LORA + KV_LORA + MEM_WIDTH
PG_COLS = PROJ_COLS + 3 * D_MODEL

MIB = 1024 * 1024


def _params(semantics, vmem_mib):
    return pltpu.CompilerParams(dimension_semantics=semantics, vmem_limit_bytes=vmem_mib * MIB)


def _const_spec(shape):
    nd = len(shape)
    return pl.BlockSpec(shape, lambda *_: (0,) * nd, pipeline_mode=pl.Buffered(1))


P1_TM = 1024
P1_TN = 1024


def _norm_proj_kernel(x_ref, g_ref, w_ref, b_ref, wr_ref, o_ref, kr_ref, h_scr, *, n_plain):
    j = pl.program_id(1)

    @pl.when(j == 0)
    def _():
        def body(c, carry):
            rows = pl.ds(pl.multiple_of(c * 128, 128), 128)
            x = x_ref[rows, :]
            ms = jnp.mean(x * x, axis=-1, keepdims=True)
            h_scr[rows, :] = (x * lax.rsqrt(ms + EPS) * g_ref[...]).astype(BF16)
            return carry

        lax.fori_loop(0, P1_TM // 128, body, 0)
        kr_ref[...] = jnp.dot(h_scr[...], wr_ref[...], preferred_element_type=F32)

    acc = jnp.dot(h_scr[...], w_ref[...], preferred_element_type=F32)
    gated = jax.nn.sigmoid(acc + b_ref[...])
    o_ref[...] = jnp.where(j >= n_plain, gated, acc).astype(BF16)


def _norm_proj(x2, ln1_g, w1, b1, wr):
    T = x2.shape[0]
    grid = (T // P1_TM, PG_COLS // P1_TN)
    return pl.pallas_call(
        functools.partial(_norm_proj_kernel, n_plain=PROJ_COLS // P1_TN),
        grid=grid,
        in_specs=[
            pl.BlockSpec((P1_TM, D_MODEL), lambda i, j: (i, 0)),
            pl.BlockSpec((1, D_MODEL), lambda i, j: (0, 0)),
            pl.BlockSpec((D_MODEL, P1_TN), lambda i, j: (0, j)),
            pl.BlockSpec((1, P1_TN), lambda i, j: (0, j)),
            pl.BlockSpec((D_MODEL, LANES), lambda i, j: (0, 0)),
        ],
        out_specs=[
            pl.BlockSpec((P1_TM, P1_TN), lambda i, j: (i, j)),
            pl.BlockSpec((P1_TM, LANES), lambda i, j: (i, 0)),
        ],
        out_shape=[
            jax.ShapeDtypeStruct((T, PG_COLS), BF16),
            jax.ShapeDtypeStruct((T, LANES), F32),
        ],
        scratch_shapes=[pltpu.VMEM((P1_TM, D_MODEL), BF16)],
        compiler_params=_params(("parallel", "arbitrary"), 48),
        name="norm_proj",
    )(x2, ln1_g, w1, b1, wr)


GM_TM = 512


def _gmlp_kernel(u_ref, v_ref, g0_ref, lng_ref, lnb_ref, ws_ref, bias_ref, wpa_ref, o_ref, vb_scr, a_scr):
    v = v_ref[...].astype(F32)
    mu = jnp.mean(v, axis=-1, keepdims=True)
    c = v - mu
    var = jnp.mean(c * c, axis=-1, keepdims=True)
    vb_scr[...] = (c * lax.rsqrt(var + EPS) * lng_ref[...] + lnb_ref[...]).astype(BF16)

    row = lax.broadcasted_iota(jnp.int32, (GM_CHUNK, GM_CHUNK), 0)
    col = lax.broadcasted_iota(jnp.int32, (GM_CHUNK, GM_CHUNK), 1)
    causal = col <= row
    for g in range(GM_GROUPS):
        cols = slice(g * GM_CHUNK, (g + 1) * GM_CHUNK)
        wg = jnp.where(causal, ws_ref[g], jnp.zeros((), BF16))
        for ch in range(GM_TM // GM_CHUNK):
            rows = slice(ch * GM_CHUNK, (ch + 1) * GM_CHUNK)
            mixed = jnp.dot(wg, vb_scr[rows, cols], preferred_element_type=F32) + bias_ref[:, cols]
            a_scr[rows, cols] = (u_ref[rows, cols].astype(F32) * mixed).astype(BF16)

    ma = jnp.dot(a_scr[...], wpa_ref[...], preferred_element_type=F32)
    o_ref[...] = (g0_ref[...].astype(F32) * ma).astype(BF16)


def _gmlp(pg, ln_g, ln_b, ws, bias_full, wpa):
    T = pg.shape[0]
    return pl.pallas_call(
        _gmlp_kernel,
        grid=(T // GM_TM,),
        in_specs=[
            pl.BlockSpec((GM_TM, GM_WIDTH), lambda i: (i, 0)),
            pl.BlockSpec((GM_TM, GM_WIDTH), lambda i: (i, 1)),
            pl.BlockSpec((GM_TM, D_MODEL), lambda i: (i, PROJ_COLS // D_MODEL)),
            _const_spec((1, GM_WIDTH)),
            _const_spec((1, GM_WIDTH)),
            _const_spec((GM_GROUPS, GM_CHUNK, GM_CHUNK)),
            _const_spec((GM_CHUNK, GM_WIDTH)),
            _const_spec((GM_WIDTH, D_MODEL)),
        ],
        out_specs=pl.BlockSpec((GM_TM, D_MODEL), lambda i: (i, 0)),
        out_shape=jax.ShapeDtypeStruct((T, D_MODEL), BF16),
        scratch_shapes=[pltpu.VMEM((GM_TM, GM_WIDTH), BF16), pltpu.VMEM((GM_TM, GM_WIDTH), BF16)],
        compiler_params=_params(("parallel",), 40),
        name="gmlp",
    )(pg, pg, pg, ln_g, ln_b, ws, bias_full, wpa)


QKV_TM = 1024
QKV_HEADS = 2
HEAD_BLK = 4
FA_T = 512
FA_QS = 256


def _rope(t, cos, sin):
    return t * cos + pltpu.roll(t, QK_ROPE, 1) * sin


def _row_sum_all_lanes(sq, weights):
    hi = sq.astype(BF16)
    lo = (sq - hi.astype(F32)).astype(BF16)
    w = weights.astype(BF16)
    return (jnp.dot(hi, w, preferred_element_type=F32) + jnp.dot(lo, w, preferred_element_type=F32))


def _mla_qkv_kernel(cq_ref, ckv_ref, kr_ref, pos_ref, gcq_ref, gckv_ref, wq_ref, wkv_ref, gq_ref, gk_ref,
                    cst_ref, sumw_ref, q_ref, k_ref, vt_ref, cqn_scr, ckvn_scr, cos_scr, sin_scr, krsq_scr):
    hb = pl.program_id(1)
    w_nope = sumw_ref[:QK_NOPE, :]
    w_rope = sumw_ref[QK_NOPE:, :]

    @pl.when(hb == 0)
    def _():
        cq = cq_ref[...].astype(F32)
        ms = jnp.mean(cq * cq, axis=-1, keepdims=True)
        cqn_scr[...] = (cq * lax.rsqrt(ms + EPS) * gcq_ref[...]).astype(BF16)
        ckv = ckv_ref[...].astype(F32)
        ms = jnp.mean(ckv * ckv, axis=-1, keepdims=True)
        ckvn_scr[...] = (ckv * lax.rsqrt(ms + EPS) * gckv_ref[...]).astype(BF16)
        ang = pos_ref[...].astype(F32) * cst_ref[0:1, :]
        cos_scr[...] = jnp.cos(ang) * cst_ref[1:2, :]
        sin_scr[...] = jnp.sin(ang) * cst_ref[2:3, :]
        kr = kr_ref[...]
        krsq_scr[...] = _row_sum_all_lanes(kr * kr, w_rope)

    cos = cos_scr[...]
    sin = sin_scr[...]
    inv_dim = 1.0 / QK_DIM
    kr = kr_ref[...]

    yq2 = jnp.dot(cqn_scr[...], wq_ref[...], preferred_element_type=F32)
    ykv2 = jnp.dot(ckvn_scr[...], wkv_ref[...], preferred_element_type=F32)
    for hh in range(QKV_HEADS):
        yq = yq2[:, hh * QK_PAD:(hh + 1) * QK_PAD]
        qn = yq[:, :QK_NOPE]
        qt = yq[:, QK_NOPE:]
        ssq = _row_sum_all_lanes(qn * qn, w_nope) + _row_sum_all_lanes(qt * qt, w_rope)
        rs = lax.rsqrt(ssq * inv_dim + EPS) * (MLA_SCALE * LOG2_E)
        q_ref[hh, :, :QK_NOPE] = (qn * rs * gq_ref[:, :QK_NOPE]).astype(BF16)
        q_ref[hh, :, QK_NOPE:] = _rope(qt * rs * gq_ref[:, QK_NOPE:], cos, sin).astype(BF16)

        ykv = ykv2[:, hh * (QK_NOPE + V_DIM):(hh + 1) * (QK_NOPE + V_DIM)]
        kn = ykv[:, :QK_NOPE]
        ssq = _row_sum_all_lanes(kn * kn, w_nope) + krsq_scr[...]
        rs = lax.rsqrt(ssq * inv_dim + EPS)
        k_ref[hh, :, :QK_NOPE] = (kn * rs * gk_ref[:, :QK_NOPE]).astype(BF16)
        k_ref[hh, :, QK_NOPE:] = _rope(kr * rs * gk_ref[:, QK_NOPE:], cos, sin).astype(BF16)
        vv = ykv[:, QK_NOPE:]
        for c in range(QKV_TM // FA_T):
            vt_ref[hh, c] = vv[c * FA_T:(c + 1) * FA_T, :].T.astype(BF16)


def _mla_qkv(pg, kr, pos, gcq, gckv, wq, wkv, gq, gk, cst, B, S):
    T = pg.shape[0]
    sumw = jnp.concatenate([jnp.ones((QK_NOPE, LANES), F32), jnp.full((QK_PAD - QK_NOPE, LANES), 0.5, F32)])
    spb = S // QKV_TM
    cpt = QKV_TM // FA_T
    head_spec = lambda w: pl.BlockSpec((None, QKV_HEADS, QKV_TM, w), lambda i, h: (i // spb, h, i % spb, 0))
    return pl.pallas_call(
        _mla_qkv_kernel,
        grid=(T // QKV_TM, MLA_HEADS // QKV_HEADS),
        in_specs=[
            pl.BlockSpec((QKV_TM, Q_LORA), lambda i, h: (i, 2 * GM_WIDTH // Q_LORA)),
            pl.BlockSpec((QKV_TM, KV_LORA), lambda i, h: (i, 2 * GM_WIDTH // KV_LORA + 1)),
            pl.BlockSpec((QKV_TM, LANES), lambda i, h: (i, 0)),
            pl.BlockSpec((QKV_TM, 1), lambda i, h: (i, 0)),
            pl.BlockSpec((1, Q_LORA), lambda i, h: (0, 0)),
            pl.BlockSpec((1, KV_LORA), lambda i, h: (0, 0)),
            pl.BlockSpec((Q_LORA, QKV_HEADS * QK_PAD), lambda i, h: (0, h)),
            pl.BlockSpec((KV_LORA, QKV_HEADS * (QK_NOPE + V_DIM)), lambda i, h: (0, h)),
            pl.BlockSpec((1, QK_PAD), lambda i, h: (0, 0)),
            pl.BlockSpec((1, QK_PAD), lambda i, h: (0, 0)),
            pl.BlockSpec((8, LANES), lambda i, h: (0, 0)),
            pl.BlockSpec((QK_PAD, LANES), lambda i, h: (0, 0)),
        ],
        out_specs=[
            head_spec(QK_PAD),
            head_spec(QK_PAD),
            pl.BlockSpec((None, QKV_HEADS, cpt, V_DIM, FA_T), lambda i, h: (i // spb, h, i % spb, 0, 0)),
        ],
        out_shape=[
            jax.ShapeDtypeStruct((B, MLA_HEADS, S, QK_PAD), BF16),
            jax.ShapeDtypeStruct((B, MLA_HEADS, S, QK_PAD), BF16),
            jax.ShapeDtypeStruct((B, MLA_HEADS, S // FA_T, V_DIM, FA_T), BF16),
        ],
        scratch_shapes=[
            pltpu.VMEM((QKV_TM, Q_LORA), BF16),
            pltpu.VMEM((QKV_TM, KV_LORA), BF16),
            pltpu.VMEM((QKV_TM, LANES), F32),
            pltpu.VMEM((QKV_TM, LANES), F32),
            pltpu.VMEM((QKV_TM, LANES), F32),
        ],
        compiler_params=_params(("parallel", "arbitrary"), 48),
        name="mla_qkv",
    )(pg, pg, kr, pos, gcq, gckv, wq, wkv, gq, gk, cst, sumw)


def _flash_kernel(q_ref, k_ref, vt_ref, o_ref, m_scr, l_scr, acc_scr):
    qi = pl.program_id(2)
    m_scr[...] = jnp.full_like(m_scr, NEG_INF)
    l_scr[...] = jnp.zeros_like(l_scr)
    acc_scr[...] = jnp.zeros_like(acc_scr)

    def step(kb, diagonal):
        ks = pl.ds(pl.multiple_of(kb * FA_T, FA_T), FA_T)
        chains = [(hh, qs) for hh in range(HEAD_BLK) for qs in range(FA_T // FA_QS)]
        scores = []
        for hh, qs in chains:
            qc = slice(qs * FA_QS, (qs + 1) * FA_QS)
            st = lax.dot_general(k_ref[hh, ks, :], q_ref[hh, qc, :], (((1,), (1,)), ((), ())),
                                 preferred_element_type=F32)
            if diagonal:
                krow = lax.broadcasted_iota(jnp.int32, (FA_T, FA_QS), 0)
                qcol = lax.broadcasted_iota(jnp.int32, (FA_T, FA_QS), 1) + qs * FA_QS
                st = jnp.where(krow <= qcol, st, NEG_INF)
            scores.append(st)
        probs = []
        for (hh, qs), st in zip(chains, scores):
            qc = slice(qs * FA_QS, (qs + 1) * FA_QS)
            m_prev = m_scr[hh, :, qc]
            m_new = jnp.maximum(m_prev, jnp.max(st, axis=0, keepdims=True))
            alpha = jnp.exp2(m_prev - m_new)
            p = jnp.exp2(st - m_new)
            l_scr[hh, :, qc] = alpha * l_scr[hh, :, qc] + jnp.sum(p, axis=0, keepdims=True)
            m_scr[hh, :, qc] = m_new
            probs.append((alpha, p.astype(BF16)))
        for (hh, qs), (alpha, p) in zip(chains, probs):
            qc = slice(qs * FA_QS, (qs + 1) * FA_QS)
            acc_scr[hh, :, qc] = alpha * acc_scr[hh, :, qc] + jnp.dot(vt_ref[hh, kb], p,
                                                                      preferred_element_type=F32)

    def body(kb, carry):
        step(kb, False)
        return carry

    lax.fori_loop(0, qi, body, 0)
    step(qi, True)
    for hh in range(HEAD_BLK):
        o_ref[:, hh * V_DIM:(hh + 1) * V_DIM] = (acc_scr[hh] / l_scr[hh]).T.astype(BF16)


def _flash(q, k, vt):
    B, H, S, _ = q.shape
    nq = S // FA_T
    return pl.pallas_call(
        _flash_kernel,
        grid=(B, H // HEAD_BLK, nq),
        in_specs=[
            pl.BlockSpec((None, HEAD_BLK, FA_T, QK_PAD), lambda b, h, i: (b, h, i, 0)),
            pl.BlockSpec((None, HEAD_BLK, S, QK_PAD), lambda b, h, i: (b, h, 0, 0)),
            pl.BlockSpec((None, HEAD_BLK, nq, V_DIM, FA_T), lambda b, h, i: (b, h, 0, 0, 0)),
        ],
        out_specs=pl.BlockSpec((FA_T, HEAD_BLK * V_DIM), lambda b, h, i: (b * nq + i, h)),
        out_shape=jax.ShapeDtypeStruct((B * S, H * V_DIM), BF16),
        scratch_shapes=[
            pltpu.VMEM((HEAD_BLK, 1, FA_T), F32),
            pltpu.VMEM((HEAD_BLK, 1, FA_T), F32),
            pltpu.VMEM((HEAD_BLK, V_DIM, FA_T), F32),
        ],
        compiler_params=_params(("parallel", "parallel", "arbitrary"), 40),
        name="mla_attention",
    )(q, k, vt)


def _mem_kv_kernel(mem_ref, g_ref, wk_ref, wv_ref, kng_ref, k_ref, v_ref):
    m = mem_ref[...]
    ms = jnp.mean(m * m, axis=-1, keepdims=True)
    mb = (m * lax.rsqrt(ms + EPS) * g_ref[...]).astype(BF16)
    kk = jnp.dot(mb, wk_ref[...], preferred_element_type=F32)
    for h in range(MEM_HEADS):
        cols = slice(h * MEM_HEAD_DIM, (h + 1) * MEM_HEAD_DIM)
        kh = kk[:, cols]
        ms = jnp.mean(kh * kh, axis=-1, keepdims=True)
        k_ref[:, cols] = (kh * lax.rsqrt(ms + EPS) * kng_ref[...]).astype(BF16)
    v_ref[...] = jnp.dot(mb, wv_ref[...], preferred_element_type=F32).astype(BF16)


def _mem_kv(mem, g, wk, wv, kng):
    B = mem.shape[0]
    return pl.pallas_call(
        _mem_kv_kernel,
        grid=(B,),
        in_specs=[
            pl.BlockSpec((None, MEM_LEN, D_MODEL), lambda b: (b, 0, 0)),
            _const_spec((1, D_MODEL)),
            _const_spec((D_MODEL, MEM_WIDTH)),
            _const_spec((D_MODEL, MEM_WIDTH)),
            _const_spec((1, MEM_HEAD_DIM)),
        ],
        out_specs=[
            pl.BlockSpec((None, MEM_LEN, MEM_WIDTH), lambda b: (b, 0, 0)),
            pl.BlockSpec((None, MEM_LEN, MEM_WIDTH), lambda b: (b, 0, 0)),
        ],
        out_shape=[
            jax.ShapeDtypeStruct((B, MEM_LEN, MEM_WIDTH), BF16),
            jax.ShapeDtypeStruct((B, MEM_LEN, MEM_WIDTH), BF16),
        ],
        compiler_params=_params(("parallel",), 32),
        name="mem_kv",
    )(mem, g, wk, wv, kng)


MA_TM = 512


def _mem_attn_kernel(qm_ref, km_ref, vm_ref, g2_ref, m0_ref, qng_ref, wpc_ref, o_ref, c_scr):
    for h in range(MEM_HEADS):
        cols = slice(h * MEM_HEAD_DIM, (h + 1) * MEM_HEAD_DIM)
        qh = qm_ref[:, cols].astype(F32)
        ms = jnp.mean(qh * qh, axis=-1, keepdims=True)
        qn = (qh * (lax.rsqrt(ms + EPS) * MEM_SCALE) * qng_ref[...]).astype(BF16)
        s = lax.dot_general(qn, km_ref[:, cols], (((1,), (1,)), ((), ())), preferred_element_type=F32)
        e = jnp.exp(s - jnp.max(s, axis=-1, keepdims=True))
        p = (e / jnp.sum(e, axis=-1, keepdims=True)).astype(BF16)
        c_scr[:, cols] = jnp.dot(p, vm_ref[:, cols], preferred_element_type=F32).astype(BF16)
    mc = jnp.dot(c_scr[...], wpc_ref[...], preferred_element_type=F32)
    o_ref[...] = (m0_ref[...].astype(F32) + g2_ref[...].astype(F32) * mc).astype(BF16)


def _mem_attn(pg, km, vm, m0, qng, wpc, S):
    T = pg.shape[0]
    spb = S // MA_TM
    return pl.pallas_call(
        _mem_attn_kernel,
        grid=(T // MA_TM,),
        in_specs=[
            pl.BlockSpec((MA_TM, MEM_WIDTH), lambda i: (i, (2 * GM_WIDTH + Q_LORA + KV_LORA) // MEM_WIDTH)),
            pl.BlockSpec((None, MEM_LEN, MEM_WIDTH), lambda i: (i // spb, 0, 0)),
            pl.BlockSpec((None, MEM_LEN, MEM_WIDTH), lambda i: (i // spb, 0, 0)),
            pl.BlockSpec((MA_TM, D_MODEL), lambda i: (i, PROJ_COLS // D_MODEL + 2)),
            pl.BlockSpec((MA_TM, D_MODEL), lambda i: (i, 0)),
            _const_spec((1, MEM_HEAD_DIM)),
            _const_spec((MEM_WIDTH, D_MODEL)),
        ],
        out_specs=pl.BlockSpec((MA_TM, D_MODEL), lambda i: (i, 0)),
        out_shape=jax.ShapeDtypeStruct((T, D_MODEL), BF16),
        scratch_shapes=[pltpu.VMEM((MA_TM, MEM_WIDTH), BF16)],
        compiler_params=_params(("parallel",), 40),
        name="mem_attention",
    )(pg, km, vm, pg, m0, qng, wpc)


MG_TM = 512


def _pack_bf16_pair(a, b):
    hi = pltpu.bitcast(a.astype(BF16).astype(F32), jnp.uint32)
    lo = pltpu.bitcast(b.astype(BF16).astype(F32), jnp.uint32)
    return hi | (lo >> 16)


def _unpack_bf16_pair(p):
    hi = pltpu.bitcast(p & jnp.uint32(0xFFFF0000), F32)
    lo = pltpu.bitcast(p << 16, F32)
    return hi, lo


SUBLANES = 8
ROW_WORDS = D_MODEL // 2
ROW_CHUNKS = ROW_WORDS // LANES
assert ROW_CHUNKS == SUBLANES


def _row_chunk(n_rows, c):
    return pl.ds(c, n_rows, stride=SUBLANES)


def _store_row_tiles(ref, n_rows, packed):
    for c in range(ROW_CHUNKS):
        ref[_row_chunk(n_rows, c), :] = packed[:, c * LANES:(c + 1) * LANES]


def _attn_proj_kernel(b_ref, g1_ref, m1_ref, wpb_ref, o_ref):
    mb = jnp.dot(b_ref[...], wpb_ref[...], preferred_element_type=F32)
    o_ref[...] = (m1_ref[...].astype(F32) + g1_ref[...].astype(F32) * mb).astype(BF16)


def _attn_proj(b_attn, pg, m1, wpb):
    T = m1.shape[0]
    row = pl.BlockSpec((MG_TM, D_MODEL), lambda i: (i, 0))
    return pl.pallas_call(
        _attn_proj_kernel,
        grid=(T // MG_TM,),
        in_specs=[
            row,
            pl.BlockSpec((MG_TM, D_MODEL), lambda i: (i, PROJ_COLS // D_MODEL + 1)),
            row,
            _const_spec((MLA_HEADS * V_DIM, D_MODEL)),
        ],
        out_specs=row,
        out_shape=jax.ShapeDtypeStruct((T, D_MODEL), BF16),
        compiler_params=_params(("parallel",), 40),
        name="attn_proj",
    )(b_attn, pg, m1, wpb)


def _merge_kernel(mg_ref, x_ref, wo_ref, ln2_ref, rw_ref, rb_ref,
                  x1_ref, h2p_ref, code_ref, gate_ref, cnt_ref, cnt_scr):
    @pl.when(pl.program_id(0) == 0)
    def _():
        cnt_scr[...] = jnp.zeros_like(cnt_scr)

    x1 = x_ref[...] + jnp.dot(mg_ref[...], wo_ref[...], preferred_element_type=F32)
    x1_ref[...] = x1
    ms = jnp.mean(x1 * x1, axis=-1, keepdims=True)
    h2 = x1 * lax.rsqrt(ms + EPS) * ln2_ref[...]
    _store_row_tiles(h2p_ref, MG_TM, _pack_bf16_pair(h2[:, :D_MODEL // 2], h2[:, D_MODEL // 2:]))

    logits = jnp.dot(h2.astype(BF16), rw_ref[...], preferred_element_type=F32)
    lane = lax.broadcasted_iota(jnp.int32, (MG_TM, LANES), 1)
    work = jnp.where(lane < N_EXPERTS, logits + rb_ref[...], NEG_INF)
    earlier = (lax.broadcasted_iota(jnp.int32, (MG_TM, MG_TM), 1)
               < lax.broadcasted_iota(jnp.int32, (MG_TM, MG_TM), 0)).astype(BF16)
    base = cnt_scr[...]
    code_out = jnp.zeros((MG_TM, LANES), jnp.int32)
    val_out = jnp.zeros((MG_TM, LANES), F32)
    top = None
    denom = jnp.zeros((MG_TM, 1), F32)
    for k in range(TOP_K):
        mk = jnp.max(work, axis=-1, keepdims=True)
        ik = jnp.min(jnp.where(work == mk, lane, LANES), axis=-1, keepdims=True)
        hit = lane == ik
        work = jnp.where(hit, NEG_INF, work)
        if top is None:
            top = mk
        ek = jnp.exp(mk - top)
        denom = denom + ek
        val_out = jnp.where(lane == k, ek, val_out)
        onehot = hit.astype(BF16)
        prefix = jnp.dot(earlier, onehot, preferred_element_type=F32) + base
        rank = jnp.sum(jnp.where(hit, prefix, 0.0), axis=-1, keepdims=True).astype(jnp.int32)
        code_out = jnp.where(lane == k, rank * N_EXPERTS + ik, code_out)
        base = base + jnp.sum(onehot.astype(F32), axis=0, keepdims=True)
    code_ref[...] = code_out
    gate_ref[...] = val_out / denom
    cnt_scr[...] = base
    cnt_ref[...] = base


def _merge(merged, x2, wo, ln2, rw, rb):
    T = x2.shape[0]
    row = lambda w: pl.BlockSpec((MG_TM, w), lambda i: (i, 0))
    return pl.pallas_call(
        _merge_kernel,
        grid=(T // MG_TM,),
        in_specs=[
            row(D_MODEL),
            row(D_MODEL),
            _const_spec((D_MODEL, D_MODEL)),
            _const_spec((1, D_MODEL)),
            _const_spec((D_MODEL, LANES)),
            _const_spec((1, LANES)),
        ],
        out_specs=[row(D_MODEL), pl.BlockSpec((MG_TM * SUBLANES, LANES), lambda i: (i, 0)), row(LANES),
                   row(LANES), pl.BlockSpec((1, LANES), lambda i: (0, 0))],
        out_shape=[
            jax.ShapeDtypeStruct((T, D_MODEL), F32),
            jax.ShapeDtypeStruct((T * SUBLANES, LANES), jnp.uint32),
            jax.ShapeDtypeStruct((T, LANES), jnp.int32),
            jax.ShapeDtypeStruct((T, LANES), F32),
            jax.ShapeDtypeStruct((1, LANES), F32),
        ],
        scratch_shapes=[pltpu.VMEM((1, LANES), F32)],
        compiler_params=_params(("arbitrary",), 56),
        name="merge_router",
    )(merged, x2, wo, ln2, rw, rb)


DP_TM = 256
DMA_UNROLL = 8


def _row_tile(row):
    return pl.ds(pl.multiple_of(row * SUBLANES, SUBLANES), SUBLANES)


def _dispatch_kernel(dest_ref, h2p_ref, xs_in, xs_hbm, sem):
    del xs_in

    def issue(t, carry):
        src = h2p_ref.at[_row_tile(t), :]
        for k in range(TOP_K):
            row = dest_ref[0, t * TOP_K + k]
            pltpu.make_async_copy(src, xs_hbm.at[_row_tile(row), :], sem).start()
        return carry

    lax.fori_loop(0, DP_TM, issue, 0, unroll=DMA_UNROLL // TOP_K)
    all_rows = xs_hbm.at[pl.ds(0, DP_TM * TOP_K * SUBLANES), :]
    pltpu.make_async_copy(all_rows, all_rows, sem).wait()


def _dispatch(dest3, h2p, xs_zero):
    T = h2p.shape[0] // SUBLANES
    return pl.pallas_call(
        _dispatch_kernel,
        grid=(T // DP_TM,),
        in_specs=[
            pl.BlockSpec((None, 1, DP_TM * TOP_K), lambda i: (i, 0, 0), memory_space=pltpu.SMEM),
            pl.BlockSpec((DP_TM * SUBLANES, LANES), lambda i: (i, 0)),
            pl.BlockSpec(memory_space=pl.ANY),
        ],
        out_specs=pl.BlockSpec(memory_space=pl.ANY),
        out_shape=jax.ShapeDtypeStruct(xs_zero.shape, xs_zero.dtype),
        scratch_shapes=[pltpu.SemaphoreType.DMA(())],
        input_output_aliases={2: 0},
        compiler_params=_params(("arbitrary",), 32),
        name="moe_dispatch",
    )(dest3, h2p, xs_zero)


GROUP_SUB = 4
GROUP_ROWS = GROUP_SUB * ROW_BLOCK
FF_TILE = 512
N_FF_TILES = D_FF // FF_TILE
DOWN_CHUNK = 512


def _expert_kernel(ge_ref, gs_ref, gn_ref, xs_in, wg_ref, wu_ref, bg_ref, bu_ref, wd_ref, bd_ref,
                   rows_hbm, xwin, x_scr, acc_scr, stage, wsem, sem):
    del ge_ref, xs_in
    g = pl.program_id(0)
    f = pl.program_id(1)
    ns = gn_ref[g]
    half = D_MODEL // 2
    slot = g & 1

    def win_copy(group, s):
        rows = pl.ds(gs_ref[group] * (ROW_BLOCK * SUBLANES), GROUP_ROWS * SUBLANES)
        return pltpu.make_async_copy(rows_hbm.at[rows, :], xwin.at[s], wsem.at[s])

    def out_copy(s, start_block):
        rows = pl.ds((start_block + s) * (ROW_BLOCK * SUBLANES), ROW_BLOCK * SUBLANES)
        return pltpu.make_async_copy(stage.at[s], rows_hbm.at[rows, :], sem.at[s])

    @pl.when((f == 0) & (ns > 0))
    def _():
        @pl.when(g == 0)
        def _():
            win_copy(0, 0).start()

        win_copy(g, slot).wait()
        @pl.when(g + 1 < pl.num_programs(0))
        def _():
            @pl.when(gn_ref[g + 1] > 0)
            def _():
                win_copy(g + 1, 1 - slot).start()

        for c in range(ROW_CHUNKS):
            hi, lo = _unpack_bf16_pair(xwin[slot, _row_chunk(GROUP_ROWS, c), :])
            x_scr[:, c * LANES:(c + 1) * LANES] = hi.astype(BF16)
            x_scr[:, half + c * LANES:half + (c + 1) * LANES] = lo.astype(BF16)
        acc_scr[...] = jnp.broadcast_to(bd_ref[...], acc_scr.shape)

    for n in range(1, GROUP_SUB + 1):
        @pl.when(ns == n)
        def _(n=n):
            m = n * ROW_BLOCK
            x = x_scr[:m, :]
            gate = jnp.dot(x, wg_ref[...].astype(BF16), preferred_element_type=F32) + bg_ref[...]
            up = jnp.dot(x, wu_ref[...].astype(BF16), preferred_element_type=F32) + bu_ref[...]
            gate = jnp.minimum(gate, SWIGLU_LIMIT)
            up = jnp.clip(up, -SWIGLU_LIMIT, SWIGLU_LIMIT)
            glu = gate * jax.nn.sigmoid(gate * SWIGLU_ALPHA)
            act = ((up + 1.0) * glu).astype(BF16)
            for c in range(D_MODEL // DOWN_CHUNK):
                cols = slice(c * DOWN_CHUNK, (c + 1) * DOWN_CHUNK)
                acc_scr[:m, cols] += jnp.dot(act, wd_ref[:, cols].astype(BF16), preferred_element_type=F32)

    @pl.when(f == N_FF_TILES - 1)
    def _():
        @pl.when(g > 0)
        def _():
            prev = gn_ref[g - 1]
            for s in range(GROUP_SUB):
                @pl.when(s < prev)
                def _(s=s):
                    out_copy(s, 0).wait()

        for s in range(GROUP_SUB):
            @pl.when(s < ns)
            def _(s=s):
                y = acc_scr[s * ROW_BLOCK:(s + 1) * ROW_BLOCK, :]
                _store_row_tiles(stage.at[s], ROW_BLOCK, _pack_bf16_pair(y[:, :half], y[:, half:]))
                out_copy(s, gs_ref[g]).start()

        @pl.when(g == pl.num_programs(0) - 1)
        def _():
            for s in range(GROUP_SUB):
                @pl.when(s < ns)
                def _(s=s):
                    out_copy(s, 0).wait()


def _experts(g_exp, g_start, g_nsub, xs, wgu, bgu, wd, bd):
    def ff(f, gn, g):
        return jnp.where(gn[g] > 0, f, N_FF_TILES - 1)

    grid_spec = pltpu.PrefetchScalarGridSpec(
        num_scalar_prefetch=3,
        grid=(g_exp.shape[0], N_FF_TILES),
        in_specs=[
            pl.BlockSpec(memory_space=pl.ANY),
            pl.BlockSpec((None, D_MODEL, FF_TILE), lambda g, f, ge, gs, gn: (ge[g], 0, ff(f, gn, g))),
            pl.BlockSpec((None, D_MODEL, FF_TILE),
                         lambda g, f, ge, gs, gn: (ge[g], 0, N_FF_TILES + ff(f, gn, g))),
            pl.BlockSpec((None, 1, FF_TILE), lambda g, f, ge, gs, gn: (ge[g], 0, ff(f, gn, g))),
            pl.BlockSpec((None, 1, FF_TILE), lambda g, f, ge, gs, gn: (ge[g], 0, N_FF_TILES + ff(f, gn, g))),
            pl.BlockSpec((None, FF_TILE, D_MODEL), lambda g, f, ge, gs, gn: (ge[g], ff(f, gn, g), 0)),
            pl.BlockSpec((None, 1, D_MODEL), lambda g, f, ge, gs, gn: (ge[g], 0, 0)),
        ],
        out_specs=pl.BlockSpec(memory_space=pl.ANY),
        scratch_shapes=[
            pltpu.VMEM((2, GROUP_ROWS * SUBLANES, LANES), jnp.uint32),
            pltpu.VMEM((GROUP_ROWS, D_MODEL), BF16),
            pltpu.VMEM((GROUP_ROWS, D_MODEL), F32),
            pltpu.VMEM((GROUP_SUB, ROW_BLOCK * SUBLANES, LANES), jnp.uint32),
            pltpu.SemaphoreType.DMA((2,)),
            pltpu.SemaphoreType.DMA((GROUP_SUB,)),
        ],
    )
    return pl.pallas_call(
        _expert_kernel,
        grid_spec=grid_spec,
        out_shape=jax.ShapeDtypeStruct(xs.shape, jnp.uint32),
        input_output_aliases={3: 0},
        compiler_params=_params(("arbitrary", "arbitrary"), 58),
        name="moe_experts",
    )(g_exp, g_start, g_nsub, xs, wgu, wgu, bgu, bgu, wd, bd)


CB_TM = 128


def _combine_kernel(dest_ref, dest_next_ref, ys_hbm, x1_ref, gate_ref, o_ref, buf, sem):
    i = pl.program_id(0)
    slot = i & 1

    def gather_tile(dest, s):
        def issue(t, carry):
            for k in range(TOP_K):
                row = dest[0, t * TOP_K + k]
                pltpu.make_async_copy(ys_hbm.at[_row_tile(row), :], buf.at[s, k, _row_tile(t), :],
                                      sem.at[s]).start()
            return carry

        lax.fori_loop(0, CB_TM, issue, 0, unroll=DMA_UNROLL // TOP_K)

    @pl.when(i == 0)
    def _():
        gather_tile(dest_ref, 0)

    @pl.when(i + 1 < pl.num_programs(0))
    def _():
        gather_tile(dest_next_ref, 1 - slot)

    pltpu.make_async_copy(buf.at[slot], buf.at[slot], sem.at[slot]).wait()

    half = D_MODEL // 2
    gates = gate_ref[...]
    gate_k = [jnp.broadcast_to(gates[:, k:k + 1], (CB_TM, LANES)) for k in range(TOP_K)]
    for c in range(ROW_CHUNKS):
        cols_hi = slice(c * LANES, (c + 1) * LANES)
        cols_lo = slice(half + c * LANES, half + (c + 1) * LANES)
        out_hi = x1_ref[:, cols_hi]
        out_lo = x1_ref[:, cols_lo]
        for k in range(TOP_K):
            hi, lo = _unpack_bf16_pair(buf[slot, k, _row_chunk(CB_TM, c), :])
            out_hi = out_hi + gate_k[k] * hi
            out_lo = out_lo + gate_k[k] * lo
        o_ref[:, cols_hi] = out_hi
        o_ref[:, cols_lo] = out_lo


def _combine(dest3, ys, x1, gates):
    T = x1.shape[0]
    n = T // CB_TM
    dest_spec = lambda f: pl.BlockSpec((None, 1, CB_TM * TOP_K), f, memory_space=pltpu.SMEM)
    return pl.pallas_call(
        _combine_kernel,
        grid=(n,),
        in_specs=[
            dest_spec(lambda i: (i, 0, 0)),
            dest_spec(lambda i: (jnp.minimum(i + 1, n - 1), 0, 0)),
            pl.BlockSpec(memory_space=pl.ANY),
            pl.BlockSpec((CB_TM, D_MODEL), lambda i: (i, 0)),
            pl.BlockSpec((CB_TM, LANES), lambda i: (i, 0)),
        ],
        out_specs=pl.BlockSpec((CB_TM, D_MODEL), lambda i: (i, 0)),
        out_shape=jax.ShapeDtypeStruct((T, D_MODEL), F32),
        scratch_shapes=[pltpu.VMEM((2, TOP_K, CB_TM * SUBLANES, LANES), jnp.uint32),
                        pltpu.SemaphoreType.DMA((2,))],
        compiler_params=_params(("arbitrary",), 32),
        name="moe_combine",
    )(dest3, dest3, ys, x1, gates)


def _group_tables(counts_f32, max_groups):
    i32 = jnp.int32
    counts = counts_f32[0, :N_EXPERTS].astype(i32)
    nb = (counts + ROW_BLOCK - 1) // ROW_BLOCK
    ng = (nb + GROUP_SUB - 1) // GROUP_SUB
    upto = jnp.arange(N_EXPERTS)[None, :] <= jnp.arange(N_EXPERTS)[:, None]
    pad_ends = jnp.sum(jnp.where(upto, nb[None, :], 0), axis=1) * ROW_BLOCK
    pad_starts = pad_ends - nb * ROW_BLOCK
    g_ends = jnp.sum(jnp.where(upto, ng[None, :], 0), axis=1)
    g_starts = g_ends - ng
    n_groups = g_ends[-1]

    g = jnp.arange(max_groups, dtype=i32)
    gg = jnp.minimum(g, n_groups - 1)
    e = jnp.minimum(jnp.sum((g_ends[None, :] <= gg[:, None]).astype(i32), axis=1), N_EXPERTS - 1)
    pick = e[:, None] == jnp.arange(N_EXPERTS)[None, :]
    take = lambda table: jnp.sum(jnp.where(pick, table[None, :], 0), axis=1)
    nb_g, ng_g = take(nb), jnp.maximum(take(ng), 1)
    j = gg - take(g_starts)
    base, rem = nb_g // ng_g, nb_g % ng_g
    nsub = jnp.where(g < n_groups, base + (j < rem).astype(i32), 0)
    start_block = take(pad_starts) // ROW_BLOCK + j * base + jnp.minimum(j, rem)
    return pad_starts.astype(i32), e.astype(i32), start_block.astype(i32), nsub.astype(i32)


def _rope_constants():
    lane = np.arange(LANES)
    half = QK_ROPE // 2
    inv = 1.0 / (ROPE_THETA ** (np.arange(0, QK_ROPE, 2, dtype=np.float32) / QK_ROPE))
    cst = np.zeros((8, LANES), np.float32)
    cst[0, :QK_ROPE] = inv.astype(np.float32)[lane[:QK_ROPE] % half]
    cst[1, :QK_ROPE] = 1.0
    cst[2, :half] = -1.0
    cst[2, half:QK_ROPE] = 1.0
    return jnp.asarray(cst)


def _swap_halves(a):
    half = QK_ROPE // 2
    return jnp.concatenate([a[..., half:], a[..., :half]], axis=-1)


def kernel(x, mem, positions, ln1_g, w_in, w_gate, b_gate, gmlp_ln_g, gmlp_ln_b, gmlp_ws, gmlp_bs, w_pa,
           mla_cq_g, mla_w_uq, mla_ckv_g, mla_w_ukv, mla_qn_g, mla_kn_g, w_pb, mem_ln_g, mem_w_k, mem_w_v,
           mem_qn_g, mem_kn_g, w_pc, w_o, ln2_g, router_w, router_b, moe_w_gu, moe_b_gu, moe_w_down,
           moe_b_down):
    B, S, D = x.shape
    T = B * S
    x2 = x.reshape(T, D)
    for l in range(ln1_g.shape[0]):
        o_kr = 2 * GM_WIDTH + Q_LORA + KV_LORA
        o_qm = o_kr + QK_ROPE
        wi = w_in[l]
        w1 = jnp.concatenate([wi[:, :o_kr], wi[:, o_qm:], w_gate[l]], axis=1).astype(BF16)
        b1 = jnp.concatenate([jnp.zeros((PROJ_COLS,), F32), b_gate[l]])[None, :]
        w_kr = wi[:, o_kr:o_qm]
        wr = jnp.concatenate([w_kr, _swap_halves(w_kr)], axis=1).astype(BF16)

        wq3 = mla_w_uq[l].reshape(Q_LORA, MLA_HEADS, QK_DIM)
        wq = jnp.concatenate([wq3, _swap_halves(wq3[..., QK_NOPE:])], axis=-1)
        wq = wq.reshape(Q_LORA, MLA_HEADS * QK_PAD).astype(BF16)
        wkv = mla_w_ukv[l].astype(BF16)
        gq = jnp.concatenate([mla_qn_g[l], _swap_halves(mla_qn_g[l][QK_NOPE:])])[None, :]
        gk = jnp.concatenate([mla_kn_g[l], _swap_halves(mla_kn_g[l][QK_NOPE:])])[None, :]

        bias_full = jnp.broadcast_to(gmlp_bs[l].T[:, :, None], (GM_CHUNK, GM_GROUPS, GM_CHUNK))
        bias_full = bias_full.reshape(GM_CHUNK, GM_WIDTH)

        rw = jnp.pad(router_w[l], ((0, 0), (0, LANES - N_EXPERTS))).astype(BF16)
        rb = jnp.pad(router_b[l], (0, LANES - N_EXPERTS))[None, :]

        pg, kr = _norm_proj(x2, ln1_g[l][None, :], w1, b1, wr)
        m0 = _gmlp(pg, gmlp_ln_g[l][None, :], gmlp_ln_b[l][None, :], gmlp_ws[l].astype(BF16), bias_full,
                   w_pa[l].astype(BF16))
        q, k, vt = _mla_qkv(pg, kr, positions.reshape(T, 1), mla_cq_g[l][None, :], mla_ckv_g[l][None, :],
                           wq, wkv, gq, gk, _rope_constants(), B, S)
        b_attn = _flash(q, k, vt)
        km, vm = _mem_kv(mem, mem_ln_g[l][None, :], mem_w_k[l].astype(BF16), mem_w_v[l].astype(BF16),
                         mem_kn_g[l][None, :])
        m1 = _mem_attn(pg, km, vm, m0, mem_qn_g[l][None, :], w_pc[l].astype(BF16), S)
        merged = _attn_proj(b_attn, pg, m1, w_pb[l].astype(BF16))
        x1, h2p, code, gates, counts = _merge(merged, x2, w_o[l].astype(BF16), ln2_g[l][None, :], rw, rb)

        n_rows = T * TOP_K + N_EXPERTS * ROW_BLOCK
        max_groups = n_rows // GROUP_ROWS + N_EXPERTS
        pad_starts, g_exp, g_start, g_nsub = _group_tables(counts, max_groups)
        codes = code[:, :TOP_K]
        pick = (codes & (N_EXPERTS - 1))[..., None] == jnp.arange(N_EXPERTS)
        dest = (codes >> EXPERT_BITS) + jnp.sum(jnp.where(pick, pad_starts, 0), axis=-1)
        xs = _dispatch(dest.reshape(T // DP_TM, 1, DP_TM * TOP_K), h2p,
                       jnp.zeros(((n_rows + GROUP_ROWS) * SUBLANES, LANES), jnp.uint32))
        ys = _experts(g_exp, g_start, g_nsub, xs, moe_w_gu[l], moe_b_gu[l][:, None, :],
                      moe_w_down[l], moe_b_down[l][:, None, :])
        x2 = _combine(dest.reshape(T // CB_TM, 1, CB_TM * TOP_K), ys, x1, gates)
    return x2.reshape(B, S, D)
```

```python
import functools

import numpy as np
import jax
import jax.numpy as jnp
from jax import lax
from jax.experimental import pallas as pl
from jax.experimental.pallas import tpu as pltpu

F32 = jnp.float32
BF16 = jnp.bfloat16

D_MODEL = 2048
GM_WIDTH = 1024
GM_GROUPS = 8
GM_CHUNK = 128
MLA_HEADS = 16
Q_LORA = 512
KV_LORA = 512
QK_NOPE = 128
QK_ROPE = 64
V_DIM = 128
QK_DIM = QK_NOPE + QK_ROPE
QK_PAD = 256
MLA_SCALE = QK_DIM ** -0.5
LOG2_E = 1.4426950408889634
ROPE_THETA = 10000.0
MEM_LEN = 256
MEM_HEADS = 4
MEM_HEAD_DIM = 256
MEM_WIDTH = MEM_HEADS * MEM_HEAD_DIM
MEM_SCALE = MEM_HEAD_DIM ** -0.5
N_EXPERTS = 32
EXPERT_BITS = 5
TOP_K = 4
D_FF = 2048
SWIGLU_LIMIT = 7.0
SWIGLU_ALPHA = 1.702
ROW_BLOCK = 256
EPS = 1e-6
LANES = 128
NEG_INF = float("-inf")

PROJ_COLS = 2 * GM_WIDTH + Q_LORA + KV_LORA + MEM_WIDTH
PG_COLS = PROJ_COLS + 3 * D_MODEL

MIB = 1024 * 1024


def _params(semantics, vmem_mib):
    return pltpu.CompilerParams(dimension_semantics=semantics, vmem_limit_bytes=vmem_mib * MIB)


def _const_spec(shape):
    nd = len(shape)
    return pl.BlockSpec(shape, lambda *_: (0,) * nd, pipeline_mode=pl.Buffered(1))


P1_TM = 1024
P1_TN = 1024


def _norm_proj_kernel(x_ref, g_ref, w_ref, b_ref, wr_ref, o_ref, kr_ref, h_scr, *, n_plain):
    j = pl.program_id(1)

    @pl.when(j == 0)
    def _():
        def body(c, carry):
            rows = pl.ds(pl.multiple_of(c * 128, 128), 128)
            x = x_ref[rows, :]
            ms = jnp.mean(x * x, axis=-1, keepdims=True)
            h_scr[rows, :] = (x * lax.rsqrt(ms + EPS) * g_ref[...]).astype(BF16)
            return carry

        lax.fori_loop(0, P1_TM // 128, body, 0)
        kr_ref[...] = jnp.dot(h_scr[...], wr_ref[...], preferred_element_type=F32)

    acc = jnp.dot(h_scr[...], w_ref[...], preferred_element_type=F32)
    gated = jax.nn.sigmoid(acc + b_ref[...])
    o_ref[...] = jnp.where(j >= n_plain, gated, acc).astype(BF16)


def _norm_proj(x2, ln1_g, w1, b1, wr):
    T = x2.shape[0]
    grid = (T // P1_TM, PG_COLS // P1_TN)
    return pl.pallas_call(
        functools.partial(_norm_proj_kernel, n_plain=PROJ_COLS // P1_TN),
        grid=grid,
        in_specs=[
            pl.BlockSpec((P1_TM, D_MODEL), lambda i, j: (i, 0)),
            pl.BlockSpec((1, D_MODEL), lambda i, j: (0, 0)),
            pl.BlockSpec((D_MODEL, P1_TN), lambda i, j: (0, j)),
            pl.BlockSpec((1, P1_TN), lambda i, j: (0, j)),
            pl.BlockSpec((D_MODEL, LANES), lambda i, j: (0, 0)),
        ],
        out_specs=[
            pl.BlockSpec((P1_TM, P1_TN), lambda i, j: (i, j)),
            pl.BlockSpec((P1_TM, LANES), lambda i, j: (i, 0)),
        ],
        out_shape=[
            jax.ShapeDtypeStruct((T, PG_COLS), BF16),
            jax.ShapeDtypeStruct((T, LANES), F32),
        ],
        scratch_shapes=[pltpu.VMEM((P1_TM, D_MODEL), BF16)],
        compiler_params=_params(("parallel", "arbitrary"), 48),
        name="norm_proj",
    )(x2, ln1_g, w1, b1, wr)


GM_TM = 512


def _gmlp_kernel(u_ref, v_ref, g0_ref, lng_ref, lnb_ref, ws_ref, bias_ref, wpa_ref, o_ref, vb_scr, a_scr):
    v = v_ref[...].astype(F32)
    mu = jnp.mean(v, axis=-1, keepdims=True)
    c = v - mu
    var = jnp.mean(c * c, axis=-1, keepdims=True)
    vb_scr[...] = (c * lax.rsqrt(var + EPS) * lng_ref[...] + lnb_ref[...]).astype(BF16)

    row = lax.broadcasted_iota(jnp.int32, (GM_CHUNK, GM_CHUNK), 0)
    col = lax.broadcasted_iota(jnp.int32, (GM_CHUNK, GM_CHUNK), 1)
    causal = col <= row
    for g in range(GM_GROUPS):
        cols = slice(g * GM_CHUNK, (g + 1) * GM_CHUNK)
        wg = jnp.where(causal, ws_ref[g], jnp.zeros((), BF16))
        for ch in range(GM_TM // GM_CHUNK):
            rows = slice(ch * GM_CHUNK, (ch + 1) * GM_CHUNK)
            mixed = jnp.dot(wg, vb_scr[rows, cols], preferred_element_type=F32) + bias_ref[:, cols]
            a_scr[rows, cols] = (u_ref[rows, cols].astype(F32) * mixed).astype(BF16)

    ma = jnp.dot(a_scr[...], wpa_ref[...], preferred_element_type=F32)
    o_ref[...] = (g0_ref[...].astype(F32) * ma).astype(BF16)


def _gmlp(pg, ln_g, ln_b, ws, bias_full, wpa):
    T = pg.shape[0]
    return pl.pallas_call(
        _gmlp_kernel,
        grid=(T // GM_TM,),
        in_specs=[
            pl.BlockSpec((GM_TM, GM_WIDTH), lambda i: (i, 0)),
            pl.BlockSpec((GM_TM, GM_WIDTH), lambda i: (i, 1)),
            pl.BlockSpec((GM_TM, D_MODEL), lambda i: (i, PROJ_COLS // D_MODEL)),
            _const_spec((1, GM_WIDTH)),
            _const_spec((1, GM_WIDTH)),
            _const_spec((GM_GROUPS, GM_CHUNK, GM_CHUNK)),
            _const_spec((GM_CHUNK, GM_WIDTH)),
            _const_spec((GM_WIDTH, D_MODEL)),
        ],
        out_specs=pl.BlockSpec((GM_TM, D_MODEL), lambda i: (i, 0)),
        out_shape=jax.ShapeDtypeStruct((T, D_MODEL), BF16),
        scratch_shapes=[pltpu.VMEM((GM_TM, GM_WIDTH), BF16), pltpu.VMEM((GM_TM, GM_WIDTH), BF16)],
        compiler_params=_params(("parallel",), 40),
        name="gmlp",
    )(pg, pg, pg, ln_g, ln_b, ws, bias_full, wpa)


QKV_TM = 1024
QKV_HEADS = 2
HEAD_BLK = 4
FA_T = 512
FA_QS = 256


def _rope(t, cos, sin):
    return t * cos + pltpu.roll(t, QK_ROPE, 1) * sin


def _row_sum_all_lanes(sq, weights):
    hi = sq.astype(BF16)
    lo = (sq - hi.astype(F32)).astype(BF16)
    w = weights.astype(BF16)
    return (jnp.dot(hi, w, preferred_element_type=F32) + jnp.dot(lo, w, preferred_element_type=F32))


def _mla_qkv_kernel(cq_ref, ckv_ref, kr_ref, pos_ref, gcq_ref, gckv_ref, wq_ref, wkv_ref, gq_ref, gk_ref,
                    cst_ref, sumw_ref, q_ref, k_ref, vt_ref, cqn_scr, ckvn_scr, cos_scr, sin_scr, krsq_scr):
    hb = pl.program_id(1)
    w_nope = sumw_ref[:QK_NOPE, :]
    w_rope = sumw_ref[QK_NOPE:, :]

    @pl.when(hb == 0)
    def _():
        cq = cq_ref[...].astype(F32)
        ms = jnp.mean(cq * cq, axis=-1, keepdims=True)
        cqn_scr[...] = (cq * lax.rsqrt(ms + EPS) * gcq_ref[...]).astype(BF16)
        ckv = ckv_ref[...].astype(F32)
        ms = jnp.mean(ckv * ckv, axis=-1, keepdims=True)
        ckvn_scr[...] = (ckv * lax.rsqrt(ms + EPS) * gckv_ref[...]).astype(BF16)
        ang = pos_ref[...].astype(F32) * cst_ref[0:1, :]
        cos_scr[...] = jnp.cos(ang) * cst_ref[1:2, :]
        sin_scr[...] = jnp.sin(ang) * cst_ref[2:3, :]
        kr = kr_ref[...]
        krsq_scr[...] = _row_sum_all_lanes(kr * kr, w_rope)

    cos = cos_scr[...]
    sin = sin_scr[...]
    inv_dim = 1.0 / QK_DIM
    kr = kr_ref[...]

    yq2 = jnp.dot(cqn_scr[...], wq_ref[...], preferred_element_type=F32)
    ykv2 = jnp.dot(ckvn_scr[...], wkv_ref[...], preferred_element_type=F32)
    for hh in range(QKV_HEADS):
        yq = yq2[:, hh * QK_PAD:(hh + 1) * QK_PAD]
        qn = yq[:, :QK_NOPE]
        qt = yq[:, QK_NOPE:]
        ssq = _row_sum_all_lanes(qn * qn, w_nope) + _row_sum_all_lanes(qt * qt, w_rope)
        rs = lax.rsqrt(ssq * inv_dim + EPS) * (MLA_SCALE * LOG2_E)
        q_ref[hh, :, :QK_NOPE] = (qn * rs * gq_ref[:, :QK_NOPE]).astype(BF16)
        q_ref[hh, :, QK_NOPE:] = _rope(qt * rs * gq_ref[:, QK_NOPE:], cos, sin).astype(BF16)

        ykv = ykv2[:, hh * (QK_NOPE + V_DIM):(hh + 1) * (QK_NOPE + V_DIM)]
        kn = ykv[:, :QK_NOPE]
        ssq = _row_sum_all_lanes(kn * kn, w_nope) + krsq_scr[...]
        rs = lax.rsqrt(ssq * inv_dim + EPS)
        k_ref[hh, :, :QK_NOPE] = (kn * rs * gk_ref[:, :QK_NOPE]).astype(BF16)
        k_ref[hh, :, QK_NOPE:] = _rope(kr * rs * gk_ref[:, QK_NOPE:], cos, sin).astype(BF16)
        vv = ykv[:, QK_NOPE:]
        for c in range(QKV_TM // FA_T):
            vt_ref[hh, c] = vv[c * FA_T:(c + 1) * FA_T, :].T.astype(BF16)


def _mla_qkv(pg, kr, pos, gcq, gckv, wq, wkv, gq, gk, cst, B, S):
    T = pg.shape[0]
    sumw = jnp.concatenate([jnp.ones((QK_NOPE, LANES), F32), jnp.full((QK_PAD - QK_NOPE, LANES), 0.5, F32)])
    spb = S // QKV_TM
    cpt = QKV_TM // FA_T
    head_spec = lambda w: pl.BlockSpec((None, QKV_HEADS, QKV_TM, w), lambda i, h: (i // spb, h, i % spb, 0))
    return pl.pallas_call(
        _mla_qkv_kernel,
        grid=(T // QKV_TM, MLA_HEADS // QKV_HEADS),
        in_specs=[
            pl.BlockSpec((QKV_TM, Q_LORA), lambda i, h: (i, 2 * GM_WIDTH // Q_LORA)),
            pl.BlockSpec((QKV_TM, KV_LORA), lambda i, h: (i, 2 * GM_WIDTH // KV_LORA + 1)),
            pl.BlockSpec((QKV_TM, LANES), lambda i, h: (i, 0)),
            pl.BlockSpec((QKV_TM, 1), lambda i, h: (i, 0)),
            pl.BlockSpec((1, Q_LORA), lambda i, h: (0, 0)),
            pl.BlockSpec((1, KV_LORA), lambda i, h: (0, 0)),
            pl.BlockSpec((Q_LORA, QKV_HEADS * QK_PAD), lambda i, h: (0, h)),
            pl.BlockSpec((KV_LORA, QKV_HEADS * (QK_NOPE + V_DIM)), lambda i, h: (0, h)),
            pl.BlockSpec((1, QK_PAD), lambda i, h: (0, 0)),
            pl.BlockSpec((1, QK_PAD), lambda i, h: (0, 0)),
            pl.BlockSpec((8, LANES), lambda i, h: (0, 0)),
            pl.BlockSpec((QK_PAD, LANES), lambda i, h: (0, 0)),
        ],
        out_specs=[
            head_spec(QK_PAD),
            head_spec(QK_PAD),
            pl.BlockSpec((None, QKV_HEADS, cpt, V_DIM, FA_T), lambda i, h: (i // spb, h, i % spb, 0, 0)),
        ],
        out_shape=[
            jax.ShapeDtypeStruct((B, MLA_HEADS, S, QK_PAD), BF16),
            jax.ShapeDtypeStruct((B, MLA_HEADS, S, QK_PAD), BF16),
            jax.ShapeDtypeStruct((B, MLA_HEADS, S // FA_T, V_DIM, FA_T), BF16),
        ],
        scratch_shapes=[
            pltpu.VMEM((QKV_TM, Q_LORA), BF16),
            pltpu.VMEM((QKV_TM, KV_LORA), BF16),
            pltpu.VMEM((QKV_TM, LANES), F32),
            pltpu.VMEM((QKV_TM, LANES), F32),
            pltpu.VMEM((QKV_TM, LANES), F32),
        ],
        compiler_params=_params(("parallel", "arbitrary"), 48),
        name="mla_qkv",
    )(pg, pg, kr, pos, gcq, gckv, wq, wkv, gq, gk, cst, sumw)


def _flash_kernel(q_ref, k_ref, vt_ref, o_ref, m_scr, l_scr, acc_scr):
    qi = pl.program_id(2)
    m_scr[...] = jnp.full_like(m_scr, NEG_INF)
    l_scr[...] = jnp.zeros_like(l_scr)
    acc_scr[...] = jnp.zeros_like(acc_scr)

    def step(kb, diagonal):
        ks = pl.ds(pl.multiple_of(kb * FA_T, FA_T), FA_T)
        chains = [(hh, qs) for hh in range(HEAD_BLK) for qs in range(FA_T // FA_QS)]
        scores = []
        for hh, qs in chains:
            qc = slice(qs * FA_QS, (qs + 1) * FA_QS)
            st = lax.dot_general(k_ref[hh, ks, :], q_ref[hh, qc, :], (((1,), (1,)), ((), ())),
                                 preferred_element_type=F32)
            if diagonal:
                krow = lax.broadcasted_iota(jnp.int32, (FA_T, FA_QS), 0)
                qcol = lax.broadcasted_iota(jnp.int32, (FA_T, FA_QS), 1) + qs * FA_QS
                st = jnp.where(krow <= qcol, st, NEG_INF)
            scores.append(st)
        probs = []
        for (hh, qs), st in zip(chains, scores):
            qc = slice(qs * FA_QS, (qs + 1) * FA_QS)
            m_prev = m_scr[hh, :, qc]
            m_new = jnp.maximum(m_prev, jnp.max(st, axis=0, keepdims=True))
            alpha = jnp.exp2(m_prev - m_new)
            p = jnp.exp2(st - m_new)
            l_scr[hh, :, qc] = alpha * l_scr[hh, :, qc] + jnp.sum(p, axis=0, keepdims=True)
            m_scr[hh, :, qc] = m_new
            probs.append((alpha, p.astype(BF16)))
        for (hh, qs), (alpha, p) in zip(chains, probs):
            qc = slice(qs * FA_QS, (qs + 1) * FA_QS)
            acc_scr[hh, :, qc] = alpha * acc_scr[hh, :, qc] + jnp.dot(vt_ref[hh, kb], p,
                                                                      preferred_element_type=F32)

    def body(kb, carry):
        step(kb, False)
        return carry

    lax.fori_loop(0, qi, body, 0)
    step(qi, True)
    for hh in range(HEAD_BLK):
        o_ref[:, hh * V_DIM:(hh + 1) * V_DIM] = (acc_scr[hh] / l_scr[hh]).T.astype(BF16)


def _flash(q, k, vt):
    B, H, S, _ = q.shape
    nq = S // FA_T
    return pl.pallas_call(
        _flash_kernel,
        grid=(B, H // HEAD_BLK, nq),
        in_specs=[
            pl.BlockSpec((None, HEAD_BLK, FA_T, QK_PAD), lambda b, h, i: (b, h, i, 0)),
            pl.BlockSpec((None, HEAD_BLK, S, QK_PAD), lambda b, h, i: (b, h, 0, 0)),
            pl.BlockSpec((None, HEAD_BLK, nq, V_DIM, FA_T), lambda b, h, i: (b, h, 0, 0, 0)),
        ],
        out_specs=pl.BlockSpec((FA_T, HEAD_BLK * V_DIM), lambda b, h, i: (b * nq + i, h)),
        out_shape=jax.ShapeDtypeStruct((B * S, H * V_DIM), BF16),
        scratch_shapes=[
            pltpu.VMEM((HEAD_BLK, 1, FA_T), F32),
            pltpu.VMEM((HEAD_BLK, 1, FA_T), F32),
            pltpu.VMEM((HEAD_BLK, V_DIM, FA_T), F32),
        ],
        compiler_params=_params(("parallel", "parallel", "arbitrary"), 40),
        name="mla_attention",
    )(q, k, vt)


def _mem_kv_kernel(mem_ref, g_ref, wk_ref, wv_ref, kng_ref, k_ref, v_ref):
    m = mem_ref[...]
    ms = jnp.mean(m * m, axis=-1, keepdims=True)
    mb = (m * lax.rsqrt(ms + EPS) * g_ref[...]).astype(BF16)
    kk = jnp.dot(mb, wk_ref[...], preferred_element_type=F32)
    for h in range(MEM_HEADS):
        cols = slice(h * MEM_HEAD_DIM, (h + 1) * MEM_HEAD_DIM)
        kh = kk[:, cols]
        ms = jnp.mean(kh * kh, axis=-1, keepdims=True)
        k_ref[:, cols] = (kh * lax.rsqrt(ms + EPS) * kng_ref[...]).astype(BF16)
    v_ref[...] = jnp.dot(mb, wv_ref[...], preferred_element_type=F32).astype(BF16)


def _mem_kv(mem, g, wk, wv, kng):
    B = mem.shape[0]
    return pl.pallas_call(
        _mem_kv_kernel,
        grid=(B,),
        in_specs=[
            pl.BlockSpec((None, MEM_LEN, D_MODEL), lambda b: (b, 0, 0)),
            _const_spec((1, D_MODEL)),
            _const_spec((D_MODEL, MEM_WIDTH)),
            _const_spec((D_MODEL, MEM_WIDTH)),
            _const_spec((1, MEM_HEAD_DIM)),
        ],
        out_specs=[
            pl.BlockSpec((None, MEM_LEN, MEM_WIDTH), lambda b: (b, 0, 0)),
            pl.BlockSpec((None, MEM_LEN, MEM_WIDTH), lambda b: (b, 0, 0)),
        ],
        out_shape=[
            jax.ShapeDtypeStruct((B, MEM_LEN, MEM_WIDTH), BF16),
            jax.ShapeDtypeStruct((B, MEM_LEN, MEM_WIDTH), BF16),
        ],
        compiler_params=_params(("parallel",), 32),
        name="mem_kv",
    )(mem, g, wk, wv, kng)


MA_TM = 512


def _mem_attn_kernel(qm_ref, km_ref, vm_ref, g2_ref, m0_ref, qng_ref, wpc_ref, o_ref, c_scr):
    for h in range(MEM_HEADS):
        cols = slice(h * MEM_HEAD_DIM, (h + 1) * MEM_HEAD_DIM)
        qh = qm_ref[:, cols].astype(F32)
        ms = jnp.mean(qh * qh, axis=-1, keepdims=True)
        qn = (qh * (lax.rsqrt(ms + EPS) * MEM_SCALE) * qng_ref[...]).astype(BF16)
        s = lax.dot_general(qn, km_ref[:, cols], (((1,), (1,)), ((), ())), preferred_element_type=F32)
        e = jnp.exp(s - jnp.max(s, axis=-1, keepdims=True))
        p = (e / jnp.sum(e, axis=-1, keepdims=True)).astype(BF16)
        c_scr[:, cols] = jnp.dot(p, vm_ref[:, cols], preferred_element_type=F32).astype(BF16)
    mc = jnp.dot(c_scr[...], wpc_ref[...], preferred_element_type=F32)
    o_ref[...] = (m0_ref[...].astype(F32) + g2_ref[...].astype(F32) * mc).astype(BF16)


def _mem_attn(pg, km, vm, m0, qng, wpc, S):
    T = pg.shape[0]
    spb = S // MA_TM
    return pl.pallas_call(
        _mem_attn_kernel,
        grid=(T // MA_TM,),
        in_specs=[
            pl.BlockSpec((MA_TM, MEM_WIDTH), lambda i: (i, (2 * GM_WIDTH + Q_LORA + KV_LORA) // MEM_WIDTH)),
            pl.BlockSpec((None, MEM_LEN, MEM_WIDTH), lambda i: (i // spb, 0, 0)),
            pl.BlockSpec((None, MEM_LEN, MEM_WIDTH), lambda i: (i // spb, 0, 0)),
            pl.BlockSpec((MA_TM, D_MODEL), lambda i: (i, PROJ_COLS // D_MODEL + 2)),
            pl.BlockSpec((MA_TM, D_MODEL), lambda i: (i, 0)),
            _const_spec((1, MEM_HEAD_DIM)),
            _const_spec((MEM_WIDTH, D_MODEL)),
        ],
        out_specs=pl.BlockSpec((MA_TM, D_MODEL), lambda i: (i, 0)),
        out_shape=jax.ShapeDtypeStruct((T, D_MODEL), BF16),
        scratch_shapes=[pltpu.VMEM((MA_TM, MEM_WIDTH), BF16)],
        compiler_params=_params(("parallel",), 40),
        name="mem_attention",
    )(pg, km, vm, pg, m0, qng, wpc)


MG_TM = 512


def _pack_bf16_pair(a, b):
    hi = pltpu.bitcast(a.astype(BF16).astype(F32), jnp.uint32)
    lo = pltpu.bitcast(b.astype(BF16).astype(F32), jnp.uint32)
    return hi | (lo >> 16)


def _unpack_bf16_pair(p):
    hi = pltpu.bitcast(p & jnp.uint32(0xFFFF0000), F32)
    lo = pltpu.bitcast(p << 16, F32)
    return hi, lo


SUBLANES = 8
ROW_WORDS = D_MODEL // 2
ROW_CHUNKS = ROW_WORDS // LANES
assert ROW_CHUNKS == SUBLANES


def _row_chunk(n_rows, c):
    return pl.ds(c, n_rows, stride=SUBLANES)


def _store_row_tiles(ref, n_rows, packed):
    for c in range(ROW_CHUNKS):
        ref[_row_chunk(n_rows, c), :] = packed[:, c * LANES:(c + 1) * LANES]


def _attn_proj_kernel(b_ref, g1_ref, m1_ref, wpb_ref, o_ref):
    mb = jnp.dot(b_ref[...], wpb_ref[...], preferred_element_type=F32)
    o_ref[...] = (m1_ref[...].astype(F32) + g1_ref[...].astype(F32) * mb).astype(BF16)


def _attn_proj(b_attn, pg, m1, wpb):
    T = m1.shape[0]
    row = pl.BlockSpec((MG_TM, D_MODEL), lambda i: (i, 0))
    return pl.pallas_call(
        _attn_proj_kernel,
        grid=(T // MG_TM,),
        in_specs=[
            row,
            pl.BlockSpec((MG_TM, D_MODEL), lambda i: (i, PROJ_COLS // D_MODEL + 1)),
            row,
            _const_spec((MLA_HEADS * V_DIM, D_MODEL)),
        ],
        out_specs=row,
        out_shape=jax.ShapeDtypeStruct((T, D_MODEL), BF16),
        compiler_params=_params(("parallel",), 40),
        name="attn_proj",
    )(b_attn, pg, m1, wpb)


def _merge_kernel(mg_ref, x_ref, wo_ref, ln2_ref, rw_ref, rb_ref,
                  x1_ref, h2p_ref, code_ref, gate_ref, cnt_ref, cnt_scr):
    @pl.when(pl.program_id(0) == 0)
    def _():
        cnt_scr[...] = jnp.zeros_like(cnt_scr)

    x1 = x_ref[...] + jnp.dot(mg_ref[...], wo_ref[...], preferred_element_type=F32)
    x1_ref[...] = x1
    ms = jnp.mean(x1 * x1, axis=-1, keepdims=True)
    h2 = x1 * lax.rsqrt(ms + EPS) * ln2_ref[...]
    _store_row_tiles(h2p_ref, MG_TM, _pack_bf16_pair(h2[:, :D_MODEL // 2], h2[:, D_MODEL // 2:]))

    logits = jnp.dot(h2.astype(BF16), rw_ref[...], preferred_element_type=F32)
    lane = lax.broadcasted_iota(jnp.int32, (MG_TM, LANES), 1)
    work = jnp.where(lane < N_EXPERTS, logits + rb_ref[...], NEG_INF)
    earlier = (lax.broadcasted_iota(jnp.int32, (MG_TM, MG_TM), 1)
               < lax.broadcasted_iota(jnp.int32, (MG_TM, MG_TM), 0)).astype(BF16)
    base = cnt_scr[...]
    code_out = jnp.zeros((MG_TM, LANES), jnp.int32)
    val_out = jnp.zeros((MG_TM, LANES), F32)
    top = None
    denom = jnp.zeros((MG_TM, 1), F32)
    for k in range(TOP_K):
        mk = jnp.max(work, axis=-1, keepdims=True)
        ik = jnp.min(jnp.where(work == mk, lane, LANES), axis=-1, keepdims=True)
        hit = lane == ik
        work = jnp.where(hit, NEG_INF, work)
        if top is None:
            top = mk
        ek = jnp.exp(mk - top)
        denom = denom + ek
        val_out = jnp.where(lane == k, ek, val_out)
        onehot = hit.astype(BF16)
        prefix = jnp.dot(earlier, onehot, preferred_element_type=F32) + base
        rank = jnp.sum(jnp.where(hit, prefix, 0.0), axis=-1, keepdims=True).astype(jnp.int32)
        code_out = jnp.where(lane == k, rank * N_EXPERTS + ik, code_out)
        base = base + jnp.sum(onehot.astype(F32), axis=0, keepdims=True)
    code_ref[...] = code_out
    gate_ref[...] = val_out / denom
    cnt_scr[...] = base
    cnt_ref[...] = base


def _merge(merged, x2, wo, ln2, rw, rb):
    T = x2.shape[0]
    row = lambda w: pl.BlockSpec((MG_TM, w), lambda i: (i, 0))
    return pl.pallas_call(
        _merge_kernel,
        grid=(T // MG_TM,),
        in_specs=[
            row(D_MODEL),
            row(D_MODEL),
            _const_spec((D_MODEL, D_MODEL)),
            _const_spec((1, D_MODEL)),
            _const_spec((D_MODEL, LANES)),
            _const_spec((1, LANES)),
        ],
        out_specs=[row(D_MODEL), pl.BlockSpec((MG_TM * SUBLANES, LANES), lambda i: (i, 0)), row(LANES),
                   row(LANES), pl.BlockSpec((1, LANES), lambda i: (0, 0))],
        out_shape=[
            jax.ShapeDtypeStruct((T, D_MODEL), F32),
            jax.ShapeDtypeStruct((T * SUBLANES, LANES), jnp.uint32),
            jax.ShapeDtypeStruct((T, LANES), jnp.int32),
            jax.ShapeDtypeStruct((T, LANES), F32),
            jax.ShapeDtypeStruct((1, LANES), F32),
        ],
        scratch_shapes=[pltpu.VMEM((1, LANES), F32)],
        compiler_params=_params(("arbitrary",), 56),
        name="merge_router",
    )(merged, x2, wo, ln2, rw, rb)


DP_TM = 256
DMA_UNROLL = 8


def _row_tile(row):
    return pl.ds(pl.multiple_of(row * SUBLANES, SUBLANES), SUBLANES)


def _dispatch_kernel(dest_ref, h2p_ref, xs_in, xs_hbm, sem):
    del xs_in

    def issue(t, carry):
        src = h2p_ref.at[_row_tile(t), :]
        for k in range(TOP_K):
            row = dest_ref[0, t * TOP_K + k]
            pltpu.make_async_copy(src, xs_hbm.at[_row_tile(row), :], sem).start(priority=k % 2)
        return carry

    lax.fori_loop(0, DP_TM, issue, 0, unroll=DMA_UNROLL // TOP_K)
    all_rows = xs_hbm.at[pl.ds(0, DP_TM * TOP_K * SUBLANES), :]
    pltpu.make_async_copy(all_rows, all_rows, sem).wait()


def _dispatch(dest3, h2p, xs_zero):
    T = h2p.shape[0] // SUBLANES
    return pl.pallas_call(
        _dispatch_kernel,
        grid=(T // DP_TM,),
        in_specs=[
            pl.BlockSpec((None, 1, DP_TM * TOP_K), lambda i: (i, 0, 0), memory_space=pltpu.SMEM),
            pl.BlockSpec((DP_TM * SUBLANES, LANES), lambda i: (i, 0)),
            pl.BlockSpec(memory_space=pl.ANY),
        ],
        out_specs=pl.BlockSpec(memory_space=pl.ANY),
        out_shape=jax.ShapeDtypeStruct(xs_zero.shape, xs_zero.dtype),
        scratch_shapes=[pltpu.SemaphoreType.DMA(())],
        input_output_aliases={2: 0},
        compiler_params=_params(("arbitrary",), 32),
        name="moe_dispatch",
    )(dest3, h2p, xs_zero)


GROUP_SUB = 4
GROUP_ROWS = GROUP_SUB * ROW_BLOCK
FF_TILE = 512
N_FF_TILES = D_FF // FF_TILE
DOWN_CHUNK = 512


def _expert_kernel(ge_ref, gs_ref, gn_ref, xs_in, wg_ref, wu_ref, bg_ref, bu_ref, wd_ref, bd_ref,
                   rows_hbm, xwin, x_scr, acc_scr, stage, wsem, sem):
    del ge_ref, xs_in
    g = pl.program_id(0)
    f = pl.program_id(1)
    ns = gn_ref[g]
    half = D_MODEL // 2
    slot = g & 1

    def win_copy(group, s):
        rows = pl.ds(gs_ref[group] * (ROW_BLOCK * SUBLANES), GROUP_ROWS * SUBLANES)
        return pltpu.make_async_copy(rows_hbm.at[rows, :], xwin.at[s], wsem.at[s])

    def out_copy(s, start_block):
        rows = pl.ds((start_block + s) * (ROW_BLOCK * SUBLANES), ROW_BLOCK * SUBLANES)
        return pltpu.make_async_copy(stage.at[s], rows_hbm.at[rows, :], sem.at[s])

    @pl.when((f == 0) & (ns > 0))
    def _():
        @pl.when(g == 0)
        def _():
            win_copy(0, 0).start()

        win_copy(g, slot).wait()
        @pl.when(g + 1 < pl.num_programs(0))
        def _():
            @pl.when(gn_ref[g + 1] > 0)
            def _():
                win_copy(g + 1, 1 - slot).start()

        for c in range(ROW_CHUNKS):
            hi, lo = _unpack_bf16_pair(xwin[slot, _row_chunk(GROUP_ROWS, c), :])
            x_scr[:, c * LANES:(c + 1) * LANES] = hi.astype(BF16)
            x_scr[:, half + c * LANES:half + (c + 1) * LANES] = lo.astype(BF16)
        acc_scr[...] = jnp.broadcast_to(bd_ref[...], acc_scr.shape)

    for n in range(1, GROUP_SUB + 1):
        @pl.when(ns == n)
        def _(n=n):
            m = n * ROW_BLOCK
            x = x_scr[:m, :]
            gate = jnp.dot(x, wg_ref[...].astype(BF16), preferred_element_type=F32) + bg_ref[...]
            up = jnp.dot(x, wu_ref[...].astype(BF16), preferred_element_type=F32) + bu_ref[...]
            gate = jnp.minimum(gate, SWIGLU_LIMIT)
            up = jnp.clip(up, -SWIGLU_LIMIT, SWIGLU_LIMIT)
            glu = gate * jax.nn.sigmoid(gate * SWIGLU_ALPHA)
            act = ((up + 1.0) * glu).astype(BF16)
            for c in range(D_MODEL // DOWN_CHUNK):
                cols = slice(c * DOWN_CHUNK, (c + 1) * DOWN_CHUNK)
                acc_scr[:m, cols] += jnp.dot(act, wd_ref[:, cols].astype(BF16), preferred_element_type=F32)

    @pl.when(f == N_FF_TILES - 1)
    def _():
        @pl.when(g > 0)
        def _():
            prev = gn_ref[g - 1]
            for s in range(GROUP_SUB):
                @pl.when(s < prev)
                def _(s=s):
                    out_copy(s, 0).wait()

        for s in range(GROUP_SUB):
            @pl.when(s < ns)
            def _(s=s):
                y = acc_scr[s * ROW_BLOCK:(s + 1) * ROW_BLOCK, :]
                _store_row_tiles(stage.at[s], ROW_BLOCK, _pack_bf16_pair(y[:, :half], y[:, half:]))
                out_copy(s, gs_ref[g]).start()

        @pl.when(g == pl.num_programs(0) - 1)
        def _():
            for s in range(GROUP_SUB):
                @pl.when(s < ns)
                def _(s=s):
                    out_copy(s, 0).wait()


def _experts(g_exp, g_start, g_nsub, xs, wgu, bgu, wd, bd):
    def ff(f, gn, g):
        return jnp.where(gn[g] > 0, f, N_FF_TILES - 1)

    grid_spec = pltpu.PrefetchScalarGridSpec(
        num_scalar_prefetch=3,
        grid=(g_exp.shape[0], N_FF_TILES),
        in_specs=[
            pl.BlockSpec(memory_space=pl.ANY),
            pl.BlockSpec((None, D_MODEL, FF_TILE), lambda g, f, ge, gs, gn: (ge[g], 0, ff(f, gn, g))),
            pl.BlockSpec((None, D_MODEL, FF_TILE),
                         lambda g, f, ge, gs, gn: (ge[g], 0, N_FF_TILES + ff(f, gn, g))),
            pl.BlockSpec((None, 1, FF_TILE), lambda g, f, ge, gs, gn: (ge[g], 0, ff(f, gn, g))),
            pl.BlockSpec((None, 1, FF_TILE), lambda g, f, ge, gs, gn: (ge[g], 0, N_FF_TILES + ff(f, gn, g))),
            pl.BlockSpec((None, FF_TILE, D_MODEL), lambda g, f, ge, gs, gn: (ge[g], ff(f, gn, g), 0)),
            pl.BlockSpec((None, 1, D_MODEL), lambda g, f, ge, gs, gn: (ge[g], 0, 0)),
        ],
        out_specs=pl.BlockSpec(memory_space=pl.ANY),
        scratch_shapes=[
            pltpu.VMEM((2, GROUP_ROWS * SUBLANES, LANES), jnp.uint32),
            pltpu.VMEM((GROUP_ROWS, D_MODEL), BF16),
            pltpu.VMEM((GROUP_ROWS, D_MODEL), F32),
            pltpu.VMEM((GROUP_SUB, ROW_BLOCK * SUBLANES, LANES), jnp.uint32),
            pltpu.SemaphoreType.DMA((2,)),
            pltpu.SemaphoreType.DMA((GROUP_SUB,)),
        ],
    )
    return pl.pallas_call(
        _expert_kernel,
        grid_spec=grid_spec,
        out_shape=jax.ShapeDtypeStruct(xs.shape, jnp.uint32),
        input_output_aliases={3: 0},
        compiler_params=_params(("arbitrary", "arbitrary"), 58),
        name="moe_experts",
    )(g_exp, g_start, g_nsub, xs, wgu, wgu, bgu, bgu, wd, bd)


CB_TM = 128


def _combine_kernel(dest_ref, dest_next_ref, ys_hbm, x1_ref, gate_ref, o_ref, buf, sem):
    i = pl.program_id(0)
    slot = i & 1

    def gather_tile(dest, s):
        def issue(t, carry):
            for k in range(TOP_K):
                row = dest[0, t * TOP_K + k]
                pltpu.make_async_copy(ys_hbm.at[_row_tile(row), :], buf.at[s, k, _row_tile(t), :],
                                      sem.at[s]).start(priority=k % 2)
            return carry

        lax.fori_loop(0, CB_TM, issue, 0, unroll=DMA_UNROLL // TOP_K)

    @pl.when(i == 0)
    def _():
        gather_tile(dest_ref, 0)

    @pl.when(i + 1 < pl.num_programs(0))
    def _():
        gather_tile(dest_next_ref, 1 - slot)

    pltpu.make_async_copy(buf.at[slot], buf.at[slot], sem.at[slot]).wait()

    half = D_MODEL // 2
    gates = gate_ref[...]
    gate_k = [jnp.broadcast_to(gates[:, k:k + 1], (CB_TM, LANES)) for k in range(TOP_K)]
    for c in range(ROW_CHUNKS):
        cols_hi = slice(c * LANES, (c + 1) * LANES)
        cols_lo = slice(half + c * LANES, half + (c + 1) * LANES)
        out_hi = x1_ref[:, cols_hi]
        out_lo = x1_ref[:, cols_lo]
        for k in range(TOP_K):
            hi, lo = _unpack_bf16_pair(buf[slot, k, _row_chunk(CB_TM, c), :])
            out_hi = out_hi + gate_k[k] * hi
            out_lo = out_lo + gate_k[k] * lo
        o_ref[:, cols_hi] = out_hi
        o_ref[:, cols_lo] = out_lo


def _combine(dest3, ys, x1, gates):
    T = x1.shape[0]
    n = T // CB_TM
    dest_spec = lambda f: pl.BlockSpec((None, 1, CB_TM * TOP_K), f, memory_space=pltpu.SMEM)
    return pl.pallas_call(
        _combine_kernel,
        grid=(n,),
        in_specs=[
            dest_spec(lambda i: (i, 0, 0)),
            dest_spec(lambda i: (jnp.minimum(i + 1, n - 1), 0, 0)),
            pl.BlockSpec(memory_space=pl.ANY),
            pl.BlockSpec((CB_TM, D_MODEL), lambda i: (i, 0)),
            pl.BlockSpec((CB_TM, LANES), lambda i: (i, 0)),
        ],
        out_specs=pl.BlockSpec((CB_TM, D_MODEL), lambda i: (i, 0)),
        out_shape=jax.ShapeDtypeStruct((T, D_MODEL), F32),
        scratch_shapes=[pltpu.VMEM((2, TOP_K, CB_TM * SUBLANES, LANES), jnp.uint32),
                        pltpu.SemaphoreType.DMA((2,))],
        compiler_params=_params(("arbitrary",), 32),
        name="moe_combine",
    )(dest3, dest3, ys, x1, gates)


def _group_tables(counts_f32, max_groups):
    i32 = jnp.int32
    counts = counts_f32[0, :N_EXPERTS].astype(i32)
    nb = (counts + ROW_BLOCK - 1) // ROW_BLOCK
    ng = (nb + GROUP_SUB - 1) // GROUP_SUB
    upto = jnp.arange(N_EXPERTS)[None, :] <= jnp.arange(N_EXPERTS)[:, None]
    pad_ends = jnp.sum(jnp.where(upto, nb[None, :], 0), axis=1) * ROW_BLOCK
    pad_starts = pad_ends - nb * ROW_BLOCK
    g_ends = jnp.sum(jnp.where(upto, ng[None, :], 0), axis=1)
    g_starts = g_ends - ng
    n_groups = g_ends[-1]

    g = jnp.arange(max_groups, dtype=i32)
    gg = jnp.minimum(g, n_groups - 1)
    e = jnp.minimum(jnp.sum((g_ends[None, :] <= gg[:, None]).astype(i32), axis=1), N_EXPERTS - 1)
    pick = e[:, None] == jnp.arange(N_EXPERTS)[None, :]
    take = lambda table: jnp.sum(jnp.where(pick, table[None, :], 0), axis=1)
    nb_g, ng_g = take(nb), jnp.maximum(take(ng), 1)
    j = gg - take(g_starts)
    base, rem = nb_g // ng_g, nb_g % ng_g
    nsub = jnp.where(g < n_groups, base + (j < rem).astype(i32), 0)
    start_block = take(pad_starts) // ROW_BLOCK + j * base + jnp.minimum(j, rem)
    return pad_starts.astype(i32), e.astype(i32), start_block.astype(i32), nsub.astype(i32)


def _rope_constants():
    lane = np.arange(LANES)
    half = QK_ROPE // 2
    inv = 1.0 / (ROPE_THETA ** (np.arange(0, QK_ROPE, 2, dtype=np.float32) / QK_ROPE))
    cst = np.zeros((8, LANES), np.float32)
    cst[0, :QK_ROPE] = inv.astype(np.float32)[lane[:QK_ROPE] % half]
    cst[1, :QK_ROPE] = 1.0
    cst[2, :half] = -1.0
    cst[2, half:QK_ROPE] = 1.0
    return jnp.asarray(cst)


def _swap_halves(a):
    half = QK_ROPE // 2
    return jnp.concatenate([a[..., half:], a[..., :half]], axis=-1)


def kernel(x, mem, positions, ln1_g, w_in, w_gate, b_gate, gmlp_ln_g, gmlp_ln_b, gmlp_ws, gmlp_bs, w_pa,
           mla_cq_g, mla_w_uq, mla_ckv_g, mla_w_ukv, mla_qn_g, mla_kn_g, w_pb, mem_ln_g, mem_w_k, mem_w_v,
           mem_qn_g, mem_kn_g, w_pc, w_o, ln2_g, router_w, router_b, moe_w_gu, moe_b_gu, moe_w_down,
           moe_b_down):
    B, S, D = x.shape
    T = B * S
    x2 = x.reshape(T, D)
    for l in range(ln1_g.shape[0]):
        o_kr = 2 * GM_WIDTH + Q_LORA + KV_LORA
        o_qm = o_kr + QK_ROPE
        wi = w_in[l]
        w1 = jnp.concatenate([wi[:, :o_kr], wi[:, o_qm:], w_gate[l]], axis=1).astype(BF16)
        b1 = jnp.concatenate([jnp.zeros((PROJ_COLS,), F32), b_gate[l]])[None, :]
        w_kr = wi[:, o_kr:o_qm]
        wr = jnp.concatenate([w_kr, _swap_halves(w_kr)], axis=1).astype(BF16)

        wq3 = mla_w_uq[l].reshape(Q_LORA, MLA_HEADS, QK_DIM)
        wq = jnp.concatenate([wq3, _swap_halves(wq3[..., QK_NOPE:])], axis=-1)
        wq = wq.reshape(Q_LORA, MLA_HEADS * QK_PAD).astype(BF16)
        wkv = mla_w_ukv[l].astype(BF16)
        gq = jnp.concatenate([mla_qn_g[l], _swap_halves(mla_qn_g[l][QK_NOPE:])])[None, :]
        gk = jnp.concatenate([mla_kn_g[l], _swap_halves(mla_kn_g[l][QK_NOPE:])])[None, :]

        bias_full = jnp.broadcast_to(gmlp_bs[l].T[:, :, None], (GM_CHUNK, GM_GROUPS, GM_CHUNK))
        bias_full = bias_full.reshape(GM_CHUNK, GM_WIDTH)

        rw = jnp.pad(router_w[l], ((0, 0), (0, LANES - N_EXPERTS))).astype(BF16)
        rb = jnp.pad(router_b[l], (0, LANES - N_EXPERTS))[None, :]

        pg, kr = _norm_proj(x2, ln1_g[l][None, :], w1, b1, wr)
        m0 = _gmlp(pg, gmlp_ln_g[l][None, :], gmlp_ln_b[l][None, :], gmlp_ws[l].astype(BF16), bias_full,
                   w_pa[l].astype(BF16))
        q, k, vt = _mla_qkv(pg, kr, positions.reshape(T, 1), mla_cq_g[l][None, :], mla_ckv_g[l][None, :],
                           wq, wkv, gq, gk, _rope_constants(), B, S)
        b_attn = _flash(q, k, vt)
        km, vm = _mem_kv(mem, mem_ln_g[l][None, :], mem_w_k[l].astype(BF16), mem_w_v[l].astype(BF16),
                         mem_kn_g[l][None, :])
        m1 = _mem_attn(pg, km, vm, m0, mem_qn_g[l][None, :], w_pc[l].astype(BF16), S)
        merged = _attn_proj(b_attn, pg, m1, w_pb[l].astype(BF16))
        x1, h2p, code, gates, counts = _merge(merged, x2, w_o[l].astype(BF16), ln2_g[l][None, :], rw, rb)

        n_rows = T * TOP_K + N_EXPERTS * ROW_BLOCK
        max_groups = n_rows // GROUP_ROWS + N_EXPERTS
        pad_starts, g_exp, g_start, g_nsub = _group_tables(counts, max_groups)
        codes = code[:, :TOP_K]
        pick = (codes & (N_EXPERTS - 1))[..., None] == jnp.arange(N_EXPERTS)
        dest = (codes >> EXPERT_BITS) + jnp.sum(jnp.where(pick, pad_starts, 0), axis=-1)
        xs = _dispatch(dest.reshape(T // DP_TM, 1, DP_TM * TOP_K), h2p,
                       jnp.zeros(((n_rows + GROUP_ROWS) * SUBLANES, LANES), jnp.uint32))
        ys = _experts(g_exp, g_start, g_nsub, xs, moe_w_gu[l], moe_b_gu[l][:, None, :],
                      moe_w_down[l], moe_b_down[l][:, None, :])
        x2 = _combine(dest.reshape(T // CB_TM, 1, CB_TM * TOP_K), ys, x1, gates)
    return x2.reshape(B, S, D)
```

```python
import functools

import numpy as np
import jax
import jax.numpy as jnp
from jax import lax
from jax.experimental import pallas as pl
from jax.experimental.pallas import tpu as pltpu

F32 = jnp.float32
BF16 = jnp.bfloat16

D_MODEL = 2048
GM_WIDTH = 1024
GM_GROUPS = 8
GM_CHUNK = 128
MLA_HEADS = 16
Q_LORA = 512
KV_LORA = 512
QK_NOPE = 128
QK_ROPE = 64
V_DIM = 128
QK_DIM = QK_NOPE + QK_ROPE
QK_PAD = 256
MLA_SCALE = QK_DIM ** -0.5
LOG2_E = 1.4426950408889634
ROPE_THETA = 10000.0
MEM_LEN = 256
MEM_HEADS = 4
MEM_HEAD_DIM = 256
MEM_WIDTH = MEM_HEADS * MEM_HEAD_DIM
MEM_SCALE = MEM_HEAD_DIM ** -0.5
N_EXPERTS = 32
EXPERT_BITS = 5
TOP_K = 4
D_FF = 2048
SWIGLU_LIMIT = 7.0
SWIGLU_ALPHA = 1.702
ROW_BLOCK = 256
EPS = 1e-6
LANES = 128
NEG_INF = float("-inf")

PROJ_COLS = 2 * GM_WIDTH + Q_LORA + KV_LORA + MEM_WIDTH
PG_COLS = PROJ_COLS + 3 * D_MODEL

MIB = 1024 * 1024


def _params(semantics, vmem_mib):
    return pltpu.CompilerParams(dimension_semantics=semantics, vmem_limit_bytes=vmem_mib * MIB)


def _const_spec(shape):
    nd = len(shape)
    return pl.BlockSpec(shape, lambda *_: (0,) * nd, pipeline_mode=pl.Buffered(1))


P1_TM = 1024
P1_TN = 1024


def _norm_proj_kernel(x_ref, g_ref, w_ref, b_ref, wr_ref, o_ref, kr_ref, h_scr, *, n_plain):
    j = pl.program_id(1)

    @pl.when(j == 0)
    def _():
        def body(c, carry):
            rows = pl.ds(pl.multiple_of(c * 128, 128), 128)
            x = x_ref[rows, :]
            ms = jnp.mean(x * x, axis=-1, keepdims=True)
            h_scr[rows, :] = (x * lax.rsqrt(ms + EPS) * g_ref[...]).astype(BF16)
            return carry

        lax.fori_loop(0, P1_TM // 128, body, 0)
        kr_ref[...] = jnp.dot(h_scr[...], wr_ref[...], preferred_element_type=F32)

    acc = jnp.dot(h_scr[...], w_ref[...], preferred_element_type=F32)
    gated = jax.nn.sigmoid(acc + b_ref[...])
    o_ref[...] = jnp.where(j >= n_plain, gated, acc).astype(BF16)


def _norm_proj(x2, ln1_g, w1, b1, wr):
    T = x2.shape[0]
    grid = (T // P1_TM, PG_COLS // P1_TN)
    return pl.pallas_call(
        functools.partial(_norm_proj_kernel, n_plain=PROJ_COLS // P1_TN),
        grid=grid,
        in_specs=[
            pl.BlockSpec((P1_TM, D_MODEL), lambda i, j: (i, 0)),
            pl.BlockSpec((1, D_MODEL), lambda i, j: (0, 0)),
            pl.BlockSpec((D_MODEL, P1_TN), lambda i, j: (0, j)),
            pl.BlockSpec((1, P1_TN), lambda i, j: (0, j)),
            pl.BlockSpec((D_MODEL, LANES), lambda i, j: (0, 0)),
        ],
        out_specs=[
            pl.BlockSpec((P1_TM, P1_TN), lambda i, j: (i, j)),
            pl.BlockSpec((P1_TM, LANES), lambda i, j: (i, 0)),
        ],
        out_shape=[
            jax.ShapeDtypeStruct((T, PG_COLS), BF16),
            jax.ShapeDtypeStruct((T, LANES), F32),
        ],
        scratch_shapes=[pltpu.VMEM((P1_TM, D_MODEL), BF16)],
        compiler_params=_params(("parallel", "arbitrary"), 48),
        name="norm_proj",
    )(x2, ln1_g, w1, b1, wr)


GM_TM = 512


def _gmlp_kernel(u_ref, v_ref, g0_ref, lng_ref, lnb_ref, ws_ref, bias_ref, wpa_ref, o_ref, vb_scr, a_scr):
    v = v_ref[...].astype(F32)
    mu = jnp.mean(v, axis=-1, keepdims=True)
    c = v - mu
    var = jnp.mean(c * c, axis=-1, keepdims=True)
    vb_scr[...] = (c * lax.rsqrt(var + EPS) * lng_ref[...] + lnb_ref[...]).astype(BF16)

    row = lax.broadcasted_iota(jnp.int32, (GM_CHUNK, GM_CHUNK), 0)
    col = lax.broadcasted_iota(jnp.int32, (GM_CHUNK, GM_CHUNK), 1)
    causal = col <= row
    for g in range(GM_GROUPS):
        cols = slice(g * GM_CHUNK, (g + 1) * GM_CHUNK)
        wg = jnp.where(causal, ws_ref[g], jnp.zeros((), BF16))
        for ch in range(GM_TM // GM_CHUNK):
            rows = slice(ch * GM_CHUNK, (ch + 1) * GM_CHUNK)
            mixed = jnp.dot(wg, vb_scr[rows, cols], preferred_element_type=F32) + bias_ref[:, cols]
            a_scr[rows, cols] = (u_ref[rows, cols].astype(F32) * mixed).astype(BF16)

    ma = jnp.dot(a_scr[...], wpa_ref[...], preferred_element_type=F32)
    o_ref[...] = (g0_ref[...].astype(F32) * ma).astype(BF16)


def _gmlp(pg, ln_g, ln_b, ws, bias_full, wpa):
    T = pg.shape[0]
    return pl.pallas_call(
        _gmlp_kernel,
        grid=(T // GM_TM,),
        in_specs=[
            pl.BlockSpec((GM_TM, GM_WIDTH), lambda i: (i, 0)),
            pl.BlockSpec((GM_TM, GM_WIDTH), lambda i: (i, 1)),
            pl.BlockSpec((GM_TM, D_MODEL), lambda i: (i, PROJ_COLS // D_MODEL)),
            _const_spec((1, GM_WIDTH)),
            _const_spec((1, GM_WIDTH)),
            _const_spec((GM_GROUPS, GM_CHUNK, GM_CHUNK)),
            _const_spec((GM_CHUNK, GM_WIDTH)),
            _const_spec((GM_WIDTH, D_MODEL)),
        ],
        out_specs=pl.BlockSpec((GM_TM, D_MODEL), lambda i: (i, 0)),
        out_shape=jax.ShapeDtypeStruct((T, D_MODEL), BF16),
        scratch_shapes=[pltpu.VMEM((GM_TM, GM_WIDTH), BF16), pltpu.VMEM((GM_TM, GM_WIDTH), BF16)],
        compiler_params=_params(("parallel",), 40),
        name="gmlp",
    )(pg, pg, pg, ln_g, ln_b, ws, bias_full, wpa)


QKV_TM = 1024
QKV_HEADS = 2
HEAD_BLK = 4
FA_T = 512
FA_QS = 256


def _rope(t, cos, sin):
    return t * cos + pltpu.roll(t, QK_ROPE, 1) * sin


def _row_sum_all_lanes(sq, weights):
    hi = sq.astype(BF16)
    lo = (sq - hi.astype(F32)).astype(BF16)
    w = weights.astype(BF16)
    return (jnp.dot(hi, w, preferred_element_type=F32) + jnp.dot(lo, w, preferred_element_type=F32))


def _mla_qkv_kernel(cq_ref, ckv_ref, kr_ref, pos_ref, gcq_ref, gckv_ref, wq_ref, wkv_ref, gq_ref, gk_ref,
                    cst_ref, sumw_ref, q_ref, k_ref, vt_ref, cqn_scr, ckvn_scr, cos_scr, sin_scr, krsq_scr):
    hb = pl.program_id(1)
    w_nope = sumw_ref[:QK_NOPE, :]
    w_rope = sumw_ref[QK_NOPE:, :]

    @pl.when(hb == 0)
    def _():
        cq = cq_ref[...].astype(F32)
        ms = jnp.mean(cq * cq, axis=-1, keepdims=True)
        cqn_scr[...] = (cq * lax.rsqrt(ms + EPS) * gcq_ref[...]).astype(BF16)
        ckv = ckv_ref[...].astype(F32)
        ms = jnp.mean(ckv * ckv, axis=-1, keepdims=True)
        ckvn_scr[...] = (ckv * lax.rsqrt(ms + EPS) * gckv_ref[...]).astype(BF16)
        ang = pos_ref[...].astype(F32) * cst_ref[0:1, :]
        cos_scr[...] = jnp.cos(ang) * cst_ref[1:2, :]
        sin_scr[...] = jnp.sin(ang) * cst_ref[2:3, :]
        kr = kr_ref[...]
        krsq_scr[...] = _row_sum_all_lanes(kr * kr, w_rope)

    cos = cos_scr[...]
    sin = sin_scr[...]
    inv_dim = 1.0 / QK_DIM
    kr = kr_ref[...]

    yq2 = jnp.dot(cqn_scr[...], wq_ref[...], preferred_element_type=F32)
    ykv2 = jnp.dot(ckvn_scr[...], wkv_ref[...], preferred_element_type=F32)
    for hh in range(QKV_HEADS):
        yq = yq2[:, hh * QK_PAD:(hh + 1) * QK_PAD]
        qn = yq[:, :QK_NOPE]
        qt = yq[:, QK_NOPE:]
        ssq = _row_sum_all_lanes(qn * qn, w_nope) + _row_sum_all_lanes(qt * qt, w_rope)
        rs = lax.rsqrt(ssq * inv_dim + EPS) * (MLA_SCALE * LOG2_E)
        q_ref[hh, :, :QK_NOPE] = (qn * rs * gq_ref[:, :QK_NOPE]).astype(BF16)
        q_ref[hh, :, QK_NOPE:] = _rope(qt * rs * gq_ref[:, QK_NOPE:], cos, sin).astype(BF16)

        ykv = ykv2[:, hh * (QK_NOPE + V_DIM):(hh + 1) * (QK_NOPE + V_DIM)]
        kn = ykv[:, :QK_NOPE]
        ssq = _row_sum_all_lanes(kn * kn, w_nope) + krsq_scr[...]
        rs = lax.rsqrt(ssq * inv_dim + EPS)
        k_ref[hh, :, :QK_NOPE] = (kn * rs * gk_ref[:, :QK_NOPE]).astype(BF16)
        k_ref[hh, :, QK_NOPE:] = _rope(kr * rs * gk_ref[:, QK_NOPE:], cos, sin).astype(BF16)
        vv = ykv[:, QK_NOPE:]
        for c in range(QKV_TM // FA_T):
            vt_ref[hh, c] = vv[c * FA_T:(c + 1) * FA_T, :].T.astype(BF16)


def _mla_qkv(pg, kr, pos, gcq, gckv, wq, wkv, gq, gk, cst, B, S):
    T = pg.shape[0]
    sumw = jnp.concatenate([jnp.ones((QK_NOPE, LANES), F32), jnp.full((QK_PAD - QK_NOPE, LANES), 0.5, F32)])
    spb = S // QKV_TM
    cpt = QKV_TM // FA_T
    head_spec = lambda w: pl.BlockSpec((None, QKV_HEADS, QKV_TM, w), lambda i, h: (i // spb, h, i % spb, 0))
    return pl.pallas_call(
        _mla_qkv_kernel,
        grid=(T // QKV_TM, MLA_HEADS // QKV_HEADS),
        in_specs=[
            pl.BlockSpec((QKV_TM, Q_LORA), lambda i, h: (i, 2 * GM_WIDTH // Q_LORA)),
            pl.BlockSpec((QKV_TM, KV_LORA), lambda i, h: (i, 2 * GM_WIDTH // KV_LORA + 1)),
            pl.BlockSpec((QKV_TM, LANES), lambda i, h: (i, 0)),
            pl.BlockSpec((QKV_TM, 1), lambda i, h: (i, 0)),
            pl.BlockSpec((1, Q_LORA), lambda i, h: (0, 0)),
            pl.BlockSpec((1, KV_LORA), lambda i, h: (0, 0)),
            pl.BlockSpec((Q_LORA, QKV_HEADS * QK_PAD), lambda i, h: (0, h)),
            pl.BlockSpec((KV_LORA, QKV_HEADS * (QK_NOPE + V_DIM)), lambda i, h: (0, h)),
            pl.BlockSpec((1, QK_PAD), lambda i, h: (0, 0)),
            pl.BlockSpec((1, QK_PAD), lambda i, h: (0, 0)),
            pl.BlockSpec((8, LANES), lambda i, h: (0, 0)),
            pl.BlockSpec((QK_PAD, LANES), lambda i, h: (0, 0)),
        ],
        out_specs=[
            head_spec(QK_PAD),
            head_spec(QK_PAD),
            pl.BlockSpec((None, QKV_HEADS, cpt, V_DIM, FA_T), lambda i, h: (i // spb, h, i % spb, 0, 0)),
        ],
        out_shape=[
            jax.ShapeDtypeStruct((B, MLA_HEADS, S, QK_PAD), BF16),
            jax.ShapeDtypeStruct((B, MLA_HEADS, S, QK_PAD), BF16),
            jax.ShapeDtypeStruct((B, MLA_HEADS, S // FA_T, V_DIM, FA_T), BF16),
        ],
        scratch_shapes=[
            pltpu.VMEM((QKV_TM, Q_LORA), BF16),
            pltpu.VMEM((QKV_TM, KV_LORA), BF16),
            pltpu.VMEM((QKV_TM, LANES), F32),
            pltpu.VMEM((QKV_TM, LANES), F32),
            pltpu.VMEM((QKV_TM, LANES), F32),
        ],
        compiler_params=_params(("parallel", "arbitrary"), 48),
        name="mla_qkv",
    )(pg, pg, kr, pos, gcq, gckv, wq, wkv, gq, gk, cst, sumw)


def _flash_kernel(q_ref, k_ref, vt_ref, o_ref, m_scr, l_scr, acc_scr):
    qi = pl.program_id(2)
    m_scr[...] = jnp.full_like(m_scr, NEG_INF)
    l_scr[...] = jnp.zeros_like(l_scr)
    acc_scr[...] = jnp.zeros_like(acc_scr)

    def step(kb, diagonal):
        ks = pl.ds(pl.multiple_of(kb * FA_T, FA_T), FA_T)
        chains = [(hh, qs) for hh in range(HEAD_BLK) for qs in range(FA_T // FA_QS)]
        scores = []
        for hh, qs in chains:
            qc = slice(qs * FA_QS, (qs + 1) * FA_QS)
            st = lax.dot_general(k_ref[hh, ks, :], q_ref[hh, qc, :], (((1,), (1,)), ((), ())),
                                 preferred_element_type=F32)
            if diagonal:
                krow = lax.broadcasted_iota(jnp.int32, (FA_T, FA_QS), 0)
                qcol = lax.broadcasted_iota(jnp.int32, (FA_T, FA_QS), 1) + qs * FA_QS
                st = jnp.where(krow <= qcol, st, NEG_INF)
            scores.append(st)
        probs = []
        for (hh, qs), st in zip(chains, scores):
            qc = slice(qs * FA_QS, (qs + 1) * FA_QS)
            m_prev = m_scr[hh, :, qc]
            m_new = jnp.maximum(m_prev, jnp.max(st, axis=0, keepdims=True))
            alpha = jnp.exp2(m_prev - m_new)
            p = jnp.exp2(st - m_new)
            l_scr[hh, :, qc] = alpha * l_scr[hh, :, qc] + jnp.sum(p, axis=0, keepdims=True)
            m_scr[hh, :, qc] = m_new
            probs.append((alpha, p.astype(BF16)))
        for (hh, qs), (alpha, p) in zip(chains, probs):
            qc = slice(qs * FA_QS, (qs + 1) * FA_QS)
            acc_scr[hh, :, qc] = alpha * acc_scr[hh, :, qc] + jnp.dot(vt_ref[hh, kb], p,
                                                                      preferred_element_type=F32)

    def body(kb, carry):
        step(kb, False)
        return carry

    lax.fori_loop(0, qi, body, 0)
    step(qi, True)
    for hh in range(HEAD_BLK):
        o_ref[:, hh * V_DIM:(hh + 1) * V_DIM] = (acc_scr[hh] / l_scr[hh]).T.astype(BF16)


def _flash(q, k, vt):
    B, H, S, _ = q.shape
    nq = S // FA_T
    return pl.pallas_call(
        _flash_kernel,
        grid=(B, H // HEAD_BLK, nq),
        in_specs=[
            pl.BlockSpec((None, HEAD_BLK, FA_T, QK_PAD), lambda b, h, i: (b, h, i, 0)),
            pl.BlockSpec((None, HEAD_BLK, S, QK_PAD), lambda b, h, i: (b, h, 0, 0)),
            pl.BlockSpec((None, HEAD_BLK, nq, V_DIM, FA_T), lambda b, h, i: (b, h, 0, 0, 0)),
        ],
        out_specs=pl.BlockSpec((FA_T, HEAD_BLK * V_DIM), lambda b, h, i: (b * nq + i, h)),
        out_shape=jax.ShapeDtypeStruct((B * S, H * V_DIM), BF16),
        scratch_shapes=[
            pltpu.VMEM((HEAD_BLK, 1, FA_T), F32),
            pltpu.VMEM((HEAD_BLK, 1, FA_T), F32),
            pltpu.VMEM((HEAD_BLK, V_DIM, FA_T), F32),
        ],
        compiler_params=_params(("parallel", "parallel", "arbitrary"), 40),
        name="mla_attention",
    )(q, k, vt)


def _mem_kv_kernel(mem_ref, g_ref, wk_ref, wv_ref, kng_ref, k_ref, v_ref):
    m = mem_ref[...]
    ms = jnp.mean(m * m, axis=-1, keepdims=True)
    mb = (m * lax.rsqrt(ms + EPS) * g_ref[...]).astype(BF16)
    kk = jnp.dot(mb, wk_ref[...], preferred_element_type=F32)
    for h in range(MEM_HEADS):
        cols = slice(h * MEM_HEAD_DIM, (h + 1) * MEM_HEAD_DIM)
        kh = kk[:, cols]
        ms = jnp.mean(kh * kh, axis=-1, keepdims=True)
        k_ref[:, cols] = (kh * lax.rsqrt(ms + EPS) * kng_ref[...]).astype(BF16)
    v_ref[...] = jnp.dot(mb, wv_ref[...], preferred_element_type=F32).astype(BF16)


def _mem_kv(mem, g, wk, wv, kng):
    B = mem.shape[0]
    return pl.pallas_call(
        _mem_kv_kernel,
        grid=(B,),
        in_specs=[
            pl.BlockSpec((None, MEM_LEN, D_MODEL), lambda b: (b, 0, 0)),
            _const_spec((1, D_MODEL)),
            _const_spec((D_MODEL, MEM_WIDTH)),
            _const_spec((D_MODEL, MEM_WIDTH)),
            _const_spec((1, MEM_HEAD_DIM)),
        ],
        out_specs=[
            pl.BlockSpec((None, MEM_LEN, MEM_WIDTH), lambda b: (b, 0, 0)),
            pl.BlockSpec((None, MEM_LEN, MEM_WIDTH), lambda b: (b, 0, 0)),
        ],
        out_shape=[
            jax.ShapeDtypeStruct((B, MEM_LEN, MEM_WIDTH), BF16),
            jax.ShapeDtypeStruct((B, MEM_LEN, MEM_WIDTH), BF16),
        ],
        compiler_params=_params(("parallel",), 32),
        name="mem_kv",
    )(mem, g, wk, wv, kng)


MA_TM = 512


def _mem_attn_kernel(qm_ref, km_ref, vm_ref, g2_ref, m0_ref, qng_ref, wpc_ref, o_ref, c_scr):
    for h in range(MEM_HEADS):
        cols = slice(h * MEM_HEAD_DIM, (h + 1) * MEM_HEAD_DIM)
        qh = qm_ref[:, cols].astype(F32)
        ms = jnp.mean(qh * qh, axis=-1, keepdims=True)
        qn = (qh * (lax.rsqrt(ms + EPS) * MEM_SCALE) * qng_ref[...]).astype(BF16)
        s = lax.dot_general(qn, km_ref[:, cols], (((1,), (1,)), ((), ())), preferred_element_type=F32)
        e = jnp.exp(s - jnp.max(s, axis=-1, keepdims=True))
        p = (e / jnp.sum(e, axis=-1, keepdims=True)).astype(BF16)
        c_scr[:, cols] = jnp.dot(p, vm_ref[:, cols], preferred_element_type=F32).astype(BF16)
    mc = jnp.dot(c_scr[...], wpc_ref[...], preferred_element_type=F32)
    o_ref[...] = (m0_ref[...].astype(F32) + g2_ref[...].astype(F32) * mc).astype(BF16)


def _mem_attn(pg, km, vm, m0, qng, wpc, S):
    T = pg.shape[0]
    spb = S // MA_TM
    return pl.pallas_call(
        _mem_attn_kernel,
        grid=(T // MA_TM,),
        in_specs=[
            pl.BlockSpec((MA_TM, MEM_WIDTH), lambda i: (i, (2 * GM_WIDTH + Q_LORA + KV_LORA) // MEM_WIDTH)),
            pl.BlockSpec((None, MEM_LEN, MEM_WIDTH), lambda i: (i // spb, 0, 0)),
            pl.BlockSpec((None, MEM_LEN, MEM_WIDTH), lambda i: (i // spb, 0, 0)),
            pl.BlockSpec((MA_TM, D_MODEL), lambda i: (i, PROJ_COLS // D_MODEL + 2)),
            pl.BlockSpec((MA_TM, D_MODEL), lambda i: (i, 0)),
            _const_spec((1, MEM_HEAD_DIM)),
            _const_spec((MEM_WIDTH, D_MODEL)),
        ],
        out_specs=pl.BlockSpec((MA_TM, D_MODEL), lambda i: (i, 0)),
        out_shape=jax.ShapeDtypeStruct((T, D_MODEL), BF16),
        scratch_shapes=[pltpu.VMEM((MA_TM, MEM_WIDTH), BF16)],
        compiler_params=_params(("parallel",), 40),
        name="mem_attention",
    )(pg, km, vm, pg, m0, qng, wpc)


MG_TM = 512


def _pack_bf16_pair(a, b):
    hi = pltpu.bitcast(a.astype(BF16).astype(F32), jnp.uint32)
    lo = pltpu.bitcast(b.astype(BF16).astype(F32), jnp.uint32)
    return hi | (lo >> 16)


def _unpack_bf16_pair(p):
    hi = pltpu.bitcast(p & jnp.uint32(0xFFFF0000), F32)
    lo = pltpu.bitcast(p << 16, F32)
    return hi, lo


SUBLANES = 8
ROW_WORDS = D_MODEL // 2
ROW_CHUNKS = ROW_WORDS // LANES
assert ROW_CHUNKS == SUBLANES


def _row_chunk(n_rows, c):
    return pl.ds(c, n_rows, stride=SUBLANES)


def _store_row_tiles(ref, n_rows, packed):
    for c in range(ROW_CHUNKS):
        ref[_row_chunk(n_rows, c), :] = packed[:, c * LANES:(c + 1) * LANES]


def _attn_proj_kernel(b_ref, g1_ref, m1_ref, wpb_ref, o_ref, zero_ref):
    mb = jnp.dot(b_ref[...], wpb_ref[...], preferred_element_type=F32)
    o_ref[...] = (m1_ref[...].astype(F32) + g1_ref[...].astype(F32) * mb).astype(BF16)
    zero_ref[...] = jnp.zeros_like(zero_ref)


def _attn_proj(b_attn, pg, m1, wpb, n_zero_rows):
    T = m1.shape[0]
    steps = T // MG_TM
    assert n_zero_rows % steps == 0
    zero_block = (n_zero_rows // steps * SUBLANES, LANES)
    row = pl.BlockSpec((MG_TM, D_MODEL), lambda i: (i, 0))
    return pl.pallas_call(
        _attn_proj_kernel,
        grid=(steps,),
        in_specs=[
            row,
            pl.BlockSpec((MG_TM, D_MODEL), lambda i: (i, PROJ_COLS // D_MODEL + 1)),
            row,
            _const_spec((MLA_HEADS * V_DIM, D_MODEL)),
        ],
        out_specs=[row, pl.BlockSpec(zero_block, lambda i: (i, 0))],
        out_shape=[jax.ShapeDtypeStruct((T, D_MODEL), BF16),
                   jax.ShapeDtypeStruct((n_zero_rows * SUBLANES, LANES), jnp.uint32)],
        compiler_params=_params(("parallel",), 56),
        name="attn_proj",
    )(b_attn, pg, m1, wpb)


def _merge_kernel(mg_ref, x_ref, wo_ref, ln2_ref, rw_ref, rb_ref,
                  x1_ref, h2p_ref, code_ref, gate_ref, cnt_ref, cnt_scr):
    @pl.when(pl.program_id(0) == 0)
    def _():
        cnt_scr[...] = jnp.zeros_like(cnt_scr)

    x1 = x_ref[...] + jnp.dot(mg_ref[...], wo_ref[...], preferred_element_type=F32)
    x1_ref[...] = x1
    ms = jnp.mean(x1 * x1, axis=-1, keepdims=True)
    h2 = x1 * lax.rsqrt(ms + EPS) * ln2_ref[...]
    _store_row_tiles(h2p_ref, MG_TM, _pack_bf16_pair(h2[:, :D_MODEL // 2], h2[:, D_MODEL // 2:]))

    logits = jnp.dot(h2.astype(BF16), rw_ref[...], preferred_element_type=F32)
    lane = lax.broadcasted_iota(jnp.int32, (MG_TM, LANES), 1)
    work = jnp.where(lane < N_EXPERTS, logits + rb_ref[...], NEG_INF)
    earlier = (lax.broadcasted_iota(jnp.int32, (MG_TM, MG_TM), 1)
               < lax.broadcasted_iota(jnp.int32, (MG_TM, MG_TM), 0)).astype(BF16)
    base = cnt_scr[...]
    code_out = jnp.zeros((MG_TM, LANES), jnp.int32)
    val_out = jnp.zeros((MG_TM, LANES), F32)
    top = None
    denom = jnp.zeros((MG_TM, 1), F32)
    for k in range(TOP_K):
        mk = jnp.max(work, axis=-1, keepdims=True)
        ik = jnp.min(jnp.where(work == mk, lane, LANES), axis=-1, keepdims=True)
        hit = lane == ik
        work = jnp.where(hit, NEG_INF, work)
        if top is None:
            top = mk
        ek = jnp.exp(mk - top)
        denom = denom + ek
        val_out = jnp.where(lane == k, ek, val_out)
        onehot = hit.astype(BF16)
        prefix = jnp.dot(earlier, onehot, preferred_element_type=F32) + base
        rank = jnp.sum(jnp.where(hit, prefix, 0.0), axis=-1, keepdims=True).astype(jnp.int32)
        code_out = jnp.where(lane == k, rank * N_EXPERTS + ik, code_out)
        base = base + jnp.sum(onehot.astype(F32), axis=0, keepdims=True)
    code_ref[...] = code_out
    gate_ref[...] = val_out / denom
    cnt_scr[...] = base
    cnt_ref[...] = base


def _merge(merged, x2, wo, ln2, rw, rb):
    T = x2.shape[0]
    row = lambda w: pl.BlockSpec((MG_TM, w), lambda i: (i, 0))
    return pl.pallas_call(
        _merge_kernel,
        grid=(T // MG_TM,),
        in_specs=[
            row(D_MODEL),
            row(D_MODEL),
            _const_spec((D_MODEL, D_MODEL)),
            _const_spec((1, D_MODEL)),
            _const_spec((D_MODEL, LANES)),
            _const_spec((1, LANES)),
        ],
        out_specs=[row(D_MODEL), pl.BlockSpec((MG_TM * SUBLANES, LANES), lambda i: (i, 0)), row(LANES),
                   row(LANES), pl.BlockSpec((1, LANES), lambda i: (0, 0))],
        out_shape=[
            jax.ShapeDtypeStruct((T, D_MODEL), F32),
            jax.ShapeDtypeStruct((T * SUBLANES, LANES), jnp.uint32),
            jax.ShapeDtypeStruct((T, LANES), jnp.int32),
            jax.ShapeDtypeStruct((T, LANES), F32),
            jax.ShapeDtypeStruct((1, LANES), F32),
        ],
        scratch_shapes=[pltpu.VMEM((1, LANES), F32)],
        compiler_params=_params(("arbitrary",), 56),
        name="merge_router",
    )(merged, x2, wo, ln2, rw, rb)


DP_TM = 256
DMA_UNROLL = 8


def _row_tile(row):
    return pl.ds(pl.multiple_of(row * SUBLANES, SUBLANES), SUBLANES)


def _dispatch_kernel(dest_ref, h2p_ref, xs_in, xs_hbm, sem):
    del xs_in

    def issue(t, carry):
        src = h2p_ref.at[_row_tile(t), :]
        for k in range(TOP_K):
            row = dest_ref[0, t * TOP_K + k]
            pltpu.make_async_copy(src, xs_hbm.at[_row_tile(row), :], sem).start(priority=k % 2)
        return carry

    lax.fori_loop(0, DP_TM, issue, 0, unroll=DMA_UNROLL // TOP_K)
    all_rows = xs_hbm.at[pl.ds(0, DP_TM * TOP_K * SUBLANES), :]
    pltpu.make_async_copy(all_rows, all_rows, sem).wait()


def _dispatch(dest3, h2p, xs_zero):
    T = h2p.shape[0] // SUBLANES
    return pl.pallas_call(
        _dispatch_kernel,
        grid=(T // DP_TM,),
        in_specs=[
            pl.BlockSpec((None, 1, DP_TM * TOP_K), lambda i: (i, 0, 0), memory_space=pltpu.SMEM),
            pl.BlockSpec((DP_TM * SUBLANES, LANES), lambda i: (i, 0)),
            pl.BlockSpec(memory_space=pl.ANY),
        ],
        out_specs=pl.BlockSpec(memory_space=pl.ANY),
        out_shape=jax.ShapeDtypeStruct(xs_zero.shape, xs_zero.dtype),
        scratch_shapes=[pltpu.SemaphoreType.DMA(())],
        input_output_aliases={2: 0},
        compiler_params=_params(("arbitrary",), 32),
        name="moe_dispatch",
    )(dest3, h2p, xs_zero)


GROUP_SUB = 4
GROUP_ROWS = GROUP_SUB * ROW_BLOCK
FF_TILE = 512
N_FF_TILES = D_FF // FF_TILE
DOWN_CHUNK = 512


def _expert_kernel(ge_ref, gs_ref, gn_ref, xs_in, wg_ref, wu_ref, bg_ref, bu_ref, wd_ref, bd_ref,
                   rows_hbm, xwin, x_scr, acc_scr, stage, wsem, sem):
    del ge_ref, xs_in
    g = pl.program_id(0)
    f = pl.program_id(1)
    ns = gn_ref[g]
    half = D_MODEL // 2
    slot = g & 1

    def win_copy(group, s):
        rows = pl.ds(gs_ref[group] * (ROW_BLOCK * SUBLANES), GROUP_ROWS * SUBLANES)
        return pltpu.make_async_copy(rows_hbm.at[rows, :], xwin.at[s], wsem.at[s])

    def out_copy(s, start_block):
        rows = pl.ds((start_block + s) * (ROW_BLOCK * SUBLANES), ROW_BLOCK * SUBLANES)
        return pltpu.make_async_copy(stage.at[s], rows_hbm.at[rows, :], sem.at[s])

    @pl.when((f == 0) & (ns > 0))
    def _():
        @pl.when(g == 0)
        def _():
            win_copy(0, 0).start()
            acc_scr[...] = jnp.zeros_like(acc_scr)

        win_copy(g, slot).wait()
        @pl.when(g + 1 < pl.num_programs(0))
        def _():
            @pl.when(gn_ref[g + 1] > 0)
            def _():
                win_copy(g + 1, 1 - slot).start()

        for c in range(ROW_CHUNKS):
            hi, lo = _unpack_bf16_pair(xwin[slot, _row_chunk(GROUP_ROWS, c), :])
            x_scr[:, c * LANES:(c + 1) * LANES] = hi.astype(BF16)
            x_scr[:, half + c * LANES:half + (c + 1) * LANES] = lo.astype(BF16)

    for n in range(1, GROUP_SUB + 1):
        @pl.when(ns == n)
        def _(n=n):
            m = n * ROW_BLOCK
            x = x_scr[:m, :]
            gate = jnp.dot(x, wg_ref[...].astype(BF16), preferred_element_type=F32) + bg_ref[...]
            up = jnp.dot(x, wu_ref[...].astype(BF16), preferred_element_type=F32) + bu_ref[...]
            gate = jnp.minimum(gate, SWIGLU_LIMIT)
            up = jnp.clip(up, -SWIGLU_LIMIT, SWIGLU_LIMIT)
            glu = gate * jax.nn.sigmoid(gate * SWIGLU_ALPHA)
            act = ((up + 1.0) * glu).astype(BF16)
            for c in range(D_MODEL // DOWN_CHUNK):
                cols = slice(c * DOWN_CHUNK, (c + 1) * DOWN_CHUNK)
                start = jnp.where(f == 0, jnp.broadcast_to(bd_ref[:, cols], (m, DOWN_CHUNK)), acc_scr[:m, cols])
                acc_scr[:m, cols] = start + jnp.dot(act, wd_ref[:, cols].astype(BF16),
                                                    preferred_element_type=F32)

    @pl.when(f == N_FF_TILES - 1)
    def _():
        @pl.when(g > 0)
        def _():
            prev = gn_ref[g - 1]
            for s in range(GROUP_SUB):
                @pl.when(s < prev)
                def _(s=s):
                    out_copy(s, 0).wait()

        for s in range(GROUP_SUB):
            @pl.when(s < ns)
            def _(s=s):
                y = acc_scr[s * ROW_BLOCK:(s + 1) * ROW_BLOCK, :]
                _store_row_tiles(stage.at[s], ROW_BLOCK, _pack_bf16_pair(y[:, :half], y[:, half:]))
                out_copy(s, gs_ref[g]).start()

        @pl.when(g == pl.num_programs(0) - 1)
        def _():
            for s in range(GROUP_SUB):
                @pl.when(s < ns)
                def _(s=s):
                    out_copy(s, 0).wait()


def _experts(g_exp, g_start, g_nsub, xs, wgu, bgu, wd, bd):
    def ff(f, gn, g):
        return jnp.where(gn[g] > 0, f, N_FF_TILES - 1)

    grid_spec = pltpu.PrefetchScalarGridSpec(
        num_scalar_prefetch=3,
        grid=(g_exp.shape[0], N_FF_TILES),
        in_specs=[
            pl.BlockSpec(memory_space=pl.ANY),
            pl.BlockSpec((None, D_MODEL, FF_TILE), lambda g, f, ge, gs, gn: (ge[g], 0, ff(f, gn, g))),
            pl.BlockSpec((None, D_MODEL, FF_TILE),
                         lambda g, f, ge, gs, gn: (ge[g], 0, N_FF_TILES + ff(f, gn, g))),
            pl.BlockSpec((None, 1, FF_TILE), lambda g, f, ge, gs, gn: (ge[g], 0, ff(f, gn, g))),
            pl.BlockSpec((None, 1, FF_TILE), lambda g, f, ge, gs, gn: (ge[g], 0, N_FF_TILES + ff(f, gn, g))),
            pl.BlockSpec((None, FF_TILE, D_MODEL), lambda g, f, ge, gs, gn: (ge[g], ff(f, gn, g), 0)),
            pl.BlockSpec((None, 1, D_MODEL), lambda g, f, ge, gs, gn: (ge[g], 0, 0)),
        ],
        out_specs=pl.BlockSpec(memory_space=pl.ANY),
        scratch_shapes=[
            pltpu.VMEM((2, GROUP_ROWS * SUBLANES, LANES), jnp.uint32),
            pltpu.VMEM((GROUP_ROWS, D_MODEL), BF16),
            pltpu.VMEM((GROUP_ROWS, D_MODEL), F32),
            pltpu.VMEM((GROUP_SUB, ROW_BLOCK * SUBLANES, LANES), jnp.uint32),
            pltpu.SemaphoreType.DMA((2,)),
            pltpu.SemaphoreType.DMA((GROUP_SUB,)),
        ],
    )
    return pl.pallas_call(
        _expert_kernel,
        grid_spec=grid_spec,
        out_shape=jax.ShapeDtypeStruct(xs.shape, jnp.uint32),
        input_output_aliases={3: 0},
        compiler_params=_params(("arbitrary", "arbitrary"), 58),
        name="moe_experts",
    )(g_exp, g_start, g_nsub, xs, wgu, wgu, bgu, bgu, wd, bd)


CB_TM = 128


def _combine_kernel(dest_ref, dest_next_ref, ys_hbm, x1_ref, gate_ref, o_ref, buf, sem):
    i = pl.program_id(0)
    slot = i & 1

    def gather_tile(dest, s):
        def issue(t, carry):
            for k in range(TOP_K):
                row = dest[0, t * TOP_K + k]
                pltpu.make_async_copy(ys_hbm.at[_row_tile(row), :], buf.at[s, k, _row_tile(t), :],
                                      sem.at[s]).start(priority=k % 2)
            return carry

        lax.fori_loop(0, CB_TM, issue, 0, unroll=DMA_UNROLL // TOP_K)

    @pl.when(i == 0)
    def _():
        gather_tile(dest_ref, 0)

    @pl.when(i + 1 < pl.num_programs(0))
    def _():
        gather_tile(dest_next_ref, 1 - slot)

    pltpu.make_async_copy(buf.at[slot], buf.at[slot], sem.at[slot]).wait()

    half = D_MODEL // 2
    gates = gate_ref[...]
    gate_k = [jnp.broadcast_to(gates[:, k:k + 1], (CB_TM, LANES)) for k in range(TOP_K)]
    for c in range(ROW_CHUNKS):
        cols_hi = slice(c * LANES, (c + 1) * LANES)
        cols_lo = slice(half + c * LANES, half + (c + 1) * LANES)
        out_hi = x1_ref[:, cols_hi]
        out_lo = x1_ref[:, cols_lo]
        for k in range(TOP_K):
            hi, lo = _unpack_bf16_pair(buf[slot, k, _row_chunk(CB_TM, c), :])
            out_hi = out_hi + gate_k[k] * hi
            out_lo = out_lo + gate_k[k] * lo
        o_ref[:, cols_hi] = out_hi
        o_ref[:, cols_lo] = out_lo


def _combine(dest3, ys, x1, gates):
    T = x1.shape[0]
    n = T // CB_TM
    dest_spec = lambda f: pl.BlockSpec((None, 1, CB_TM * TOP_K), f, memory_space=pltpu.SMEM)
    return pl.pallas_call(
        _combine_kernel,
        grid=(n,),
        in_specs=[
            dest_spec(lambda i: (i, 0, 0)),
            dest_spec(lambda i: (jnp.minimum(i + 1, n - 1), 0, 0)),
            pl.BlockSpec(memory_space=pl.ANY),
            pl.BlockSpec((CB_TM, D_MODEL), lambda i: (i, 0)),
            pl.BlockSpec((CB_TM, LANES), lambda i: (i, 0)),
        ],
        out_specs=pl.BlockSpec((CB_TM, D_MODEL), lambda i: (i, 0)),
        out_shape=jax.ShapeDtypeStruct((T, D_MODEL), F32),
        scratch_shapes=[pltpu.VMEM((2, TOP_K, CB_TM * SUBLANES, LANES), jnp.uint32),
                        pltpu.SemaphoreType.DMA((2,))],
        compiler_params=_params(("arbitrary",), 32),
        name="moe_combine",
    )(dest3, dest3, ys, x1, gates)


def _group_tables(counts_f32, max_groups):
    i32 = jnp.int32
    counts = counts_f32[0, :N_EXPERTS].astype(i32)
    nb = (counts + ROW_BLOCK - 1) // ROW_BLOCK
    ng = (nb + GROUP_SUB - 1) // GROUP_SUB
    upto = jnp.arange(N_EXPERTS)[None, :] <= jnp.arange(N_EXPERTS)[:, None]
    pad_ends = jnp.sum(jnp.where(upto, nb[None, :], 0), axis=1) * ROW_BLOCK
    pad_starts = pad_ends - nb * ROW_BLOCK
    g_ends = jnp.sum(jnp.where(upto, ng[None, :], 0), axis=1)
    g_starts = g_ends - ng
    n_groups = g_ends[-1]

    g = jnp.arange(max_groups, dtype=i32)
    gg = jnp.minimum(g, n_groups - 1)
    e = jnp.minimum(jnp.sum((g_ends[None, :] <= gg[:, None]).astype(i32), axis=1), N_EXPERTS - 1)
    pick = e[:, None] == jnp.arange(N_EXPERTS)[None, :]
    take = lambda table: jnp.sum(jnp.where(pick, table[None, :], 0), axis=1)
    nb_g, ng_g = take(nb), jnp.maximum(take(ng), 1)
    j = gg - take(g_starts)
    base, rem = nb_g // ng_g, nb_g % ng_g
    nsub = jnp.where(g < n_groups, base + (j < rem).astype(i32), 0)
    start_block = take(pad_starts) // ROW_BLOCK + j * base + jnp.minimum(j, rem)
    return pad_starts.astype(i32), e.astype(i32), start_block.astype(i32), nsub.astype(i32)


def _rope_constants():
    lane = np.arange(LANES)
    half = QK_ROPE // 2
    inv = 1.0 / (ROPE_THETA ** (np.arange(0, QK_ROPE, 2, dtype=np.float32) / QK_ROPE))
    cst = np.zeros((8, LANES), np.float32)
    cst[0, :QK_ROPE] = inv.astype(np.float32)[lane[:QK_ROPE] % half]
    cst[1, :QK_ROPE] = 1.0
    cst[2, :half] = -1.0
    cst[2, half:QK_ROPE] = 1.0
    return jnp.asarray(cst)


def _swap_halves(a):
    half = QK_ROPE // 2
    return jnp.concatenate([a[..., half:], a[..., :half]], axis=-1)


def kernel(x, mem, positions, ln1_g, w_in, w_gate, b_gate, gmlp_ln_g, gmlp_ln_b, gmlp_ws, gmlp_bs, w_pa,
           mla_cq_g, mla_w_uq, mla_ckv_g, mla_w_ukv, mla_qn_g, mla_kn_g, w_pb, mem_ln_g, mem_w_k, mem_w_v,
           mem_qn_g, mem_kn_g, w_pc, w_o, ln2_g, router_w, router_b, moe_w_gu, moe_b_gu, moe_w_down,
           moe_b_down):
    B, S, D = x.shape
    T = B * S
    x2 = x.reshape(T, D)
    for l in range(ln1_g.shape[0]):
        o_kr = 2 * GM_WIDTH + Q_LORA + KV_LORA
        o_qm = o_kr + QK_ROPE
        wi = w_in[l]
        w1 = jnp.concatenate([wi[:, :o_kr], wi[:, o_qm:], w_gate[l]], axis=1).astype(BF16)
        b1 = jnp.concatenate([jnp.zeros((PROJ_COLS,), F32), b_gate[l]])[None, :]
        w_kr = wi[:, o_kr:o_qm]
        wr = jnp.concatenate([w_kr, _swap_halves(w_kr)], axis=1).astype(BF16)

        wq3 = mla_w_uq[l].reshape(Q_LORA, MLA_HEADS, QK_DIM)
        wq = jnp.concatenate([wq3, _swap_halves(wq3[..., QK_NOPE:])], axis=-1)
        wq = wq.reshape(Q_LORA, MLA_HEADS * QK_PAD).astype(BF16)
        wkv = mla_w_ukv[l].astype(BF16)
        gq = jnp.concatenate([mla_qn_g[l], _swap_halves(mla_qn_g[l][QK_NOPE:])])[None, :]
        gk = jnp.concatenate([mla_kn_g[l], _swap_halves(mla_kn_g[l][QK_NOPE:])])[None, :]

        bias_full = jnp.broadcast_to(gmlp_bs[l].T[:, :, None], (GM_CHUNK, GM_GROUPS, GM_CHUNK))
        bias_full = bias_full.reshape(GM_CHUNK, GM_WIDTH)

        rw = jnp.pad(router_w[l], ((0, 0), (0, LANES - N_EXPERTS))).astype(BF16)
        rb = jnp.pad(router_b[l], (0, LANES - N_EXPERTS))[None, :]

        pg, kr = _norm_proj(x2, ln1_g[l][None, :], w1, b1, wr)
        m0 = _gmlp(pg, gmlp_ln_g[l][None, :], gmlp_ln_b[l][None, :], gmlp_ws[l].astype(BF16), bias_full,
                   w_pa[l].astype(BF16))
        q, k, vt = _mla_qkv(pg, kr, positions.reshape(T, 1), mla_cq_g[l][None, :], mla_ckv_g[l][None, :],
                           wq, wkv, gq, gk, _rope_constants(), B, S)
        b_attn = _flash(q, k, vt)
        km, vm = _mem_kv(mem, mem_ln_g[l][None, :], mem_w_k[l].astype(BF16), mem_w_v[l].astype(BF16),
                         mem_kn_g[l][None, :])
        m1 = _mem_attn(pg, km, vm, m0, mem_qn_g[l][None, :], w_pc[l].astype(BF16), S)
        n_rows = T * TOP_K + N_EXPERTS * ROW_BLOCK
        merged, xs_zero = _attn_proj(b_attn, pg, m1, w_pb[l].astype(BF16), n_rows + GROUP_ROWS)
        x1, h2p, code, gates, counts = _merge(merged, x2, w_o[l].astype(BF16), ln2_g[l][None, :], rw, rb)

        max_groups = n_rows // GROUP_ROWS + N_EXPERTS
        pad_starts, g_exp, g_start, g_nsub = _group_tables(counts, max_groups)
        codes = code[:, :TOP_K]
        pick = (codes & (N_EXPERTS - 1))[..., None] == jnp.arange(N_EXPERTS)
        dest = (codes >> EXPERT_BITS) + jnp.sum(jnp.where(pick, pad_starts, 0), axis=-1)
        xs = _dispatch(dest.reshape(T // DP_TM, 1, DP_TM * TOP_K), h2p, xs_zero)
        ys = _experts(g_exp, g_start, g_nsub, xs, moe_w_gu[l], moe_b_gu[l][:, None, :],
                      moe_w_down[l], moe_b_down[l][:, None, :])
        x2 = _combine(dest.reshape(T // CB_TM, 1, CB_TM * TOP_K), ys, x1, gates)
    return x2.reshape(B, S, D)
```

```python
import functools

import numpy as np
import jax
import jax.numpy as jnp
from jax import lax
from jax.experimental import pallas as pl
from jax.experimental.pallas import tpu as pltpu

F32 = jnp.float32
BF16 = jnp.bfloat16

D_MODEL = 2048
GM_WIDTH = 1024
GM_GROUPS = 8
GM_CHUNK = 128
MLA_HEADS = 16
Q_LORA = 512
KV_LORA = 512
QK_NOPE = 128
QK_ROPE = 64
V_DIM = 128
QK_DIM = QK_NOPE + QK_ROPE
QK_PAD = 256
MLA_SCALE = QK_DIM ** -0.5
LOG2_E = 1.4426950408889634
ROPE_THETA = 10000.0
MEM_LEN = 256
MEM_HEADS = 4
MEM_HEAD_DIM = 256
MEM_WIDTH = MEM_HEADS * MEM_HEAD_DIM
MEM_SCALE = MEM_HEAD_DIM ** -0.5
N_EXPERTS = 32
EXPERT_BITS = 5
TOP_K = 4
D_FF = 2048
SWIGLU_LIMIT = 7.0
SWIGLU_ALPHA = 1.702
ROW_BLOCK = 256
EPS = 1e-6
LANES = 128
NEG_INF = float("-inf")

PROJ_COLS = 2 * GM_WIDTH + Q_LORA + KV_LORA + MEM_WIDTH
PG_COLS = PROJ_COLS + 3 * D_MODEL

MIB = 1024 * 1024


def _params(semantics, vmem_mib):
    return pltpu.CompilerParams(dimension_semantics=semantics, vmem_limit_bytes=vmem_mib * MIB)


def _const_spec(shape):
    nd = len(shape)
    return pl.BlockSpec(shape, lambda *_: (0,) * nd, pipeline_mode=pl.Buffered(1))


P1_TM = 1024
P1_TN = 1024


def _norm_proj_kernel(x_ref, g_ref, w_ref, b_ref, wr_ref, o_ref, kr_ref, h_scr, *, n_plain):
    j = pl.program_id(1)

    @pl.when(j == 0)
    def _():
        def body(c, carry):
            rows = pl.ds(pl.multiple_of(c * 128, 128), 128)
            x = x_ref[rows, :]
            ms = jnp.mean(x * x, axis=-1, keepdims=True)
            h_scr[rows, :] = (x * lax.rsqrt(ms + EPS) * g_ref[...]).astype(BF16)
            return carry

        lax.fori_loop(0, P1_TM // 128, body, 0)
        kr_ref[...] = jnp.dot(h_scr[...], wr_ref[...], preferred_element_type=F32)

    acc = jnp.dot(h_scr[...], w_ref[...], preferred_element_type=F32)
    gated = jax.nn.sigmoid(acc + b_ref[...])
    o_ref[...] = jnp.where(j >= n_plain, gated, acc).astype(BF16)


def _norm_proj(x2, ln1_g, w1, b1, wr):
    T = x2.shape[0]
    grid = (T // P1_TM, PG_COLS // P1_TN)
    return pl.pallas_call(
        functools.partial(_norm_proj_kernel, n_plain=PROJ_COLS // P1_TN),
        grid=grid,
        in_specs=[
            pl.BlockSpec((P1_TM, D_MODEL), lambda i, j: (i, 0)),
            pl.BlockSpec((1, D_MODEL), lambda i, j: (0, 0)),
            pl.BlockSpec((D_MODEL, P1_TN), lambda i, j: (0, j)),
            pl.BlockSpec((1, P1_TN), lambda i, j: (0, j)),
            pl.BlockSpec((D_MODEL, LANES), lambda i, j: (0, 0)),
        ],
        out_specs=[
            pl.BlockSpec((P1_TM, P1_TN), lambda i, j: (i, j)),
            pl.BlockSpec((P1_TM, LANES), lambda i, j: (i, 0)),
        ],
        out_shape=[
            jax.ShapeDtypeStruct((T, PG_COLS), BF16),
            jax.ShapeDtypeStruct((T, LANES), F32),
        ],
        scratch_shapes=[pltpu.VMEM((P1_TM, D_MODEL), BF16)],
        compiler_params=_params(("parallel", "arbitrary"), 48),
        name="norm_proj",
    )(x2, ln1_g, w1, b1, wr)


GM_TM = 512


def _gmlp_kernel(u_ref, v_ref, g0_ref, lng_ref, lnb_ref, ws_ref, bias_ref, wpa_ref, o_ref, vb_scr, a_scr):
    v = v_ref[...].astype(F32)
    mu = jnp.mean(v, axis=-1, keepdims=True)
    c = v - mu
    var = jnp.mean(c * c, axis=-1, keepdims=True)
    vb_scr[...] = (c * lax.rsqrt(var + EPS) * lng_ref[...] + lnb_ref[...]).astype(BF16)

    row = lax.broadcasted_iota(jnp.int32, (GM_CHUNK, GM_CHUNK), 0)
    col = lax.broadcasted_iota(jnp.int32, (GM_CHUNK, GM_CHUNK), 1)
    causal = col <= row
    for g in range(GM_GROUPS):
        cols = slice(g * GM_CHUNK, (g + 1) * GM_CHUNK)
        wg = jnp.where(causal, ws_ref[g], jnp.zeros((), BF16))
        chunks = [slice(ch * GM_CHUNK, (ch + 1) * GM_CHUNK) for ch in range(GM_TM // GM_CHUNK)]
        mixed = jnp.dot(wg, jnp.concatenate([vb_scr[rows, cols] for rows in chunks], axis=1),
                        preferred_element_type=F32)
        for ch, rows in enumerate(chunks):
            mix = mixed[:, ch * GM_CHUNK:(ch + 1) * GM_CHUNK] + bias_ref[:, cols]
            a_scr[rows, cols] = (u_ref[rows, cols].astype(F32) * mix).astype(BF16)

    ma = jnp.dot(a_scr[...], wpa_ref[...], preferred_element_type=F32)
    o_ref[...] = (g0_ref[...].astype(F32) * ma).astype(BF16)


def _gmlp(pg, ln_g, ln_b, ws, bias_full, wpa):
    T = pg.shape[0]
    return pl.pallas_call(
        _gmlp_kernel,
        grid=(T // GM_TM,),
        in_specs=[
            pl.BlockSpec((GM_TM, GM_WIDTH), lambda i: (i, 0)),
            pl.BlockSpec((GM_TM, GM_WIDTH), lambda i: (i, 1)),
            pl.BlockSpec((GM_TM, D_MODEL), lambda i: (i, PROJ_COLS // D_MODEL)),
            _const_spec((1, GM_WIDTH)),
            _const_spec((1, GM_WIDTH)),
            _const_spec((GM_GROUPS, GM_CHUNK, GM_CHUNK)),
            _const_spec((GM_CHUNK, GM_WIDTH)),
            _const_spec((GM_WIDTH, D_MODEL)),
        ],
        out_specs=pl.BlockSpec((GM_TM, D_MODEL), lambda i: (i, 0)),
        out_shape=jax.ShapeDtypeStruct((T, D_MODEL), BF16),
        scratch_shapes=[pltpu.VMEM((GM_TM, GM_WIDTH), BF16), pltpu.VMEM((GM_TM, GM_WIDTH), BF16)],
        compiler_params=_params(("parallel",), 40),
        name="gmlp",
    )(pg, pg, pg, ln_g, ln_b, ws, bias_full, wpa)


QKV_TM = 1024
QKV_HEADS = 2
HEAD_BLK = 4
FA_T = 512
FA_QS = 256


def _rope(t, cos, sin):
    return t * cos + pltpu.roll(t, QK_ROPE, 1) * sin


def _row_sum_all_lanes(sq, weights):
    return jnp.dot(sq.astype(BF16), weights.astype(BF16), preferred_element_type=F32)


def _mla_qkv_kernel(cq_ref, ckv_ref, kr_ref, pos_ref, gcq_ref, gckv_ref, wq_ref, wkv_ref, gq_ref, gk_ref,
                    cst_ref, sumw_ref, q_ref, k_ref, vt_ref, cqn_scr, ckvn_scr, cos_scr, sin_scr, krsq_scr):
    hb = pl.program_id(1)
    w_nope = sumw_ref[:QK_NOPE, :]
    w_rope = sumw_ref[QK_NOPE:, :]

    @pl.when(hb == 0)
    def _():
        cq = cq_ref[...].astype(F32)
        ms = jnp.mean(cq * cq, axis=-1, keepdims=True)
        cqn_scr[...] = (cq * lax.rsqrt(ms + EPS) * gcq_ref[...]).astype(BF16)
        ckv = ckv_ref[...].astype(F32)
        ms = jnp.mean(ckv * ckv, axis=-1, keepdims=True)
        ckvn_scr[...] = (ckv * lax.rsqrt(ms + EPS) * gckv_ref[...]).astype(BF16)
        ang = pos_ref[...].astype(F32) * cst_ref[0:1, :]
        cos_scr[...] = jnp.cos(ang) * cst_ref[1:2, :]
        sin_scr[...] = jnp.sin(ang) * cst_ref[2:3, :]
        kr = kr_ref[...]
        krsq_scr[...] = _row_sum_all_lanes(kr * kr, w_rope)

    cos = cos_scr[...]
    sin = sin_scr[...]
    inv_dim = 1.0 / QK_DIM
    kr = kr_ref[...]

    yq2 = jnp.dot(cqn_scr[...], wq_ref[...], preferred_element_type=F32)
    ykv2 = jnp.dot(ckvn_scr[...], wkv_ref[...], preferred_element_type=F32)
    for hh in range(QKV_HEADS):
        yq = yq2[:, hh * QK_PAD:(hh + 1) * QK_PAD]
        qn = yq[:, :QK_NOPE]
        qt = yq[:, QK_NOPE:]
        ssq = _row_sum_all_lanes(qn * qn, w_nope) + _row_sum_all_lanes(qt * qt, w_rope)
        rs = lax.rsqrt(ssq * inv_dim + EPS) * (MLA_SCALE * LOG2_E)
        q_ref[hh, :, :QK_NOPE] = (qn * rs * gq_ref[:, :QK_NOPE]).astype(BF16)
        q_ref[hh, :, QK_NOPE:] = _rope(qt * rs * gq_ref[:, QK_NOPE:], cos, sin).astype(BF16)

        ykv = ykv2[:, hh * (QK_NOPE + V_DIM):(hh + 1) * (QK_NOPE + V_DIM)]
        kn = ykv[:, :QK_NOPE]
        ssq = _row_sum_all_lanes(kn * kn, w_nope) + krsq_scr[...]
        rs = lax.rsqrt(ssq * inv_dim + EPS)
        k_ref[hh, :, :QK_NOPE] = (kn * rs * gk_ref[:, :QK_NOPE]).astype(BF16)
        k_ref[hh, :, QK_NOPE:] = _rope(kr * rs * gk_ref[:, QK_NOPE:], cos, sin).astype(BF16)
        vv = ykv[:, QK_NOPE:]
        for c in range(QKV_TM // FA_T):
            vt_ref[hh, c] = vv[c * FA_T:(c + 1) * FA_T, :].T.astype(BF16)


def _mla_qkv(pg, kr, pos, gcq, gckv, wq, wkv, gq, gk, cst, B, S):
    T = pg.shape[0]
    sumw = jnp.concatenate([jnp.ones((QK_NOPE, LANES), F32), jnp.full((QK_PAD - QK_NOPE, LANES), 0.5, F32)])
    spb = S // QKV_TM
    cpt = QKV_TM // FA_T
    head_spec = lambda w: pl.BlockSpec((None, QKV_HEADS, QKV_TM, w), lambda i, h: (i // spb, h, i % spb, 0))
    return pl.pallas_call(
        _mla_qkv_kernel,
        grid=(T // QKV_TM, MLA_HEADS // QKV_HEADS),
        in_specs=[
            pl.BlockSpec((QKV_TM, Q_LORA), lambda i, h: (i, 2 * GM_WIDTH // Q_LORA)),
            pl.BlockSpec((QKV_TM, KV_LORA), lambda i, h: (i, 2 * GM_WIDTH // KV_LORA + 1)),
            pl.BlockSpec((QKV_TM, LANES), lambda i, h: (i, 0)),
            pl.BlockSpec((QKV_TM, 1), lambda i, h: (i, 0)),
            pl.BlockSpec((1, Q_LORA), lambda i, h: (0, 0)),
            pl.BlockSpec((1, KV_LORA), lambda i, h: (0, 0)),
            pl.BlockSpec((Q_LORA, QKV_HEADS * QK_PAD), lambda i, h: (0, h)),
            pl.BlockSpec((KV_LORA, QKV_HEADS * (QK_NOPE + V_DIM)), lambda i, h: (0, h)),
            pl.BlockSpec((1, QK_PAD), lambda i, h: (0, 0)),
            pl.BlockSpec((1, QK_PAD), lambda i, h: (0, 0)),
            pl.BlockSpec((8, LANES), lambda i, h: (0, 0)),
            pl.BlockSpec((QK_PAD, LANES), lambda i, h: (0, 0)),
        ],
        out_specs=[
            head_spec(QK_PAD),
            head_spec(QK_PAD),
            pl.BlockSpec((None, QKV_HEADS, cpt, V_DIM, FA_T), lambda i, h: (i // spb, h, i % spb, 0, 0)),
        ],
        out_shape=[
            jax.ShapeDtypeStruct((B, MLA_HEADS, S, QK_PAD), BF16),
            jax.ShapeDtypeStruct((B, MLA_HEADS, S, QK_PAD), BF16),
            jax.ShapeDtypeStruct((B, MLA_HEADS, S // FA_T, V_DIM, FA_T), BF16),
        ],
        scratch_shapes=[
            pltpu.VMEM((QKV_TM, Q_LORA), BF16),
            pltpu.VMEM((QKV_TM, KV_LORA), BF16),
            pltpu.VMEM((QKV_TM, LANES), F32),
            pltpu.VMEM((QKV_TM, LANES), F32),
            pltpu.VMEM((QKV_TM, LANES), F32),
        ],
        compiler_params=_params(("parallel", "arbitrary"), 48),
        name="mla_qkv",
    )(pg, pg, kr, pos, gcq, gckv, wq, wkv, gq, gk, cst, sumw)


def _flash_kernel(q_ref, k_ref, vt_ref, o_ref, m_scr, l_scr, acc_scr):
    qi = pl.program_id(2)
    m_scr[...] = jnp.full_like(m_scr, NEG_INF)
    l_scr[...] = jnp.zeros_like(l_scr)
    acc_scr[...] = jnp.zeros_like(acc_scr)

    def step(kb, diagonal):
        ks = pl.ds(pl.multiple_of(kb * FA_T, FA_T), FA_T)
        chains = [(hh, qs) for hh in range(HEAD_BLK) for qs in range(FA_T // FA_QS)]
        scores = []
        for hh, qs in chains:
            qc = slice(qs * FA_QS, (qs + 1) * FA_QS)
            st = lax.dot_general(k_ref[hh, ks, :], q_ref[hh, qc, :], (((1,), (1,)), ((), ())),
                                 preferred_element_type=F32)
            if diagonal:
                krow = lax.broadcasted_iota(jnp.int32, (FA_T, FA_QS), 0)
                qcol = lax.broadcasted_iota(jnp.int32, (FA_T, FA_QS), 1) + qs * FA_QS
                st = jnp.where(krow <= qcol, st, NEG_INF)
            scores.append(st)
        probs = []
        for (hh, qs), st in zip(chains, scores):
            qc = slice(qs * FA_QS, (qs + 1) * FA_QS)
            m_prev = m_scr[hh, :, qc]
            m_new = jnp.maximum(m_prev, jnp.max(st, axis=0, keepdims=True))
            alpha = jnp.exp2(m_prev - m_new)
            p = jnp.exp2(st - m_new)
            l_scr[hh, :, qc] = alpha * l_scr[hh, :, qc] + jnp.sum(p, axis=0, keepdims=True)
            m_scr[hh, :, qc] = m_new
            probs.append((alpha, p.astype(BF16)))
        for (hh, qs), (alpha, p) in zip(chains, probs):
            qc = slice(qs * FA_QS, (qs + 1) * FA_QS)
            acc_scr[hh, :, qc] = alpha * acc_scr[hh, :, qc] + jnp.dot(vt_ref[hh, kb], p,
                                                                      preferred_element_type=F32)

    def body(kb, carry):
        step(kb, False)
        return carry

    lax.fori_loop(0, qi, body, 0)
    step(qi, True)
    for hh in range(HEAD_BLK):
        o_ref[:, hh * V_DIM:(hh + 1) * V_DIM] = (acc_scr[hh] / l_scr[hh]).T.astype(BF16)


def _flash(q, k, vt):
    B, H, S, _ = q.shape
    nq = S // FA_T
    return pl.pallas_call(
        _flash_kernel,
        grid=(B, H // HEAD_BLK, nq),
        in_specs=[
            pl.BlockSpec((None, HEAD_BLK, FA_T, QK_PAD), lambda b, h, i: (b, h, i, 0)),
            pl.BlockSpec((None, HEAD_BLK, S, QK_PAD), lambda b, h, i: (b, h, 0, 0)),
            pl.BlockSpec((None, HEAD_BLK, nq, V_DIM, FA_T), lambda b, h, i: (b, h, 0, 0, 0)),
        ],
        out_specs=pl.BlockSpec((FA_T, HEAD_BLK * V_DIM), lambda b, h, i: (b * nq + i, h)),
        out_shape=jax.ShapeDtypeStruct((B * S, H * V_DIM), BF16),
        scratch_shapes=[
            pltpu.VMEM((HEAD_BLK, 1, FA_T), F32),
            pltpu.VMEM((HEAD_BLK, 1, FA_T), F32),
            pltpu.VMEM((HEAD_BLK, V_DIM, FA_T), F32),
        ],
        compiler_params=_params(("parallel", "parallel", "arbitrary"), 40),
        name="mla_attention",
    )(q, k, vt)


def _mem_kv_kernel(mem_ref, g_ref, wk_ref, wv_ref, kng_ref, k_ref, v_ref):
    m = mem_ref[...]
    ms = jnp.mean(m * m, axis=-1, keepdims=True)
    mb = (m * lax.rsqrt(ms + EPS) * g_ref[...]).astype(BF16)
    kk = jnp.dot(mb, wk_ref[...], preferred_element_type=F32)
    for h in range(MEM_HEADS):
        cols = slice(h * MEM_HEAD_DIM, (h + 1) * MEM_HEAD_DIM)
        kh = kk[:, cols]
        ms = jnp.mean(kh * kh, axis=-1, keepdims=True)
        k_ref[:, cols] = (kh * lax.rsqrt(ms + EPS) * kng_ref[...]).astype(BF16)
    v_ref[...] = jnp.dot(mb, wv_ref[...], preferred_element_type=F32).astype(BF16)


def _mem_kv(mem, g, wk, wv, kng):
    B = mem.shape[0]
    return pl.pallas_call(
        _mem_kv_kernel,
        grid=(B,),
        in_specs=[
            pl.BlockSpec((None, MEM_LEN, D_MODEL), lambda b: (b, 0, 0)),
            _const_spec((1, D_MODEL)),
            _const_spec((D_MODEL, MEM_WIDTH)),
            _const_spec((D_MODEL, MEM_WIDTH)),
            _const_spec((1, MEM_HEAD_DIM)),
        ],
        out_specs=[
            pl.BlockSpec((None, MEM_LEN, MEM_WIDTH), lambda b: (b, 0, 0)),
            pl.BlockSpec((None, MEM_LEN, MEM_WIDTH), lambda b: (b, 0, 0)),
        ],
        out_shape=[
            jax.ShapeDtypeStruct((B, MEM_LEN, MEM_WIDTH), BF16),
            jax.ShapeDtypeStruct((B, MEM_LEN, MEM_WIDTH), BF16),
        ],
        compiler_params=_params(("parallel",), 32),
        name="mem_kv",
    )(mem, g, wk, wv, kng)


MA_TM = 512


def _mem_attn_kernel(qm_ref, km_ref, vm_ref, g2_ref, m0_ref, qng_ref, wpc_ref, o_ref, c_scr):
    for h in range(MEM_HEADS):
        cols = slice(h * MEM_HEAD_DIM, (h + 1) * MEM_HEAD_DIM)
        qh = qm_ref[:, cols].astype(F32)
        ms = jnp.mean(qh * qh, axis=-1, keepdims=True)
        qn = (qh * (lax.rsqrt(ms + EPS) * MEM_SCALE) * qng_ref[...]).astype(BF16)
        s = lax.dot_general(qn, km_ref[:, cols], (((1,), (1,)), ((), ())), preferred_element_type=F32)
        e = jnp.exp(s - jnp.max(s, axis=-1, keepdims=True))
        p = (e / jnp.sum(e, axis=-1, keepdims=True)).astype(BF16)
        c_scr[:, cols] = jnp.dot(p, vm_ref[:, cols], preferred_element_type=F32).astype(BF16)
    mc = jnp.dot(c_scr[...], wpc_ref[...], preferred_element_type=F32)
    o_ref[...] = (m0_ref[...].astype(F32) + g2_ref[...].astype(F32) * mc).astype(BF16)


def _mem_attn(pg, km, vm, m0, qng, wpc, S):
    T = pg.shape[0]
    spb = S // MA_TM
    return pl.pallas_call(
        _mem_attn_kernel,
        grid=(T // MA_TM,),
        in_specs=[
            pl.BlockSpec((MA_TM, MEM_WIDTH), lambda i: (i, (2 * GM_WIDTH + Q_LORA + KV_LORA) // MEM_WIDTH)),
            pl.BlockSpec((None, MEM_LEN, MEM_WIDTH), lambda i: (i // spb, 0, 0)),
            pl.BlockSpec((None, MEM_LEN, MEM_WIDTH), lambda i: (i // spb, 0, 0)),
            pl.BlockSpec((MA_TM, D_MODEL), lambda i: (i, PROJ_COLS // D_MODEL + 2)),
            pl.BlockSpec((MA_TM, D_MODEL), lambda i: (i, 0)),
            _const_spec((1, MEM_HEAD_DIM)),
            _const_spec((MEM_WIDTH, D_MODEL)),
        ],
        out_specs=pl.BlockSpec((MA_TM, D_MODEL), lambda i: (i, 0)),
        out_shape=jax.ShapeDtypeStruct((T, D_MODEL), BF16),
        scratch_shapes=[pltpu.VMEM((MA_TM, MEM_WIDTH), BF16)],
        compiler_params=_params(("parallel",), 40),
        name="mem_attention",
    )(pg, km, vm, pg, m0, qng, wpc)


MG_TM = 512


def _pack_bf16_pair(a, b):
    hi = pltpu.bitcast(a.astype(BF16).astype(F32), jnp.uint32)
    lo = pltpu.bitcast(b.astype(BF16).astype(F32), jnp.uint32)
    return hi | (lo >> 16)


def _unpack_bf16_pair(p):
    hi = pltpu.bitcast(p & jnp.uint32(0xFFFF0000), F32)
    lo = pltpu.bitcast(p << 16, F32)
    return hi, lo


SUBLANES = 8
ROW_WORDS = D_MODEL // 2
ROW_CHUNKS = ROW_WORDS // LANES
assert ROW_CHUNKS == SUBLANES


def _row_chunk(n_rows, c):
    return pl.ds(c, n_rows, stride=SUBLANES)


def _store_row_tiles(ref, n_rows, packed):
    for c in range(ROW_CHUNKS):
        ref[_row_chunk(n_rows, c), :] = packed[:, c * LANES:(c + 1) * LANES]


def _attn_proj_kernel(b_ref, g1_ref, m1_ref, wpb_ref, o_ref, zero_ref):
    mb = jnp.dot(b_ref[...], wpb_ref[...], preferred_element_type=F32)
    o_ref[...] = (m1_ref[...].astype(F32) + g1_ref[...].astype(F32) * mb).astype(BF16)
    zero_ref[...] = jnp.zeros_like(zero_ref)


def _attn_proj(b_attn, pg, m1, wpb, n_zero_rows):
    T = m1.shape[0]
    steps = T // MG_TM
    assert n_zero_rows % steps == 0
    zero_block = (n_zero_rows // steps * SUBLANES, LANES)
    row = pl.BlockSpec((MG_TM, D_MODEL), lambda i: (i, 0))
    return pl.pallas_call(
        _attn_proj_kernel,
        grid=(steps,),
        in_specs=[
            row,
            pl.BlockSpec((MG_TM, D_MODEL), lambda i: (i, PROJ_COLS // D_MODEL + 1)),
            row,
            _const_spec((MLA_HEADS * V_DIM, D_MODEL)),
        ],
        out_specs=[row, pl.BlockSpec(zero_block, lambda i: (i, 0))],
        out_shape=[jax.ShapeDtypeStruct((T, D_MODEL), BF16),
                   jax.ShapeDtypeStruct((n_zero_rows * SUBLANES, LANES), jnp.uint32)],
        compiler_params=_params(("parallel",), 56),
        name="attn_proj",
    )(b_attn, pg, m1, wpb)


def _merge_kernel(mg_ref, x_ref, wo_ref, ln2_ref, rw_ref, rb_ref,
                  x1_ref, h2p_ref, code_ref, gate_ref, cnt_ref, cnt_scr):
    @pl.when(pl.program_id(0) == 0)
    def _():
        cnt_scr[...] = jnp.zeros_like(cnt_scr)

    x1 = x_ref[...] + jnp.dot(mg_ref[...], wo_ref[...], preferred_element_type=F32)
    x1_ref[...] = x1
    ms = jnp.mean(x1 * x1, axis=-1, keepdims=True)
    h2 = x1 * lax.rsqrt(ms + EPS) * ln2_ref[...]
    _store_row_tiles(h2p_ref, MG_TM, _pack_bf16_pair(h2[:, :D_MODEL // 2], h2[:, D_MODEL // 2:]))

    logits = jnp.dot(h2.astype(BF16), rw_ref[...], preferred_element_type=F32)
    lane = lax.broadcasted_iota(jnp.int32, (MG_TM, LANES), 1)
    work = jnp.where(lane < N_EXPERTS, logits + rb_ref[...], NEG_INF)
    earlier = (lax.broadcasted_iota(jnp.int32, (MG_TM, MG_TM), 1)
               < lax.broadcasted_iota(jnp.int32, (MG_TM, MG_TM), 0)).astype(BF16)
    base = cnt_scr[...]
    code_out = jnp.zeros((MG_TM, LANES), jnp.int32)
    val_out = jnp.zeros((MG_TM, LANES), F32)
    top = None
    denom = jnp.zeros((MG_TM, 1), F32)
    for k in range(TOP_K):
        mk = jnp.max(work, axis=-1, keepdims=True)
        ik = jnp.min(jnp.where(work == mk, lane, LANES), axis=-1, keepdims=True)
        hit = lane == ik
        work = jnp.where(hit, NEG_INF, work)
        if top is None:
            top = mk
        ek = jnp.exp(mk - top)
        denom = denom + ek
        val_out = jnp.where(lane == k, ek, val_out)
        onehot = hit.astype(BF16)
        prefix = jnp.dot(earlier, onehot, preferred_element_type=F32) + base
        rank = jnp.sum(jnp.where(hit, prefix, 0.0), axis=-1, keepdims=True).astype(jnp.int32)
        code_out = jnp.where(lane == k, rank * N_EXPERTS + ik, code_out)
        base = base + jnp.sum(onehot.astype(F32), axis=0, keepdims=True)
    code_ref[...] = code_out
    gate_ref[...] = val_out / denom
    cnt_scr[...] = base
    cnt_ref[...] = base


def _merge(merged, x2, wo, ln2, rw, rb):
    T = x2.shape[0]
    row = lambda w: pl.BlockSpec((MG_TM, w), lambda i: (i, 0))
    return pl.pallas_call(
        _merge_kernel,
        grid=(T // MG_TM,),
        in_specs=[
            row(D_MODEL),
            row(D_MODEL),
            _const_spec((D_MODEL, D_MODEL)),
            _const_spec((1, D_MODEL)),
            _const_spec((D_MODEL, LANES)),
            _const_spec((1, LANES)),
        ],
        out_specs=[row(D_MODEL), pl.BlockSpec((MG_TM * SUBLANES, LANES), lambda i: (i, 0)), row(LANES),
                   row(LANES), pl.BlockSpec((1, LANES), lambda i: (0, 0))],
        out_shape=[
            jax.ShapeDtypeStruct((T, D_MODEL), F32),
            jax.ShapeDtypeStruct((T * SUBLANES, LANES), jnp.uint32),
            jax.ShapeDtypeStruct((T, LANES), jnp.int32),
            jax.ShapeDtypeStruct((T, LANES), F32),
            jax.ShapeDtypeStruct((1, LANES), F32),
        ],
        scratch_shapes=[pltpu.VMEM((1, LANES), F32)],
        compiler_params=_params(("arbitrary",), 56),
        name="merge_router",
    )(merged, x2, wo, ln2, rw, rb)


DP_TM = 512
DMA_UNROLL = 8


def _row_tile(row):
    return pl.ds(pl.multiple_of(row * SUBLANES, SUBLANES), SUBLANES)


def _dispatch_kernel(dest_ref, h2p_ref, xs_in, xs_hbm, sem):
    del xs_in

    def issue(t, carry):
        src = h2p_ref.at[_row_tile(t), :]
        for k in range(TOP_K):
            row = dest_ref[0, t * TOP_K + k]
            pltpu.make_async_copy(src, xs_hbm.at[_row_tile(row), :], sem).start(priority=k % 2)
        return carry

    lax.fori_loop(0, DP_TM, issue, 0, unroll=DMA_UNROLL // TOP_K)
    all_rows = xs_hbm.at[pl.ds(0, DP_TM * TOP_K * SUBLANES), :]
    pltpu.make_async_copy(all_rows, all_rows, sem).wait()


def _dispatch(dest3, h2p, xs_zero):
    T = h2p.shape[0] // SUBLANES
    return pl.pallas_call(
        _dispatch_kernel,
        grid=(T // DP_TM,),
        in_specs=[
            pl.BlockSpec((None, 1, DP_TM * TOP_K), lambda i: (i, 0, 0), memory_space=pltpu.SMEM),
            pl.BlockSpec((DP_TM * SUBLANES, LANES), lambda i: (i, 0)),
            pl.BlockSpec(memory_space=pl.ANY),
        ],
        out_specs=pl.BlockSpec(memory_space=pl.ANY),
        out_shape=jax.ShapeDtypeStruct(xs_zero.shape, xs_zero.dtype),
        scratch_shapes=[pltpu.SemaphoreType.DMA(())],
        input_output_aliases={2: 0},
        compiler_params=_params(("arbitrary",), 32),
        name="moe_dispatch",
    )(dest3, h2p, xs_zero)


GROUP_SUB = 4
GROUP_ROWS = GROUP_SUB * ROW_BLOCK
FF_TILE = 512
N_FF_TILES = D_FF // FF_TILE
DOWN_CHUNK = 512


def _expert_kernel(ge_ref, gs_ref, gn_ref, xs_in, wg_ref, wu_ref, bg_ref, bu_ref, wd_ref, bd_ref,
                   rows_hbm, xwin, x_scr, acc_scr, stage, wsem, sem):
    del ge_ref, xs_in
    g = pl.program_id(0)
    f = pl.program_id(1)
    ns = gn_ref[g]
    half = D_MODEL // 2
    slot = g & 1

    def win_copy(group, s):
        rows = pl.ds(gs_ref[group] * (ROW_BLOCK * SUBLANES), GROUP_ROWS * SUBLANES)
        return pltpu.make_async_copy(rows_hbm.at[rows, :], xwin.at[s], wsem.at[s])

    def out_copy(s, start_block):
        rows = pl.ds((start_block + s) * (ROW_BLOCK * SUBLANES), ROW_BLOCK * SUBLANES)
        return pltpu.make_async_copy(stage.at[s], rows_hbm.at[rows, :], sem.at[s])

    @pl.when((f == 0) & (ns > 0))
    def _():
        @pl.when(g == 0)
        def _():
            win_copy(0, 0).start()
            acc_scr[...] = jnp.zeros_like(acc_scr)

        win_copy(g, slot).wait()
        @pl.when(g + 1 < pl.num_programs(0))
        def _():
            @pl.when(gn_ref[g + 1] > 0)
            def _():
                win_copy(g + 1, 1 - slot).start()

        for c in range(ROW_CHUNKS):
            hi, lo = _unpack_bf16_pair(xwin[slot, _row_chunk(GROUP_ROWS, c), :])
            x_scr[:, c * LANES:(c + 1) * LANES] = hi.astype(BF16)
            x_scr[:, half + c * LANES:half + (c + 1) * LANES] = lo.astype(BF16)

    for n in range(1, GROUP_SUB + 1):
        @pl.when(ns == n)
        def _(n=n):
            m = n * ROW_BLOCK
            x = x_scr[:m, :]
            gate = jnp.dot(x, wg_ref[...].astype(BF16), preferred_element_type=F32) + bg_ref[...]
            up = jnp.dot(x, wu_ref[...].astype(BF16), preferred_element_type=F32) + bu_ref[...]
            gate = jnp.minimum(gate, SWIGLU_LIMIT)
            up = jnp.clip(up, -SWIGLU_LIMIT, SWIGLU_LIMIT)
            glu = gate * jax.nn.sigmoid(gate * SWIGLU_ALPHA)
            act = ((up + 1.0) * glu).astype(BF16)
            for c in range(D_MODEL // DOWN_CHUNK):
                cols = slice(c * DOWN_CHUNK, (c + 1) * DOWN_CHUNK)
                start = jnp.where(f == 0, jnp.broadcast_to(bd_ref[:, cols], (m, DOWN_CHUNK)), acc_scr[:m, cols])
                acc_scr[:m, cols] = start + jnp.dot(act, wd_ref[:, cols].astype(BF16),
                                                    preferred_element_type=F32)

    @pl.when(f == N_FF_TILES - 1)
    def _():
        @pl.when(g > 0)
        def _():
            prev = gn_ref[g - 1]
            for s in range(GROUP_SUB):
                @pl.when(s < prev)
                def _(s=s):
                    out_copy(s, 0).wait()

        for s in range(GROUP_SUB):
            @pl.when(s < ns)
            def _(s=s):
                y = acc_scr[s * ROW_BLOCK:(s + 1) * ROW_BLOCK, :]
                _store_row_tiles(stage.at[s], ROW_BLOCK, _pack_bf16_pair(y[:, :half], y[:, half:]))
                out_copy(s, gs_ref[g]).start()

        @pl.when(g == pl.num_programs(0) - 1)
        def _():
            for s in range(GROUP_SUB):
                @pl.when(s < ns)
                def _(s=s):
                    out_copy(s, 0).wait()


def _experts(g_exp, g_start, g_nsub, xs, wgu, bgu, wd, bd):
    def ff(f, gn, g):
        return jnp.where(gn[g] > 0, f, N_FF_TILES - 1)

    grid_spec = pltpu.PrefetchScalarGridSpec(
        num_scalar_prefetch=3,
        grid=(g_exp.shape[0], N_FF_TILES),
        in_specs=[
            pl.BlockSpec(memory_space=pl.ANY),
            pl.BlockSpec((None, D_MODEL, FF_TILE), lambda g, f, ge, gs, gn: (ge[g], 0, ff(f, gn, g))),
            pl.BlockSpec((None, D_MODEL, FF_TILE),
                         lambda g, f, ge, gs, gn: (ge[g], 0, N_FF_TILES + ff(f, gn, g))),
            pl.BlockSpec((None, 1, FF_TILE), lambda g, f, ge, gs, gn: (ge[g], 0, ff(f, gn, g))),
            pl.BlockSpec((None, 1, FF_TILE), lambda g, f, ge, gs, gn: (ge[g], 0, N_FF_TILES + ff(f, gn, g))),
            pl.BlockSpec((None, FF_TILE, D_MODEL), lambda g, f, ge, gs, gn: (ge[g], ff(f, gn, g), 0)),
            pl.BlockSpec((None, 1, D_MODEL), lambda g, f, ge, gs, gn: (ge[g], 0, 0)),
        ],
        out_specs=pl.BlockSpec(memory_space=pl.ANY),
        scratch_shapes=[
            pltpu.VMEM((2, GROUP_ROWS * SUBLANES, LANES), jnp.uint32),
            pltpu.VMEM((GROUP_ROWS, D_MODEL), BF16),
            pltpu.VMEM((GROUP_ROWS, D_MODEL), F32),
            pltpu.VMEM((GROUP_SUB, ROW_BLOCK * SUBLANES, LANES), jnp.uint32),
            pltpu.SemaphoreType.DMA((2,)),
            pltpu.SemaphoreType.DMA((GROUP_SUB,)),
        ],
    )
    return pl.pallas_call(
        _expert_kernel,
        grid_spec=grid_spec,
        out_shape=jax.ShapeDtypeStruct(xs.shape, jnp.uint32),
        input_output_aliases={3: 0},
        compiler_params=_params(("arbitrary", "arbitrary"), 58),
        name="moe_experts",
    )(g_exp, g_start, g_nsub, xs, wgu, wgu, bgu, bgu, wd, bd)


CB_TM = 256


def _combine_kernel(dest_ref, dest_next_ref, ys_hbm, x1_ref, gate_ref, o_ref, buf, sem):
    i = pl.program_id(0)
    slot = i & 1

    def gather_tile(dest, s):
        def issue(t, carry):
            for k in range(TOP_K):
                row = dest[0, t * TOP_K + k]
                pltpu.make_async_copy(ys_hbm.at[_row_tile(row), :], buf.at[s, k, _row_tile(t), :],
                                      sem.at[s]).start(priority=k % 2)
            return carry

        lax.fori_loop(0, CB_TM, issue, 0, unroll=DMA_UNROLL // TOP_K)

    @pl.when(i == 0)
    def _():
        gather_tile(dest_ref, 0)

    @pl.when(i + 1 < pl.num_programs(0))
    def _():
        gather_tile(dest_next_ref, 1 - slot)

    pltpu.make_async_copy(buf.at[slot], buf.at[slot], sem.at[slot]).wait()

    half = D_MODEL // 2
    gates = gate_ref[...]
    gate_k = [jnp.broadcast_to(gates[:, k:k + 1], (CB_TM, LANES)) for k in range(TOP_K)]
    for c in range(ROW_CHUNKS):
        cols_hi = slice(c * LANES, (c + 1) * LANES)
        cols_lo = slice(half + c * LANES, half + (c + 1) * LANES)
        out_hi = x1_ref[:, cols_hi]
        out_lo = x1_ref[:, cols_lo]
        for k in range(TOP_K):
            hi, lo = _unpack_bf16_pair(buf[slot, k, _row_chunk(CB_TM, c), :])
            out_hi = out_hi + gate_k[k] * hi
            out_lo = out_lo + gate_k[k] * lo
        o_ref[:, cols_hi] = out_hi
        o_ref[:, cols_lo] = out_lo


def _combine(dest3, ys, x1, gates):
    T = x1.shape[0]
    n = T // CB_TM
    dest_spec = lambda f: pl.BlockSpec((None, 1, CB_TM * TOP_K), f, memory_space=pltpu.SMEM)
    return pl.pallas_call(
        _combine_kernel,
        grid=(n,),
        in_specs=[
            dest_spec(lambda i: (i, 0, 0)),
            dest_spec(lambda i: (jnp.minimum(i + 1, n - 1), 0, 0)),
            pl.BlockSpec(memory_space=pl.ANY),
            pl.BlockSpec((CB_TM, D_MODEL), lambda i: (i, 0)),
            pl.BlockSpec((CB_TM, LANES), lambda i: (i, 0)),
        ],
        out_specs=pl.BlockSpec((CB_TM, D_MODEL), lambda i: (i, 0)),
        out_shape=jax.ShapeDtypeStruct((T, D_MODEL), F32),
        scratch_shapes=[pltpu.VMEM((2, TOP_K, CB_TM * SUBLANES, LANES), jnp.uint32),
                        pltpu.SemaphoreType.DMA((2,))],
        compiler_params=_params(("arbitrary",), 32),
        name="moe_combine",
    )(dest3, dest3, ys, x1, gates)


def _group_tables(counts_f32, max_groups):
    i32 = jnp.int32
    counts = counts_f32[0, :N_EXPERTS].astype(i32)
    nb = (counts + ROW_BLOCK - 1) // ROW_BLOCK
    ng = (nb + GROUP_SUB - 1) // GROUP_SUB
    upto = jnp.arange(N_EXPERTS)[None, :] <= jnp.arange(N_EXPERTS)[:, None]
    pad_ends = jnp.sum(jnp.where(upto, nb[None, :], 0), axis=1) * ROW_BLOCK
    pad_starts = pad_ends - nb * ROW_BLOCK
    g_ends = jnp.sum(jnp.where(upto, ng[None, :], 0), axis=1)
    g_starts = g_ends - ng
    n_groups = g_ends[-1]

    g = jnp.arange(max_groups, dtype=i32)
    gg = jnp.minimum(g, n_groups - 1)
    e = jnp.minimum(jnp.sum((g_ends[None, :] <= gg[:, None]).astype(i32), axis=1), N_EXPERTS - 1)
    pick = e[:, None] == jnp.arange(N_EXPERTS)[None, :]
    take = lambda table: jnp.sum(jnp.where(pick, table[None, :], 0), axis=1)
    nb_g, ng_g = take(nb), jnp.maximum(take(ng), 1)
    j = gg - take(g_starts)
    base, rem = nb_g // ng_g, nb_g % ng_g
    nsub = jnp.where(g < n_groups, base + (j < rem).astype(i32), 0)
    start_block = take(pad_starts) // ROW_BLOCK + j * base + jnp.minimum(j, rem)
    return pad_starts.astype(i32), e.astype(i32), start_block.astype(i32), nsub.astype(i32)


def _rope_constants():
    lane = np.arange(LANES)
    half = QK_ROPE // 2
    inv = 1.0 / (ROPE_THETA ** (np.arange(0, QK_ROPE, 2, dtype=np.float32) / QK_ROPE))
    cst = np.zeros((8, LANES), np.float32)
    cst[0, :QK_ROPE] = inv.astype(np.float32)[lane[:QK_ROPE] % half]
    cst[1, :QK_ROPE] = 1.0
    cst[2, :half] = -1.0
    cst[2, half:QK_ROPE] = 1.0
    return jnp.asarray(cst)


def _swap_halves(a):
    half = QK_ROPE // 2
    return jnp.concatenate([a[..., half:], a[..., :half]], axis=-1)


def kernel(x, mem, positions, ln1_g, w_in, w_gate, b_gate, gmlp_ln_g, gmlp_ln_b, gmlp_ws, gmlp_bs, w_pa,
           mla_cq_g, mla_w_uq, mla_ckv_g, mla_w_ukv, mla_qn_g, mla_kn_g, w_pb, mem_ln_g, mem_w_k, mem_w_v,
           mem_qn_g, mem_kn_g, w_pc, w_o, ln2_g, router_w, router_b, moe_w_gu, moe_b_gu, moe_w_down,
           moe_b_down):
    B, S, D = x.shape
    T = B * S
    x2 = x.reshape(T, D)
    for l in range(ln1_g.shape[0]):
        o_kr = 2 * GM_WIDTH + Q_LORA + KV_LORA
        o_qm = o_kr + QK_ROPE
        wi = w_in[l]
        w1 = jnp.concatenate([wi[:, :o_kr], wi[:, o_qm:], w_gate[l]], axis=1).astype(BF16)
        b1 = jnp.concatenate([jnp.zeros((PROJ_COLS,), F32), b_gate[l]])[None, :]
        w_kr = wi[:, o_kr:o_qm]
        wr = jnp.concatenate([w_kr, _swap_halves(w_kr)], axis=1).astype(BF16)

        wq3 = mla_w_uq[l].reshape(Q_LORA, MLA_HEADS, QK_DIM)
        wq = jnp.concatenate([wq3, _swap_halves(wq3[..., QK_NOPE:])], axis=-1)
        wq = wq.reshape(Q_LORA, MLA_HEADS * QK_PAD).astype(BF16)
        wkv = mla_w_ukv[l].astype(BF16)
        gq = jnp.concatenate([mla_qn_g[l], _swap_halves(mla_qn_g[l][QK_NOPE:])])[None, :]
        gk = jnp.concatenate([mla_kn_g[l], _swap_halves(mla_kn_g[l][QK_NOPE:])])[None, :]

        bias_full = jnp.broadcast_to(gmlp_bs[l].T[:, :, None], (GM_CHUNK, GM_GROUPS, GM_CHUNK))
        bias_full = bias_full.reshape(GM_CHUNK, GM_WIDTH)

        rw = jnp.pad(router_w[l], ((0, 0), (0, LANES - N_EXPERTS))).astype(BF16)
        rb = jnp.pad(router_b[l], (0, LANES - N_EXPERTS))[None, :]

        pg, kr = _norm_proj(x2, ln1_g[l][None, :], w1, b1, wr)
        m0 = _gmlp(pg, gmlp_ln_g[l][None, :], gmlp_ln_b[l][None, :], gmlp_ws[l].astype(BF16), bias_full,
                   w_pa[l].astype(BF16))
        q, k, vt = _mla_qkv(pg, kr, positions.reshape(T, 1), mla_cq_g[l][None, :], mla_ckv_g[l][None, :],
                           wq, wkv, gq, gk, _rope_constants(), B, S)
        b_attn = _flash(q, k, vt)
        km, vm = _mem_kv(mem, mem_ln_g[l][None, :], mem_w_k[l].astype(BF16), mem_w_v[l].astype(BF16),
                         mem_kn_g[l][None, :])
        m1 = _mem_attn(pg, km, vm, m0, mem_qn_g[l][None, :], w_pc[l].astype(BF16), S)
        n_rows = T * TOP_K + N_EXPERTS * ROW_BLOCK
        merged, xs_zero = _attn_proj(b_attn, pg, m1, w_pb[l].astype(BF16), n_rows + GROUP_ROWS)
        x1, h2p, code, gates, counts = _merge(merged, x2, w_o[l].astype(BF16), ln2_g[l][None, :], rw, rb)

        max_groups = n_rows // GROUP_ROWS + N_EXPERTS
        pad_starts, g_exp, g_start, g_nsub = _group_tables(counts, max_groups)
        codes = code[:, :TOP_K]
        pick = (codes & (N_EXPERTS - 1))[..., None] == jnp.arange(N_EXPERTS)
        dest = (codes >> EXPERT_BITS) + jnp.sum(jnp.where(pick, pad_starts, 0), axis=-1)
        xs = _dispatch(dest.reshape(T // DP_TM, 1, DP_TM * TOP_K), h2p, xs_zero)
        ys = _experts(g_exp, g_start, g_nsub, xs, moe_w_gu[l], moe_b_gu[l][:, None, :],
                      moe_w_down[l], moe_b_down[l][:, None, :])
        x2 = _combine(dest.reshape(T // CB_TM, 1, CB_TM * TOP_K), ys, x1, gates)
    return x2.reshape(B, S, D)
```

```python
import functools

import numpy as np
import jax
import jax.numpy as jnp
from jax import lax
from jax.experimental import pallas as pl
from jax.experimental.pallas import tpu as pltpu

F32 = jnp.float32
BF16 = jnp.bfloat16

D_MODEL = 2048
GM_WIDTH = 1024
GM_GROUPS = 8
GM_CHUNK = 128
MLA_HEADS = 16
Q_LORA = 512
KV_LORA = 512
QK_NOPE = 128
QK_ROPE = 64
V_DIM = 128
QK_DIM = QK_NOPE + QK_ROPE
QK_PAD = 256
MLA_SCALE = QK_DIM ** -0.5
LOG2_E = 1.4426950408889634
ROPE_THETA = 10000.0
MEM_LEN = 256
MEM_HEADS = 4
MEM_HEAD_DIM = 256
MEM_WIDTH = MEM_HEADS * MEM_HEAD_DIM
MEM_SCALE = MEM_HEAD_DIM ** -0.5
N_EXPERTS = 32
EXPERT_BITS = 5
TOP_K = 4
D_FF = 2048
SWIGLU_LIMIT = 7.0
SWIGLU_ALPHA = 1.702
ROW_BLOCK = 256
EPS = 1e-6
LANES = 128
NEG_INF = float("-inf")

PROJ_COLS = 2 * GM_WIDTH + Q_LORA + KV_LORA + MEM_WIDTH
PG_COLS = PROJ_COLS + 3 * D_MODEL

MIB = 1024 * 1024


def _params(semantics, vmem_mib):
    return pltpu.CompilerParams(dimension_semantics=semantics, vmem_limit_bytes=vmem_mib * MIB)


def _const_spec(shape):
    nd = len(shape)
    return pl.BlockSpec(shape, lambda *_: (0,) * nd, pipeline_mode=pl.Buffered(1))


P1_TM = 1024
P1_TN = 2048


def _norm_proj_kernel(x_ref, g_ref, w_ref, b_ref, wr_ref, o_ref, kr_ref, h_scr, *, n_plain):
    j = pl.program_id(1)

    @pl.when(j == 0)
    def _():
        def body(c, carry):
            rows = pl.ds(pl.multiple_of(c * 128, 128), 128)
            x = x_ref[rows, :]
            ms = jnp.mean(x * x, axis=-1, keepdims=True)
            h_scr[rows, :] = (x * lax.rsqrt(ms + EPS) * g_ref[...]).astype(BF16)
            return carry

        lax.fori_loop(0, P1_TM // 128, body, 0)
        kr_ref[...] = jnp.dot(h_scr[...], wr_ref[...], preferred_element_type=F32)

    acc = jnp.dot(h_scr[...], w_ref[...], preferred_element_type=F32)
    gated = jax.nn.sigmoid(acc + b_ref[...])
    o_ref[...] = jnp.where(j >= n_plain, gated, acc).astype(BF16)


def _norm_proj(x2, ln1_g, w1, b1, wr):
    T = x2.shape[0]
    grid = (T // P1_TM, PG_COLS // P1_TN)
    return pl.pallas_call(
        functools.partial(_norm_proj_kernel, n_plain=PROJ_COLS // P1_TN),
        grid=grid,
        in_specs=[
            pl.BlockSpec((P1_TM, D_MODEL), lambda i, j: (i, 0)),
            pl.BlockSpec((1, D_MODEL), lambda i, j: (0, 0)),
            pl.BlockSpec((D_MODEL, P1_TN), lambda i, j: (0, j)),
            pl.BlockSpec((1, P1_TN), lambda i, j: (0, j)),
            pl.BlockSpec((D_MODEL, LANES), lambda i, j: (0, 0)),
        ],
        out_specs=[
            pl.BlockSpec((P1_TM, P1_TN), lambda i, j: (i, j)),
            pl.BlockSpec((P1_TM, LANES), lambda i, j: (i, 0)),
        ],
        out_shape=[
            jax.ShapeDtypeStruct((T, PG_COLS), BF16),
            jax.ShapeDtypeStruct((T, LANES), F32),
        ],
        scratch_shapes=[pltpu.VMEM((P1_TM, D_MODEL), BF16)],
        compiler_params=_params(("parallel", "arbitrary"), 58),
        name="norm_proj",
    )(x2, ln1_g, w1, b1, wr)


GM_TM = 512


def _gmlp_kernel(u_ref, v_ref, g0_ref, lng_ref, lnb_ref, ws_ref, bias_ref, wpa_ref, o_ref, vb_scr, a_scr):
    v = v_ref[...].astype(F32)
    mu = jnp.mean(v, axis=-1, keepdims=True)
    c = v - mu
    var = jnp.mean(c * c, axis=-1, keepdims=True)
    vb_scr[...] = (c * lax.rsqrt(var + EPS) * lng_ref[...] + lnb_ref[...]).astype(BF16)

    row = lax.broadcasted_iota(jnp.int32, (GM_CHUNK, GM_CHUNK), 0)
    col = lax.broadcasted_iota(jnp.int32, (GM_CHUNK, GM_CHUNK), 1)
    causal = col <= row
    for g in range(GM_GROUPS):
        cols = slice(g * GM_CHUNK, (g + 1) * GM_CHUNK)
        wg = jnp.where(causal, ws_ref[g], jnp.zeros((), BF16))
        chunks = [slice(ch * GM_CHUNK, (ch + 1) * GM_CHUNK) for ch in range(GM_TM // GM_CHUNK)]
        mixed = jnp.dot(wg, jnp.concatenate([vb_scr[rows, cols] for rows in chunks], axis=1),
                        preferred_element_type=F32)
        for ch, rows in enumerate(chunks):
            mix = mixed[:, ch * GM_CHUNK:(ch + 1) * GM_CHUNK] + bias_ref[:, cols]
            a_scr[rows, cols] = (u_ref[rows, cols].astype(F32) * mix).astype(BF16)

    ma = jnp.dot(a_scr[...], wpa_ref[...], preferred_element_type=F32)
    o_ref[...] = (g0_ref[...].astype(F32) * ma).astype(BF16)


def _gmlp(pg, ln_g, ln_b, ws, bias_full, wpa):
    T = pg.shape[0]
    return pl.pallas_call(
        _gmlp_kernel,
        grid=(T // GM_TM,),
        in_specs=[
            pl.BlockSpec((GM_TM, GM_WIDTH), lambda i: (i, 0)),
            pl.BlockSpec((GM_TM, GM_WIDTH), lambda i: (i, 1)),
            pl.BlockSpec((GM_TM, D_MODEL), lambda i: (i, PROJ_COLS // D_MODEL)),
            _const_spec((1, GM_WIDTH)),
            _const_spec((1, GM_WIDTH)),
            _const_spec((GM_GROUPS, GM_CHUNK, GM_CHUNK)),
            _const_spec((GM_CHUNK, GM_WIDTH)),
            _const_spec((GM_WIDTH, D_MODEL)),
        ],
        out_specs=pl.BlockSpec((GM_TM, D_MODEL), lambda i: (i, 0)),
        out_shape=jax.ShapeDtypeStruct((T, D_MODEL), BF16),
        scratch_shapes=[pltpu.VMEM((GM_TM, GM_WIDTH), BF16), pltpu.VMEM((GM_TM, GM_WIDTH), BF16)],
        compiler_params=_params(("parallel",), 40),
        name="gmlp",
    )(pg, pg, pg, ln_g, ln_b, ws, bias_full, wpa)


QKV_TM = 1024
QKV_HEADS = 2
HEAD_BLK = 4
FA_T = 512
FA_QS = 256


def _rope(t, cos, sin):
    return t * cos + pltpu.roll(t, QK_ROPE, 1) * sin


def _row_sum_all_lanes(sq, weights):
    return jnp.dot(sq.astype(BF16), weights.astype(BF16), preferred_element_type=F32)


def _mla_qkv_kernel(cq_ref, ckv_ref, kr_ref, pos_ref, gcq_ref, gckv_ref, wq_ref, wkv_ref, gq_ref, gk_ref,
                    cst_ref, sumw_ref, q_ref, k_ref, vt_ref, cqn_scr, ckvn_scr, cos_scr, sin_scr, krsq_scr):
    hb = pl.program_id(1)
    w_nope = sumw_ref[:QK_NOPE, :]
    w_rope = sumw_ref[QK_NOPE:, :]

    @pl.when(hb == 0)
    def _():
        cq = cq_ref[...].astype(F32)
        ms = jnp.mean(cq * cq, axis=-1, keepdims=True)
        cqn_scr[...] = (cq * lax.rsqrt(ms + EPS) * gcq_ref[...]).astype(BF16)
        ckv = ckv_ref[...].astype(F32)
        ms = jnp.mean(ckv * ckv, axis=-1, keepdims=True)
        ckvn_scr[...] = (ckv * lax.rsqrt(ms + EPS) * gckv_ref[...]).astype(BF16)
        ang = pos_ref[...].astype(F32) * cst_ref[0:1, :]
        cos_scr[...] = jnp.cos(ang) * cst_ref[1:2, :]
        sin_scr[...] = jnp.sin(ang) * cst_ref[2:3, :]
        kr = kr_ref[...]
        krsq_scr[...] = _row_sum_all_lanes(kr * kr, w_rope)

    cos = cos_scr[...]
    sin = sin_scr[...]
    inv_dim = 1.0 / QK_DIM
    kr = kr_ref[...]

    yq2 = jnp.dot(cqn_scr[...], wq_ref[...], preferred_element_type=F32)
    ykv2 = jnp.dot(ckvn_scr[...], wkv_ref[...], preferred_element_type=F32)
    for hh in range(QKV_HEADS):
        yq = yq2[:, hh * QK_PAD:(hh + 1) * QK_PAD]
        qn = yq[:, :QK_NOPE]
        qt = yq[:, QK_NOPE:]
        ssq = _row_sum_all_lanes(qn * qn, w_nope) + _row_sum_all_lanes(qt * qt, w_rope)
        rs = lax.rsqrt(ssq * inv_dim + EPS) * (MLA_SCALE * LOG2_E)
        q_ref[hh, :, :QK_NOPE] = (qn * rs * gq_ref[:, :QK_NOPE]).astype(BF16)
        q_ref[hh, :, QK_NOPE:] = _rope(qt * rs * gq_ref[:, QK_NOPE:], cos, sin).astype(BF16)

        ykv = ykv2[:, hh * (QK_NOPE + V_DIM):(hh + 1) * (QK_NOPE + V_DIM)]
        kn = ykv[:, :QK_NOPE]
        ssq = _row_sum_all_lanes(kn * kn, w_nope) + krsq_scr[...]
        rs = lax.rsqrt(ssq * inv_dim + EPS)
        k_ref[hh, :, :QK_NOPE] = (kn * rs * gk_ref[:, :QK_NOPE]).astype(BF16)
        k_ref[hh, :, QK_NOPE:] = _rope(kr * rs * gk_ref[:, QK_NOPE:], cos, sin).astype(BF16)
        vv = ykv[:, QK_NOPE:]
        for c in range(QKV_TM // FA_T):
            vt_ref[hh, c] = vv[c * FA_T:(c + 1) * FA_T, :].T.astype(BF16)


def _mla_qkv(pg, kr, pos, gcq, gckv, wq, wkv, gq, gk, cst, B, S):
    T = pg.shape[0]
    sumw = jnp.concatenate([jnp.ones((QK_NOPE, LANES), F32), jnp.full((QK_PAD - QK_NOPE, LANES), 0.5, F32)])
    spb = S // QKV_TM
    cpt = QKV_TM // FA_T
    head_spec = lambda w: pl.BlockSpec((None, QKV_HEADS, QKV_TM, w), lambda i, h: (i // spb, h, i % spb, 0))
    return pl.pallas_call(
        _mla_qkv_kernel,
        grid=(T // QKV_TM, MLA_HEADS // QKV_HEADS),
        in_specs=[
            pl.BlockSpec((QKV_TM, Q_LORA), lambda i, h: (i, 2 * GM_WIDTH // Q_LORA)),
            pl.BlockSpec((QKV_TM, KV_LORA), lambda i, h: (i, 2 * GM_WIDTH // KV_LORA + 1)),
            pl.BlockSpec((QKV_TM, LANES), lambda i, h: (i, 0)),
            pl.BlockSpec((QKV_TM, 1), lambda i, h: (i, 0)),
            pl.BlockSpec((1, Q_LORA), lambda i, h: (0, 0)),
            pl.BlockSpec((1, KV_LORA), lambda i, h: (0, 0)),
            pl.BlockSpec((Q_LORA, QKV_HEADS * QK_PAD), lambda i, h: (0, h)),
            pl.BlockSpec((KV_LORA, QKV_HEADS * (QK_NOPE + V_DIM)), lambda i, h: (0, h)),
            pl.BlockSpec((1, QK_PAD), lambda i, h: (0, 0)),
            pl.BlockSpec((1, QK_PAD), lambda i, h: (0, 0)),
            pl.BlockSpec((8, LANES), lambda i, h: (0, 0)),
            pl.BlockSpec((QK_PAD, LANES), lambda i, h: (0, 0)),
        ],
        out_specs=[
            head_spec(QK_PAD),
            head_spec(QK_PAD),
            pl.BlockSpec((None, QKV_HEADS, cpt, V_DIM, FA_T), lambda i, h: (i // spb, h, i % spb, 0, 0)),
        ],
        out_shape=[
            jax.ShapeDtypeStruct((B, MLA_HEADS, S, QK_PAD), BF16),
            jax.ShapeDtypeStruct((B, MLA_HEADS, S, QK_PAD), BF16),
            jax.ShapeDtypeStruct((B, MLA_HEADS, S // FA_T, V_DIM, FA_T), BF16),
        ],
        scratch_shapes=[
            pltpu.VMEM((QKV_TM, Q_LORA), BF16),
            pltpu.VMEM((QKV_TM, KV_LORA), BF16),
            pltpu.VMEM((QKV_TM, LANES), F32),
            pltpu.VMEM((QKV_TM, LANES), F32),
            pltpu.VMEM((QKV_TM, LANES), F32),
        ],
        compiler_params=_params(("parallel", "arbitrary"), 48),
        name="mla_qkv",
    )(pg, pg, kr, pos, gcq, gckv, wq, wkv, gq, gk, cst, sumw)


def _flash_kernel(q_ref, k_ref, vt_ref, o_ref, m_scr, l_scr, acc_scr):
    qi = pl.program_id(2)
    m_scr[...] = jnp.full_like(m_scr, NEG_INF)
    l_scr[...] = jnp.zeros_like(l_scr)
    acc_scr[...] = jnp.zeros_like(acc_scr)

    def step(kb, diagonal):
        ks = pl.ds(pl.multiple_of(kb * FA_T, FA_T), FA_T)
        chains = [(hh, qs) for hh in range(HEAD_BLK) for qs in range(FA_T // FA_QS)]
        scores = []
        for hh, qs in chains:
            qc = slice(qs * FA_QS, (qs + 1) * FA_QS)
            st = lax.dot_general(k_ref[hh, ks, :], q_ref[hh, qc, :], (((1,), (1,)), ((), ())),
                                 preferred_element_type=F32)
            if diagonal:
                krow = lax.broadcasted_iota(jnp.int32, (FA_T, FA_QS), 0)
                qcol = lax.broadcasted_iota(jnp.int32, (FA_T, FA_QS), 1) + qs * FA_QS
                st = jnp.where(krow <= qcol, st, NEG_INF)
            scores.append(st)
        probs = []
        for (hh, qs), st in zip(chains, scores):
            qc = slice(qs * FA_QS, (qs + 1) * FA_QS)
            m_prev = m_scr[hh, :, qc]
            m_new = jnp.maximum(m_prev, jnp.max(st, axis=0, keepdims=True))
            alpha = jnp.exp2(m_prev - m_new)
            p = jnp.exp2(st - m_new)
            l_scr[hh, :, qc] = alpha * l_scr[hh, :, qc] + jnp.sum(p, axis=0, keepdims=True)
            m_scr[hh, :, qc] = m_new
            probs.append((alpha, p.astype(BF16)))
        for (hh, qs), (alpha, p) in zip(chains, probs):
            qc = slice(qs * FA_QS, (qs + 1) * FA_QS)
            acc_scr[hh, :, qc] = alpha * acc_scr[hh, :, qc] + jnp.dot(vt_ref[hh, kb], p,
                                                                      preferred_element_type=F32)

    def body(kb, carry):
        step(kb, False)
        return carry

    lax.fori_loop(0, qi, body, 0)
    step(qi, True)
    for hh in range(HEAD_BLK):
        o_ref[:, hh * V_DIM:(hh + 1) * V_DIM] = (acc_scr[hh] / l_scr[hh]).T.astype(BF16)


def _flash(q, k, vt):
    B, H, S, _ = q.shape
    nq = S // FA_T
    return pl.pallas_call(
        _flash_kernel,
        grid=(B, H // HEAD_BLK, nq),
        in_specs=[
            pl.BlockSpec((None, HEAD_BLK, FA_T, QK_PAD), lambda b, h, i: (b, h, i, 0)),
            pl.BlockSpec((None, HEAD_BLK, S, QK_PAD), lambda b, h, i: (b, h, 0, 0)),
            pl.BlockSpec((None, HEAD_BLK, nq, V_DIM, FA_T), lambda b, h, i: (b, h, 0, 0, 0)),
        ],
        out_specs=pl.BlockSpec((FA_T, HEAD_BLK * V_DIM), lambda b, h, i: (b * nq + i, h)),
        out_shape=jax.ShapeDtypeStruct((B * S, H * V_DIM), BF16),
        scratch_shapes=[
            pltpu.VMEM((HEAD_BLK, 1, FA_T), F32),
            pltpu.VMEM((HEAD_BLK, 1, FA_T), F32),
            pltpu.VMEM((HEAD_BLK, V_DIM, FA_T), F32),
        ],
        compiler_params=_params(("parallel", "parallel", "arbitrary"), 40),
        name="mla_attention",
    )(q, k, vt)


def _mem_kv_kernel(mem_ref, g_ref, wk_ref, wv_ref, kng_ref, k_ref, v_ref):
    m = mem_ref[...]
    ms = jnp.mean(m * m, axis=-1, keepdims=True)
    mb = (m * lax.rsqrt(ms + EPS) * g_ref[...]).astype(BF16)
    kk = jnp.dot(mb, wk_ref[...], preferred_element_type=F32)
    for h in range(MEM_HEADS):
        cols = slice(h * MEM_HEAD_DIM, (h + 1) * MEM_HEAD_DIM)
        kh = kk[:, cols]
        ms = jnp.mean(kh * kh, axis=-1, keepdims=True)
        k_ref[:, cols] = (kh * lax.rsqrt(ms + EPS) * kng_ref[...]).astype(BF16)
    v_ref[...] = jnp.dot(mb, wv_ref[...], preferred_element_type=F32).astype(BF16)


def _mem_kv(mem, g, wk, wv, kng):
    B = mem.shape[0]
    return pl.pallas_call(
        _mem_kv_kernel,
        grid=(B,),
        in_specs=[
            pl.BlockSpec((None, MEM_LEN, D_MODEL), lambda b: (b, 0, 0)),
            _const_spec((1, D_MODEL)),
            _const_spec((D_MODEL, MEM_WIDTH)),
            _const_spec((D_MODEL, MEM_WIDTH)),
            _const_spec((1, MEM_HEAD_DIM)),
        ],
        out_specs=[
            pl.BlockSpec((None, MEM_LEN, MEM_WIDTH), lambda b: (b, 0, 0)),
            pl.BlockSpec((None, MEM_LEN, MEM_WIDTH), lambda b: (b, 0, 0)),
        ],
        out_shape=[
            jax.ShapeDtypeStruct((B, MEM_LEN, MEM_WIDTH), BF16),
            jax.ShapeDtypeStruct((B, MEM_LEN, MEM_WIDTH), BF16),
        ],
        compiler_params=_params(("parallel",), 32),
        name="mem_kv",
    )(mem, g, wk, wv, kng)


MA_TM = 512


def _mem_attn_kernel(qm_ref, km_ref, vm_ref, g2_ref, m0_ref, qng_ref, wpc_ref, o_ref, c_scr):
    for h in range(MEM_HEADS):
        cols = slice(h * MEM_HEAD_DIM, (h + 1) * MEM_HEAD_DIM)
        qh = qm_ref[:, cols].astype(F32)
        ms = jnp.mean(qh * qh, axis=-1, keepdims=True)
        qn = (qh * (lax.rsqrt(ms + EPS) * MEM_SCALE) * qng_ref[...]).astype(BF16)
        s = lax.dot_general(qn, km_ref[:, cols], (((1,), (1,)), ((), ())), preferred_element_type=F32)
        e = jnp.exp(s - jnp.max(s, axis=-1, keepdims=True))
        p = (e / jnp.sum(e, axis=-1, keepdims=True)).astype(BF16)
        c_scr[:, cols] = jnp.dot(p, vm_ref[:, cols], preferred_element_type=F32).astype(BF16)
    mc = jnp.dot(c_scr[...], wpc_ref[...], preferred_element_type=F32)
    o_ref[...] = (m0_ref[...].astype(F32) + g2_ref[...].astype(F32) * mc).astype(BF16)


def _mem_attn(pg, km, vm, m0, qng, wpc, S):
    T = pg.shape[0]
    spb = S // MA_TM
    return pl.pallas_call(
        _mem_attn_kernel,
        grid=(T // MA_TM,),
        in_specs=[
            pl.BlockSpec((MA_TM, MEM_WIDTH), lambda i: (i, (2 * GM_WIDTH + Q_LORA + KV_LORA) // MEM_WIDTH)),
            pl.BlockSpec((None, MEM_LEN, MEM_WIDTH), lambda i: (i // spb, 0, 0)),
            pl.BlockSpec((None, MEM_LEN, MEM_WIDTH), lambda i: (i // spb, 0, 0)),
            pl.BlockSpec((MA_TM, D_MODEL), lambda i: (i, PROJ_COLS // D_MODEL + 2)),
            pl.BlockSpec((MA_TM, D_MODEL), lambda i: (i, 0)),
            _const_spec((1, MEM_HEAD_DIM)),
            _const_spec((MEM_WIDTH, D_MODEL)),
        ],
        out_specs=pl.BlockSpec((MA_TM, D_MODEL), lambda i: (i, 0)),
        out_shape=jax.ShapeDtypeStruct((T, D_MODEL), BF16),
        scratch_shapes=[pltpu.VMEM((MA_TM, MEM_WIDTH), BF16)],
        compiler_params=_params(("parallel",), 40),
        name="mem_attention",
    )(pg, km, vm, pg, m0, qng, wpc)


MG_TM = 512


def _pack_bf16_pair(a, b):
    hi = pltpu.bitcast(a.astype(BF16).astype(F32), jnp.uint32)
    lo = pltpu.bitcast(b.astype(BF16).astype(F32), jnp.uint32)
    return hi | (lo >> 16)


def _unpack_bf16_pair(p):
    hi = pltpu.bitcast(p & jnp.uint32(0xFFFF0000), F32)
    lo = pltpu.bitcast(p << 16, F32)
    return hi, lo


SUBLANES = 8
ROW_WORDS = D_MODEL // 2
ROW_CHUNKS = ROW_WORDS // LANES
assert ROW_CHUNKS == SUBLANES


def _row_chunk(n_rows, c):
    return pl.ds(c, n_rows, stride=SUBLANES)


def _store_row_tiles(ref, n_rows, packed):
    for c in range(ROW_CHUNKS):
        ref[_row_chunk(n_rows, c), :] = packed[:, c * LANES:(c + 1) * LANES]


def _attn_proj_kernel(b_ref, g1_ref, m1_ref, wpb_ref, o_ref, zero_ref):
    mb = jnp.dot(b_ref[...], wpb_ref[...], preferred_element_type=F32)
    o_ref[...] = (m1_ref[...].astype(F32) + g1_ref[...].astype(F32) * mb).astype(BF16)
    zero_ref[...] = jnp.zeros_like(zero_ref)


def _attn_proj(b_attn, pg, m1, wpb, n_zero_rows):
    T = m1.shape[0]
    steps = T // MG_TM
    assert n_zero_rows % steps == 0
    zero_block = (n_zero_rows // steps * SUBLANES, LANES)
    row = pl.BlockSpec((MG_TM, D_MODEL), lambda i: (i, 0))
    return pl.pallas_call(
        _attn_proj_kernel,
        grid=(steps,),
        in_specs=[
            row,
            pl.BlockSpec((MG_TM, D_MODEL), lambda i: (i, PROJ_COLS // D_MODEL + 1)),
            row,
            _const_spec((MLA_HEADS * V_DIM, D_MODEL)),
        ],
        out_specs=[row, pl.BlockSpec(zero_block, lambda i: (i, 0))],
        out_shape=[jax.ShapeDtypeStruct((T, D_MODEL), BF16),
                   jax.ShapeDtypeStruct((n_zero_rows * SUBLANES, LANES), jnp.uint32)],
        compiler_params=_params(("parallel",), 56),
        name="attn_proj",
    )(b_attn, pg, m1, wpb)


def _merge_kernel(mg_ref, x_ref, wo_ref, ln2_ref, rw_ref, rb_ref,
                  x1_ref, h2p_ref, code_ref, gate_ref, cnt_ref, cnt_scr):
    @pl.when(pl.program_id(0) == 0)
    def _():
        cnt_scr[...] = jnp.zeros_like(cnt_scr)

    x1 = x_ref[...] + jnp.dot(mg_ref[...], wo_ref[...], preferred_element_type=F32)
    x1_ref[...] = x1
    ms = jnp.mean(x1 * x1, axis=-1, keepdims=True)
    h2 = x1 * lax.rsqrt(ms + EPS) * ln2_ref[...]
    _store_row_tiles(h2p_ref, MG_TM, _pack_bf16_pair(h2[:, :D_MODEL // 2], h2[:, D_MODEL // 2:]))

    logits = jnp.dot(h2.astype(BF16), rw_ref[...], preferred_element_type=F32)
    lane = lax.broadcasted_iota(jnp.int32, (MG_TM, LANES), 1)
    work = jnp.where(lane < N_EXPERTS, logits + rb_ref[...], NEG_INF)
    earlier = (lax.broadcasted_iota(jnp.int32, (MG_TM, MG_TM), 1)
               < lax.broadcasted_iota(jnp.int32, (MG_TM, MG_TM), 0)).astype(BF16)
    base = cnt_scr[...]
    code_out = jnp.zeros((MG_TM, LANES), jnp.int32)
    val_out = jnp.zeros((MG_TM, LANES), F32)
    top = None
    denom = jnp.zeros((MG_TM, 1), F32)
    for k in range(TOP_K):
        mk = jnp.max(work, axis=-1, keepdims=True)
        ik = jnp.min(jnp.where(work == mk, lane, LANES), axis=-1, keepdims=True)
        hit = lane == ik
        work = jnp.where(hit, NEG_INF, work)
        if top is None:
            top = mk
        ek = jnp.exp(mk - top)
        denom = denom + ek
        val_out = jnp.where(lane == k, ek, val_out)
        onehot = hit.astype(BF16)
        prefix = jnp.dot(earlier, onehot, preferred_element_type=F32) + base
        rank = jnp.sum(jnp.where(hit, prefix, 0.0), axis=-1, keepdims=True).astype(jnp.int32)
        code_out = jnp.where(lane == k, rank * N_EXPERTS + ik, code_out)
        base = base + jnp.sum(onehot.astype(F32), axis=0, keepdims=True)
    code_ref[...] = code_out
    gate_ref[...] = val_out / denom
    cnt_scr[...] = base
    cnt_ref[...] = base


def _merge(merged, x2, wo, ln2, rw, rb):
    T = x2.shape[0]
    row = lambda w: pl.BlockSpec((MG_TM, w), lambda i: (i, 0))
    return pl.pallas_call(
        _merge_kernel,
        grid=(T // MG_TM,),
        in_specs=[
            row(D_MODEL),
            row(D_MODEL),
            _const_spec((D_MODEL, D_MODEL)),
            _const_spec((1, D_MODEL)),
            _const_spec((D_MODEL, LANES)),
            _const_spec((1, LANES)),
        ],
        out_specs=[row(D_MODEL), pl.BlockSpec((MG_TM * SUBLANES, LANES), lambda i: (i, 0)), row(LANES),
                   row(LANES), pl.BlockSpec((1, LANES), lambda i: (0, 0))],
        out_shape=[
            jax.ShapeDtypeStruct((T, D_MODEL), F32),
            jax.ShapeDtypeStruct((T * SUBLANES, LANES), jnp.uint32),
            jax.ShapeDtypeStruct((T, LANES), jnp.int32),
            jax.ShapeDtypeStruct((T, LANES), F32),
            jax.ShapeDtypeStruct((1, LANES), F32),
        ],
        scratch_shapes=[pltpu.VMEM((1, LANES), F32)],
        compiler_params=_params(("arbitrary",), 56),
        name="merge_router",
    )(merged, x2, wo, ln2, rw, rb)


DP_TM = 512
DMA_UNROLL = 8


def _row_tile(row):
    return pl.ds(pl.multiple_of(row * SUBLANES, SUBLANES), SUBLANES)


def _dispatch_kernel(dest_ref, h2p_ref, xs_in, xs_hbm, sem):
    del xs_in

    def issue(t, carry):
        src = h2p_ref.at[_row_tile(t), :]
        for k in range(TOP_K):
            row = dest_ref[0, t * TOP_K + k]
            pltpu.make_async_copy(src, xs_hbm.at[_row_tile(row), :], sem).start(priority=k % 2)
        return carry

    lax.fori_loop(0, DP_TM, issue, 0, unroll=DMA_UNROLL // TOP_K)
    all_rows = xs_hbm.at[pl.ds(0, DP_TM * TOP_K * SUBLANES), :]
    pltpu.make_async_copy(all_rows, all_rows, sem).wait()


def _dispatch(dest3, h2p, xs_zero):
    T = h2p.shape[0] // SUBLANES
    return pl.pallas_call(
        _dispatch_kernel,
        grid=(T // DP_TM,),
        in_specs=[
            pl.BlockSpec((None, 1, DP_TM * TOP_K), lambda i: (i, 0, 0), memory_space=pltpu.SMEM),
            pl.BlockSpec((DP_TM * SUBLANES, LANES), lambda i: (i, 0)),
            pl.BlockSpec(memory_space=pl.ANY),
        ],
        out_specs=pl.BlockSpec(memory_space=pl.ANY),
        out_shape=jax.ShapeDtypeStruct(xs_zero.shape, xs_zero.dtype),
        scratch_shapes=[pltpu.SemaphoreType.DMA(())],
        input_output_aliases={2: 0},
        compiler_params=_params(("arbitrary",), 32),
        name="moe_dispatch",
    )(dest3, h2p, xs_zero)


GROUP_SUB = 4
GROUP_ROWS = GROUP_SUB * ROW_BLOCK
FF_TILE = 512
N_FF_TILES = D_FF // FF_TILE
DOWN_CHUNK = 512


def _expert_kernel(ge_ref, gs_ref, gn_ref, xs_in, wg_ref, wu_ref, bg_ref, bu_ref, wd_ref, bd_ref,
                   rows_hbm, xwin, x_scr, acc_scr, stage, wsem, sem):
    del ge_ref, xs_in
    g = pl.program_id(0)
    f = pl.program_id(1)
    ns = gn_ref[g]
    half = D_MODEL // 2
    slot = g & 1

    def win_copy(group, s):
        rows = pl.ds(gs_ref[group] * (ROW_BLOCK * SUBLANES), GROUP_ROWS * SUBLANES)
        return pltpu.make_async_copy(rows_hbm.at[rows, :], xwin.at[s], wsem.at[s])

    def out_copy(s, start_block):
        rows = pl.ds((start_block + s) * (ROW_BLOCK * SUBLANES), ROW_BLOCK * SUBLANES)
        return pltpu.make_async_copy(stage.at[s], rows_hbm.at[rows, :], sem.at[s])

    @pl.when((f == 0) & (ns > 0))
    def _():
        @pl.when(g == 0)
        def _():
            win_copy(0, 0).start()
            acc_scr[...] = jnp.zeros_like(acc_scr)

        win_copy(g, slot).wait()
        @pl.when(g + 1 < pl.num_programs(0))
        def _():
            @pl.when(gn_ref[g + 1] > 0)
            def _():
                win_copy(g + 1, 1 - slot).start()

        for c in range(ROW_CHUNKS):
            hi, lo = _unpack_bf16_pair(xwin[slot, _row_chunk(GROUP_ROWS, c), :])
            x_scr[:, c * LANES:(c + 1) * LANES] = hi.astype(BF16)
            x_scr[:, half + c * LANES:half + (c + 1) * LANES] = lo.astype(BF16)

    for n in range(1, GROUP_SUB + 1):
        @pl.when(ns == n)
        def _(n=n):
            m = n * ROW_BLOCK
            x = x_scr[:m, :]
            gate = jnp.dot(x, wg_ref[...].astype(BF16), preferred_element_type=F32) + bg_ref[...]
            up = jnp.dot(x, wu_ref[...].astype(BF16), preferred_element_type=F32) + bu_ref[...]
            gate = jnp.minimum(gate, SWIGLU_LIMIT)
            up = jnp.clip(up, -SWIGLU_LIMIT, SWIGLU_LIMIT)
            glu = gate * jax.nn.sigmoid(gate * SWIGLU_ALPHA)
            act = ((up + 1.0) * glu).astype(BF16)
            for c in range(D_MODEL // DOWN_CHUNK):
                cols = slice(c * DOWN_CHUNK, (c + 1) * DOWN_CHUNK)
                start = jnp.where(f == 0, jnp.broadcast_to(bd_ref[:, cols], (m, DOWN_CHUNK)), acc_scr[:m, cols])
                acc_scr[:m, cols] = start + jnp.dot(act, wd_ref[:, cols].astype(BF16),
                                                    preferred_element_type=F32)

    @pl.when(f == N_FF_TILES - 1)
    def _():
        @pl.when(g > 0)
        def _():
            prev = gn_ref[g - 1]
            for s in range(GROUP_SUB):
                @pl.when(s < prev)
                def _(s=s):
                    out_copy(s, 0).wait()

        for s in range(GROUP_SUB):
            @pl.when(s < ns)
            def _(s=s):
                y = acc_scr[s * ROW_BLOCK:(s + 1) * ROW_BLOCK, :]
                _store_row_tiles(stage.at[s], ROW_BLOCK, _pack_bf16_pair(y[:, :half], y[:, half:]))
                out_copy(s, gs_ref[g]).start()

        @pl.when(g == pl.num_programs(0) - 1)
        def _():
            for s in range(GROUP_SUB):
                @pl.when(s < ns)
                def _(s=s):
                    out_copy(s, 0).wait()


def _experts(g_exp, g_start, g_nsub, xs, wgu, bgu, wd, bd):
    def ff(f, gn, g):
        return jnp.where(gn[g] > 0, f, N_FF_TILES - 1)

    grid_spec = pltpu.PrefetchScalarGridSpec(
        num_scalar_prefetch=3,
        grid=(g_exp.shape[0], N_FF_TILES),
        in_specs=[
            pl.BlockSpec(memory_space=pl.ANY),
            pl.BlockSpec((None, D_MODEL, FF_TILE), lambda g, f, ge, gs, gn: (ge[g], 0, ff(f, gn, g))),
            pl.BlockSpec((None, D_MODEL, FF_TILE),
                         lambda g, f, ge, gs, gn: (ge[g], 0, N_FF_TILES + ff(f, gn, g))),
            pl.BlockSpec((None, 1, FF_TILE), lambda g, f, ge, gs, gn: (ge[g], 0, ff(f, gn, g))),
            pl.BlockSpec((None, 1, FF_TILE), lambda g, f, ge, gs, gn: (ge[g], 0, N_FF_TILES + ff(f, gn, g))),
            pl.BlockSpec((None, FF_TILE, D_MODEL), lambda g, f, ge, gs, gn: (ge[g], ff(f, gn, g), 0)),
            pl.BlockSpec((None, 1, D_MODEL), lambda g, f, ge, gs, gn: (ge[g], 0, 0)),
        ],
        out_specs=pl.BlockSpec(memory_space=pl.ANY),
        scratch_shapes=[
            pltpu.VMEM((2, GROUP_ROWS * SUBLANES, LANES), jnp.uint32),
            pltpu.VMEM((GROUP_ROWS, D_MODEL), BF16),
            pltpu.VMEM((GROUP_ROWS, D_MODEL), F32),
            pltpu.VMEM((GROUP_SUB, ROW_BLOCK * SUBLANES, LANES), jnp.uint32),
            pltpu.SemaphoreType.DMA((2,)),
            pltpu.SemaphoreType.DMA((GROUP_SUB,)),
        ],
    )
    return pl.pallas_call(
        _expert_kernel,
        grid_spec=grid_spec,
        out_shape=jax.ShapeDtypeStruct(xs.shape, jnp.uint32),
        input_output_aliases={3: 0},
        compiler_params=_params(("arbitrary", "arbitrary"), 58),
        name="moe_experts",
    )(g_exp, g_start, g_nsub, xs, wgu, wgu, bgu, bgu, wd, bd)


CB_TM = 256


def _combine_kernel(dest_ref, dest_next_ref, ys_hbm, x1_ref, gate_ref, o_ref, buf, sem):
    i = pl.program_id(0)
    slot = i & 1

    def gather_tile(dest, s):
        def issue(t, carry):
            for k in range(TOP_K):
                row = dest[0, t * TOP_K + k]
                pltpu.make_async_copy(ys_hbm.at[_row_tile(row), :], buf.at[s, k, _row_tile(t), :],
                                      sem.at[s]).start(priority=k % 2)
            return carry

        lax.fori_loop(0, CB_TM, issue, 0, unroll=DMA_UNROLL // TOP_K)

    @pl.when(i == 0)
    def _():
        gather_tile(dest_ref, 0)

    @pl.when(i + 1 < pl.num_programs(0))
    def _():
        gather_tile(dest_next_ref, 1 - slot)

    pltpu.make_async_copy(buf.at[slot], buf.at[slot], sem.at[slot]).wait()

    half = D_MODEL // 2
    gates = gate_ref[...]
    gate_k = [jnp.broadcast_to(gates[:, k:k + 1], (CB_TM, LANES)) for k in range(TOP_K)]
    for c in range(ROW_CHUNKS):
        cols_hi = slice(c * LANES, (c + 1) * LANES)
        cols_lo = slice(half + c * LANES, half + (c + 1) * LANES)
        out_hi = x1_ref[:, cols_hi]
        out_lo = x1_ref[:, cols_lo]
        for k in range(TOP_K):
            hi, lo = _unpack_bf16_pair(buf[slot, k, _row_chunk(CB_TM, c), :])
            out_hi = out_hi + gate_k[k] * hi
            out_lo = out_lo + gate_k[k] * lo
        o_ref[:, cols_hi] = out_hi
        o_ref[:, cols_lo] = out_lo


def _combine(dest3, ys, x1, gates):
    T = x1.shape[0]
    n = T // CB_TM
    dest_spec = lambda f: pl.BlockSpec((None, 1, CB_TM * TOP_K), f, memory_space=pltpu.SMEM)
    return pl.pallas_call(
        _combine_kernel,
        grid=(n,),
        in_specs=[
            dest_spec(lambda i: (i, 0, 0)),
            dest_spec(lambda i: (jnp.minimum(i + 1, n - 1), 0, 0)),
            pl.BlockSpec(memory_space=pl.ANY),
            pl.BlockSpec((CB_TM, D_MODEL), lambda i: (i, 0)),
            pl.BlockSpec((CB_TM, LANES), lambda i: (i, 0)),
        ],
        out_specs=pl.BlockSpec((CB_TM, D_MODEL), lambda i: (i, 0)),
        out_shape=jax.ShapeDtypeStruct((T, D_MODEL), F32),
        scratch_shapes=[pltpu.VMEM((2, TOP_K, CB_TM * SUBLANES, LANES), jnp.uint32),
                        pltpu.SemaphoreType.DMA((2,))],
        compiler_params=_params(("arbitrary",), 32),
        name="moe_combine",
    )(dest3, dest3, ys, x1, gates)


def _group_tables(counts_f32, max_groups):
    i32 = jnp.int32
    counts = counts_f32[0, :N_EXPERTS].astype(i32)
    nb = (counts + ROW_BLOCK - 1) // ROW_BLOCK
    ng = (nb + GROUP_SUB - 1) // GROUP_SUB
    upto = jnp.arange(N_EXPERTS)[None, :] <= jnp.arange(N_EXPERTS)[:, None]
    pad_ends = jnp.sum(jnp.where(upto, nb[None, :], 0), axis=1) * ROW_BLOCK
    pad_starts = pad_ends - nb * ROW_BLOCK
    g_ends = jnp.sum(jnp.where(upto, ng[None, :], 0), axis=1)
    g_starts = g_ends - ng
    n_groups = g_ends[-1]

    g = jnp.arange(max_groups, dtype=i32)
    gg = jnp.minimum(g, n_groups - 1)
    e = jnp.minimum(jnp.sum((g_ends[None, :] <= gg[:, None]).astype(i32), axis=1), N_EXPERTS - 1)
    pick = e[:, None] == jnp.arange(N_EXPERTS)[None, :]
    take = lambda table: jnp.sum(jnp.where(pick, table[None, :], 0), axis=1)
    nb_g, ng_g = take(nb), jnp.maximum(take(ng), 1)
    j = gg - take(g_starts)
    base, rem = nb_g // ng_g, nb_g % ng_g
    nsub = jnp.where(g < n_groups, base + (j < rem).astype(i32), 0)
    start_block = take(pad_starts) // ROW_BLOCK + j * base + jnp.minimum(j, rem)
    return pad_starts.astype(i32), e.astype(i32), start_block.astype(i32), nsub.astype(i32)


def _rope_constants():
    lane = np.arange(LANES)
    half = QK_ROPE // 2
    inv = 1.0 / (ROPE_THETA ** (np.arange(0, QK_ROPE, 2, dtype=np.float32) / QK_ROPE))
    cst = np.zeros((8, LANES), np.float32)
    cst[0, :QK_ROPE] = inv.astype(np.float32)[lane[:QK_ROPE] % half]
    cst[1, :QK_ROPE] = 1.0
    cst[2, :half] = -1.0
    cst[2, half:QK_ROPE] = 1.0
    return jnp.asarray(cst)


def _swap_halves(a):
    half = QK_ROPE // 2
    return jnp.concatenate([a[..., half:], a[..., :half]], axis=-1)


def kernel(x, mem, positions, ln1_g, w_in, w_gate, b_gate, gmlp_ln_g, gmlp_ln_b, gmlp_ws, gmlp_bs, w_pa,
           mla_cq_g, mla_w_uq, mla_ckv_g, mla_w_ukv, mla_qn_g, mla_kn_g, w_pb, mem_ln_g, mem_w_k, mem_w_v,
           mem_qn_g, mem_kn_g, w_pc, w_o, ln2_g, router_w, router_b, moe_w_gu, moe_b_gu, moe_w_down,
           moe_b_down):
    B, S, D = x.shape
    T = B * S
    x2 = x.reshape(T, D)
    for l in range(ln1_g.shape[0]):
        o_kr = 2 * GM_WIDTH + Q_LORA + KV_LORA
        o_qm = o_kr + QK_ROPE
        wi = w_in[l]
        w1 = jnp.concatenate([wi[:, :o_kr], wi[:, o_qm:], w_gate[l]], axis=1).astype(BF16)
        b1 = jnp.concatenate([jnp.zeros((PROJ_COLS,), F32), b_gate[l]])[None, :]
        w_kr = wi[:, o_kr:o_qm]
        wr = jnp.concatenate([w_kr, _swap_halves(w_kr)], axis=1).astype(BF16)

        wq3 = mla_w_uq[l].reshape(Q_LORA, MLA_HEADS, QK_DIM)
        wq = jnp.concatenate([wq3, _swap_halves(wq3[..., QK_NOPE:])], axis=-1)
        wq = wq.reshape(Q_LORA, MLA_HEADS * QK_PAD).astype(BF16)
        wkv = mla_w_ukv[l].astype(BF16)
        gq = jnp.concatenate([mla_qn_g[l], _swap_halves(mla_qn_g[l][QK_NOPE:])])[None, :]
        gk = jnp.concatenate([mla_kn_g[l], _swap_halves(mla_kn_g[l][QK_NOPE:])])[None, :]

        bias_full = jnp.broadcast_to(gmlp_bs[l].T[:, :, None], (GM_CHUNK, GM_GROUPS, GM_CHUNK))
        bias_full = bias_full.reshape(GM_CHUNK, GM_WIDTH)

        rw = jnp.pad(router_w[l], ((0, 0), (0, LANES - N_EXPERTS))).astype(BF16)
        rb = jnp.pad(router_b[l], (0, LANES - N_EXPERTS))[None, :]

        pg, kr = _norm_proj(x2, ln1_g[l][None, :], w1, b1, wr)
        m0 = _gmlp(pg, gmlp_ln_g[l][None, :], gmlp_ln_b[l][None, :], gmlp_ws[l].astype(BF16), bias_full,
                   w_pa[l].astype(BF16))
        q, k, vt = _mla_qkv(pg, kr, positions.reshape(T, 1), mla_cq_g[l][None, :], mla_ckv_g[l][None, :],
                           wq, wkv, gq, gk, _rope_constants(), B, S)
        b_attn = _flash(q, k, vt)
        km, vm = _mem_kv(mem, mem_ln_g[l][None, :], mem_w_k[l].astype(BF16), mem_w_v[l].astype(BF16),
                         mem_kn_g[l][None, :])
        m1 = _mem_attn(pg, km, vm, m0, mem_qn_g[l][None, :], w_pc[l].astype(BF16), S)
        n_rows = T * TOP_K + N_EXPERTS * ROW_BLOCK
        merged, xs_zero = _attn_proj(b_attn, pg, m1, w_pb[l].astype(BF16), n_rows + GROUP_ROWS)
        x1, h2p, code, gates, counts = _merge(merged, x2, w_o[l].astype(BF16), ln2_g[l][None, :], rw, rb)

        max_groups = n_rows // GROUP_ROWS + N_EXPERTS
        pad_starts, g_exp, g_start, g_nsub = _group_tables(counts, max_groups)
        codes = code[:, :TOP_K]
        pick = (codes & (N_EXPERTS - 1))[..., None] == jnp.arange(N_EXPERTS)
        dest = (codes >> EXPERT_BITS) + jnp.sum(jnp.where(pick, pad_starts, 0), axis=-1)
        xs = _dispatch(dest.reshape(T // DP_TM, 1, DP_TM * TOP_K), h2p, xs_zero)
        ys = _experts(g_exp, g_start, g_nsub, xs, moe_w_gu[l], moe_b_gu[l][:, None, :],
                      moe_w_down[l], moe_b_down[l][:, None, :])
        x2 = _combine(dest.reshape(T // CB_TM, 1, CB_TM * TOP_K), ys, x1, gates)
    return x2.reshape(B, S, D)
```

```python
import functools

import numpy as np
import jax
import jax.numpy as jnp
from jax import lax
from jax.experimental import pallas as pl
from jax.experimental.pallas import tpu as pltpu

F32 = jnp.float32
BF16 = jnp.bfloat16

D_MODEL = 2048
GM_WIDTH = 1024
GM_GROUPS = 8
GM_CHUNK = 128
MLA_HEADS = 16
Q_LORA = 512
KV_LORA = 512
QK_NOPE = 128
QK_ROPE = 64
V_DIM = 128
QK_DIM = QK_NOPE + QK_ROPE
QK_PAD = 256
MLA_SCALE = QK_DIM ** -0.5
LOG2_E = 1.4426950408889634
ROPE_THETA = 10000.0
MEM_LEN = 256
MEM_HEADS = 4
MEM_HEAD_DIM = 256
MEM_WIDTH = MEM_HEADS * MEM_HEAD_DIM
MEM_SCALE = MEM_HEAD_DIM ** -0.5
N_EXPERTS = 32
EXPERT_BITS = 5
TOP_K = 4
D_FF = 2048
SWIGLU_LIMIT = 7.0
SWIGLU_ALPHA = 1.702
ROW_BLOCK = 256
EPS = 1e-6
LANES = 128
NEG_INF = float("-inf")

PROJ_COLS = 2 * GM_WIDTH + Q_LORA + KV_LORA + MEM_WIDTH
PG_COLS = PROJ_COLS + 3 * D_MODEL

MIB = 1024 * 1024


def _params(semantics, vmem_mib):
    return pltpu.CompilerParams(dimension_semantics=semantics, vmem_limit_bytes=vmem_mib * MIB)


def _const_spec(shape):
    nd = len(shape)
    return pl.BlockSpec(shape, lambda *_: (0,) * nd, pipeline_mode=pl.Buffered(1))


P1_TM = 1024
P1_TN = 2048


def _norm_proj_kernel(x_ref, g_ref, w_ref, b_ref, wr_ref, o_ref, kr_ref, h_scr, *, n_plain):
    j = pl.program_id(1)

    @pl.when(j == 0)
    def _():
        def body(c, carry):
            rows = pl.ds(pl.multiple_of(c * 128, 128), 128)
            x = x_ref[rows, :]
            ms = jnp.mean(x * x, axis=-1, keepdims=True)
            h_scr[rows, :] = (x * lax.rsqrt(ms + EPS) * g_ref[...]).astype(BF16)
            return carry

        lax.fori_loop(0, P1_TM // 128, body, 0)
        kr_ref[...] = jnp.dot(h_scr[...], wr_ref[...], preferred_element_type=F32)

    acc = jnp.dot(h_scr[...], w_ref[...], preferred_element_type=F32)
    gated = jax.nn.sigmoid(acc + b_ref[...])
    o_ref[...] = jnp.where(j >= n_plain, gated, acc).astype(BF16)


def _norm_proj(x2, ln1_g, w1, b1, wr):
    T = x2.shape[0]
    grid = (T // P1_TM, PG_COLS // P1_TN)
    return pl.pallas_call(
        functools.partial(_norm_proj_kernel, n_plain=PROJ_COLS // P1_TN),
        grid=grid,
        in_specs=[
            pl.BlockSpec((P1_TM, D_MODEL), lambda i, j: (i, 0)),
            pl.BlockSpec((1, D_MODEL), lambda i, j: (0, 0)),
            pl.BlockSpec((D_MODEL, P1_TN), lambda i, j: (0, j)),
            pl.BlockSpec((1, P1_TN), lambda i, j: (0, j)),
            pl.BlockSpec((D_MODEL, LANES), lambda i, j: (0, 0)),
        ],
        out_specs=[
            pl.BlockSpec((P1_TM, P1_TN), lambda i, j: (i, j)),
            pl.BlockSpec((P1_TM, LANES), lambda i, j: (i, 0)),
        ],
        out_shape=[
            jax.ShapeDtypeStruct((T, PG_COLS), BF16),
            jax.ShapeDtypeStruct((T, LANES), F32),
        ],
        scratch_shapes=[pltpu.VMEM((P1_TM, D_MODEL), BF16)],
        compiler_params=_params(("parallel", "arbitrary"), 58),
        name="norm_proj",
    )(x2, ln1_g, w1, b1, wr)


GM_TM = 512


def _gmlp_kernel(u_ref, v_ref, g0_ref, lng_ref, lnb_ref, ws_ref, bias_ref, wpa_ref, o_ref, vb_scr, a_scr):
    v = v_ref[...].astype(F32)
    mu = jnp.mean(v, axis=-1, keepdims=True)
    c = v - mu
    var = jnp.mean(c * c, axis=-1, keepdims=True)
    vb_scr[...] = (c * lax.rsqrt(var + EPS) * lng_ref[...] + lnb_ref[...]).astype(BF16)

    row = lax.broadcasted_iota(jnp.int32, (GM_CHUNK, GM_CHUNK), 0)
    col = lax.broadcasted_iota(jnp.int32, (GM_CHUNK, GM_CHUNK), 1)
    causal = col <= row
    for g in range(GM_GROUPS):
        cols = slice(g * GM_CHUNK, (g + 1) * GM_CHUNK)
        wg = jnp.where(causal, ws_ref[g], jnp.zeros((), BF16))
        chunks = [slice(ch * GM_CHUNK, (ch + 1) * GM_CHUNK) for ch in range(GM_TM // GM_CHUNK)]
        mixed = jnp.dot(wg, jnp.concatenate([vb_scr[rows, cols] for rows in chunks], axis=1),
                        preferred_element_type=F32)
        for ch, rows in enumerate(chunks):
            mix = mixed[:, ch * GM_CHUNK:(ch + 1) * GM_CHUNK] + bias_ref[:, cols]
            a_scr[rows, cols] = (u_ref[rows, cols].astype(F32) * mix).astype(BF16)

    ma = jnp.dot(a_scr[...], wpa_ref[...], preferred_element_type=F32)
    o_ref[...] = (g0_ref[...].astype(F32) * ma).astype(BF16)


def _gmlp(pg, ln_g, ln_b, ws, bias_full, wpa):
    T = pg.shape[0]
    return pl.pallas_call(
        _gmlp_kernel,
        grid=(T // GM_TM,),
        in_specs=[
            pl.BlockSpec((GM_TM, GM_WIDTH), lambda i: (i, 0)),
            pl.BlockSpec((GM_TM, GM_WIDTH), lambda i: (i, 1)),
            pl.BlockSpec((GM_TM, D_MODEL), lambda i: (i, PROJ_COLS // D_MODEL)),
            _const_spec((1, GM_WIDTH)),
            _const_spec((1, GM_WIDTH)),
            _const_spec((GM_GROUPS, GM_CHUNK, GM_CHUNK)),
            _const_spec((GM_CHUNK, GM_WIDTH)),
            _const_spec((GM_WIDTH, D_MODEL)),
        ],
        out_specs=pl.BlockSpec((GM_TM, D_MODEL), lambda i: (i, 0)),
        out_shape=jax.ShapeDtypeStruct((T, D_MODEL), BF16),
        scratch_shapes=[pltpu.VMEM((GM_TM, GM_WIDTH), BF16), pltpu.VMEM((GM_TM, GM_WIDTH), BF16)],
        compiler_params=_params(("parallel",), 40),
        name="gmlp",
    )(pg, pg, pg, ln_g, ln_b, ws, bias_full, wpa)


QKV_TM = 1024
QKV_HEADS = 2
HEAD_BLK = 4
FA_T = 512
FA_QS = 256
V_AUG = V_DIM + 16


def _rope(t, cos, sin):
    return t * cos + pltpu.roll(t, QK_ROPE, 1) * sin


def _row_sum_all_lanes(sq, weights):
    return jnp.dot(sq.astype(BF16), weights.astype(BF16), preferred_element_type=F32)


def _mla_qkv_kernel(cq_ref, ckv_ref, kr_ref, pos_ref, gcq_ref, gckv_ref, wq_ref, wkv_ref, gq_ref, gk_ref,
                    cst_ref, sumw_ref, q_ref, k_ref, vt_ref, cqn_scr, ckvn_scr, cos_scr, sin_scr, krsq_scr):
    hb = pl.program_id(1)
    w_nope = sumw_ref[:QK_NOPE, :]
    w_rope = sumw_ref[QK_NOPE:, :]

    @pl.when(hb == 0)
    def _():
        cq = cq_ref[...].astype(F32)
        ms = jnp.mean(cq * cq, axis=-1, keepdims=True)
        cqn_scr[...] = (cq * lax.rsqrt(ms + EPS) * gcq_ref[...]).astype(BF16)
        ckv = ckv_ref[...].astype(F32)
        ms = jnp.mean(ckv * ckv, axis=-1, keepdims=True)
        ckvn_scr[...] = (ckv * lax.rsqrt(ms + EPS) * gckv_ref[...]).astype(BF16)
        ang = pos_ref[...].astype(F32) * cst_ref[0:1, :]
        cos_scr[...] = jnp.cos(ang) * cst_ref[1:2, :]
        sin_scr[...] = jnp.sin(ang) * cst_ref[2:3, :]
        kr = kr_ref[...]
        krsq_scr[...] = _row_sum_all_lanes(kr * kr, w_rope)

    cos = cos_scr[...]
    sin = sin_scr[...]
    inv_dim = 1.0 / QK_DIM
    kr = kr_ref[...]

    yq2 = jnp.dot(cqn_scr[...], wq_ref[...], preferred_element_type=F32)
    ykv2 = jnp.dot(ckvn_scr[...], wkv_ref[...], preferred_element_type=F32)
    for hh in range(QKV_HEADS):
        yq = yq2[:, hh * QK_PAD:(hh + 1) * QK_PAD]
        qn = yq[:, :QK_NOPE]
        qt = yq[:, QK_NOPE:]
        ssq = _row_sum_all_lanes(qn * qn, w_nope) + _row_sum_all_lanes(qt * qt, w_rope)
        rs = lax.rsqrt(ssq * inv_dim + EPS) * (MLA_SCALE * LOG2_E)
        q_ref[hh, :, :QK_NOPE] = (qn * rs * gq_ref[:, :QK_NOPE]).astype(BF16)
        q_ref[hh, :, QK_NOPE:] = _rope(qt * rs * gq_ref[:, QK_NOPE:], cos, sin).astype(BF16)

        ykv = ykv2[:, hh * (QK_NOPE + V_DIM):(hh + 1) * (QK_NOPE + V_DIM)]
        kn = ykv[:, :QK_NOPE]
        ssq = _row_sum_all_lanes(kn * kn, w_nope) + krsq_scr[...]
        rs = lax.rsqrt(ssq * inv_dim + EPS)
        k_ref[hh, :, :QK_NOPE] = (kn * rs * gk_ref[:, :QK_NOPE]).astype(BF16)
        k_ref[hh, :, QK_NOPE:] = _rope(kr * rs * gk_ref[:, QK_NOPE:], cos, sin).astype(BF16)
        vv = ykv[:, QK_NOPE:]
        for c in range(QKV_TM // FA_T):
            vt_ref[hh, c, :V_DIM, :] = vv[c * FA_T:(c + 1) * FA_T, :].T.astype(BF16)
            extra = lax.broadcasted_iota(jnp.int32, (V_AUG - V_DIM, FA_T), 0)
            vt_ref[hh, c, V_DIM:, :] = (extra == 0).astype(BF16)


def _mla_qkv(pg, kr, pos, gcq, gckv, wq, wkv, gq, gk, cst, B, S):
    T = pg.shape[0]
    sumw = jnp.concatenate([jnp.ones((QK_NOPE, LANES), F32), jnp.full((QK_PAD - QK_NOPE, LANES), 0.5, F32)])
    spb = S // QKV_TM
    cpt = QKV_TM // FA_T
    head_spec = lambda w: pl.BlockSpec((None, QKV_HEADS, QKV_TM, w), lambda i, h: (i // spb, h, i % spb, 0))
    return pl.pallas_call(
        _mla_qkv_kernel,
        grid=(T // QKV_TM, MLA_HEADS // QKV_HEADS),
        in_specs=[
            pl.BlockSpec((QKV_TM, Q_LORA), lambda i, h: (i, 2 * GM_WIDTH // Q_LORA)),
            pl.BlockSpec((QKV_TM, KV_LORA), lambda i, h: (i, 2 * GM_WIDTH // KV_LORA + 1)),
            pl.BlockSpec((QKV_TM, LANES), lambda i, h: (i, 0)),
            pl.BlockSpec((QKV_TM, 1), lambda i, h: (i, 0)),
            pl.BlockSpec((1, Q_LORA), lambda i, h: (0, 0)),
            pl.BlockSpec((1, KV_LORA), lambda i, h: (0, 0)),
            pl.BlockSpec((Q_LORA, QKV_HEADS * QK_PAD), lambda i, h: (0, h)),
            pl.BlockSpec((KV_LORA, QKV_HEADS * (QK_NOPE + V_DIM)), lambda i, h: (0, h)),
            pl.BlockSpec((1, QK_PAD), lambda i, h: (0, 0)),
            pl.BlockSpec((1, QK_PAD), lambda i, h: (0, 0)),
            pl.BlockSpec((8, LANES), lambda i, h: (0, 0)),
            pl.BlockSpec((QK_PAD, LANES), lambda i, h: (0, 0)),
        ],
        out_specs=[
            head_spec(QK_PAD),
            head_spec(QK_PAD),
            pl.BlockSpec((None, QKV_HEADS, cpt, V_AUG, FA_T), lambda i, h: (i // spb, h, i % spb, 0, 0)),
        ],
        out_shape=[
            jax.ShapeDtypeStruct((B, MLA_HEADS, S, QK_PAD), BF16),
            jax.ShapeDtypeStruct((B, MLA_HEADS, S, QK_PAD), BF16),
            jax.ShapeDtypeStruct((B, MLA_HEADS, S // FA_T, V_AUG, FA_T), BF16),
        ],
        scratch_shapes=[
            pltpu.VMEM((QKV_TM, Q_LORA), BF16),
            pltpu.VMEM((QKV_TM, KV_LORA), BF16),
            pltpu.VMEM((QKV_TM, LANES), F32),
            pltpu.VMEM((QKV_TM, LANES), F32),
            pltpu.VMEM((QKV_TM, LANES), F32),
        ],
        compiler_params=_params(("parallel", "arbitrary"), 48),
        name="mla_qkv",
    )(pg, pg, kr, pos, gcq, gckv, wq, wkv, gq, gk, cst, sumw)


def _flash_kernel(q_ref, k_ref, vt_ref, o_ref, m_scr, acc_scr):
    qi = pl.program_id(2)
    m_scr[...] = jnp.full_like(m_scr, NEG_INF)
    acc_scr[...] = jnp.zeros_like(acc_scr)

    def step(kb, diagonal):
        ks = pl.ds(pl.multiple_of(kb * FA_T, FA_T), FA_T)
        chains = [(hh, qs) for hh in range(HEAD_BLK) for qs in range(FA_T // FA_QS)]
        scores = []
        for hh, qs in chains:
            qc = slice(qs * FA_QS, (qs + 1) * FA_QS)
            st = lax.dot_general(k_ref[hh, ks, :], q_ref[hh, qc, :], (((1,), (1,)), ((), ())),
                                 preferred_element_type=F32)
            if diagonal:
                krow = lax.broadcasted_iota(jnp.int32, (FA_T, FA_QS), 0)
                qcol = lax.broadcasted_iota(jnp.int32, (FA_T, FA_QS), 1) + qs * FA_QS
                st = jnp.where(krow <= qcol, st, NEG_INF)
            scores.append(st)
        probs = []
        for (hh, qs), st in zip(chains, scores):
            qc = slice(qs * FA_QS, (qs + 1) * FA_QS)
            m_prev = m_scr[hh, :, qc]
            m_new = jnp.maximum(m_prev, jnp.max(st, axis=0, keepdims=True))
            alpha = jnp.exp2(m_prev - m_new)
            p = jnp.exp2(st - m_new)
            m_scr[hh, :, qc] = m_new
            probs.append((alpha, p.astype(BF16)))
        for (hh, qs), (alpha, p) in zip(chains, probs):
            qc = slice(qs * FA_QS, (qs + 1) * FA_QS)
            acc_scr[hh, :, qc] = alpha * acc_scr[hh, :, qc] + jnp.dot(vt_ref[hh, kb], p,
                                                                      preferred_element_type=F32)

    def body(kb, carry):
        step(kb, False)
        return carry

    lax.fori_loop(0, qi, body, 0)
    step(qi, True)
    for hh in range(HEAD_BLK):
        out_t = acc_scr[hh, :V_DIM, :] / acc_scr[hh, V_DIM:V_DIM + 1, :]
        o_ref[:, hh * V_DIM:(hh + 1) * V_DIM] = out_t.T.astype(BF16)


def _flash(q, k, vt):
    B, H, S, _ = q.shape
    nq = S // FA_T
    return pl.pallas_call(
        _flash_kernel,
        grid=(B, H // HEAD_BLK, nq),
        in_specs=[
            pl.BlockSpec((None, HEAD_BLK, FA_T, QK_PAD), lambda b, h, i: (b, h, i, 0)),
            pl.BlockSpec((None, HEAD_BLK, S, QK_PAD), lambda b, h, i: (b, h, 0, 0)),
            pl.BlockSpec((None, HEAD_BLK, nq, V_AUG, FA_T), lambda b, h, i: (b, h, 0, 0, 0)),
        ],
        out_specs=pl.BlockSpec((FA_T, HEAD_BLK * V_DIM), lambda b, h, i: (b * nq + i, h)),
        out_shape=jax.ShapeDtypeStruct((B * S, H * V_DIM), BF16),
        scratch_shapes=[
            pltpu.VMEM((HEAD_BLK, 1, FA_T), F32),
            pltpu.VMEM((HEAD_BLK, V_AUG, FA_T), F32),
        ],
        compiler_params=_params(("parallel", "parallel", "arbitrary"), 40),
        name="mla_attention",
    )(q, k, vt)


def _mem_kv_kernel(mem_ref, g_ref, wk_ref, wv_ref, kng_ref, k_ref, v_ref):
    m = mem_ref[...]
    ms = jnp.mean(m * m, axis=-1, keepdims=True)
    mb = (m * lax.rsqrt(ms + EPS) * g_ref[...]).astype(BF16)
    kk = jnp.dot(mb, wk_ref[...], preferred_element_type=F32)
    for h in range(MEM_HEADS):
        cols = slice(h * MEM_HEAD_DIM, (h + 1) * MEM_HEAD_DIM)
        kh = kk[:, cols]
        ms = jnp.mean(kh * kh, axis=-1, keepdims=True)
        k_ref[:, cols] = (kh * lax.rsqrt(ms + EPS) * kng_ref[...]).astype(BF16)
    v_ref[...] = jnp.dot(mb, wv_ref[...], preferred_element_type=F32).astype(BF16)


def _mem_kv(mem, g, wk, wv, kng):
    B = mem.shape[0]
    return pl.pallas_call(
        _mem_kv_kernel,
        grid=(B,),
        in_specs=[
            pl.BlockSpec((None, MEM_LEN, D_MODEL), lambda b: (b, 0, 0)),
            _const_spec((1, D_MODEL)),
            _const_spec((D_MODEL, MEM_WIDTH)),
            _const_spec((D_MODEL, MEM_WIDTH)),
            _const_spec((1, MEM_HEAD_DIM)),
        ],
        out_specs=[
            pl.BlockSpec((None, MEM_LEN, MEM_WIDTH), lambda b: (b, 0, 0)),
            pl.BlockSpec((None, MEM_LEN, MEM_WIDTH), lambda b: (b, 0, 0)),
        ],
        out_shape=[
            jax.ShapeDtypeStruct((B, MEM_LEN, MEM_WIDTH), BF16),
            jax.ShapeDtypeStruct((B, MEM_LEN, MEM_WIDTH), BF16),
        ],
        compiler_params=_params(("parallel",), 32),
        name="mem_kv",
    )(mem, g, wk, wv, kng)


MA_TM = 512


def _mem_attn_kernel(qm_ref, km_ref, vm_ref, g2_ref, m0_ref, qng_ref, wpc_ref, o_ref, c_scr):
    for h in range(MEM_HEADS):
        cols = slice(h * MEM_HEAD_DIM, (h + 1) * MEM_HEAD_DIM)
        qh = qm_ref[:, cols].astype(F32)
        ms = jnp.mean(qh * qh, axis=-1, keepdims=True)
        qn = (qh * (lax.rsqrt(ms + EPS) * MEM_SCALE) * qng_ref[...]).astype(BF16)
        s = lax.dot_general(qn, km_ref[:, cols], (((1,), (1,)), ((), ())), preferred_element_type=F32)
        e = jnp.exp(s - jnp.max(s, axis=-1, keepdims=True))
        p = (e / jnp.sum(e, axis=-1, keepdims=True)).astype(BF16)
        c_scr[:, cols] = jnp.dot(p, vm_ref[:, cols], preferred_element_type=F32).astype(BF16)
    mc = jnp.dot(c_scr[...], wpc_ref[...], preferred_element_type=F32)
    o_ref[...] = (m0_ref[...].astype(F32) + g2_ref[...].astype(F32) * mc).astype(BF16)


def _mem_attn(pg, km, vm, m0, qng, wpc, S):
    T = pg.shape[0]
    spb = S // MA_TM
    return pl.pallas_call(
        _mem_attn_kernel,
        grid=(T // MA_TM,),
        in_specs=[
            pl.BlockSpec((MA_TM, MEM_WIDTH), lambda i: (i, (2 * GM_WIDTH + Q_LORA + KV_LORA) // MEM_WIDTH)),
            pl.BlockSpec((None, MEM_LEN, MEM_WIDTH), lambda i: (i // spb, 0, 0)),
            pl.BlockSpec((None, MEM_LEN, MEM_WIDTH), lambda i: (i // spb, 0, 0)),
            pl.BlockSpec((MA_TM, D_MODEL), lambda i: (i, PROJ_COLS // D_MODEL + 2)),
            pl.BlockSpec((MA_TM, D_MODEL), lambda i: (i, 0)),
            _const_spec((1, MEM_HEAD_DIM)),
            _const_spec((MEM_WIDTH, D_MODEL)),
        ],
        out_specs=pl.BlockSpec((MA_TM, D_MODEL), lambda i: (i, 0)),
        out_shape=jax.ShapeDtypeStruct((T, D_MODEL), BF16),
        scratch_shapes=[pltpu.VMEM((MA_TM, MEM_WIDTH), BF16)],
        compiler_params=_params(("parallel",), 40),
        name="mem_attention",
    )(pg, km, vm, pg, m0, qng, wpc)


MG_TM = 512


def _pack_bf16_pair(a, b):
    hi = pltpu.bitcast(a.astype(BF16).astype(F32), jnp.uint32)
    lo = pltpu.bitcast(b.astype(BF16).astype(F32), jnp.uint32)
    return hi | (lo >> 16)


def _unpack_bf16_pair(p):
    hi = pltpu.bitcast(p & jnp.uint32(0xFFFF0000), F32)
    lo = pltpu.bitcast(p << 16, F32)
    return hi, lo


SUBLANES = 8
ROW_WORDS = D_MODEL // 2
ROW_CHUNKS = ROW_WORDS // LANES
assert ROW_CHUNKS == SUBLANES


def _row_chunk(n_rows, c):
    return pl.ds(c, n_rows, stride=SUBLANES)


def _store_row_tiles(ref, n_rows, packed):
    for c in range(ROW_CHUNKS):
        ref[_row_chunk(n_rows, c), :] = packed[:, c * LANES:(c + 1) * LANES]


def _attn_proj_kernel(b_ref, g1_ref, m1_ref, wpb_ref, o_ref, zero_ref):
    mb = jnp.dot(b_ref[...], wpb_ref[...], preferred_element_type=F32)
    o_ref[...] = (m1_ref[...].astype(F32) + g1_ref[...].astype(F32) * mb).astype(BF16)
    zero_ref[...] = jnp.zeros_like(zero_ref)


def _attn_proj(b_attn, pg, m1, wpb, n_zero_rows):
    T = m1.shape[0]
    steps = T // MG_TM
    assert n_zero_rows % steps == 0
    zero_block = (n_zero_rows // steps * SUBLANES, LANES)
    row = pl.BlockSpec((MG_TM, D_MODEL), lambda i: (i, 0))
    return pl.pallas_call(
        _attn_proj_kernel,
        grid=(steps,),
        in_specs=[
            row,
            pl.BlockSpec((MG_TM, D_MODEL), lambda i: (i, PROJ_COLS // D_MODEL + 1)),
            row,
            _const_spec((MLA_HEADS * V_DIM, D_MODEL)),
        ],
        out_specs=[row, pl.BlockSpec(zero_block, lambda i: (i, 0))],
        out_shape=[jax.ShapeDtypeStruct((T, D_MODEL), BF16),
                   jax.ShapeDtypeStruct((n_zero_rows * SUBLANES, LANES), jnp.uint32)],
        compiler_params=_params(("parallel",), 56),
        name="attn_proj",
    )(b_attn, pg, m1, wpb)


def _merge_kernel(mg_ref, x_ref, wo_ref, ln2_ref, rw_ref, rb_ref,
                  x1_ref, h2p_ref, code_ref, gate_ref, cnt_ref, cnt_scr):
    @pl.when(pl.program_id(0) == 0)
    def _():
        cnt_scr[...] = jnp.zeros_like(cnt_scr)

    x1 = x_ref[...] + jnp.dot(mg_ref[...], wo_ref[...], preferred_element_type=F32)
    x1_ref[...] = x1
    ms = jnp.mean(x1 * x1, axis=-1, keepdims=True)
    h2 = x1 * lax.rsqrt(ms + EPS) * ln2_ref[...]
    _store_row_tiles(h2p_ref, MG_TM, _pack_bf16_pair(h2[:, :D_MODEL // 2], h2[:, D_MODEL // 2:]))

    logits = jnp.dot(h2.astype(BF16), rw_ref[...], preferred_element_type=F32)
    lane = lax.broadcasted_iota(jnp.int32, (MG_TM, LANES), 1)
    work = jnp.where(lane < N_EXPERTS, logits + rb_ref[...], NEG_INF)
    earlier = (lax.broadcasted_iota(jnp.int32, (MG_TM, MG_TM), 1)
               < lax.broadcasted_iota(jnp.int32, (MG_TM, MG_TM), 0)).astype(BF16)
    base = cnt_scr[...]
    code_out = jnp.zeros((MG_TM, LANES), jnp.int32)
    val_out = jnp.zeros((MG_TM, LANES), F32)
    top = None
    denom = jnp.zeros((MG_TM, 1), F32)
    for k in range(TOP_K):
        mk = jnp.max(work, axis=-1, keepdims=True)
        ik = jnp.min(jnp.where(work == mk, lane, LANES), axis=-1, keepdims=True)
        hit = lane == ik
        work = jnp.where(hit, NEG_INF, work)
        if top is None:
            top = mk
        ek = jnp.exp(mk - top)
        denom = denom + ek
        val_out = jnp.where(lane == k, ek, val_out)
        onehot = hit.astype(BF16)
        prefix = jnp.dot(earlier, onehot, preferred_element_type=F32) + base
        rank = jnp.sum(jnp.where(hit, prefix, 0.0), axis=-1, keepdims=True).astype(jnp.int32)
        code_out = jnp.where(lane == k, rank * N_EXPERTS + ik, code_out)
        base = base + jnp.sum(onehot.astype(F32), axis=0, keepdims=True)
    code_ref[...] = code_out
    gate_ref[...] = val_out / denom
    cnt_scr[...] = base
    cnt_ref[...] = base


def _merge(merged, x2, wo, ln2, rw, rb):
    T = x2.shape[0]
    row = lambda w: pl.BlockSpec((MG_TM, w), lambda i: (i, 0))
    return pl.pallas_call(
        _merge_kernel,
        grid=(T // MG_TM,),
        in_specs=[
            row(D_MODEL),
            row(D_MODEL),
            _const_spec((D_MODEL, D_MODEL)),
            _const_spec((1, D_MODEL)),
            _const_spec((D_MODEL, LANES)),
            _const_spec((1, LANES)),
        ],
        out_specs=[row(D_MODEL), pl.BlockSpec((MG_TM * SUBLANES, LANES), lambda i: (i, 0)), row(LANES),
                   row(LANES), pl.BlockSpec((1, LANES), lambda i: (0, 0))],
        out_shape=[
            jax.ShapeDtypeStruct((T, D_MODEL), F32),
            jax.ShapeDtypeStruct((T * SUBLANES, LANES), jnp.uint32),
            jax.ShapeDtypeStruct((T, LANES), jnp.int32),
            jax.ShapeDtypeStruct((T, LANES), F32),
            jax.ShapeDtypeStruct((1, LANES), F32),
        ],
        scratch_shapes=[pltpu.VMEM((1, LANES), F32)],
        compiler_params=_params(("arbitrary",), 56),
        name="merge_router",
    )(merged, x2, wo, ln2, rw, rb)


DP_TM = 512
DMA_UNROLL = 8


def _row_tile(row):
    return pl.ds(pl.multiple_of(row * SUBLANES, SUBLANES), SUBLANES)


def _dispatch_kernel(dest_ref, h2p_ref, xs_in, xs_hbm, sem):
    del xs_in

    def issue(t, carry):
        src = h2p_ref.at[_row_tile(t), :]
        for k in range(TOP_K):
            row = dest_ref[0, t * TOP_K + k]
            pltpu.make_async_copy(src, xs_hbm.at[_row_tile(row), :], sem).start(priority=k % 2)
        return carry

    lax.fori_loop(0, DP_TM, issue, 0, unroll=DMA_UNROLL // TOP_K)
    all_rows = xs_hbm.at[pl.ds(0, DP_TM * TOP_K * SUBLANES), :]
    pltpu.make_async_copy(all_rows, all_rows, sem).wait()


def _dispatch(dest3, h2p, xs_zero):
    T = h2p.shape[0] // SUBLANES
    return pl.pallas_call(
        _dispatch_kernel,
        grid=(T // DP_TM,),
        in_specs=[
            pl.BlockSpec((None, 1, DP_TM * TOP_K), lambda i: (i, 0, 0), memory_space=pltpu.SMEM),
            pl.BlockSpec((DP_TM * SUBLANES, LANES), lambda i: (i, 0)),
            pl.BlockSpec(memory_space=pl.ANY),
        ],
        out_specs=pl.BlockSpec(memory_space=pl.ANY),
        out_shape=jax.ShapeDtypeStruct(xs_zero.shape, xs_zero.dtype),
        scratch_shapes=[pltpu.SemaphoreType.DMA(())],
        input_output_aliases={2: 0},
        compiler_params=_params(("arbitrary",), 32),
        name="moe_dispatch",
    )(dest3, h2p, xs_zero)


GROUP_SUB = 4
GROUP_ROWS = GROUP_SUB * ROW_BLOCK
FF_TILE = 512
N_FF_TILES = D_FF // FF_TILE
DOWN_CHUNK = 512


def _expert_kernel(ge_ref, gs_ref, gn_ref, xs_in, wg_ref, wu_ref, bg_ref, bu_ref, wd_ref, bd_ref,
                   rows_hbm, xwin, x_scr, acc_scr, stage, wsem, sem):
    del ge_ref, xs_in
    g = pl.program_id(0)
    f = pl.program_id(1)
    ns = gn_ref[g]
    half = D_MODEL // 2
    slot = g & 1

    def win_copy(group, s):
        rows = pl.ds(gs_ref[group] * (ROW_BLOCK * SUBLANES), GROUP_ROWS * SUBLANES)
        return pltpu.make_async_copy(rows_hbm.at[rows, :], xwin.at[s], wsem.at[s])

    def out_copy(s, start_block):
        rows = pl.ds((start_block + s) * (ROW_BLOCK * SUBLANES), ROW_BLOCK * SUBLANES)
        return pltpu.make_async_copy(stage.at[s], rows_hbm.at[rows, :], sem.at[s])

    @pl.when((f == 0) & (ns > 0))
    def _():
        @pl.when(g == 0)
        def _():
            win_copy(0, 0).start()
            acc_scr[...] = jnp.zeros_like(acc_scr)

        win_copy(g, slot).wait()
        @pl.when(g + 1 < pl.num_programs(0))
        def _():
            @pl.when(gn_ref[g + 1] > 0)
            def _():
                win_copy(g + 1, 1 - slot).start()

        for c in range(ROW_CHUNKS):
            hi, lo = _unpack_bf16_pair(xwin[slot, _row_chunk(GROUP_ROWS, c), :])
            x_scr[:, c * LANES:(c + 1) * LANES] = hi.astype(BF16)
            x_scr[:, half + c * LANES:half + (c + 1) * LANES] = lo.astype(BF16)

    for n in range(1, GROUP_SUB + 1):
        @pl.when(ns == n)
        def _(n=n):
            m = n * ROW_BLOCK
            x = x_scr[:m, :]
            gate = jnp.dot(x, wg_ref[...].astype(BF16), preferred_element_type=F32) + bg_ref[...]
            up = jnp.dot(x, wu_ref[...].astype(BF16), preferred_element_type=F32) + bu_ref[...]
            gate = jnp.minimum(gate, SWIGLU_LIMIT)
            up = jnp.clip(up, -SWIGLU_LIMIT, SWIGLU_LIMIT)
            glu = gate * jax.nn.sigmoid(gate * SWIGLU_ALPHA)
            act = ((up + 1.0) * glu).astype(BF16)
            for c in range(D_MODEL // DOWN_CHUNK):
                cols = slice(c * DOWN_CHUNK, (c + 1) * DOWN_CHUNK)
                start = jnp.where(f == 0, jnp.broadcast_to(bd_ref[:, cols], (m, DOWN_CHUNK)), acc_scr[:m, cols])
                acc_scr[:m, cols] = start + jnp.dot(act, wd_ref[:, cols].astype(BF16),
                                                    preferred_element_type=F32)

    @pl.when(f == N_FF_TILES - 1)
    def _():
        @pl.when(g > 0)
        def _():
            prev = gn_ref[g - 1]
            for s in range(GROUP_SUB):
                @pl.when(s < prev)
                def _(s=s):
                    out_copy(s, 0).wait()

        for s in range(GROUP_SUB):
            @pl.when(s < ns)
            def _(s=s):
                y = acc_scr[s * ROW_BLOCK:(s + 1) * ROW_BLOCK, :]
                _store_row_tiles(stage.at[s], ROW_BLOCK, _pack_bf16_pair(y[:, :half], y[:, half:]))
                out_copy(s, gs_ref[g]).start()

        @pl.when(g == pl.num_programs(0) - 1)
        def _():
            for s in range(GROUP_SUB):
                @pl.when(s < ns)
                def _(s=s):
                    out_copy(s, 0).wait()


def _experts(g_exp, g_start, g_nsub, xs, wgu, bgu, wd, bd):
    def ff(f, gn, g):
        return jnp.where(gn[g] > 0, f, N_FF_TILES - 1)

    grid_spec = pltpu.PrefetchScalarGridSpec(
        num_scalar_prefetch=3,
        grid=(g_exp.shape[0], N_FF_TILES),
        in_specs=[
            pl.BlockSpec(memory_space=pl.ANY),
            pl.BlockSpec((None, D_MODEL, FF_TILE), lambda g, f, ge, gs, gn: (ge[g], 0, ff(f, gn, g))),
            pl.BlockSpec((None, D_MODEL, FF_TILE),
                         lambda g, f, ge, gs, gn: (ge[g], 0, N_FF_TILES + ff(f, gn, g))),
            pl.BlockSpec((None, 1, FF_TILE), lambda g, f, ge, gs, gn: (ge[g], 0, ff(f, gn, g))),
            pl.BlockSpec((None, 1, FF_TILE), lambda g, f, ge, gs, gn: (ge[g], 0, N_FF_TILES + ff(f, gn, g))),
            pl.BlockSpec((None, FF_TILE, D_MODEL), lambda g, f, ge, gs, gn: (ge[g], ff(f, gn, g), 0)),
            pl.BlockSpec((None, 1, D_MODEL), lambda g, f, ge, gs, gn: (ge[g], 0, 0)),
        ],
        out_specs=pl.BlockSpec(memory_space=pl.ANY),
        scratch_shapes=[
            pltpu.VMEM((2, GROUP_ROWS * SUBLANES, LANES), jnp.uint32),
            pltpu.VMEM((GROUP_ROWS, D_MODEL), BF16),
            pltpu.VMEM((GROUP_ROWS, D_MODEL), F32),
            pltpu.VMEM((GROUP_SUB, ROW_BLOCK * SUBLANES, LANES), jnp.uint32),
            pltpu.SemaphoreType.DMA((2,)),
            pltpu.SemaphoreType.DMA((GROUP_SUB,)),
        ],
    )
    return pl.pallas_call(
        _expert_kernel,
        grid_spec=grid_spec,
        out_shape=jax.ShapeDtypeStruct(xs.shape, jnp.uint32),
        input_output_aliases={3: 0},
        compiler_params=_params(("arbitrary", "arbitrary"), 58),
        name="moe_experts",
    )(g_exp, g_start, g_nsub, xs, wgu, wgu, bgu, bgu, wd, bd)


CB_TM = 256


def _combine_kernel(dest_ref, dest_next_ref, ys_hbm, x1_ref, gate_ref, o_ref, buf, sem):
    i = pl.program_id(0)
    slot = i & 1

    def gather_tile(dest, s):
        def issue(t, carry):
            for k in range(TOP_K):
                row = dest[0, t * TOP_K + k]
                pltpu.make_async_copy(ys_hbm.at[_row_tile(row), :], buf.at[s, k, _row_tile(t), :],
                                      sem.at[s]).start(priority=k % 2)
            return carry

        lax.fori_loop(0, CB_TM, issue, 0, unroll=DMA_UNROLL // TOP_K)

    @pl.when(i == 0)
    def _():
        gather_tile(dest_ref, 0)

    @pl.when(i + 1 < pl.num_programs(0))
    def _():
        gather_tile(dest_next_ref, 1 - slot)

    pltpu.make_async_copy(buf.at[slot], buf.at[slot], sem.at[slot]).wait()

    half = D_MODEL // 2
    gates = gate_ref[...]
    gate_k = [jnp.broadcast_to(gates[:, k:k + 1], (CB_TM, LANES)) for k in range(TOP_K)]
    for c in range(ROW_CHUNKS):
        cols_hi = slice(c * LANES, (c + 1) * LANES)
        cols_lo = slice(half + c * LANES, half + (c + 1) * LANES)
        out_hi = x1_ref[:, cols_hi]
        out_lo = x1_ref[:, cols_lo]
        for k in range(TOP_K):
            hi, lo = _unpack_bf16_pair(buf[slot, k, _row_chunk(CB_TM, c), :])
            out_hi = out_hi + gate_k[k] * hi
            out_lo = out_lo + gate_k[k] * lo
        o_ref[:, cols_hi] = out_hi
        o_ref[:, cols_lo] = out_lo


def _combine(dest3, ys, x1, gates):
    T = x1.shape[0]
    n = T // CB_TM
    dest_spec = lambda f: pl.BlockSpec((None, 1, CB_TM * TOP_K), f, memory_space=pltpu.SMEM)
    return pl.pallas_call(
        _combine_kernel,
        grid=(n,),
        in_specs=[
            dest_spec(lambda i: (i, 0, 0)),
            dest_spec(lambda i: (jnp.minimum(i + 1, n - 1), 0, 0)),
            pl.BlockSpec(memory_space=pl.ANY),
            pl.BlockSpec((CB_TM, D_MODEL), lambda i: (i, 0)),
            pl.BlockSpec((CB_TM, LANES), lambda i: (i, 0)),
        ],
        out_specs=pl.BlockSpec((CB_TM, D_MODEL), lambda i: (i, 0)),
        out_shape=jax.ShapeDtypeStruct((T, D_MODEL), F32),
        scratch_shapes=[pltpu.VMEM((2, TOP_K, CB_TM * SUBLANES, LANES), jnp.uint32),
                        pltpu.SemaphoreType.DMA((2,))],
        compiler_params=_params(("arbitrary",), 32),
        name="moe_combine",
    )(dest3, dest3, ys, x1, gates)


def _group_tables(counts_f32, max_groups):
    i32 = jnp.int32
    counts = counts_f32[0, :N_EXPERTS].astype(i32)
    nb = (counts + ROW_BLOCK - 1) // ROW_BLOCK
    ng = (nb + GROUP_SUB - 1) // GROUP_SUB
    upto = jnp.arange(N_EXPERTS)[None, :] <= jnp.arange(N_EXPERTS)[:, None]
    pad_ends = jnp.sum(jnp.where(upto, nb[None, :], 0), axis=1) * ROW_BLOCK
    pad_starts = pad_ends - nb * ROW_BLOCK
    g_ends = jnp.sum(jnp.where(upto, ng[None, :], 0), axis=1)
    g_starts = g_ends - ng
    n_groups = g_ends[-1]

    g = jnp.arange(max_groups, dtype=i32)
    gg = jnp.minimum(g, n_groups - 1)
    e = jnp.minimum(jnp.sum((g_ends[None, :] <= gg[:, None]).astype(i32), axis=1), N_EXPERTS - 1)
    pick = e[:, None] == jnp.arange(N_EXPERTS)[None, :]
    take = lambda table: jnp.sum(jnp.where(pick, table[None, :], 0), axis=1)
    nb_g, ng_g = take(nb), jnp.maximum(take(ng), 1)
    j = gg - take(g_starts)
    base, rem = nb_g // ng_g, nb_g % ng_g
    nsub = jnp.where(g < n_groups, base + (j < rem).astype(i32), 0)
    start_block = take(pad_starts) // ROW_BLOCK + j * base + jnp.minimum(j, rem)
    return pad_starts.astype(i32), e.astype(i32), start_block.astype(i32), nsub.astype(i32)


def _rope_constants():
    lane = np.arange(LANES)
    half = QK_ROPE // 2
    inv = 1.0 / (ROPE_THETA ** (np.arange(0, QK_ROPE, 2, dtype=np.float32) / QK_ROPE))
    cst = np.zeros((8, LANES), np.float32)
    cst[0, :QK_ROPE] = inv.astype(np.float32)[lane[:QK_ROPE] % half]
    cst[1, :QK_ROPE] = 1.0
    cst[2, :half] = -1.0
    cst[2, half:QK_ROPE] = 1.0
    return jnp.asarray(cst)


def _swap_halves(a):
    half = QK_ROPE // 2
    return jnp.concatenate([a[..., half:], a[..., :half]], axis=-1)


def kernel(x, mem, positions, ln1_g, w_in, w_gate, b_gate, gmlp_ln_g, gmlp_ln_b, gmlp_ws, gmlp_bs, w_pa,
           mla_cq_g, mla_w_uq, mla_ckv_g, mla_w_ukv, mla_qn_g, mla_kn_g, w_pb, mem_ln_g, mem_w_k, mem_w_v,
           mem_qn_g, mem_kn_g, w_pc, w_o, ln2_g, router_w, router_b, moe_w_gu, moe_b_gu, moe_w_down,
           moe_b_down):
    B, S, D = x.shape
    T = B * S
    x2 = x.reshape(T, D)
    for l in range(ln1_g.shape[0]):
        o_kr = 2 * GM_WIDTH + Q_LORA + KV_LORA
        o_qm = o_kr + QK_ROPE
        wi = w_in[l]
        w1 = jnp.concatenate([wi[:, :o_kr], wi[:, o_qm:], w_gate[l]], axis=1).astype(BF16)
        b1 = jnp.concatenate([jnp.zeros((PROJ_COLS,), F32), b_gate[l]])[None, :]
        w_kr = wi[:, o_kr:o_qm]
        wr = jnp.concatenate([w_kr, _swap_halves(w_kr)], axis=1).astype(BF16)

        wq3 = mla_w_uq[l].reshape(Q_LORA, MLA_HEADS, QK_DIM)
        wq = jnp.concatenate([wq3, _swap_halves(wq3[..., QK_NOPE:])], axis=-1)
        wq = wq.reshape(Q_LORA, MLA_HEADS * QK_PAD).astype(BF16)
        wkv = mla_w_ukv[l].astype(BF16)
        gq = jnp.concatenate([mla_qn_g[l], _swap_halves(mla_qn_g[l][QK_NOPE:])])[None, :]
        gk = jnp.concatenate([mla_kn_g[l], _swap_halves(mla_kn_g[l][QK_NOPE:])])[None, :]

        bias_full = jnp.broadcast_to(gmlp_bs[l].T[:, :, None], (GM_CHUNK, GM_GROUPS, GM_CHUNK))
        bias_full = bias_full.reshape(GM_CHUNK, GM_WIDTH)

        rw = jnp.pad(router_w[l], ((0, 0), (0, LANES - N_EXPERTS))).astype(BF16)
        rb = jnp.pad(router_b[l], (0, LANES - N_EXPERTS))[None, :]

        pg, kr = _norm_proj(x2, ln1_g[l][None, :], w1, b1, wr)
        m0 = _gmlp(pg, gmlp_ln_g[l][None, :], gmlp_ln_b[l][None, :], gmlp_ws[l].astype(BF16), bias_full,
                   w_pa[l].astype(BF16))
        q, k, vt = _mla_qkv(pg, kr, positions.reshape(T, 1), mla_cq_g[l][None, :], mla_ckv_g[l][None, :],
                           wq, wkv, gq, gk, _rope_constants(), B, S)
        b_attn = _flash(q, k, vt)
        km, vm = _mem_kv(mem, mem_ln_g[l][None, :], mem_w_k[l].astype(BF16), mem_w_v[l].astype(BF16),
                         mem_kn_g[l][None, :])
        m1 = _mem_attn(pg, km, vm, m0, mem_qn_g[l][None, :], w_pc[l].astype(BF16), S)
        n_rows = T * TOP_K + N_EXPERTS * ROW_BLOCK
        merged, xs_zero = _attn_proj(b_attn, pg, m1, w_pb[l].astype(BF16), n_rows + GROUP_ROWS)
        x1, h2p, code, gates, counts = _merge(merged, x2, w_o[l].astype(BF16), ln2_g[l][None, :], rw, rb)

        max_groups = n_rows // GROUP_ROWS + N_EXPERTS
        pad_starts, g_exp, g_start, g_nsub = _group_tables(counts, max_groups)
        codes = code[:, :TOP_K]
        pick = (codes & (N_EXPERTS - 1))[..., None] == jnp.arange(N_EXPERTS)
        dest = (codes >> EXPERT_BITS) + jnp.sum(jnp.where(pick, pad_starts, 0), axis=-1)
        xs = _dispatch(dest.reshape(T // DP_TM, 1, DP_TM * TOP_K), h2p, xs_zero)
        ys = _experts(g_exp, g_start, g_nsub, xs, moe_w_gu[l], moe_b_gu[l][:, None, :],
                      moe_w_down[l], moe_b_down[l][:, None, :])
        x2 = _combine(dest.reshape(T // CB_TM, 1, CB_TM * TOP_K), ys, x1, gates)
    return x2.reshape(B, S, D)
```

```python
import functools

import numpy as np
import jax
import jax.numpy as jnp
from jax import lax
from jax.experimental import pallas as pl
from jax.experimental.pallas import tpu as pltpu

F32 = jnp.float32
BF16 = jnp.bfloat16

D_MODEL = 2048
GM_WIDTH = 1024
GM_GROUPS = 8
GM_CHUNK = 128
MLA_HEADS = 16
Q_LORA = 512
KV_LORA = 512
QK_NOPE = 128
QK_ROPE = 64
V_DIM = 128
QK_DIM = QK_NOPE + QK_ROPE
QK_PAD = 256
MLA_SCALE = QK_DIM ** -0.5
LOG2_E = 1.4426950408889634
ROPE_THETA = 10000.0
MEM_LEN = 256
MEM_HEADS = 4
MEM_HEAD_DIM = 256
MEM_WIDTH = MEM_HEADS * MEM_HEAD_DIM
MEM_SCALE = MEM_HEAD_DIM ** -0.5
N_EXPERTS = 32
EXPERT_BITS = 5
TOP_K = 4
D_FF = 2048
SWIGLU_LIMIT = 7.0
SWIGLU_ALPHA = 1.702
ROW_BLOCK = 256
EPS = 1e-6
LANES = 128
NEG_INF = float("-inf")

PROJ_COLS = 2 * GM_WIDTH + Q_LORA + KV_LORA + MEM_WIDTH
PG_COLS = PROJ_COLS + 3 * D_MODEL

MIB = 1024 * 1024


def _params(semantics, vmem_mib):
    return pltpu.CompilerParams(dimension_semantics=semantics, vmem_limit_bytes=vmem_mib * MIB)


def _const_spec(shape):
    nd = len(shape)
    return pl.BlockSpec(shape, lambda *_: (0,) * nd, pipeline_mode=pl.Buffered(1))


P1_TM = 1024
P1_TN = 2048


def _norm_proj_kernel(x_ref, g_ref, w_ref, b_ref, wr_ref, o_ref, kr_ref, h_scr, *, n_plain):
    j = pl.program_id(1)

    @pl.when(j == 0)
    def _():
        def body(c, carry):
            rows = pl.ds(pl.multiple_of(c * 128, 128), 128)
            x = x_ref[rows, :]
            ms = jnp.mean(x * x, axis=-1, keepdims=True)
            h_scr[rows, :] = (x * lax.rsqrt(ms + EPS) * g_ref[...]).astype(BF16)
            return carry

        lax.fori_loop(0, P1_TM // 128, body, 0)
        kr_ref[...] = jnp.dot(h_scr[...], wr_ref[...], preferred_element_type=F32)

    acc = jnp.dot(h_scr[...], w_ref[...], preferred_element_type=F32)
    gated = jax.nn.sigmoid(acc + b_ref[...])
    o_ref[...] = jnp.where(j >= n_plain, gated, acc).astype(BF16)


def _norm_proj(x2, ln1_g, w1, b1, wr):
    T = x2.shape[0]
    grid = (T // P1_TM, PG_COLS // P1_TN)
    return pl.pallas_call(
        functools.partial(_norm_proj_kernel, n_plain=PROJ_COLS // P1_TN),
        grid=grid,
        in_specs=[
            pl.BlockSpec((P1_TM, D_MODEL), lambda i, j: (i, 0)),
            pl.BlockSpec((1, D_MODEL), lambda i, j: (0, 0)),
            pl.BlockSpec((D_MODEL, P1_TN), lambda i, j: (0, j)),
            pl.BlockSpec((1, P1_TN), lambda i, j: (0, j)),
            pl.BlockSpec((D_MODEL, LANES), lambda i, j: (0, 0)),
        ],
        out_specs=[
            pl.BlockSpec((P1_TM, P1_TN), lambda i, j: (i, j)),
            pl.BlockSpec((P1_TM, LANES), lambda i, j: (i, 0)),
        ],
        out_shape=[
            jax.ShapeDtypeStruct((T, PG_COLS), BF16),
            jax.ShapeDtypeStruct((T, LANES), F32),
        ],
        scratch_shapes=[pltpu.VMEM((P1_TM, D_MODEL), BF16)],
        compiler_params=_params(("parallel", "arbitrary"), 58),
        name="norm_proj",
    )(x2, ln1_g, w1, b1, wr)


GM_TM = 512


def _gmlp_kernel(u_ref, v_ref, g0_ref, lng_ref, lnb_ref, ws_ref, bias_ref, wpa_ref, o_ref, vb_scr, a_scr):
    v = v_ref[...].astype(F32)
    mu = jnp.mean(v, axis=-1, keepdims=True)
    c = v - mu
    var = jnp.mean(c * c, axis=-1, keepdims=True)
    vb_scr[...] = (c * lax.rsqrt(var + EPS) * lng_ref[...] + lnb_ref[...]).astype(BF16)

    row = lax.broadcasted_iota(jnp.int32, (GM_CHUNK, GM_CHUNK), 0)
    col = lax.broadcasted_iota(jnp.int32, (GM_CHUNK, GM_CHUNK), 1)
    causal = col <= row
    for g in range(GM_GROUPS):
        cols = slice(g * GM_CHUNK, (g + 1) * GM_CHUNK)
        wg = jnp.where(causal, ws_ref[g], jnp.zeros((), BF16))
        chunks = [slice(ch * GM_CHUNK, (ch + 1) * GM_CHUNK) for ch in range(GM_TM // GM_CHUNK)]
        mixed = jnp.dot(wg, jnp.concatenate([vb_scr[rows, cols] for rows in chunks], axis=1),
                        preferred_element_type=F32)
        for ch, rows in enumerate(chunks):
            mix = mixed[:, ch * GM_CHUNK:(ch + 1) * GM_CHUNK] + bias_ref[:, cols]
            a_scr[rows, cols] = (u_ref[rows, cols].astype(F32) * mix).astype(BF16)

    ma = jnp.dot(a_scr[...], wpa_ref[...], preferred_element_type=F32)
    o_ref[...] = (g0_ref[...].astype(F32) * ma).astype(BF16)


def _gmlp(pg, ln_g, ln_b, ws, bias_full, wpa):
    T = pg.shape[0]
    return pl.pallas_call(
        _gmlp_kernel,
        grid=(T // GM_TM,),
        in_specs=[
            pl.BlockSpec((GM_TM, GM_WIDTH), lambda i: (i, 0)),
            pl.BlockSpec((GM_TM, GM_WIDTH), lambda i: (i, 1)),
            pl.BlockSpec((GM_TM, D_MODEL), lambda i: (i, PROJ_COLS // D_MODEL)),
            _const_spec((1, GM_WIDTH)),
            _const_spec((1, GM_WIDTH)),
            _const_spec((GM_GROUPS, GM_CHUNK, GM_CHUNK)),
            _const_spec((GM_CHUNK, GM_WIDTH)),
            _const_spec((GM_WIDTH, D_MODEL)),
        ],
        out_specs=pl.BlockSpec((GM_TM, D_MODEL), lambda i: (i, 0)),
        out_shape=jax.ShapeDtypeStruct((T, D_MODEL), BF16),
        scratch_shapes=[pltpu.VMEM((GM_TM, GM_WIDTH), BF16), pltpu.VMEM((GM_TM, GM_WIDTH), BF16)],
        compiler_params=_params(("parallel",), 40),
        name="gmlp",
    )(pg, pg, pg, ln_g, ln_b, ws, bias_full, wpa)


QKV_TM = 1024
QKV_HEADS = 4
HEAD_BLK = 4
FA_T = 512
FA_QS = 256
V_AUG = V_DIM + 16


def _rope(t, cos, sin):
    return t * cos + pltpu.roll(t, QK_ROPE, 1) * sin


def _row_sum_all_lanes(sq, weights):
    return jnp.dot(sq.astype(BF16), weights.astype(BF16), preferred_element_type=F32)


def _mla_qkv_kernel(cq_ref, ckv_ref, kr_ref, pos_ref, gcq_ref, gckv_ref, wq_ref, wkv_ref, gq_ref, gk_ref,
                    cst_ref, sumw_ref, q_ref, k_ref, vt_ref, cqn_scr, ckvn_scr, cos_scr, sin_scr, krsq_scr):
    hb = pl.program_id(1)
    w_nope = sumw_ref[:QK_NOPE, :]
    w_rope = sumw_ref[QK_NOPE:, :]

    @pl.when(hb == 0)
    def _():
        cq = cq_ref[...].astype(F32)
        ms = jnp.mean(cq * cq, axis=-1, keepdims=True)
        cqn_scr[...] = (cq * lax.rsqrt(ms + EPS) * gcq_ref[...]).astype(BF16)
        ckv = ckv_ref[...].astype(F32)
        ms = jnp.mean(ckv * ckv, axis=-1, keepdims=True)
        ckvn_scr[...] = (ckv * lax.rsqrt(ms + EPS) * gckv_ref[...]).astype(BF16)
        ang = pos_ref[...].astype(F32) * cst_ref[0:1, :]
        cos_scr[...] = jnp.cos(ang) * cst_ref[1:2, :]
        sin_scr[...] = jnp.sin(ang) * cst_ref[2:3, :]
        kr = kr_ref[...]
        krsq_scr[...] = _row_sum_all_lanes(kr * kr, w_rope)

    cos = cos_scr[...]
    sin = sin_scr[...]
    inv_dim = 1.0 / QK_DIM
    kr = kr_ref[...]

    yq2 = jnp.dot(cqn_scr[...], wq_ref[...], preferred_element_type=F32)
    ykv2 = jnp.dot(ckvn_scr[...], wkv_ref[...], preferred_element_type=F32)
    for hh in range(QKV_HEADS):
        yq = yq2[:, hh * QK_PAD:(hh + 1) * QK_PAD]
        qn = yq[:, :QK_NOPE]
        qt = yq[:, QK_NOPE:]
        ssq = _row_sum_all_lanes(qn * qn, w_nope) + _row_sum_all_lanes(qt * qt, w_rope)
        rs = lax.rsqrt(ssq * inv_dim + EPS) * (MLA_SCALE * LOG2_E)
        q_ref[hh, :, :QK_NOPE] = (qn * rs * gq_ref[:, :QK_NOPE]).astype(BF16)
        q_ref[hh, :, QK_NOPE:] = _rope(qt * rs * gq_ref[:, QK_NOPE:], cos, sin).astype(BF16)

        ykv = ykv2[:, hh * (QK_NOPE + V_DIM):(hh + 1) * (QK_NOPE + V_DIM)]
        kn = ykv[:, :QK_NOPE]
        ssq = _row_sum_all_lanes(kn * kn, w_nope) + krsq_scr[...]
        rs = lax.rsqrt(ssq * inv_dim + EPS)
        k_ref[hh, :, :QK_NOPE] = (kn * rs * gk_ref[:, :QK_NOPE]).astype(BF16)
        k_ref[hh, :, QK_NOPE:] = _rope(kr * rs * gk_ref[:, QK_NOPE:], cos, sin).astype(BF16)
        vv = ykv[:, QK_NOPE:]
        for c in range(QKV_TM // FA_T):
            vt_ref[hh, c, :V_DIM, :] = vv[c * FA_T:(c + 1) * FA_T, :].T.astype(BF16)
            extra = lax.broadcasted_iota(jnp.int32, (V_AUG - V_DIM, FA_T), 0)
            vt_ref[hh, c, V_DIM:, :] = (extra == 0).astype(BF16)


def _mla_qkv(pg, kr, pos, gcq, gckv, wq, wkv, gq, gk, cst, B, S):
    T = pg.shape[0]
    sumw = jnp.concatenate([jnp.ones((QK_NOPE, LANES), F32), jnp.full((QK_PAD - QK_NOPE, LANES), 0.5, F32)])
    spb = S // QKV_TM
    cpt = QKV_TM // FA_T
    head_spec = lambda w: pl.BlockSpec((None, QKV_HEADS, QKV_TM, w), lambda i, h: (i // spb, h, i % spb, 0))
    return pl.pallas_call(
        _mla_qkv_kernel,
        grid=(T // QKV_TM, MLA_HEADS // QKV_HEADS),
        in_specs=[
            pl.BlockSpec((QKV_TM, Q_LORA), lambda i, h: (i, 2 * GM_WIDTH // Q_LORA)),
            pl.BlockSpec((QKV_TM, KV_LORA), lambda i, h: (i, 2 * GM_WIDTH // KV_LORA + 1)),
            pl.BlockSpec((QKV_TM, LANES), lambda i, h: (i, 0)),
            pl.BlockSpec((QKV_TM, 1), lambda i, h: (i, 0)),
            pl.BlockSpec((1, Q_LORA), lambda i, h: (0, 0)),
            pl.BlockSpec((1, KV_LORA), lambda i, h: (0, 0)),
            pl.BlockSpec((Q_LORA, QKV_HEADS * QK_PAD), lambda i, h: (0, h)),
            pl.BlockSpec((KV_LORA, QKV_HEADS * (QK_NOPE + V_DIM)), lambda i, h: (0, h)),
            pl.BlockSpec((1, QK_PAD), lambda i, h: (0, 0)),
            pl.BlockSpec((1, QK_PAD), lambda i, h: (0, 0)),
            pl.BlockSpec((8, LANES), lambda i, h: (0, 0)),
            pl.BlockSpec((QK_PAD, LANES), lambda i, h: (0, 0)),
        ],
        out_specs=[
            head_spec(QK_PAD),
            head_spec(QK_PAD),
            pl.BlockSpec((None, QKV_HEADS, cpt, V_AUG, FA_T), lambda i, h: (i // spb, h, i % spb, 0, 0)),
        ],
        out_shape=[
            jax.ShapeDtypeStruct((B, MLA_HEADS, S, QK_PAD), BF16),
            jax.ShapeDtypeStruct((B, MLA_HEADS, S, QK_PAD), BF16),
            jax.ShapeDtypeStruct((B, MLA_HEADS, S // FA_T, V_AUG, FA_T), BF16),
        ],
        scratch_shapes=[
            pltpu.VMEM((QKV_TM, Q_LORA), BF16),
            pltpu.VMEM((QKV_TM, KV_LORA), BF16),
            pltpu.VMEM((QKV_TM, LANES), F32),
            pltpu.VMEM((QKV_TM, LANES), F32),
            pltpu.VMEM((QKV_TM, LANES), F32),
        ],
        compiler_params=_params(("parallel", "arbitrary"), 48),
        name="mla_qkv",
    )(pg, pg, kr, pos, gcq, gckv, wq, wkv, gq, gk, cst, sumw)


def _flash_kernel(q_ref, k_ref, vt_ref, o_ref, m_scr, acc_scr):
    qi = pl.program_id(2)
    m_scr[...] = jnp.full_like(m_scr, NEG_INF)
    acc_scr[...] = jnp.zeros_like(acc_scr)

    def step(kb, diagonal):
        ks = pl.ds(pl.multiple_of(kb * FA_T, FA_T), FA_T)
        chains = [(hh, qs) for hh in range(HEAD_BLK) for qs in range(FA_T // FA_QS)]
        scores = []
        for hh, qs in chains:
            qc = slice(qs * FA_QS, (qs + 1) * FA_QS)
            st = lax.dot_general(k_ref[hh, ks, :], q_ref[hh, qc, :], (((1,), (1,)), ((), ())),
                                 preferred_element_type=F32)
            if diagonal:
                krow = lax.broadcasted_iota(jnp.int32, (FA_T, FA_QS), 0)
                qcol = lax.broadcasted_iota(jnp.int32, (FA_T, FA_QS), 1) + qs * FA_QS
                st = jnp.where(krow <= qcol, st, NEG_INF)
            scores.append(st)
        probs = []
        for (hh, qs), st in zip(chains, scores):
            qc = slice(qs * FA_QS, (qs + 1) * FA_QS)
            m_prev = m_scr[hh, :, qc]
            m_new = jnp.maximum(m_prev, jnp.max(st, axis=0, keepdims=True))
            alpha = jnp.exp2(m_prev - m_new)
            p = jnp.exp2(st - m_new)
            m_scr[hh, :, qc] = m_new
            probs.append((alpha, p.astype(BF16)))
        for (hh, qs), (alpha, p) in zip(chains, probs):
            qc = slice(qs * FA_QS, (qs + 1) * FA_QS)
            acc_scr[hh, :, qc] = alpha * acc_scr[hh, :, qc] + jnp.dot(vt_ref[hh, kb], p,
                                                                      preferred_element_type=F32)

    def body(kb, carry):
        step(kb, False)
        return carry

    lax.fori_loop(0, qi, body, 0)
    step(qi, True)
    for hh in range(HEAD_BLK):
        out_t = acc_scr[hh, :V_DIM, :] / acc_scr[hh, V_DIM:V_DIM + 1, :]
        o_ref[:, hh * V_DIM:(hh + 1) * V_DIM] = out_t.T.astype(BF16)


def _flash(q, k, vt):
    B, H, S, _ = q.shape
    nq = S // FA_T
    return pl.pallas_call(
        _flash_kernel,
        grid=(B, H // HEAD_BLK, nq),
        in_specs=[
            pl.BlockSpec((None, HEAD_BLK, FA_T, QK_PAD), lambda b, h, i: (b, h, i, 0)),
            pl.BlockSpec((None, HEAD_BLK, S, QK_PAD), lambda b, h, i: (b, h, 0, 0)),
            pl.BlockSpec((None, HEAD_BLK, nq, V_AUG, FA_T), lambda b, h, i: (b, h, 0, 0, 0)),
        ],
        out_specs=pl.BlockSpec((FA_T, HEAD_BLK * V_DIM), lambda b, h, i: (b * nq + i, h)),
        out_shape=jax.ShapeDtypeStruct((B * S, H * V_DIM), BF16),
        scratch_shapes=[
            pltpu.VMEM((HEAD_BLK, 1, FA_T), F32),
            pltpu.VMEM((HEAD_BLK, V_AUG, FA_T), F32),
        ],
        compiler_params=_params(("parallel", "parallel", "arbitrary"), 40),
        name="mla_attention",
    )(q, k, vt)


def _mem_kv_kernel(mem_ref, g_ref, wk_ref, wv_ref, kng_ref, k_ref, v_ref):
    m = mem_ref[...]
    ms = jnp.mean(m * m, axis=-1, keepdims=True)
    mb = (m * lax.rsqrt(ms + EPS) * g_ref[...]).astype(BF16)
    kk = jnp.dot(mb, wk_ref[...], preferred_element_type=F32)
    for h in range(MEM_HEADS):
        cols = slice(h * MEM_HEAD_DIM, (h + 1) * MEM_HEAD_DIM)
        kh = kk[:, cols]
        ms = jnp.mean(kh * kh, axis=-1, keepdims=True)
        k_ref[:, cols] = (kh * lax.rsqrt(ms + EPS) * kng_ref[...]).astype(BF16)
    v_ref[...] = jnp.dot(mb, wv_ref[...], preferred_element_type=F32).astype(BF16)


def _mem_kv(mem, g, wk, wv, kng):
    B = mem.shape[0]
    return pl.pallas_call(
        _mem_kv_kernel,
        grid=(B,),
        in_specs=[
            pl.BlockSpec((None, MEM_LEN, D_MODEL), lambda b: (b, 0, 0)),
            _const_spec((1, D_MODEL)),
            _const_spec((D_MODEL, MEM_WIDTH)),
            _const_spec((D_MODEL, MEM_WIDTH)),
            _const_spec((1, MEM_HEAD_DIM)),
        ],
        out_specs=[
            pl.BlockSpec((None, MEM_LEN, MEM_WIDTH), lambda b: (b, 0, 0)),
            pl.BlockSpec((None, MEM_LEN, MEM_WIDTH), lambda b: (b, 0, 0)),
        ],
        out_shape=[
            jax.ShapeDtypeStruct((B, MEM_LEN, MEM_WIDTH), BF16),
            jax.ShapeDtypeStruct((B, MEM_LEN, MEM_WIDTH), BF16),
        ],
        compiler_params=_params(("parallel",), 32),
        name="mem_kv",
    )(mem, g, wk, wv, kng)


MA_TM = 512


def _mem_attn_kernel(qm_ref, km_ref, vm_ref, g2_ref, m0_ref, qng_ref, wpc_ref, o_ref, c_scr):
    for h in range(MEM_HEADS):
        cols = slice(h * MEM_HEAD_DIM, (h + 1) * MEM_HEAD_DIM)
        qh = qm_ref[:, cols].astype(F32)
        ms = jnp.mean(qh * qh, axis=-1, keepdims=True)
        qn = (qh * (lax.rsqrt(ms + EPS) * MEM_SCALE) * qng_ref[...]).astype(BF16)
        s = lax.dot_general(qn, km_ref[:, cols], (((1,), (1,)), ((), ())), preferred_element_type=F32)
        e = jnp.exp(s - jnp.max(s, axis=-1, keepdims=True))
        p = (e / jnp.sum(e, axis=-1, keepdims=True)).astype(BF16)
        c_scr[:, cols] = jnp.dot(p, vm_ref[:, cols], preferred_element_type=F32).astype(BF16)
    mc = jnp.dot(c_scr[...], wpc_ref[...], preferred_element_type=F32)
    o_ref[...] = (m0_ref[...].astype(F32) + g2_ref[...].astype(F32) * mc).astype(BF16)


def _mem_attn(pg, km, vm, m0, qng, wpc, S):
    T = pg.shape[0]
    spb = S // MA_TM
    return pl.pallas_call(
        _mem_attn_kernel,
        grid=(T // MA_TM,),
        in_specs=[
            pl.BlockSpec((MA_TM, MEM_WIDTH), lambda i: (i, (2 * GM_WIDTH + Q_LORA + KV_LORA) // MEM_WIDTH)),
            pl.BlockSpec((None, MEM_LEN, MEM_WIDTH), lambda i: (i // spb, 0, 0)),
            pl.BlockSpec((None, MEM_LEN, MEM_WIDTH), lambda i: (i // spb, 0, 0)),
            pl.BlockSpec((MA_TM, D_MODEL), lambda i: (i, PROJ_COLS // D_MODEL + 2)),
            pl.BlockSpec((MA_TM, D_MODEL), lambda i: (i, 0)),
            _const_spec((1, MEM_HEAD_DIM)),
            _const_spec((MEM_WIDTH, D_MODEL)),
        ],
        out_specs=pl.BlockSpec((MA_TM, D_MODEL), lambda i: (i, 0)),
        out_shape=jax.ShapeDtypeStruct((T, D_MODEL), BF16),
        scratch_shapes=[pltpu.VMEM((MA_TM, MEM_WIDTH), BF16)],
        compiler_params=_params(("parallel",), 40),
        name="mem_attention",
    )(pg, km, vm, pg, m0, qng, wpc)


MG_TM = 512


def _pack_bf16_pair(a, b):
    hi = pltpu.bitcast(a.astype(BF16).astype(F32), jnp.uint32)
    lo = pltpu.bitcast(b.astype(BF16).astype(F32), jnp.uint32)
    return hi | (lo >> 16)


def _unpack_bf16_pair(p):
    hi = pltpu.bitcast(p & jnp.uint32(0xFFFF0000), F32)
    lo = pltpu.bitcast(p << 16, F32)
    return hi, lo


SUBLANES = 8
ROW_WORDS = D_MODEL // 2
ROW_CHUNKS = ROW_WORDS // LANES
assert ROW_CHUNKS == SUBLANES


def _row_chunk(n_rows, c):
    return pl.ds(c, n_rows, stride=SUBLANES)


def _store_row_tiles(ref, n_rows, packed):
    for c in range(ROW_CHUNKS):
        ref[_row_chunk(n_rows, c), :] = packed[:, c * LANES:(c + 1) * LANES]


def _attn_proj_kernel(b_ref, g1_ref, m1_ref, wpb_ref, o_ref, zero_ref):
    mb = jnp.dot(b_ref[...], wpb_ref[...], preferred_element_type=F32)
    o_ref[...] = (m1_ref[...].astype(F32) + g1_ref[...].astype(F32) * mb).astype(BF16)
    zero_ref[...] = jnp.zeros_like(zero_ref)


def _attn_proj(b_attn, pg, m1, wpb, n_zero_rows):
    T = m1.shape[0]
    steps = T // MG_TM
    assert n_zero_rows % steps == 0
    zero_block = (n_zero_rows // steps * SUBLANES, LANES)
    row = pl.BlockSpec((MG_TM, D_MODEL), lambda i: (i, 0))
    return pl.pallas_call(
        _attn_proj_kernel,
        grid=(steps,),
        in_specs=[
            row,
            pl.BlockSpec((MG_TM, D_MODEL), lambda i: (i, PROJ_COLS // D_MODEL + 1)),
            row,
            _const_spec((MLA_HEADS * V_DIM, D_MODEL)),
        ],
        out_specs=[row, pl.BlockSpec(zero_block, lambda i: (i, 0))],
        out_shape=[jax.ShapeDtypeStruct((T, D_MODEL), BF16),
                   jax.ShapeDtypeStruct((n_zero_rows * SUBLANES, LANES), jnp.uint32)],
        compiler_params=_params(("parallel",), 56),
        name="attn_proj",
    )(b_attn, pg, m1, wpb)


def _merge_kernel(mg_ref, x_ref, wo_ref, ln2_ref, rw_ref, rb_ref,
                  x1_ref, h2p_ref, code_ref, gate_ref, cnt_ref, cnt_scr):
    @pl.when(pl.program_id(0) == 0)
    def _():
        cnt_scr[...] = jnp.zeros_like(cnt_scr)

    x1 = x_ref[...] + jnp.dot(mg_ref[...], wo_ref[...], preferred_element_type=F32)
    x1_ref[...] = x1
    ms = jnp.mean(x1 * x1, axis=-1, keepdims=True)
    h2 = x1 * lax.rsqrt(ms + EPS) * ln2_ref[...]
    _store_row_tiles(h2p_ref, MG_TM, _pack_bf16_pair(h2[:, :D_MODEL // 2], h2[:, D_MODEL // 2:]))

    logits = jnp.dot(h2.astype(BF16), rw_ref[...], preferred_element_type=F32)
    lane = lax.broadcasted_iota(jnp.int32, (MG_TM, LANES), 1)
    work = jnp.where(lane < N_EXPERTS, logits + rb_ref[...], NEG_INF)
    earlier = (lax.broadcasted_iota(jnp.int32, (MG_TM, MG_TM), 1)
               < lax.broadcasted_iota(jnp.int32, (MG_TM, MG_TM), 0)).astype(BF16)
    base = cnt_scr[...]
    code_out = jnp.zeros((MG_TM, LANES), jnp.int32)
    val_out = jnp.zeros((MG_TM, LANES), F32)
    top = None
    denom = jnp.zeros((MG_TM, 1), F32)
    for k in range(TOP_K):
        mk = jnp.max(work, axis=-1, keepdims=True)
        ik = jnp.min(jnp.where(work == mk, lane, LANES), axis=-1, keepdims=True)
        hit = lane == ik
        work = jnp.where(hit, NEG_INF, work)
        if top is None:
            top = mk
        ek = jnp.exp(mk - top)
        denom = denom + ek
        val_out = jnp.where(lane == k, ek, val_out)
        onehot = hit.astype(BF16)
        prefix = jnp.dot(earlier, onehot, preferred_element_type=F32) + base
        rank = jnp.sum(jnp.where(hit, prefix, 0.0), axis=-1, keepdims=True).astype(jnp.int32)
        code_out = jnp.where(lane == k, rank * N_EXPERTS + ik, code_out)
        base = base + jnp.sum(onehot.astype(F32), axis=0, keepdims=True)
    code_ref[...] = code_out
    gate_ref[...] = val_out / denom
    cnt_scr[...] = base
    cnt_ref[...] = base


def _merge(merged, x2, wo, ln2, rw, rb):
    T = x2.shape[0]
    row = lambda w: pl.BlockSpec((MG_TM, w), lambda i: (i, 0))
    return pl.pallas_call(
        _merge_kernel,
        grid=(T // MG_TM,),
        in_specs=[
            row(D_MODEL),
            row(D_MODEL),
            _const_spec((D_MODEL, D_MODEL)),
            _const_spec((1, D_MODEL)),
            _const_spec((D_MODEL, LANES)),
            _const_spec((1, LANES)),
        ],
        out_specs=[row(D_MODEL), pl.BlockSpec((MG_TM * SUBLANES, LANES), lambda i: (i, 0)), row(LANES),
                   row(LANES), pl.BlockSpec((1, LANES), lambda i: (0, 0))],
        out_shape=[
            jax.ShapeDtypeStruct((T, D_MODEL), F32),
            jax.ShapeDtypeStruct((T * SUBLANES, LANES), jnp.uint32),
            jax.ShapeDtypeStruct((T, LANES), jnp.int32),
            jax.ShapeDtypeStruct((T, LANES), F32),
            jax.ShapeDtypeStruct((1, LANES), F32),
        ],
        scratch_shapes=[pltpu.VMEM((1, LANES), F32)],
        compiler_params=_params(("arbitrary",), 56),
        name="merge_router",
    )(merged, x2, wo, ln2, rw, rb)


DP_TM = 512
DMA_UNROLL = 8


def _row_tile(row):
    return pl.ds(pl.multiple_of(row * SUBLANES, SUBLANES), SUBLANES)


def _dispatch_kernel(dest_ref, h2p_ref, xs_in, xs_hbm, sem):
    del xs_in

    def issue(t, carry):
        src = h2p_ref.at[_row_tile(t), :]
        for k in range(TOP_K):
            row = dest_ref[0, t * TOP_K + k]
            pltpu.make_async_copy(src, xs_hbm.at[_row_tile(row), :], sem).start(priority=k % 2)
        return carry

    lax.fori_loop(0, DP_TM, issue, 0, unroll=DMA_UNROLL // TOP_K)
    all_rows = xs_hbm.at[pl.ds(0, DP_TM * TOP_K * SUBLANES), :]
    pltpu.make_async_copy(all_rows, all_rows, sem).wait()


def _dispatch(dest3, h2p, xs_zero):
    T = h2p.shape[0] // SUBLANES
    return pl.pallas_call(
        _dispatch_kernel,
        grid=(T // DP_TM,),
        in_specs=[
            pl.BlockSpec((None, 1, DP_TM * TOP_K), lambda i: (i, 0, 0), memory_space=pltpu.SMEM),
            pl.BlockSpec((DP_TM * SUBLANES, LANES), lambda i: (i, 0)),
            pl.BlockSpec(memory_space=pl.ANY),
        ],
        out_specs=pl.BlockSpec(memory_space=pl.ANY),
        out_shape=jax.ShapeDtypeStruct(xs_zero.shape, xs_zero.dtype),
        scratch_shapes=[pltpu.SemaphoreType.DMA(())],
        input_output_aliases={2: 0},
        compiler_params=_params(("arbitrary",), 32),
        name="moe_dispatch",
    )(dest3, h2p, xs_zero)


GROUP_SUB = 4
GROUP_ROWS = GROUP_SUB * ROW_BLOCK
FF_TILE = 512
N_FF_TILES = D_FF // FF_TILE
DOWN_CHUNK = 512


def _expert_kernel(ge_ref, gs_ref, gn_ref, xs_in, wg_ref, wu_ref, bg_ref, bu_ref, wd_ref, bd_ref,
                   rows_hbm, xwin, x_scr, acc_scr, stage, wsem, sem):
    del ge_ref, xs_in
    g = pl.program_id(0)
    f = pl.program_id(1)
    ns = gn_ref[g]
    half = D_MODEL // 2
    slot = g & 1

    def win_copy(group, s):
        rows = pl.ds(gs_ref[group] * (ROW_BLOCK * SUBLANES), GROUP_ROWS * SUBLANES)
        return pltpu.make_async_copy(rows_hbm.at[rows, :], xwin.at[s], wsem.at[s])

    def out_copy(s, start_block):
        rows = pl.ds((start_block + s) * (ROW_BLOCK * SUBLANES), ROW_BLOCK * SUBLANES)
        return pltpu.make_async_copy(stage.at[s], rows_hbm.at[rows, :], sem.at[s])

    @pl.when((f == 0) & (ns > 0))
    def _():
        @pl.when(g == 0)
        def _():
            win_copy(0, 0).start()
            acc_scr[...] = jnp.zeros_like(acc_scr)

        win_copy(g, slot).wait()
        @pl.when(g + 1 < pl.num_programs(0))
        def _():
            @pl.when(gn_ref[g + 1] > 0)
            def _():
                win_copy(g + 1, 1 - slot).start()

        for c in range(ROW_CHUNKS):
            hi, lo = _unpack_bf16_pair(xwin[slot, _row_chunk(GROUP_ROWS, c), :])
            x_scr[:, c * LANES:(c + 1) * LANES] = hi.astype(BF16)
            x_scr[:, half + c * LANES:half + (c + 1) * LANES] = lo.astype(BF16)

    for n in range(1, GROUP_SUB + 1):
        @pl.when(ns == n)
        def _(n=n):
            m = n * ROW_BLOCK
            x = x_scr[:m, :]
            gate = jnp.dot(x, wg_ref[...].astype(BF16), preferred_element_type=F32) + bg_ref[...]
            up = jnp.dot(x, wu_ref[...].astype(BF16), preferred_element_type=F32) + bu_ref[...]
            gate = jnp.minimum(gate, SWIGLU_LIMIT)
            up = jnp.clip(up, -SWIGLU_LIMIT, SWIGLU_LIMIT)
            glu = gate * jax.nn.sigmoid(gate * SWIGLU_ALPHA)
            act = ((up + 1.0) * glu).astype(BF16)
            for c in range(D_MODEL // DOWN_CHUNK):
                cols = slice(c * DOWN_CHUNK, (c + 1) * DOWN_CHUNK)
                start = jnp.where(f == 0, jnp.broadcast_to(bd_ref[:, cols], (m, DOWN_CHUNK)), acc_scr[:m, cols])
                acc_scr[:m, cols] = start + jnp.dot(act, wd_ref[:, cols].astype(BF16),
                                                    preferred_element_type=F32)

    @pl.when(f == N_FF_TILES - 1)
    def _():
        @pl.when(g > 0)
        def _():
            prev = gn_ref[g - 1]
            for s in range(GROUP_SUB):
                @pl.when(s < prev)
                def _(s=s):
                    out_copy(s, 0).wait()

        for s in range(GROUP_SUB):
            @pl.when(s < ns)
            def _(s=s):
                y = acc_scr[s * ROW_BLOCK:(s + 1) * ROW_BLOCK, :]
                _store_row_tiles(stage.at[s], ROW_BLOCK, _pack_bf16_pair(y[:, :half], y[:, half:]))
                out_copy(s, gs_ref[g]).start()

        @pl.when(g == pl.num_programs(0) - 1)
        def _():
            for s in range(GROUP_SUB):
                @pl.when(s < ns)
                def _(s=s):
                    out_copy(s, 0).wait()


def _experts(g_exp, g_start, g_nsub, xs, wgu, bgu, wd, bd):
    def ff(f, gn, g):
        return jnp.where(gn[g] > 0, f, N_FF_TILES - 1)

    grid_spec = pltpu.PrefetchScalarGridSpec(
        num_scalar_prefetch=3,
        grid=(g_exp.shape[0], N_FF_TILES),
        in_specs=[
            pl.BlockSpec(memory_space=pl.ANY),
            pl.BlockSpec((None, D_MODEL, FF_TILE), lambda g, f, ge, gs, gn: (ge[g], 0, ff(f, gn, g))),
            pl.BlockSpec((None, D_MODEL, FF_TILE),
                         lambda g, f, ge, gs, gn: (ge[g], 0, N_FF_TILES + ff(f, gn, g))),
            pl.BlockSpec((None, 1, FF_TILE), lambda g, f, ge, gs, gn: (ge[g], 0, ff(f, gn, g))),
            pl.BlockSpec((None, 1, FF_TILE), lambda g, f, ge, gs, gn: (ge[g], 0, N_FF_TILES + ff(f, gn, g))),
            pl.BlockSpec((None, FF_TILE, D_MODEL), lambda g, f, ge, gs, gn: (ge[g], ff(f, gn, g), 0)),
            pl.BlockSpec((None, 1, D_MODEL), lambda g, f, ge, gs, gn: (ge[g], 0, 0)),
        ],
        out_specs=pl.BlockSpec(memory_space=pl.ANY),
        scratch_shapes=[
            pltpu.VMEM((2, GROUP_ROWS * SUBLANES, LANES), jnp.uint32),
            pltpu.VMEM((GROUP_ROWS, D_MODEL), BF16),
            pltpu.VMEM((GROUP_ROWS, D_MODEL), F32),
            pltpu.VMEM((GROUP_SUB, ROW_BLOCK * SUBLANES, LANES), jnp.uint32),
            pltpu.SemaphoreType.DMA((2,)),
            pltpu.SemaphoreType.DMA((GROUP_SUB,)),
        ],
    )
    return pl.pallas_call(
        _expert_kernel,
        grid_spec=grid_spec,
        out_shape=jax.ShapeDtypeStruct(xs.shape, jnp.uint32),
        input_output_aliases={3: 0},
        compiler_params=_params(("arbitrary", "arbitrary"), 58),
        name="moe_experts",
    )(g_exp, g_start, g_nsub, xs, wgu, wgu, bgu, bgu, wd, bd)


CB_TM = 256


def _combine_kernel(dest_ref, dest_next_ref, ys_hbm, x1_ref, gate_ref, o_ref, buf, sem):
    i = pl.program_id(0)
    slot = i & 1

    def gather_tile(dest, s):
        def issue(t, carry):
            for k in range(TOP_K):
                row = dest[0, t * TOP_K + k]
                pltpu.make_async_copy(ys_hbm.at[_row_tile(row), :], buf.at[s, k, _row_tile(t), :],
                                      sem.at[s]).start(priority=k % 2)
            return carry

        lax.fori_loop(0, CB_TM, issue, 0, unroll=DMA_UNROLL // TOP_K)

    @pl.when(i == 0)
    def _():
        gather_tile(dest_ref, 0)

    @pl.when(i + 1 < pl.num_programs(0))
    def _():
        gather_tile(dest_next_ref, 1 - slot)

    pltpu.make_async_copy(buf.at[slot], buf.at[slot], sem.at[slot]).wait()

    half = D_MODEL // 2
    gates = gate_ref[...]
    gate_k = [jnp.broadcast_to(gates[:, k:k + 1], (CB_TM, LANES)) for k in range(TOP_K)]
    for c in range(ROW_CHUNKS):
        cols_hi = slice(c * LANES, (c + 1) * LANES)
        cols_lo = slice(half + c * LANES, half + (c + 1) * LANES)
        out_hi = x1_ref[:, cols_hi]
        out_lo = x1_ref[:, cols_lo]
        for k in range(TOP_K):
            hi, lo = _unpack_bf16_pair(buf[slot, k, _row_chunk(CB_TM, c), :])
            out_hi = out_hi + gate_k[k] * hi
            out_lo = out_lo + gate_k[k] * lo
        o_ref[:, cols_hi] = out_hi
        o_ref[:, cols_lo] = out_lo


def _combine(dest3, ys, x1, gates):
    T = x1.shape[0]
    n = T // CB_TM
    dest_spec = lambda f: pl.BlockSpec((None, 1, CB_TM * TOP_K), f, memory_space=pltpu.SMEM)
    return pl.pallas_call(
        _combine_kernel,
        grid=(n,),
        in_specs=[
            dest_spec(lambda i: (i, 0, 0)),
            dest_spec(lambda i: (jnp.minimum(i + 1, n - 1), 0, 0)),
            pl.BlockSpec(memory_space=pl.ANY),
            pl.BlockSpec((CB_TM, D_MODEL), lambda i: (i, 0)),
            pl.BlockSpec((CB_TM, LANES), lambda i: (i, 0)),
        ],
        out_specs=pl.BlockSpec((CB_TM, D_MODEL), lambda i: (i, 0)),
        out_shape=jax.ShapeDtypeStruct((T, D_MODEL), F32),
        scratch_shapes=[pltpu.VMEM((2, TOP_K, CB_TM * SUBLANES, LANES), jnp.uint32),
                        pltpu.SemaphoreType.DMA((2,))],
        compiler_params=_params(("arbitrary",), 32),
        name="moe_combine",
    )(dest3, dest3, ys, x1, gates)


def _group_tables(counts_f32, max_groups):
    i32 = jnp.int32
    counts = counts_f32[0, :N_EXPERTS].astype(i32)
    nb = (counts + ROW_BLOCK - 1) // ROW_BLOCK
    ng = (nb + GROUP_SUB - 1) // GROUP_SUB
    upto = jnp.arange(N_EXPERTS)[None, :] <= jnp.arange(N_EXPERTS)[:, None]
    pad_ends = jnp.sum(jnp.where(upto, nb[None, :], 0), axis=1) * ROW_BLOCK
    pad_starts = pad_ends - nb * ROW_BLOCK
    g_ends = jnp.sum(jnp.where(upto, ng[None, :], 0), axis=1)
    g_starts = g_ends - ng
    n_groups = g_ends[-1]

    g = jnp.arange(max_groups, dtype=i32)
    gg = jnp.minimum(g, n_groups - 1)
    e = jnp.minimum(jnp.sum((g_ends[None, :] <= gg[:, None]).astype(i32), axis=1), N_EXPERTS - 1)
    pick = e[:, None] == jnp.arange(N_EXPERTS)[None, :]
    take = lambda table: jnp.sum(jnp.where(pick, table[None, :], 0), axis=1)
    nb_g, ng_g = take(nb), jnp.maximum(take(ng), 1)
    j = gg - take(g_starts)
    base, rem = nb_g // ng_g, nb_g % ng_g
    nsub = jnp.where(g < n_groups, base + (j < rem).astype(i32), 0)
    start_block = take(pad_starts) // ROW_BLOCK + j * base + jnp.minimum(j, rem)
    return pad_starts.astype(i32), e.astype(i32), start_block.astype(i32), nsub.astype(i32)


def _rope_constants():
    lane = np.arange(LANES)
    half = QK_ROPE // 2
    inv = 1.0 / (ROPE_THETA ** (np.arange(0, QK_ROPE, 2, dtype=np.float32) / QK_ROPE))
    cst = np.zeros((8, LANES), np.float32)
    cst[0, :QK_ROPE] = inv.astype(np.float32)[lane[:QK_ROPE] % half]
    cst[1, :QK_ROPE] = 1.0
    cst[2, :half] = -1.0
    cst[2, half:QK_ROPE] = 1.0
    return jnp.asarray(cst)


def _swap_halves(a):
    half = QK_ROPE // 2
    return jnp.concatenate([a[..., half:], a[..., :half]], axis=-1)


def kernel(x, mem, positions, ln1_g, w_in, w_gate, b_gate, gmlp_ln_g, gmlp_ln_b, gmlp_ws, gmlp_bs, w_pa,
           mla_cq_g, mla_w_uq, mla_ckv_g, mla_w_ukv, mla_qn_g, mla_kn_g, w_pb, mem_ln_g, mem_w_k, mem_w_v,
           mem_qn_g, mem_kn_g, w_pc, w_o, ln2_g, router_w, router_b, moe_w_gu, moe_b_gu, moe_w_down,
           moe_b_down):
    B, S, D = x.shape
    T = B * S
    x2 = x.reshape(T, D)
    for l in range(ln1_g.shape[0]):
        o_kr = 2 * GM_WIDTH + Q_LORA + KV_LORA
        o_qm = o_kr + QK_ROPE
        wi = w_in[l]
        w1 = jnp.concatenate([wi[:, :o_kr], wi[:, o_qm:], w_gate[l]], axis=1).astype(BF16)
        b1 = jnp.concatenate([jnp.zeros((PROJ_COLS,), F32), b_gate[l]])[None, :]
        w_kr = wi[:, o_kr:o_qm]
        wr = jnp.concatenate([w_kr, _swap_halves(w_kr)], axis=1).astype(BF16)

        wq3 = mla_w_uq[l].reshape(Q_LORA, MLA_HEADS, QK_DIM)
        wq = jnp.concatenate([wq3, _swap_halves(wq3[..., QK_NOPE:])], axis=-1)
        wq = wq.reshape(Q_LORA, MLA_HEADS * QK_PAD).astype(BF16)
        wkv = mla_w_ukv[l].astype(BF16)
        gq = jnp.concatenate([mla_qn_g[l], _swap_halves(mla_qn_g[l][QK_NOPE:])])[None, :]
        gk = jnp.concatenate([mla_kn_g[l], _swap_halves(mla_kn_g[l][QK_NOPE:])])[None, :]

        bias_full = jnp.broadcast_to(gmlp_bs[l].T[:, :, None], (GM_CHUNK, GM_GROUPS, GM_CHUNK))
        bias_full = bias_full.reshape(GM_CHUNK, GM_WIDTH)

        rw = jnp.pad(router_w[l], ((0, 0), (0, LANES - N_EXPERTS))).astype(BF16)
        rb = jnp.pad(router_b[l], (0, LANES - N_EXPERTS))[None, :]

        pg, kr = _norm_proj(x2, ln1_g[l][None, :], w1, b1, wr)
        m0 = _gmlp(pg, gmlp_ln_g[l][None, :], gmlp_ln_b[l][None, :], gmlp_ws[l].astype(BF16), bias_full,
                   w_pa[l].astype(BF16))
        q, k, vt = _mla_qkv(pg, kr, positions.reshape(T, 1), mla_cq_g[l][None, :], mla_ckv_g[l][None, :],
                           wq, wkv, gq, gk, _rope_constants(), B, S)
        b_attn = _flash(q, k, vt)
        km, vm = _mem_kv(mem, mem_ln_g[l][None, :], mem_w_k[l].astype(BF16), mem_w_v[l].astype(BF16),
                         mem_kn_g[l][None, :])
        m1 = _mem_attn(pg, km, vm, m0, mem_qn_g[l][None, :], w_pc[l].astype(BF16), S)
        n_rows = T * TOP_K + N_EXPERTS * ROW_BLOCK
        merged, xs_zero = _attn_proj(b_attn, pg, m1, w_pb[l].astype(BF16), n_rows + GROUP_ROWS)
        x1, h2p, code, gates, counts = _merge(merged, x2, w_o[l].astype(BF16), ln2_g[l][None, :], rw, rb)

        max_groups = n_rows // GROUP_ROWS + N_EXPERTS
        pad_starts, g_exp, g_start, g_nsub = _group_tables(counts, max_groups)
        codes = code[:, :TOP_K]
        pick = (codes & (N_EXPERTS - 1))[..., None] == jnp.arange(N_EXPERTS)
        dest = (codes >> EXPERT_BITS) + jnp.sum(jnp.where(pick, pad_starts, 0), axis=-1)
        xs = _dispatch(dest.reshape(T // DP_TM, 1, DP_TM * TOP_K), h2p, xs_zero)
        ys = _experts(g_exp, g_start, g_nsub, xs, moe_w_gu[l], moe_b_gu[l][:, None, :],
                      moe_w_down[l], moe_b_down[l][:, None, :])
        x2 = _combine(dest.reshape(T // CB_TM, 1, CB_TM * TOP_K), ys, x1, gates)
    return x2.reshape(B, S, D)
```

```python
import functools

import numpy as np
import jax
import jax.numpy as jnp
from jax import lax
from jax.experimental import pallas as pl
from jax.experimental.pallas import tpu as pltpu

F32 = jnp.float32
BF16 = jnp.bfloat16

D_MODEL = 2048
GM_WIDTH = 1024
GM_GROUPS = 8
GM_CHUNK = 128
MLA_HEADS = 16
Q_LORA = 512
KV_LORA = 512
QK_NOPE = 128
QK_ROPE = 64
V_DIM = 128
QK_DIM = QK_NOPE + QK_ROPE
QK_PAD = 256
MLA_SCALE = QK_DIM ** -0.5
LOG2_E = 1.4426950408889634
ROPE_THETA = 10000.0
MEM_LEN = 256
MEM_HEADS = 4
MEM_HEAD_DIM = 256
MEM_WIDTH = MEM_HEADS * MEM_HEAD_DIM
MEM_SCALE = MEM_HEAD_DIM ** -0.5
N_EXPERTS = 32
EXPERT_BITS = 5
TOP_K = 4
D_FF = 2048
SWIGLU_LIMIT = 7.0
SWIGLU_ALPHA = 1.702
ROW_BLOCK = 256
EPS = 1e-6
LANES = 128
NEG_INF = float("-inf")

PROJ_COLS = 2 * GM_WIDTH + Q_LORA + KV_LORA + MEM_WIDTH
PG_COLS = PROJ_COLS + 3 * D_MODEL

MIB = 1024 * 1024


def _params(semantics, vmem_mib):
    return pltpu.CompilerParams(dimension_semantics=semantics, vmem_limit_bytes=vmem_mib * MIB)


def _const_spec(shape):
    nd = len(shape)
    return pl.BlockSpec(shape, lambda *_: (0,) * nd, pipeline_mode=pl.Buffered(1))


P1_TM = 1024
P1_TN = 2048


def _norm_proj_kernel(x_ref, g_ref, w_ref, b_ref, wr_ref, o_ref, kr_ref, h_scr, *, n_plain):
    j = pl.program_id(1)

    @pl.when(j == 0)
    def _():
        def body(c, carry):
            rows = pl.ds(pl.multiple_of(c * 128, 128), 128)
            x = x_ref[rows, :]
            ms = jnp.mean(x * x, axis=-1, keepdims=True)
            h_scr[rows, :] = (x * lax.rsqrt(ms + EPS) * g_ref[...]).astype(BF16)
            return carry

        lax.fori_loop(0, P1_TM // 128, body, 0)
        kr_ref[...] = jnp.dot(h_scr[...], wr_ref[...], preferred_element_type=F32)

    acc = jnp.dot(h_scr[...], w_ref[...], preferred_element_type=F32)
    gated = jax.nn.sigmoid(acc + b_ref[...])
    o_ref[...] = jnp.where(j >= n_plain, gated, acc).astype(BF16)


def _norm_proj(x2, ln1_g, w1, b1, wr):
    T = x2.shape[0]
    grid = (T // P1_TM, PG_COLS // P1_TN)
    return pl.pallas_call(
        functools.partial(_norm_proj_kernel, n_plain=PROJ_COLS // P1_TN),
        grid=grid,
        in_specs=[
            pl.BlockSpec((P1_TM, D_MODEL), lambda i, j: (i, 0)),
            pl.BlockSpec((1, D_MODEL), lambda i, j: (0, 0)),
            pl.BlockSpec((D_MODEL, P1_TN), lambda i, j: (0, j)),
            pl.BlockSpec((1, P1_TN), lambda i, j: (0, j)),
            pl.BlockSpec((D_MODEL, LANES), lambda i, j: (0, 0)),
        ],
        out_specs=[
            pl.BlockSpec((P1_TM, P1_TN), lambda i, j: (i, j)),
            pl.BlockSpec((P1_TM, LANES), lambda i, j: (i, 0)),
        ],
        out_shape=[
            jax.ShapeDtypeStruct((T, PG_COLS), BF16),
            jax.ShapeDtypeStruct((T, LANES), F32),
        ],
        scratch_shapes=[pltpu.VMEM((P1_TM, D_MODEL), BF16)],
        compiler_params=_params(("parallel", "arbitrary"), 58),
        name="norm_proj",
    )(x2, ln1_g, w1, b1, wr)


GM_TM = 512


def _gmlp_kernel(u_ref, v_ref, g0_ref, lng_ref, lnb_ref, ws_ref, bias_ref, wpa_ref, o_ref, vb_scr, a_scr):
    v = v_ref[...].astype(F32)
    mu = jnp.mean(v, axis=-1, keepdims=True)
    c = v - mu
    var = jnp.mean(c * c, axis=-1, keepdims=True)
    vb_scr[...] = (c * lax.rsqrt(var + EPS) * lng_ref[...] + lnb_ref[...]).astype(BF16)

    row = lax.broadcasted_iota(jnp.int32, (GM_CHUNK, GM_CHUNK), 0)
    col = lax.broadcasted_iota(jnp.int32, (GM_CHUNK, GM_CHUNK), 1)
    causal = col <= row
    for g in range(GM_GROUPS):
        cols = slice(g * GM_CHUNK, (g + 1) * GM_CHUNK)
        wg = jnp.where(causal, ws_ref[g], jnp.zeros((), BF16))
        chunks = [slice(ch * GM_CHUNK, (ch + 1) * GM_CHUNK) for ch in range(GM_TM // GM_CHUNK)]
        mixed = jnp.dot(wg, jnp.concatenate([vb_scr[rows, cols] for rows in chunks], axis=1),
                        preferred_element_type=F32)
        for ch, rows in enumerate(chunks):
            mix = mixed[:, ch * GM_CHUNK:(ch + 1) * GM_CHUNK] + bias_ref[:, cols]
            a_scr[rows, cols] = (u_ref[rows, cols].astype(F32) * mix).astype(BF16)

    ma = jnp.dot(a_scr[...], wpa_ref[...], preferred_element_type=F32)
    o_ref[...] = (g0_ref[...].astype(F32) * ma).astype(BF16)


def _gmlp(pg, ln_g, ln_b, ws, bias_full, wpa):
    T = pg.shape[0]
    return pl.pallas_call(
        _gmlp_kernel,
        grid=(T // GM_TM,),
        in_specs=[
            pl.BlockSpec((GM_TM, GM_WIDTH), lambda i: (i, 0)),
            pl.BlockSpec((GM_TM, GM_WIDTH), lambda i: (i, 1)),
            pl.BlockSpec((GM_TM, D_MODEL), lambda i: (i, PROJ_COLS // D_MODEL)),
            _const_spec((1, GM_WIDTH)),
            _const_spec((1, GM_WIDTH)),
            _const_spec((GM_GROUPS, GM_CHUNK, GM_CHUNK)),
            _const_spec((GM_CHUNK, GM_WIDTH)),
            _const_spec((GM_WIDTH, D_MODEL)),
        ],
        out_specs=pl.BlockSpec((GM_TM, D_MODEL), lambda i: (i, 0)),
        out_shape=jax.ShapeDtypeStruct((T, D_MODEL), BF16),
        scratch_shapes=[pltpu.VMEM((GM_TM, GM_WIDTH), BF16), pltpu.VMEM((GM_TM, GM_WIDTH), BF16)],
        compiler_params=_params(("parallel",), 40),
        name="gmlp",
    )(pg, pg, pg, ln_g, ln_b, ws, bias_full, wpa)


QKV_TM = 1024
QKV_HEADS = 4
HEAD_BLK = 4
FA_T = 512
FA_QS = 256
V_AUG = V_DIM + 16


def _rope(t, cos, sin):
    return t * cos + pltpu.roll(t, QK_ROPE, 1) * sin


def _row_sum_all_lanes(sq, weights):
    return jnp.dot(sq.astype(BF16), weights.astype(BF16), preferred_element_type=F32)


def _mla_qkv_kernel(cq_ref, ckv_ref, kr_ref, pos_ref, gcq_ref, gckv_ref, wq_ref, wkv_ref, gq_ref, gk_ref,
                    cst_ref, sumw_ref, q_ref, k_ref, vt_ref, cqn_scr, ckvn_scr, cos_scr, sin_scr, krsq_scr,
                    krope_scr):
    hb = pl.program_id(1)
    w_nope = sumw_ref[:QK_NOPE, :]
    w_rope = sumw_ref[QK_NOPE:, :]

    @pl.when(hb == 0)
    def _():
        cq = cq_ref[...].astype(F32)
        ms = jnp.mean(cq * cq, axis=-1, keepdims=True)
        cqn_scr[...] = (cq * lax.rsqrt(ms + EPS) * gcq_ref[...]).astype(BF16)
        ckv = ckv_ref[...].astype(F32)
        ms = jnp.mean(ckv * ckv, axis=-1, keepdims=True)
        ckvn_scr[...] = (ckv * lax.rsqrt(ms + EPS) * gckv_ref[...]).astype(BF16)
        ang = pos_ref[...].astype(F32) * cst_ref[0:1, :]
        cos_scr[...] = jnp.cos(ang) * cst_ref[1:2, :]
        sin_scr[...] = jnp.sin(ang) * cst_ref[2:3, :]
        kr = kr_ref[...]
        krsq_scr[...] = _row_sum_all_lanes(kr * kr, w_rope)
        krope_scr[...] = _rope(kr * gk_ref[:, QK_NOPE:], cos_scr[...], sin_scr[...])

    cos = cos_scr[...]
    sin = sin_scr[...]
    inv_dim = 1.0 / QK_DIM

    yq2 = jnp.dot(cqn_scr[...], wq_ref[...], preferred_element_type=F32)
    ykv2 = jnp.dot(ckvn_scr[...], wkv_ref[...], preferred_element_type=F32)
    for hh in range(QKV_HEADS):
        yq = yq2[:, hh * QK_PAD:(hh + 1) * QK_PAD]
        qn = yq[:, :QK_NOPE]
        qt = yq[:, QK_NOPE:]
        ssq = _row_sum_all_lanes(qn * qn, w_nope) + _row_sum_all_lanes(qt * qt, w_rope)
        rs = lax.rsqrt(ssq * inv_dim + EPS) * (MLA_SCALE * LOG2_E)
        q_ref[hh, :, :QK_NOPE] = (qn * rs * gq_ref[:, :QK_NOPE]).astype(BF16)
        q_ref[hh, :, QK_NOPE:] = _rope(qt * rs * gq_ref[:, QK_NOPE:], cos, sin).astype(BF16)

        ykv = ykv2[:, hh * (QK_NOPE + V_DIM):(hh + 1) * (QK_NOPE + V_DIM)]
        kn = ykv[:, :QK_NOPE]
        ssq = _row_sum_all_lanes(kn * kn, w_nope) + krsq_scr[...]
        rs = lax.rsqrt(ssq * inv_dim + EPS)
        k_ref[hh, :, :QK_NOPE] = (kn * rs * gk_ref[:, :QK_NOPE]).astype(BF16)
        k_ref[hh, :, QK_NOPE:] = (krope_scr[...] * rs).astype(BF16)
        vv = ykv[:, QK_NOPE:]
        for c in range(QKV_TM // FA_T):
            vt_ref[hh, c, :V_DIM, :] = vv[c * FA_T:(c + 1) * FA_T, :].T.astype(BF16)
            extra = lax.broadcasted_iota(jnp.int32, (V_AUG - V_DIM, FA_T), 0)
            vt_ref[hh, c, V_DIM:, :] = (extra == 0).astype(BF16)


def _mla_qkv(pg, kr, pos, gcq, gckv, wq, wkv, gq, gk, cst, B, S):
    T = pg.shape[0]
    sumw = jnp.concatenate([jnp.ones((QK_NOPE, LANES), F32), jnp.full((QK_PAD - QK_NOPE, LANES), 0.5, F32)])
    spb = S // QKV_TM
    cpt = QKV_TM // FA_T
    head_spec = lambda w: pl.BlockSpec((None, QKV_HEADS, QKV_TM, w), lambda i, h: (i // spb, h, i % spb, 0))
    return pl.pallas_call(
        _mla_qkv_kernel,
        grid=(T // QKV_TM, MLA_HEADS // QKV_HEADS),
        in_specs=[
            pl.BlockSpec((QKV_TM, Q_LORA), lambda i, h: (i, 2 * GM_WIDTH // Q_LORA)),
            pl.BlockSpec((QKV_TM, KV_LORA), lambda i, h: (i, 2 * GM_WIDTH // KV_LORA + 1)),
            pl.BlockSpec((QKV_TM, LANES), lambda i, h: (i, 0)),
            pl.BlockSpec((QKV_TM, 1), lambda i, h: (i, 0)),
            pl.BlockSpec((1, Q_LORA), lambda i, h: (0, 0)),
            pl.BlockSpec((1, KV_LORA), lambda i, h: (0, 0)),
            pl.BlockSpec((Q_LORA, QKV_HEADS * QK_PAD), lambda i, h: (0, h)),
            pl.BlockSpec((KV_LORA, QKV_HEADS * (QK_NOPE + V_DIM)), lambda i, h: (0, h)),
            pl.BlockSpec((1, QK_PAD), lambda i, h: (0, 0)),
            pl.BlockSpec((1, QK_PAD), lambda i, h: (0, 0)),
            pl.BlockSpec((8, LANES), lambda i, h: (0, 0)),
            pl.BlockSpec((QK_PAD, LANES), lambda i, h: (0, 0)),
        ],
        out_specs=[
            head_spec(QK_PAD),
            head_spec(QK_PAD),
            pl.BlockSpec((None, QKV_HEADS, cpt, V_AUG, FA_T), lambda i, h: (i // spb, h, i % spb, 0, 0)),
        ],
        out_shape=[
            jax.ShapeDtypeStruct((B, MLA_HEADS, S, QK_PAD), BF16),
            jax.ShapeDtypeStruct((B, MLA_HEADS, S, QK_PAD), BF16),
            jax.ShapeDtypeStruct((B, MLA_HEADS, S // FA_T, V_AUG, FA_T), BF16),
        ],
        scratch_shapes=[
            pltpu.VMEM((QKV_TM, Q_LORA), BF16),
            pltpu.VMEM((QKV_TM, KV_LORA), BF16),
            pltpu.VMEM((QKV_TM, LANES), F32),
            pltpu.VMEM((QKV_TM, LANES), F32),
            pltpu.VMEM((QKV_TM, LANES), F32),
            pltpu.VMEM((QKV_TM, LANES), F32),
        ],
        compiler_params=_params(("parallel", "arbitrary"), 48),
        name="mla_qkv",
    )(pg, pg, kr, pos, gcq, gckv, wq, wkv, gq, gk, cst, sumw)


def _flash_kernel(q_ref, k_ref, vt_ref, o_ref, m_scr, acc_scr):
    qi = pl.program_id(2)
    m_scr[...] = jnp.full_like(m_scr, NEG_INF)
    acc_scr[...] = jnp.zeros_like(acc_scr)

    def step(kb, diagonal):
        ks = pl.ds(pl.multiple_of(kb * FA_T, FA_T), FA_T)
        chains = [(hh, qs) for hh in range(HEAD_BLK) for qs in range(FA_T // FA_QS)]
        scores = []
        for hh, qs in chains:
            qc = slice(qs * FA_QS, (qs + 1) * FA_QS)
            st = lax.dot_general(k_ref[hh, ks, :], q_ref[hh, qc, :], (((1,), (1,)), ((), ())),
                                 preferred_element_type=F32)
            if diagonal:
                krow = lax.broadcasted_iota(jnp.int32, (FA_T, FA_QS), 0)
                qcol = lax.broadcasted_iota(jnp.int32, (FA_T, FA_QS), 1) + qs * FA_QS
                st = jnp.where(krow <= qcol, st, NEG_INF)
            scores.append(st)
        probs = []
        for (hh, qs), st in zip(chains, scores):
            qc = slice(qs * FA_QS, (qs + 1) * FA_QS)
            m_prev = m_scr[hh, :, qc]
            m_new = jnp.maximum(m_prev, jnp.max(st, axis=0, keepdims=True))
            alpha = jnp.exp2(m_prev - m_new)
            p = jnp.exp2(st - m_new)
            m_scr[hh, :, qc] = m_new
            probs.append((alpha, p.astype(BF16)))
        for (hh, qs), (alpha, p) in zip(chains, probs):
            qc = slice(qs * FA_QS, (qs + 1) * FA_QS)
            acc_scr[hh, :, qc] = alpha * acc_scr[hh, :, qc] + jnp.dot(vt_ref[hh, kb], p,
                                                                      preferred_element_type=F32)

    def body(kb, carry):
        step(kb, False)
        return carry

    lax.fori_loop(0, qi, body, 0)
    step(qi, True)
    for hh in range(HEAD_BLK):
        out_t = acc_scr[hh, :V_DIM, :] / acc_scr[hh, V_DIM:V_DIM + 1, :]
        o_ref[:, hh * V_DIM:(hh + 1) * V_DIM] = out_t.T.astype(BF16)


def _flash(q, k, vt):
    B, H, S, _ = q.shape
    nq = S // FA_T
    return pl.pallas_call(
        _flash_kernel,
        grid=(B, H // HEAD_BLK, nq),
        in_specs=[
            pl.BlockSpec((None, HEAD_BLK, FA_T, QK_PAD), lambda b, h, i: (b, h, i, 0)),
            pl.BlockSpec((None, HEAD_BLK, S, QK_PAD), lambda b, h, i: (b, h, 0, 0)),
            pl.BlockSpec((None, HEAD_BLK, nq, V_AUG, FA_T), lambda b, h, i: (b, h, 0, 0, 0)),
        ],
        out_specs=pl.BlockSpec((FA_T, HEAD_BLK * V_DIM), lambda b, h, i: (b * nq + i, h)),
        out_shape=jax.ShapeDtypeStruct((B * S, H * V_DIM), BF16),
        scratch_shapes=[
            pltpu.VMEM((HEAD_BLK, 1, FA_T), F32),
            pltpu.VMEM((HEAD_BLK, V_AUG, FA_T), F32),
        ],
        compiler_params=_params(("parallel", "parallel", "arbitrary"), 40),
        name="mla_attention",
    )(q, k, vt)


def _mem_kv_kernel(mem_ref, g_ref, wk_ref, wv_ref, kng_ref, k_ref, v_ref):
    m = mem_ref[...]
    ms = jnp.mean(m * m, axis=-1, keepdims=True)
    mb = (m * lax.rsqrt(ms + EPS) * g_ref[...]).astype(BF16)
    kk = jnp.dot(mb, wk_ref[...], preferred_element_type=F32)
    for h in range(MEM_HEADS):
        cols = slice(h * MEM_HEAD_DIM, (h + 1) * MEM_HEAD_DIM)
        kh = kk[:, cols]
        ms = jnp.mean(kh * kh, axis=-1, keepdims=True)
        k_ref[:, cols] = (kh * lax.rsqrt(ms + EPS) * kng_ref[...]).astype(BF16)
    v_ref[...] = jnp.dot(mb, wv_ref[...], preferred_element_type=F32).astype(BF16)


def _mem_kv(mem, g, wk, wv, kng):
    B = mem.shape[0]
    return pl.pallas_call(
        _mem_kv_kernel,
        grid=(B,),
        in_specs=[
            pl.BlockSpec((None, MEM_LEN, D_MODEL), lambda b: (b, 0, 0)),
            _const_spec((1, D_MODEL)),
            _const_spec((D_MODEL, MEM_WIDTH)),
            _const_spec((D_MODEL, MEM_WIDTH)),
            _const_spec((1, MEM_HEAD_DIM)),
        ],
        out_specs=[
            pl.BlockSpec((None, MEM_LEN, MEM_WIDTH), lambda b: (b, 0, 0)),
            pl.BlockSpec((None, MEM_LEN, MEM_WIDTH), lambda b: (b, 0, 0)),
        ],
        out_shape=[
            jax.ShapeDtypeStruct((B, MEM_LEN, MEM_WIDTH), BF16),
            jax.ShapeDtypeStruct((B, MEM_LEN, MEM_WIDTH), BF16),
        ],
        compiler_params=_params(("parallel",), 32),
        name="mem_kv",
    )(mem, g, wk, wv, kng)


MA_TM = 512


def _mem_attn_kernel(qm_ref, km_ref, vm_ref, g2_ref, m0_ref, qng_ref, wpc_ref, o_ref, c_scr):
    for h in range(MEM_HEADS):
        cols = slice(h * MEM_HEAD_DIM, (h + 1) * MEM_HEAD_DIM)
        qh = qm_ref[:, cols].astype(F32)
        ms = jnp.mean(qh * qh, axis=-1, keepdims=True)
        qn = (qh * (lax.rsqrt(ms + EPS) * MEM_SCALE) * qng_ref[...]).astype(BF16)
        s = lax.dot_general(qn, km_ref[:, cols], (((1,), (1,)), ((), ())), preferred_element_type=F32)
        e = jnp.exp(s - jnp.max(s, axis=-1, keepdims=True))
        p = (e / jnp.sum(e, axis=-1, keepdims=True)).astype(BF16)
        c_scr[:, cols] = jnp.dot(p, vm_ref[:, cols], preferred_element_type=F32).astype(BF16)
    mc = jnp.dot(c_scr[...], wpc_ref[...], preferred_element_type=F32)
    o_ref[...] = (m0_ref[...].astype(F32) + g2_ref[...].astype(F32) * mc).astype(BF16)


def _mem_attn(pg, km, vm, m0, qng, wpc, S):
    T = pg.shape[0]
    spb = S // MA_TM
    return pl.pallas_call(
        _mem_attn_kernel,
        grid=(T // MA_TM,),
        in_specs=[
            pl.BlockSpec((MA_TM, MEM_WIDTH), lambda i: (i, (2 * GM_WIDTH + Q_LORA + KV_LORA) // MEM_WIDTH)),
            pl.BlockSpec((None, MEM_LEN, MEM_WIDTH), lambda i: (i // spb, 0, 0)),
            pl.BlockSpec((None, MEM_LEN, MEM_WIDTH), lambda i: (i // spb, 0, 0)),
            pl.BlockSpec((MA_TM, D_MODEL), lambda i: (i, PROJ_COLS // D_MODEL + 2)),
            pl.BlockSpec((MA_TM, D_MODEL), lambda i: (i, 0)),
            _const_spec((1, MEM_HEAD_DIM)),
            _const_spec((MEM_WIDTH, D_MODEL)),
        ],
        out_specs=pl.BlockSpec((MA_TM, D_MODEL), lambda i: (i, 0)),
        out_shape=jax.ShapeDtypeStruct((T, D_MODEL), BF16),
        scratch_shapes=[pltpu.VMEM((MA_TM, MEM_WIDTH), BF16)],
        compiler_params=_params(("parallel",), 40),
        name="mem_attention",
    )(pg, km, vm, pg, m0, qng, wpc)


MG_TM = 512


def _pack_bf16_pair(a, b):
    hi = pltpu.bitcast(a.astype(BF16).astype(F32), jnp.uint32)
    lo = pltpu.bitcast(b.astype(BF16).astype(F32), jnp.uint32)
    return hi | (lo >> 16)


def _unpack_bf16_pair(p):
    hi = pltpu.bitcast(p & jnp.uint32(0xFFFF0000), F32)
    lo = pltpu.bitcast(p << 16, F32)
    return hi, lo


SUBLANES = 8
ROW_WORDS = D_MODEL // 2
ROW_CHUNKS = ROW_WORDS // LANES
assert ROW_CHUNKS == SUBLANES


def _row_chunk(n_rows, c):
    return pl.ds(c, n_rows, stride=SUBLANES)


def _store_row_tiles(ref, n_rows, packed):
    for c in range(ROW_CHUNKS):
        ref[_row_chunk(n_rows, c), :] = packed[:, c * LANES:(c + 1) * LANES]


def _attn_proj_kernel(b_ref, g1_ref, m1_ref, wpb_ref, o_ref, zero_ref):
    mb = jnp.dot(b_ref[...], wpb_ref[...], preferred_element_type=F32)
    o_ref[...] = (m1_ref[...].astype(F32) + g1_ref[...].astype(F32) * mb).astype(BF16)
    zero_ref[...] = jnp.zeros_like(zero_ref)


def _attn_proj(b_attn, pg, m1, wpb, n_zero_rows):
    T = m1.shape[0]
    steps = T // MG_TM
    assert n_zero_rows % steps == 0
    zero_block = (n_zero_rows // steps * SUBLANES, LANES)
    row = pl.BlockSpec((MG_TM, D_MODEL), lambda i: (i, 0))
    return pl.pallas_call(
        _attn_proj_kernel,
        grid=(steps,),
        in_specs=[
            row,
            pl.BlockSpec((MG_TM, D_MODEL), lambda i: (i, PROJ_COLS // D_MODEL + 1)),
            row,
            _const_spec((MLA_HEADS * V_DIM, D_MODEL)),
        ],
        out_specs=[row, pl.BlockSpec(zero_block, lambda i: (i, 0))],
        out_shape=[jax.ShapeDtypeStruct((T, D_MODEL), BF16),
                   jax.ShapeDtypeStruct((n_zero_rows * SUBLANES, LANES), jnp.uint32)],
        compiler_params=_params(("parallel",), 56),
        name="attn_proj",
    )(b_attn, pg, m1, wpb)


def _merge_kernel(mg_ref, x_ref, wo_ref, ln2_ref, rw_ref, rb_ref,
                  x1_ref, h2p_ref, code_ref, gate_ref, cnt_ref, cnt_scr):
    @pl.when(pl.program_id(0) == 0)
    def _():
        cnt_scr[...] = jnp.zeros_like(cnt_scr)

    x1 = x_ref[...] + jnp.dot(mg_ref[...], wo_ref[...], preferred_element_type=F32)
    x1_ref[...] = x1
    ms = jnp.mean(x1 * x1, axis=-1, keepdims=True)
    h2 = x1 * lax.rsqrt(ms + EPS) * ln2_ref[...]
    _store_row_tiles(h2p_ref, MG_TM, _pack_bf16_pair(h2[:, :D_MODEL // 2], h2[:, D_MODEL // 2:]))

    logits = jnp.dot(h2.astype(BF16), rw_ref[...], preferred_element_type=F32)
    lane = lax.broadcasted_iota(jnp.int32, (MG_TM, LANES), 1)
    work = jnp.where(lane < N_EXPERTS, logits + rb_ref[...], NEG_INF)
    earlier = (lax.broadcasted_iota(jnp.int32, (MG_TM, MG_TM), 1)
               < lax.broadcasted_iota(jnp.int32, (MG_TM, MG_TM), 0)).astype(BF16)
    base = cnt_scr[...]
    code_out = jnp.zeros((MG_TM, LANES), jnp.int32)
    val_out = jnp.zeros((MG_TM, LANES), F32)
    top = None
    denom = jnp.zeros((MG_TM, 1), F32)
    for k in range(TOP_K):
        mk = jnp.max(work, axis=-1, keepdims=True)
        ik = jnp.min(jnp.where(work == mk, lane, LANES), axis=-1, keepdims=True)
        hit = lane == ik
        work = jnp.where(hit, NEG_INF, work)
        if top is None:
            top = mk
        ek = jnp.exp(mk - top)
        denom = denom + ek
        val_out = jnp.where(lane == k, ek, val_out)
        onehot = hit.astype(BF16)
        prefix = jnp.dot(earlier, onehot, preferred_element_type=F32) + base
        rank = jnp.sum(jnp.where(hit, prefix, 0.0), axis=-1, keepdims=True).astype(jnp.int32)
        code_out = jnp.where(lane == k, rank * N_EXPERTS + ik, code_out)
        base = base + jnp.sum(onehot.astype(F32), axis=0, keepdims=True)
    code_ref[...] = code_out
    gate_ref[...] = val_out / denom
    cnt_scr[...] = base
    cnt_ref[...] = base


def _merge(merged, x2, wo, ln2, rw, rb):
    T = x2.shape[0]
    row = lambda w: pl.BlockSpec((MG_TM, w), lambda i: (i, 0))
    return pl.pallas_call(
        _merge_kernel,
        grid=(T // MG_TM,),
        in_specs=[
            row(D_MODEL),
            row(D_MODEL),
            _const_spec((D_MODEL, D_MODEL)),
            _const_spec((1, D_MODEL)),
            _const_spec((D_MODEL, LANES)),
            _const_spec((1, LANES)),
        ],
        out_specs=[row(D_MODEL), pl.BlockSpec((MG_TM * SUBLANES, LANES), lambda i: (i, 0)), row(LANES),
                   row(LANES), pl.BlockSpec((1, LANES), lambda i: (0, 0))],
        out_shape=[
            jax.ShapeDtypeStruct((T, D_MODEL), F32),
            jax.ShapeDtypeStruct((T * SUBLANES, LANES), jnp.uint32),
            jax.ShapeDtypeStruct((T, LANES), jnp.int32),
            jax.ShapeDtypeStruct((T, LANES), F32),
            jax.ShapeDtypeStruct((1, LANES), F32),
        ],
        scratch_shapes=[pltpu.VMEM((1, LANES), F32)],
        compiler_params=_params(("arbitrary",), 56),
        name="merge_router",
    )(merged, x2, wo, ln2, rw, rb)


DP_TM = 512
DMA_UNROLL = 8


def _row_tile(row):
    return pl.ds(pl.multiple_of(row * SUBLANES, SUBLANES), SUBLANES)


def _dispatch_kernel(dest_ref, h2p_ref, xs_in, xs_hbm, sem):
    del xs_in

    def issue(t, carry):
        src = h2p_ref.at[_row_tile(t), :]
        for k in range(TOP_K):
            row = dest_ref[0, t * TOP_K + k]
            pltpu.make_async_copy(src, xs_hbm.at[_row_tile(row), :], sem).start(priority=k % 2)
        return carry

    lax.fori_loop(0, DP_TM, issue, 0, unroll=DMA_UNROLL // TOP_K)
    all_rows = xs_hbm.at[pl.ds(0, DP_TM * TOP_K * SUBLANES), :]
    pltpu.make_async_copy(all_rows, all_rows, sem).wait()


def _dispatch(dest3, h2p, xs_zero):
    T = h2p.shape[0] // SUBLANES
    return pl.pallas_call(
        _dispatch_kernel,
        grid=(T // DP_TM,),
        in_specs=[
            pl.BlockSpec((None, 1, DP_TM * TOP_K), lambda i: (i, 0, 0), memory_space=pltpu.SMEM),
            pl.BlockSpec((DP_TM * SUBLANES, LANES), lambda i: (i, 0)),
            pl.BlockSpec(memory_space=pl.ANY),
        ],
        out_specs=pl.BlockSpec(memory_space=pl.ANY),
        out_shape=jax.ShapeDtypeStruct(xs_zero.shape, xs_zero.dtype),
        scratch_shapes=[pltpu.SemaphoreType.DMA(())],
        input_output_aliases={2: 0},
        compiler_params=_params(("arbitrary",), 32),
        name="moe_dispatch",
    )(dest3, h2p, xs_zero)


GROUP_SUB = 4
GROUP_ROWS = GROUP_SUB * ROW_BLOCK
FF_TILE = 512
N_FF_TILES = D_FF // FF_TILE
DOWN_CHUNK = 512


def _expert_kernel(ge_ref, gs_ref, gn_ref, xs_in, wg_ref, wu_ref, bg_ref, bu_ref, wd_ref, bd_ref,
                   rows_hbm, xwin, x_scr, acc_scr, stage, wsem, sem):
    del ge_ref, xs_in
    g = pl.program_id(0)
    f = pl.program_id(1)
    ns = gn_ref[g]
    half = D_MODEL // 2
    slot = g & 1

    def win_copy(group, s):
        rows = pl.ds(gs_ref[group] * (ROW_BLOCK * SUBLANES), GROUP_ROWS * SUBLANES)
        return pltpu.make_async_copy(rows_hbm.at[rows, :], xwin.at[s], wsem.at[s])

    def out_copy(s, start_block):
        rows = pl.ds((start_block + s) * (ROW_BLOCK * SUBLANES), ROW_BLOCK * SUBLANES)
        return pltpu.make_async_copy(stage.at[s], rows_hbm.at[rows, :], sem.at[s])

    @pl.when((f == 0) & (ns > 0))
    def _():
        @pl.when(g == 0)
        def _():
            win_copy(0, 0).start()
            acc_scr[...] = jnp.zeros_like(acc_scr)

        win_copy(g, slot).wait()
        @pl.when(g + 1 < pl.num_programs(0))
        def _():
            @pl.when(gn_ref[g + 1] > 0)
            def _():
                win_copy(g + 1, 1 - slot).start()

        for c in range(ROW_CHUNKS):
            hi, lo = _unpack_bf16_pair(xwin[slot, _row_chunk(GROUP_ROWS, c), :])
            x_scr[:, c * LANES:(c + 1) * LANES] = hi.astype(BF16)
            x_scr[:, half + c * LANES:half + (c + 1) * LANES] = lo.astype(BF16)

    for n in range(1, GROUP_SUB + 1):
        @pl.when(ns == n)
        def _(n=n):
            m = n * ROW_BLOCK
            x = x_scr[:m, :]
            gate = jnp.dot(x, wg_ref[...].astype(BF16), preferred_element_type=F32) + bg_ref[...]
            up = jnp.dot(x, wu_ref[...].astype(BF16), preferred_element_type=F32) + bu_ref[...]
            gate = jnp.minimum(gate, SWIGLU_LIMIT)
            up = jnp.clip(up, -SWIGLU_LIMIT, SWIGLU_LIMIT)
            glu = gate * jax.nn.sigmoid(gate * SWIGLU_ALPHA)
            act = ((up + 1.0) * glu).astype(BF16)
            for c in range(D_MODEL // DOWN_CHUNK):
                cols = slice(c * DOWN_CHUNK, (c + 1) * DOWN_CHUNK)
                start = jnp.where(f == 0, jnp.broadcast_to(bd_ref[:, cols], (m, DOWN_CHUNK)), acc_scr[:m, cols])
                acc_scr[:m, cols] = start + jnp.dot(act, wd_ref[:, cols].astype(BF16),
                                                    preferred_element_type=F32)

    @pl.when(f == N_FF_TILES - 1)
    def _():
        @pl.when(g > 0)
        def _():
            prev = gn_ref[g - 1]
            for s in range(GROUP_SUB):
                @pl.when(s < prev)
                def _(s=s):
                    out_copy(s, 0).wait()

        for s in range(GROUP_SUB):
            @pl.when(s < ns)
            def _(s=s):
                y = acc_scr[s * ROW_BLOCK:(s + 1) * ROW_BLOCK, :]
                _store_row_tiles(stage.at[s], ROW_BLOCK, _pack_bf16_pair(y[:, :half], y[:, half:]))
                out_copy(s, gs_ref[g]).start()

        @pl.when(g == pl.num_programs(0) - 1)
        def _():
            for s in range(GROUP_SUB):
                @pl.when(s < ns)
                def _(s=s):
                    out_copy(s, 0).wait()


def _experts(g_exp, g_start, g_nsub, xs, wgu, bgu, wd, bd):
    def ff(f, gn, g):
        return jnp.where(gn[g] > 0, f, N_FF_TILES - 1)

    grid_spec = pltpu.PrefetchScalarGridSpec(
        num_scalar_prefetch=3,
        grid=(g_exp.shape[0], N_FF_TILES),
        in_specs=[
            pl.BlockSpec(memory_space=pl.ANY),
            pl.BlockSpec((None, D_MODEL, FF_TILE), lambda g, f, ge, gs, gn: (ge[g], 0, ff(f, gn, g))),
            pl.BlockSpec((None, D_MODEL, FF_TILE),
                         lambda g, f, ge, gs, gn: (ge[g], 0, N_FF_TILES + ff(f, gn, g))),
            pl.BlockSpec((None, 1, FF_TILE), lambda g, f, ge, gs, gn: (ge[g], 0, ff(f, gn, g))),
            pl.BlockSpec((None, 1, FF_TILE), lambda g, f, ge, gs, gn: (ge[g], 0, N_FF_TILES + ff(f, gn, g))),
            pl.BlockSpec((None, FF_TILE, D_MODEL), lambda g, f, ge, gs, gn: (ge[g], ff(f, gn, g), 0)),
            pl.BlockSpec((None, 1, D_MODEL), lambda g, f, ge, gs, gn: (ge[g], 0, 0)),
        ],
        out_specs=pl.BlockSpec(memory_space=pl.ANY),
        scratch_shapes=[
            pltpu.VMEM((2, GROUP_ROWS * SUBLANES, LANES), jnp.uint32),
            pltpu.VMEM((GROUP_ROWS, D_MODEL), BF16),
            pltpu.VMEM((GROUP_ROWS, D_MODEL), F32),
            pltpu.VMEM((GROUP_SUB, ROW_BLOCK * SUBLANES, LANES), jnp.uint32),
            pltpu.SemaphoreType.DMA((2,)),
            pltpu.SemaphoreType.DMA((GROUP_SUB,)),
        ],
    )
    return pl.pallas_call(
        _expert_kernel,
        grid_spec=grid_spec,
        out_shape=jax.ShapeDtypeStruct(xs.shape, jnp.uint32),
        input_output_aliases={3: 0},
        compiler_params=_params(("arbitrary", "arbitrary"), 58),
        name="moe_experts",
    )(g_exp, g_start, g_nsub, xs, wgu, wgu, bgu, bgu, wd, bd)


CB_TM = 256


def _combine_kernel(dest_ref, dest_next_ref, ys_hbm, x1_ref, gate_ref, o_ref, buf, sem):
    i = pl.program_id(0)
    slot = i & 1

    def gather_tile(dest, s):
        def issue(t, carry):
            for k in range(TOP_K):
                row = dest[0, t * TOP_K + k]
                pltpu.make_async_copy(ys_hbm.at[_row_tile(row), :], buf.at[s, k, _row_tile(t), :],
                                      sem.at[s]).start(priority=k % 2)
            return carry

        lax.fori_loop(0, CB_TM, issue, 0, unroll=DMA_UNROLL // TOP_K)

    @pl.when(i == 0)
    def _():
        gather_tile(dest_ref, 0)

    @pl.when(i + 1 < pl.num_programs(0))
    def _():
        gather_tile(dest_next_ref, 1 - slot)

    pltpu.make_async_copy(buf.at[slot], buf.at[slot], sem.at[slot]).wait()

    half = D_MODEL // 2
    gates = gate_ref[...]
    gate_k = [jnp.broadcast_to(gates[:, k:k + 1], (CB_TM, LANES)) for k in range(TOP_K)]
    for c in range(ROW_CHUNKS):
        cols_hi = slice(c * LANES, (c + 1) * LANES)
        cols_lo = slice(half + c * LANES, half + (c + 1) * LANES)
        out_hi = x1_ref[:, cols_hi]
        out_lo = x1_ref[:, cols_lo]
        for k in range(TOP_K):
            hi, lo = _unpack_bf16_pair(buf[slot, k, _row_chunk(CB_TM, c), :])
            out_hi = out_hi + gate_k[k] * hi
            out_lo = out_lo + gate_k[k] * lo
        o_ref[:, cols_hi] = out_hi
        o_ref[:, cols_lo] = out_lo


def _combine(dest3, ys, x1, gates):
    T = x1.shape[0]
    n = T // CB_TM
    dest_spec = lambda f: pl.BlockSpec((None, 1, CB_TM * TOP_K), f, memory_space=pltpu.SMEM)
    return pl.pallas_call(
        _combine_kernel,
        grid=(n,),
        in_specs=[
            dest_spec(lambda i: (i, 0, 0)),
            dest_spec(lambda i: (jnp.minimum(i + 1, n - 1), 0, 0)),
            pl.BlockSpec(memory_space=pl.ANY),
            pl.BlockSpec((CB_TM, D_MODEL), lambda i: (i, 0)),
            pl.BlockSpec((CB_TM, LANES), lambda i: (i, 0)),
        ],
        out_specs=pl.BlockSpec((CB_TM, D_MODEL), lambda i: (i, 0)),
        out_shape=jax.ShapeDtypeStruct((T, D_MODEL), F32),
        scratch_shapes=[pltpu.VMEM((2, TOP_K, CB_TM * SUBLANES, LANES), jnp.uint32),
                        pltpu.SemaphoreType.DMA((2,))],
        compiler_params=_params(("arbitrary",), 32),
        name="moe_combine",
    )(dest3, dest3, ys, x1, gates)


def _group_tables(counts_f32, max_groups):
    i32 = jnp.int32
    counts = counts_f32[0, :N_EXPERTS].astype(i32)
    nb = (counts + ROW_BLOCK - 1) // ROW_BLOCK
    ng = (nb + GROUP_SUB - 1) // GROUP_SUB
    upto = jnp.arange(N_EXPERTS)[None, :] <= jnp.arange(N_EXPERTS)[:, None]
    pad_ends = jnp.sum(jnp.where(upto, nb[None, :], 0), axis=1) * ROW_BLOCK
    pad_starts = pad_ends - nb * ROW_BLOCK
    g_ends = jnp.sum(jnp.where(upto, ng[None, :], 0), axis=1)
    g_starts = g_ends - ng
    n_groups = g_ends[-1]

    g = jnp.arange(max_groups, dtype=i32)
    gg = jnp.minimum(g, n_groups - 1)
    e = jnp.minimum(jnp.sum((g_ends[None, :] <= gg[:, None]).astype(i32), axis=1), N_EXPERTS - 1)
    pick = e[:, None] == jnp.arange(N_EXPERTS)[None, :]
    take = lambda table: jnp.sum(jnp.where(pick, table[None, :], 0), axis=1)
    nb_g, ng_g = take(nb), jnp.maximum(take(ng), 1)
    j = gg - take(g_starts)
    base, rem = nb_g // ng_g, nb_g % ng_g
    nsub = jnp.where(g < n_groups, base + (j < rem).astype(i32), 0)
    start_block = take(pad_starts) // ROW_BLOCK + j * base + jnp.minimum(j, rem)
    return pad_starts.astype(i32), e.astype(i32), start_block.astype(i32), nsub.astype(i32)


def _rope_constants():
    lane = np.arange(LANES)
    half = QK_ROPE // 2
    inv = 1.0 / (ROPE_THETA ** (np.arange(0, QK_ROPE, 2, dtype=np.float32) / QK_ROPE))
    cst = np.zeros((8, LANES), np.float32)
    cst[0, :QK_ROPE] = inv.astype(np.float32)[lane[:QK_ROPE] % half]
    cst[1, :QK_ROPE] = 1.0
    cst[2, :half] = -1.0
    cst[2, half:QK_ROPE] = 1.0
    return jnp.asarray(cst)


def _swap_halves(a):
    half = QK_ROPE // 2
    return jnp.concatenate([a[..., half:], a[..., :half]], axis=-1)


def kernel(x, mem, positions, ln1_g, w_in, w_gate, b_gate, gmlp_ln_g, gmlp_ln_b, gmlp_ws, gmlp_bs, w_pa,
           mla_cq_g, mla_w_uq, mla_ckv_g, mla_w_ukv, mla_qn_g, mla_kn_g, w_pb, mem_ln_g, mem_w_k, mem_w_v,
           mem_qn_g, mem_kn_g, w_pc, w_o, ln2_g, router_w, router_b, moe_w_gu, moe_b_gu, moe_w_down,
           moe_b_down):
    B, S, D = x.shape
    T = B * S
    x2 = x.reshape(T, D)
    for l in range(ln1_g.shape[0]):
        o_kr = 2 * GM_WIDTH + Q_LORA + KV_LORA
        o_qm = o_kr + QK_ROPE
        wi = w_in[l]
        w1 = jnp.concatenate([wi[:, :o_kr], wi[:, o_qm:], w_gate[l]], axis=1).astype(BF16)
        b1 = jnp.concatenate([jnp.zeros((PROJ_COLS,), F32), b_gate[l]])[None, :]
        w_kr = wi[:, o_kr:o_qm]
        wr = jnp.concatenate([w_kr, _swap_halves(w_kr)], axis=1).astype(BF16)

        wq3 = mla_w_uq[l].reshape(Q_LORA, MLA_HEADS, QK_DIM)
        wq = jnp.concatenate([wq3, _swap_halves(wq3[..., QK_NOPE:])], axis=-1)
        wq = wq.reshape(Q_LORA, MLA_HEADS * QK_PAD).astype(BF16)
        wkv = mla_w_ukv[l].astype(BF16)
        gq = jnp.concatenate([mla_qn_g[l], _swap_halves(mla_qn_g[l][QK_NOPE:])])[None, :]
        gk = jnp.concatenate([mla_kn_g[l], _swap_halves(mla_kn_g[l][QK_NOPE:])])[None, :]

        bias_full = jnp.broadcast_to(gmlp_bs[l].T[:, :, None], (GM_CHUNK, GM_GROUPS, GM_CHUNK))
        bias_full = bias_full.reshape(GM_CHUNK, GM_WIDTH)

        rw = jnp.pad(router_w[l], ((0, 0), (0, LANES - N_EXPERTS))).astype(BF16)
        rb = jnp.pad(router_b[l], (0, LANES - N_EXPERTS))[None, :]

        pg, kr = _norm_proj(x2, ln1_g[l][None, :], w1, b1, wr)
        m0 = _gmlp(pg, gmlp_ln_g[l][None, :], gmlp_ln_b[l][None, :], gmlp_ws[l].astype(BF16), bias_full,
                   w_pa[l].astype(BF16))
        q, k, vt = _mla_qkv(pg, kr, positions.reshape(T, 1), mla_cq_g[l][None, :], mla_ckv_g[l][None, :],
                           wq, wkv, gq, gk, _rope_constants(), B, S)
        b_attn = _flash(q, k, vt)
        km, vm = _mem_kv(mem, mem_ln_g[l][None, :], mem_w_k[l].astype(BF16), mem_w_v[l].astype(BF16),
                         mem_kn_g[l][None, :])
        m1 = _mem_attn(pg, km, vm, m0, mem_qn_g[l][None, :], w_pc[l].astype(BF16), S)
        n_rows = T * TOP_K + N_EXPERTS * ROW_BLOCK
        merged, xs_zero = _attn_proj(b_attn, pg, m1, w_pb[l].astype(BF16), n_rows + GROUP_ROWS)
        x1, h2p, code, gates, counts = _merge(merged, x2, w_o[l].astype(BF16), ln2_g[l][None, :], rw, rb)

        max_groups = n_rows // GROUP_ROWS + N_EXPERTS
        pad_starts, g_exp, g_start, g_nsub = _group_tables(counts, max_groups)
        codes = code[:, :TOP_K]
        pick = (codes & (N_EXPERTS - 1))[..., None] == jnp.arange(N_EXPERTS)
        dest = (codes >> EXPERT_BITS) + jnp.sum(jnp.where(pick, pad_starts, 0), axis=-1)
        xs = _dispatch(dest.reshape(T // DP_TM, 1, DP_TM * TOP_K), h2p, xs_zero)
        ys = _experts(g_exp, g_start, g_nsub, xs, moe_w_gu[l], moe_b_gu[l][:, None, :],
                      moe_w_down[l], moe_b_down[l][:, None, :])
        x2 = _combine(dest.reshape(T // CB_TM, 1, CB_TM * TOP_K), ys, x1, gates)
    return x2.reshape(B, S, D)
```

```python
import functools

import numpy as np
import jax
import jax.numpy as jnp
from jax import lax
from jax.experimental import pallas as pl
from jax.experimental.pallas import tpu as pltpu

F32 = jnp.float32
BF16 = jnp.bfloat16

D_MODEL = 2048
GM_WIDTH = 1024
GM_GROUPS = 8
GM_CHUNK = 128
MLA_HEADS = 16
Q_LORA = 512
KV_LORA = 512
QK_NOPE = 128
QK_ROPE = 64
V_DIM = 128
QK_DIM = QK_NOPE + QK_ROPE
QK_PAD = 256
MLA_SCALE = QK_DIM ** -0.5
LOG2_E = 1.4426950408889634
ROPE_THETA = 10000.0
MEM_LEN = 256
MEM_HEADS = 4
MEM_HEAD_DIM = 256
MEM_WIDTH = MEM_HEADS * MEM_HEAD_DIM
MEM_SCALE = MEM_HEAD_DIM ** -0.5
N_EXPERTS = 32
EXPERT_BITS = 5
TOP_K = 4
D_FF = 2048
SWIGLU_LIMIT = 7.0
SWIGLU_ALPHA = 1.702
ROW_BLOCK = 256
EPS = 1e-6
LANES = 128
NEG_INF = float("-inf")

PROJ_COLS = 2 * GM_WIDTH + Q_LORA + KV_LORA + MEM_WIDTH
PG_COLS = PROJ_COLS + 3 * D_MODEL

MIB = 1024 * 1024


def _params(semantics, vmem_mib):
    return pltpu.CompilerParams(dimension_semantics=semantics, vmem_limit_bytes=vmem_mib * MIB)


def _const_spec(shape):
    nd = len(shape)
    return pl.BlockSpec(shape, lambda *_: (0,) * nd, pipeline_mode=pl.Buffered(1))


P1_TM = 1024
P1_TN = 2048


def _norm_proj_kernel(x_ref, g_ref, w_ref, b_ref, wr_ref, o_ref, kr_ref, h_scr, *, n_plain):
    j = pl.program_id(1)

    @pl.when(j == 0)
    def _():
        def body(c, carry):
            rows = pl.ds(pl.multiple_of(c * 128, 128), 128)
            x = x_ref[rows, :]
            ms = jnp.mean(x * x, axis=-1, keepdims=True)
            h_scr[rows, :] = (x * lax.rsqrt(ms + EPS) * g_ref[...]).astype(BF16)
            return carry

        lax.fori_loop(0, P1_TM // 128, body, 0)
        kr_ref[...] = jnp.dot(h_scr[...], wr_ref[...], preferred_element_type=F32)

    acc = jnp.dot(h_scr[...], w_ref[...], preferred_element_type=F32)
    gated = jax.nn.sigmoid(acc + b_ref[...])
    o_ref[...] = jnp.where(j >= n_plain, gated, acc).astype(BF16)


def _norm_proj(x2, ln1_g, w1, b1, wr):
    T = x2.shape[0]
    grid = (T // P1_TM, PG_COLS // P1_TN)
    return pl.pallas_call(
        functools.partial(_norm_proj_kernel, n_plain=PROJ_COLS // P1_TN),
        grid=grid,
        in_specs=[
            pl.BlockSpec((P1_TM, D_MODEL), lambda i, j: (i, 0)),
            pl.BlockSpec((1, D_MODEL), lambda i, j: (0, 0)),
            pl.BlockSpec((D_MODEL, P1_TN), lambda i, j: (0, j)),
            pl.BlockSpec((1, P1_TN), lambda i, j: (0, j)),
            pl.BlockSpec((D_MODEL, LANES), lambda i, j: (0, 0)),
        ],
        out_specs=[
            pl.BlockSpec((P1_TM, P1_TN), lambda i, j: (i, j)),
            pl.BlockSpec((P1_TM, LANES), lambda i, j: (i, 0)),
        ],
        out_shape=[
            jax.ShapeDtypeStruct((T, PG_COLS), BF16),
            jax.ShapeDtypeStruct((T, LANES), F32),
        ],
        scratch_shapes=[pltpu.VMEM((P1_TM, D_MODEL), BF16)],
        compiler_params=_params(("parallel", "arbitrary"), 58),
        name="norm_proj",
    )(x2, ln1_g, w1, b1, wr)


GM_TM = 512


def _gmlp_kernel(u_ref, v_ref, g0_ref, lng_ref, lnb_ref, ws_ref, bias_ref, wpa_ref, o_ref, vb_scr, a_scr):
    v = v_ref[...].astype(F32)
    mu = jnp.mean(v, axis=-1, keepdims=True)
    c = v - mu
    var = jnp.mean(c * c, axis=-1, keepdims=True)
    vb_scr[...] = (c * lax.rsqrt(var + EPS) * lng_ref[...] + lnb_ref[...]).astype(BF16)

    row = lax.broadcasted_iota(jnp.int32, (GM_CHUNK, GM_CHUNK), 0)
    col = lax.broadcasted_iota(jnp.int32, (GM_CHUNK, GM_CHUNK), 1)
    causal = col <= row
    for g in range(GM_GROUPS):
        cols = slice(g * GM_CHUNK, (g + 1) * GM_CHUNK)
        wg = jnp.where(causal, ws_ref[g], jnp.zeros((), BF16))
        chunks = [slice(ch * GM_CHUNK, (ch + 1) * GM_CHUNK) for ch in range(GM_TM // GM_CHUNK)]
        mixed = jnp.dot(wg, jnp.concatenate([vb_scr[rows, cols] for rows in chunks], axis=1),
                        preferred_element_type=F32)
        for ch, rows in enumerate(chunks):
            mix = mixed[:, ch * GM_CHUNK:(ch + 1) * GM_CHUNK] + bias_ref[:, cols]
            a_scr[rows, cols] = (u_ref[rows, cols].astype(F32) * mix).astype(BF16)

    ma = jnp.dot(a_scr[...], wpa_ref[...], preferred_element_type=F32)
    o_ref[...] = (g0_ref[...].astype(F32) * ma).astype(BF16)


def _gmlp(pg, ln_g, ln_b, ws, bias_full, wpa):
    T = pg.shape[0]
    return pl.pallas_call(
        _gmlp_kernel,
        grid=(T // GM_TM,),
        in_specs=[
            pl.BlockSpec((GM_TM, GM_WIDTH), lambda i: (i, 0)),
            pl.BlockSpec((GM_TM, GM_WIDTH), lambda i: (i, 1)),
            pl.BlockSpec((GM_TM, D_MODEL), lambda i: (i, PROJ_COLS // D_MODEL)),
            _const_spec((1, GM_WIDTH)),
            _const_spec((1, GM_WIDTH)),
            _const_spec((GM_GROUPS, GM_CHUNK, GM_CHUNK)),
            _const_spec((GM_CHUNK, GM_WIDTH)),
            _const_spec((GM_WIDTH, D_MODEL)),
        ],
        out_specs=pl.BlockSpec((GM_TM, D_MODEL), lambda i: (i, 0)),
        out_shape=jax.ShapeDtypeStruct((T, D_MODEL), BF16),
        scratch_shapes=[pltpu.VMEM((GM_TM, GM_WIDTH), BF16), pltpu.VMEM((GM_TM, GM_WIDTH), BF16)],
        compiler_params=_params(("parallel",), 40),
        name="gmlp",
    )(pg, pg, pg, ln_g, ln_b, ws, bias_full, wpa)


QKV_TM = 1024
QKV_HEADS = 4
HEAD_BLK = 4
FA_T = 512
FA_QS = 256
V_AUG = V_DIM + 16


def _rope(t, cos, sin):
    return t * cos + pltpu.roll(t, QK_ROPE, 1) * sin


def _row_sum_all_lanes(sq, weights):
    return jnp.dot(sq.astype(BF16), weights.astype(BF16), preferred_element_type=F32)


def _mla_qkv_kernel(cq_ref, ckv_ref, kr_ref, pos_ref, gcq_ref, gckv_ref, wq_ref, wkv_ref, gq_ref, gk_ref,
                    cst_ref, sumw_ref, q_ref, k_ref, vt_ref, cqn_scr, ckvn_scr, cos_scr, sin_scr, krsq_scr,
                    krope_scr):
    hb = pl.program_id(1)
    w_nope = sumw_ref[:QK_NOPE, :]
    w_rope = sumw_ref[QK_NOPE:, :]

    @pl.when(hb == 0)
    def _():
        cq = cq_ref[...].astype(F32)
        ms = jnp.mean(cq * cq, axis=-1, keepdims=True)
        cqn_scr[...] = (cq * lax.rsqrt(ms + EPS) * gcq_ref[...]).astype(BF16)
        ckv = ckv_ref[...].astype(F32)
        ms = jnp.mean(ckv * ckv, axis=-1, keepdims=True)
        ckvn_scr[...] = (ckv * lax.rsqrt(ms + EPS) * gckv_ref[...]).astype(BF16)
        ang = pos_ref[...].astype(F32) * cst_ref[0:1, :]
        cos_scr[...] = jnp.cos(ang) * cst_ref[1:2, :]
        sin_scr[...] = jnp.sin(ang) * cst_ref[2:3, :]
        kr = kr_ref[...]
        krsq_scr[...] = _row_sum_all_lanes(kr * kr, w_rope)
        krope_scr[...] = _rope(kr * gk_ref[:, QK_NOPE:], cos_scr[...], sin_scr[...])

    cos = cos_scr[...]
    sin = sin_scr[...]
    inv_dim = 1.0 / QK_DIM

    yq2 = jnp.dot(cqn_scr[...], wq_ref[...], preferred_element_type=F32)
    ykv2 = jnp.dot(ckvn_scr[...], wkv_ref[...], preferred_element_type=F32)
    for hh in range(QKV_HEADS):
        yq = yq2[:, hh * QK_PAD:(hh + 1) * QK_PAD]
        qn = yq[:, :QK_NOPE]
        qt = yq[:, QK_NOPE:]
        ssq = _row_sum_all_lanes(qn * qn, w_nope) + _row_sum_all_lanes(qt * qt, w_rope)
        rs = lax.rsqrt(ssq * inv_dim + EPS) * (MLA_SCALE * LOG2_E)
        q_ref[hh, :, :QK_NOPE] = (qn * rs * gq_ref[:, :QK_NOPE]).astype(BF16)
        q_ref[hh, :, QK_NOPE:] = _rope(qt * rs * gq_ref[:, QK_NOPE:], cos, sin).astype(BF16)

        ykv = ykv2[:, hh * (QK_NOPE + V_DIM):(hh + 1) * (QK_NOPE + V_DIM)]
        kn = ykv[:, :QK_NOPE]
        ssq = _row_sum_all_lanes(kn * kn, w_nope) + krsq_scr[...]
        rs = lax.rsqrt(ssq * inv_dim + EPS)
        k_ref[hh, :, :QK_NOPE] = (kn * rs * gk_ref[:, :QK_NOPE]).astype(BF16)
        k_ref[hh, :, QK_NOPE:] = (krope_scr[...] * rs).astype(BF16)
        vv = ykv[:, QK_NOPE:]
        for c in range(QKV_TM // FA_T):
            vt_ref[hh, c, :V_DIM, :] = vv[c * FA_T:(c + 1) * FA_T, :].T.astype(BF16)
            extra = lax.broadcasted_iota(jnp.int32, (V_AUG - V_DIM, FA_T), 0)
            vt_ref[hh, c, V_DIM:, :] = (extra == 0).astype(BF16)


def _mla_qkv(pg, kr, pos, gcq, gckv, wq, wkv, gq, gk, cst, B, S):
    T = pg.shape[0]
    sumw = jnp.concatenate([jnp.ones((QK_NOPE, LANES), F32), jnp.full((QK_PAD - QK_NOPE, LANES), 0.5, F32)])
    spb = S // QKV_TM
    cpt = QKV_TM // FA_T
    head_spec = lambda w: pl.BlockSpec((None, QKV_HEADS, QKV_TM, w), lambda i, h: (i // spb, h, i % spb, 0))
    return pl.pallas_call(
        _mla_qkv_kernel,
        grid=(T // QKV_TM, MLA_HEADS // QKV_HEADS),
        in_specs=[
            pl.BlockSpec((QKV_TM, Q_LORA), lambda i, h: (i, 2 * GM_WIDTH // Q_LORA)),
            pl.BlockSpec((QKV_TM, KV_LORA), lambda i, h: (i, 2 * GM_WIDTH // KV_LORA + 1)),
            pl.BlockSpec((QKV_TM, LANES), lambda i, h: (i, 0)),
            pl.BlockSpec((QKV_TM, 1), lambda i, h: (i, 0)),
            pl.BlockSpec((1, Q_LORA), lambda i, h: (0, 0)),
            pl.BlockSpec((1, KV_LORA), lambda i, h: (0, 0)),
            pl.BlockSpec((Q_LORA, QKV_HEADS * QK_PAD), lambda i, h: (0, h)),
            pl.BlockSpec((KV_LORA, QKV_HEADS * (QK_NOPE + V_DIM)), lambda i, h: (0, h)),
            pl.BlockSpec((1, QK_PAD), lambda i, h: (0, 0)),
            pl.BlockSpec((1, QK_PAD), lambda i, h: (0, 0)),
            pl.BlockSpec((8, LANES), lambda i, h: (0, 0)),
            pl.BlockSpec((QK_PAD, LANES), lambda i, h: (0, 0)),
        ],
        out_specs=[
            head_spec(QK_PAD),
            head_spec(QK_PAD),
            pl.BlockSpec((None, QKV_HEADS, cpt, V_AUG, FA_T), lambda i, h: (i // spb, h, i % spb, 0, 0)),
        ],
        out_shape=[
            jax.ShapeDtypeStruct((B, MLA_HEADS, S, QK_PAD), BF16),
            jax.ShapeDtypeStruct((B, MLA_HEADS, S, QK_PAD), BF16),
            jax.ShapeDtypeStruct((B, MLA_HEADS, S // FA_T, V_AUG, FA_T), BF16),
        ],
        scratch_shapes=[
            pltpu.VMEM((QKV_TM, Q_LORA), BF16),
            pltpu.VMEM((QKV_TM, KV_LORA), BF16),
            pltpu.VMEM((QKV_TM, LANES), F32),
            pltpu.VMEM((QKV_TM, LANES), F32),
            pltpu.VMEM((QKV_TM, LANES), F32),
            pltpu.VMEM((QKV_TM, LANES), F32),
        ],
        compiler_params=_params(("parallel", "arbitrary"), 48),
        name="mla_qkv",
    )(pg, pg, kr, pos, gcq, gckv, wq, wkv, gq, gk, cst, sumw)


def _flash_kernel(q_ref, k_ref, vt_ref, o_ref, m_scr, acc_scr):
    qi = pl.program_id(2)
    m_scr[...] = jnp.full_like(m_scr, NEG_INF)
    acc_scr[...] = jnp.zeros_like(acc_scr)

    def step(kb, diagonal):
        ks = pl.ds(pl.multiple_of(kb * FA_T, FA_T), FA_T)
        chains = [(hh, qs) for hh in range(HEAD_BLK) for qs in range(FA_T // FA_QS)]
        scores = []
        for hh, qs in chains:
            qc = slice(qs * FA_QS, (qs + 1) * FA_QS)
            st = lax.dot_general(k_ref[hh, ks, :], q_ref[hh, qc, :], (((1,), (1,)), ((), ())),
                                 preferred_element_type=F32)
            if diagonal:
                krow = lax.broadcasted_iota(jnp.int32, (FA_T, FA_QS), 0)
                qcol = lax.broadcasted_iota(jnp.int32, (FA_T, FA_QS), 1) + qs * FA_QS
                st = jnp.where(krow <= qcol, st, NEG_INF)
            scores.append(st)
        probs = []
        for (hh, qs), st in zip(chains, scores):
            qc = slice(qs * FA_QS, (qs + 1) * FA_QS)
            m_prev = m_scr[hh, :, qc]
            m_new = jnp.maximum(m_prev, jnp.max(st, axis=0, keepdims=True))
            alpha = jnp.exp2(m_prev - m_new)
            p = jnp.exp2(st - m_new)
            m_scr[hh, :, qc] = m_new
            probs.append((alpha, p.astype(BF16)))
        for (hh, qs), (alpha, p) in zip(chains, probs):
            qc = slice(qs * FA_QS, (qs + 1) * FA_QS)
            acc_scr[hh, :, qc] = alpha * acc_scr[hh, :, qc] + jnp.dot(vt_ref[hh, kb], p,
                                                                      preferred_element_type=F32)

    def body(kb, carry):
        step(kb, False)
        return carry

    lax.fori_loop(0, qi, body, 0)
    step(qi, True)
    for hh in range(HEAD_BLK):
        out_t = acc_scr[hh, :V_DIM, :] / acc_scr[hh, V_DIM:V_DIM + 1, :]
        o_ref[:, hh * V_DIM:(hh + 1) * V_DIM] = out_t.T.astype(BF16)


def _flash(q, k, vt):
    B, H, S, _ = q.shape
    nq = S // FA_T
    return pl.pallas_call(
        _flash_kernel,
        grid=(B, H // HEAD_BLK, nq),
        in_specs=[
            pl.BlockSpec((None, HEAD_BLK, FA_T, QK_PAD), lambda b, h, i: (b, h, i, 0)),
            pl.BlockSpec((None, HEAD_BLK, S, QK_PAD), lambda b, h, i: (b, h, 0, 0)),
            pl.BlockSpec((None, HEAD_BLK, nq, V_AUG, FA_T), lambda b, h, i: (b, h, 0, 0, 0)),
        ],
        out_specs=pl.BlockSpec((FA_T, HEAD_BLK * V_DIM), lambda b, h, i: (b * nq + i, h)),
        out_shape=jax.ShapeDtypeStruct((B * S, H * V_DIM), BF16),
        scratch_shapes=[
            pltpu.VMEM((HEAD_BLK, 1, FA_T), F32),
            pltpu.VMEM((HEAD_BLK, V_AUG, FA_T), F32),
        ],
        compiler_params=_params(("parallel", "parallel", "arbitrary"), 40),
        name="mla_attention",
    )(q, k, vt)


def _mem_kv_kernel(mem_ref, g_ref, wk_ref, wv_ref, kng_ref, k_ref, v_ref):
    m = mem_ref[...]
    ms = jnp.mean(m * m, axis=-1, keepdims=True)
    mb = (m * lax.rsqrt(ms + EPS) * g_ref[...]).astype(BF16)
    kk = jnp.dot(mb, wk_ref[...], preferred_element_type=F32)
    for h in range(MEM_HEADS):
        cols = slice(h * MEM_HEAD_DIM, (h + 1) * MEM_HEAD_DIM)
        kh = kk[:, cols]
        ms = jnp.mean(kh * kh, axis=-1, keepdims=True)
        k_ref[:, cols] = (kh * lax.rsqrt(ms + EPS) * kng_ref[...]).astype(BF16)
    v_ref[...] = jnp.dot(mb, wv_ref[...], preferred_element_type=F32).astype(BF16)


def _mem_kv(mem, g, wk, wv, kng):
    B = mem.shape[0]
    return pl.pallas_call(
        _mem_kv_kernel,
        grid=(B,),
        in_specs=[
            pl.BlockSpec((None, MEM_LEN, D_MODEL), lambda b: (b, 0, 0)),
            _const_spec((1, D_MODEL)),
            _const_spec((D_MODEL, MEM_WIDTH)),
            _const_spec((D_MODEL, MEM_WIDTH)),
            _const_spec((1, MEM_HEAD_DIM)),
        ],
        out_specs=[
            pl.BlockSpec((None, MEM_LEN, MEM_WIDTH), lambda b: (b, 0, 0)),
            pl.BlockSpec((None, MEM_LEN, MEM_WIDTH), lambda b: (b, 0, 0)),
        ],
        out_shape=[
            jax.ShapeDtypeStruct((B, MEM_LEN, MEM_WIDTH), BF16),
            jax.ShapeDtypeStruct((B, MEM_LEN, MEM_WIDTH), BF16),
        ],
        compiler_params=_params(("parallel",), 32),
        name="mem_kv",
    )(mem, g, wk, wv, kng)


MA_TM = 512


def _mem_attn_kernel(qm_ref, km_ref, vm_ref, g2_ref, m0_ref, qng_ref, wpc_ref, o_ref, c_scr):
    for h in range(MEM_HEADS):
        cols = slice(h * MEM_HEAD_DIM, (h + 1) * MEM_HEAD_DIM)
        qh = qm_ref[:, cols].astype(F32)
        ms = jnp.mean(qh * qh, axis=-1, keepdims=True)
        qn = (qh * (lax.rsqrt(ms + EPS) * MEM_SCALE) * qng_ref[...]).astype(BF16)
        s = lax.dot_general(qn, km_ref[:, cols], (((1,), (1,)), ((), ())), preferred_element_type=F32)
        e = jnp.exp(s - jnp.max(s, axis=-1, keepdims=True))
        p = (e / jnp.sum(e, axis=-1, keepdims=True)).astype(BF16)
        c_scr[:, cols] = jnp.dot(p, vm_ref[:, cols], preferred_element_type=F32).astype(BF16)
    mc = jnp.dot(c_scr[...], wpc_ref[...], preferred_element_type=F32)
    o_ref[...] = (m0_ref[...].astype(F32) + g2_ref[...].astype(F32) * mc).astype(BF16)


def _mem_attn(pg, km, vm, m0, qng, wpc, S):
    T = pg.shape[0]
    spb = S // MA_TM
    return pl.pallas_call(
        _mem_attn_kernel,
        grid=(T // MA_TM,),
        in_specs=[
            pl.BlockSpec((MA_TM, MEM_WIDTH), lambda i: (i, (2 * GM_WIDTH + Q_LORA + KV_LORA) // MEM_WIDTH)),
            pl.BlockSpec((None, MEM_LEN, MEM_WIDTH), lambda i: (i // spb, 0, 0)),
            pl.BlockSpec((None, MEM_LEN, MEM_WIDTH), lambda i: (i // spb, 0, 0)),
            pl.BlockSpec((MA_TM, D_MODEL), lambda i: (i, PROJ_COLS // D_MODEL + 2)),
            pl.BlockSpec((MA_TM, D_MODEL), lambda i: (i, 0)),
            _const_spec((1, MEM_HEAD_DIM)),
            _const_spec((MEM_WIDTH, D_MODEL)),
        ],
        out_specs=pl.BlockSpec((MA_TM, D_MODEL), lambda i: (i, 0)),
        out_shape=jax.ShapeDtypeStruct((T, D_MODEL), BF16),
        scratch_shapes=[pltpu.VMEM((MA_TM, MEM_WIDTH), BF16)],
        compiler_params=_params(("parallel",), 40),
        name="mem_attention",
    )(pg, km, vm, pg, m0, qng, wpc)


MG_TM = 512


def _pack_bf16_pair(a, b):
    hi = pltpu.bitcast(a.astype(BF16).astype(F32), jnp.uint32)
    lo = pltpu.bitcast(b.astype(BF16).astype(F32), jnp.uint32)
    return hi | (lo >> 16)


def _unpack_bf16_pair(p):
    hi = pltpu.bitcast(p & jnp.uint32(0xFFFF0000), F32)
    lo = pltpu.bitcast(p << 16, F32)
    return hi, lo


SUBLANES = 8
ROW_WORDS = D_MODEL // 2
ROW_CHUNKS = ROW_WORDS // LANES
assert ROW_CHUNKS == SUBLANES


def _row_chunk(n_rows, c):
    return pl.ds(c, n_rows, stride=SUBLANES)


def _store_row_tiles(ref, n_rows, packed):
    for c in range(ROW_CHUNKS):
        ref[_row_chunk(n_rows, c), :] = packed[:, c * LANES:(c + 1) * LANES]


def _attn_proj_kernel(b_ref, g1_ref, m1_ref, wpb_ref, o_ref, zero_ref):
    mb = jnp.dot(b_ref[...], wpb_ref[...], preferred_element_type=F32)
    o_ref[...] = (m1_ref[...].astype(F32) + g1_ref[...].astype(F32) * mb).astype(BF16)
    zero_ref[...] = jnp.zeros_like(zero_ref)


def _attn_proj(b_attn, pg, m1, wpb, n_zero_rows):
    T = m1.shape[0]
    steps = T // MG_TM
    assert n_zero_rows % steps == 0
    zero_block = (n_zero_rows // steps * SUBLANES, LANES)
    row = pl.BlockSpec((MG_TM, D_MODEL), lambda i: (i, 0))
    return pl.pallas_call(
        _attn_proj_kernel,
        grid=(steps,),
        in_specs=[
            row,
            pl.BlockSpec((MG_TM, D_MODEL), lambda i: (i, PROJ_COLS // D_MODEL + 1)),
            row,
            _const_spec((MLA_HEADS * V_DIM, D_MODEL)),
        ],
        out_specs=[row, pl.BlockSpec(zero_block, lambda i: (i, 0))],
        out_shape=[jax.ShapeDtypeStruct((T, D_MODEL), BF16),
                   jax.ShapeDtypeStruct((n_zero_rows * SUBLANES, LANES), jnp.uint32)],
        compiler_params=_params(("parallel",), 56),
        name="attn_proj",
    )(b_attn, pg, m1, wpb)


def _merge_kernel(mg_ref, x_ref, wo_ref, ln2_ref, rw_ref, rb_ref,
                  x1_ref, h2p_ref, code_ref, gate_ref, cnt_ref, cnt_scr):
    @pl.when(pl.program_id(0) == 0)
    def _():
        cnt_scr[...] = jnp.zeros_like(cnt_scr)

    x1 = x_ref[...] + jnp.dot(mg_ref[...], wo_ref[...], preferred_element_type=F32)
    x1_ref[...] = x1
    ms = jnp.mean(x1 * x1, axis=-1, keepdims=True)
    h2 = x1 * lax.rsqrt(ms + EPS) * ln2_ref[...]
    _store_row_tiles(h2p_ref, MG_TM, _pack_bf16_pair(h2[:, :D_MODEL // 2], h2[:, D_MODEL // 2:]))

    logits = jnp.dot(h2.astype(BF16), rw_ref[...], preferred_element_type=F32)
    lane = lax.broadcasted_iota(jnp.int32, (MG_TM, LANES), 1)
    work = jnp.where(lane < N_EXPERTS, logits + rb_ref[...], NEG_INF)
    earlier = (lax.broadcasted_iota(jnp.int32, (MG_TM, MG_TM), 1)
               < lax.broadcasted_iota(jnp.int32, (MG_TM, MG_TM), 0)).astype(BF16)
    base = cnt_scr[...]
    code_out = jnp.zeros((MG_TM, LANES), jnp.int32)
    val_out = jnp.zeros((MG_TM, LANES), F32)
    top = None
    denom = jnp.zeros((MG_TM, 1), F32)
    for k in range(TOP_K):
        mk = jnp.max(work, axis=-1, keepdims=True)
        ik = jnp.min(jnp.where(work == mk, lane, LANES), axis=-1, keepdims=True)
        hit = lane == ik
        work = jnp.where(hit, NEG_INF, work)
        if top is None:
            top = mk
        ek = jnp.exp(mk - top)
        denom = denom + ek
        val_out = jnp.where(lane == k, ek, val_out)
        onehot = hit.astype(BF16)
        prefix = jnp.dot(earlier, onehot, preferred_element_type=F32) + base
        rank = jnp.sum(jnp.where(hit, prefix, 0.0), axis=-1, keepdims=True).astype(jnp.int32)
        code_out = jnp.where(lane == k, rank * N_EXPERTS + ik, code_out)
        base = base + jnp.sum(onehot.astype(F32), axis=0, keepdims=True)
    code_ref[...] = code_out
    gate_ref[...] = val_out / denom
    cnt_scr[...] = base
    cnt_ref[...] = base


def _merge(merged, x2, wo, ln2, rw, rb):
    T = x2.shape[0]
    row = lambda w: pl.BlockSpec((MG_TM, w), lambda i: (i, 0))
    return pl.pallas_call(
        _merge_kernel,
        grid=(T // MG_TM,),
        in_specs=[
            row(D_MODEL),
            row(D_MODEL),
            _const_spec((D_MODEL, D_MODEL)),
            _const_spec((1, D_MODEL)),
            _const_spec((D_MODEL, LANES)),
            _const_spec((1, LANES)),
        ],
        out_specs=[row(D_MODEL), pl.BlockSpec((MG_TM * SUBLANES, LANES), lambda i: (i, 0)), row(LANES),
                   row(LANES), pl.BlockSpec((1, LANES), lambda i: (0, 0))],
        out_shape=[
            jax.ShapeDtypeStruct((T, D_MODEL), F32),
            jax.ShapeDtypeStruct((T * SUBLANES, LANES), jnp.uint32),
            jax.ShapeDtypeStruct((T, LANES), jnp.int32),
            jax.ShapeDtypeStruct((T, LANES), F32),
            jax.ShapeDtypeStruct((1, LANES), F32),
        ],
        scratch_shapes=[pltpu.VMEM((1, LANES), F32)],
        compiler_params=_params(("arbitrary",), 56),
        name="merge_router",
    )(merged, x2, wo, ln2, rw, rb)


DP_TM = 512
DMA_UNROLL = 8


def _row_tile(row):
    return pl.ds(pl.multiple_of(row * SUBLANES, SUBLANES), SUBLANES)


def _dispatch_kernel(dest_ref, h2p_ref, xs_in, xs_hbm, sem):
    del xs_in

    def issue(t, carry):
        src = h2p_ref.at[_row_tile(t), :]
        for k in range(TOP_K):
            row = dest_ref[0, t * TOP_K + k]
            pltpu.make_async_copy(src, xs_hbm.at[_row_tile(row), :], sem).start(priority=k % 2)
        return carry

    lax.fori_loop(0, DP_TM, issue, 0, unroll=DMA_UNROLL // TOP_K)
    all_rows = xs_hbm.at[pl.ds(0, DP_TM * TOP_K * SUBLANES), :]
    pltpu.make_async_copy(all_rows, all_rows, sem).wait()


def _dispatch(dest3, h2p, xs_zero):
    T = h2p.shape[0] // SUBLANES
    return pl.pallas_call(
        _dispatch_kernel,
        grid=(T // DP_TM,),
        in_specs=[
            pl.BlockSpec((None, 1, DP_TM * TOP_K), lambda i: (i, 0, 0), memory_space=pltpu.SMEM),
            pl.BlockSpec((DP_TM * SUBLANES, LANES), lambda i: (i, 0)),
            pl.BlockSpec(memory_space=pl.ANY),
        ],
        out_specs=pl.BlockSpec(memory_space=pl.ANY),
        out_shape=jax.ShapeDtypeStruct(xs_zero.shape, xs_zero.dtype),
        scratch_shapes=[pltpu.SemaphoreType.DMA(())],
        input_output_aliases={2: 0},
        compiler_params=_params(("arbitrary",), 32),
        name="moe_dispatch",
    )(dest3, h2p, xs_zero)


GROUP_SUB = 4
GROUP_ROWS = GROUP_SUB * ROW_BLOCK
FF_TILE = 512
N_FF_TILES = D_FF // FF_TILE
DOWN_CHUNK = 512


def _expert_kernel(ge_ref, gs_ref, gn_ref, xs_in, wg_ref, wu_ref, bg_ref, bu_ref, wd_ref, bd_ref,
                   rows_hbm, xwin, x_scr, acc_scr, stage, wsem, sem):
    del ge_ref, xs_in
    g = pl.program_id(0)
    f = pl.program_id(1)
    ns = gn_ref[g]
    half = D_MODEL // 2
    slot = g & 1

    def win_copy(group, s):
        rows = pl.ds(gs_ref[group] * (ROW_BLOCK * SUBLANES), GROUP_ROWS * SUBLANES)
        return pltpu.make_async_copy(rows_hbm.at[rows, :], xwin.at[s], wsem.at[s])

    def out_copy(s, start_block):
        rows = pl.ds((start_block + s) * (ROW_BLOCK * SUBLANES), ROW_BLOCK * SUBLANES)
        return pltpu.make_async_copy(stage.at[s], rows_hbm.at[rows, :], sem.at[s])

    @pl.when((f == 0) & (ns > 0))
    def _():
        @pl.when(g == 0)
        def _():
            win_copy(0, 0).start()
            acc_scr[...] = jnp.zeros_like(acc_scr)

        win_copy(g, slot).wait()
        @pl.when(g + 1 < pl.num_programs(0))
        def _():
            @pl.when(gn_ref[g + 1] > 0)
            def _():
                win_copy(g + 1, 1 - slot).start()

        for n in range(1, GROUP_SUB + 1):
            @pl.when(ns == n)
            def _(n=n):
                m = n * ROW_BLOCK
                for c in range(ROW_CHUNKS):
                    hi, lo = _unpack_bf16_pair(xwin[slot, _row_chunk(m, c), :])
                    x_scr[:m, c * LANES:(c + 1) * LANES] = hi.astype(BF16)
                    x_scr[:m, half + c * LANES:half + (c + 1) * LANES] = lo.astype(BF16)

    for n in range(1, GROUP_SUB + 1):
        @pl.when(ns == n)
        def _(n=n):
            m = n * ROW_BLOCK
            x = x_scr[:m, :]
            gate = jnp.dot(x, wg_ref[...].astype(BF16), preferred_element_type=F32) + bg_ref[...]
            up = jnp.dot(x, wu_ref[...].astype(BF16), preferred_element_type=F32) + bu_ref[...]
            gate = jnp.minimum(gate, SWIGLU_LIMIT)
            up = jnp.clip(up, -SWIGLU_LIMIT, SWIGLU_LIMIT)
            glu = gate * jax.nn.sigmoid(gate * SWIGLU_ALPHA)
            act = ((up + 1.0) * glu).astype(BF16)
            for c in range(D_MODEL // DOWN_CHUNK):
                cols = slice(c * DOWN_CHUNK, (c + 1) * DOWN_CHUNK)
                start = jnp.where(f == 0, jnp.broadcast_to(bd_ref[:, cols], (m, DOWN_CHUNK)), acc_scr[:m, cols])
                acc_scr[:m, cols] = start + jnp.dot(act, wd_ref[:, cols].astype(BF16),
                                                    preferred_element_type=F32)

    @pl.when(f == N_FF_TILES - 1)
    def _():
        @pl.when(g > 0)
        def _():
            prev = gn_ref[g - 1]
            for s in range(GROUP_SUB):
                @pl.when(s < prev)
                def _(s=s):
                    out_copy(s, 0).wait()

        for s in range(GROUP_SUB):
            @pl.when(s < ns)
            def _(s=s):
                y = acc_scr[s * ROW_BLOCK:(s + 1) * ROW_BLOCK, :]
                _store_row_tiles(stage.at[s], ROW_BLOCK, _pack_bf16_pair(y[:, :half], y[:, half:]))
                out_copy(s, gs_ref[g]).start()

        @pl.when(g == pl.num_programs(0) - 1)
        def _():
            for s in range(GROUP_SUB):
                @pl.when(s < ns)
                def _(s=s):
                    out_copy(s, 0).wait()


def _experts(g_exp, g_start, g_nsub, xs, wgu, bgu, wd, bd):
    def ff(f, gn, g):
        return jnp.where(gn[g] > 0, f, N_FF_TILES - 1)

    grid_spec = pltpu.PrefetchScalarGridSpec(
        num_scalar_prefetch=3,
        grid=(g_exp.shape[0], N_FF_TILES),
        in_specs=[
            pl.BlockSpec(memory_space=pl.ANY),
            pl.BlockSpec((None, D_MODEL, FF_TILE), lambda g, f, ge, gs, gn: (ge[g], 0, ff(f, gn, g))),
            pl.BlockSpec((None, D_MODEL, FF_TILE),
                         lambda g, f, ge, gs, gn: (ge[g], 0, N_FF_TILES + ff(f, gn, g))),
            pl.BlockSpec((None, 1, FF_TILE), lambda g, f, ge, gs, gn: (ge[g], 0, ff(f, gn, g))),
            pl.BlockSpec((None, 1, FF_TILE), lambda g, f, ge, gs, gn: (ge[g], 0, N_FF_TILES + ff(f, gn, g))),
            pl.BlockSpec((None, FF_TILE, D_MODEL), lambda g, f, ge, gs, gn: (ge[g], ff(f, gn, g), 0)),
            pl.BlockSpec((None, 1, D_MODEL), lambda g, f, ge, gs, gn: (ge[g], 0, 0)),
        ],
        out_specs=pl.BlockSpec(memory_space=pl.ANY),
        scratch_shapes=[
            pltpu.VMEM((2, GROUP_ROWS * SUBLANES, LANES), jnp.uint32),
            pltpu.VMEM((GROUP_ROWS, D_MODEL), BF16),
            pltpu.VMEM((GROUP_ROWS, D_MODEL), F32),
            pltpu.VMEM((GROUP_SUB, ROW_BLOCK * SUBLANES, LANES), jnp.uint32),
            pltpu.SemaphoreType.DMA((2,)),
            pltpu.SemaphoreType.DMA((GROUP_SUB,)),
        ],
    )
    return pl.pallas_call(
        _expert_kernel,
        grid_spec=grid_spec,
        out_shape=jax.ShapeDtypeStruct(xs.shape, jnp.uint32),
        input_output_aliases={3: 0},
        compiler_params=_params(("arbitrary", "arbitrary"), 58),
        name="moe_experts",
    )(g_exp, g_start, g_nsub, xs, wgu, wgu, bgu, bgu, wd, bd)


CB_TM = 256


def _combine_kernel(dest_ref, dest_next_ref, ys_hbm, x1_ref, gate_ref, o_ref, buf, sem):
    i = pl.program_id(0)
    slot = i & 1

    def gather_tile(dest, s):
        def issue(t, carry):
            for k in range(TOP_K):
                row = dest[0, t * TOP_K + k]
                pltpu.make_async_copy(ys_hbm.at[_row_tile(row), :], buf.at[s, k, _row_tile(t), :],
                                      sem.at[s]).start(priority=k % 2)
            return carry

        lax.fori_loop(0, CB_TM, issue, 0, unroll=DMA_UNROLL // TOP_K)

    @pl.when(i == 0)
    def _():
        gather_tile(dest_ref, 0)

    @pl.when(i + 1 < pl.num_programs(0))
    def _():
        gather_tile(dest_next_ref, 1 - slot)

    pltpu.make_async_copy(buf.at[slot], buf.at[slot], sem.at[slot]).wait()

    half = D_MODEL // 2
    gates = gate_ref[...]
    gate_k = [jnp.broadcast_to(gates[:, k:k + 1], (CB_TM, LANES)) for k in range(TOP_K)]
    for c in range(ROW_CHUNKS):
        cols_hi = slice(c * LANES, (c + 1) * LANES)
        cols_lo = slice(half + c * LANES, half + (c + 1) * LANES)
        out_hi = x1_ref[:, cols_hi]
        out_lo = x1_ref[:, cols_lo]
        for k in range(TOP_K):
            hi, lo = _unpack_bf16_pair(buf[slot, k, _row_chunk(CB_TM, c), :])
            out_hi = out_hi + gate_k[k] * hi
            out_lo = out_lo + gate_k[k] * lo
        o_ref[:, cols_hi] = out_hi
        o_ref[:, cols_lo] = out_lo


def _combine(dest3, ys, x1, gates):
    T = x1.shape[0]
    n = T // CB_TM
    dest_spec = lambda f: pl.BlockSpec((None, 1, CB_TM * TOP_K), f, memory_space=pltpu.SMEM)
    return pl.pallas_call(
        _combine_kernel,
        grid=(n,),
        in_specs=[
            dest_spec(lambda i: (i, 0, 0)),
            dest_spec(lambda i: (jnp.minimum(i + 1, n - 1), 0, 0)),
            pl.BlockSpec(memory_space=pl.ANY),
            pl.BlockSpec((CB_TM, D_MODEL), lambda i: (i, 0)),
            pl.BlockSpec((CB_TM, LANES), lambda i: (i, 0)),
        ],
        out_specs=pl.BlockSpec((CB_TM, D_MODEL), lambda i: (i, 0)),
        out_shape=jax.ShapeDtypeStruct((T, D_MODEL), F32),
        scratch_shapes=[pltpu.VMEM((2, TOP_K, CB_TM * SUBLANES, LANES), jnp.uint32),
                        pltpu.SemaphoreType.DMA((2,))],
        compiler_params=_params(("arbitrary",), 32),
        name="moe_combine",
    )(dest3, dest3, ys, x1, gates)


def _group_tables(counts_f32, max_groups):
    i32 = jnp.int32
    counts = counts_f32[0, :N_EXPERTS].astype(i32)
    nb = (counts + ROW_BLOCK - 1) // ROW_BLOCK
    ng = (nb + GROUP_SUB - 1) // GROUP_SUB
    upto = jnp.arange(N_EXPERTS)[None, :] <= jnp.arange(N_EXPERTS)[:, None]
    pad_ends = jnp.sum(jnp.where(upto, nb[None, :], 0), axis=1) * ROW_BLOCK
    pad_starts = pad_ends - nb * ROW_BLOCK
    g_ends = jnp.sum(jnp.where(upto, ng[None, :], 0), axis=1)
    g_starts = g_ends - ng
    n_groups = g_ends[-1]

    g = jnp.arange(max_groups, dtype=i32)
    gg = jnp.minimum(g, n_groups - 1)
    e = jnp.minimum(jnp.sum((g_ends[None, :] <= gg[:, None]).astype(i32), axis=1), N_EXPERTS - 1)
    pick = e[:, None] == jnp.arange(N_EXPERTS)[None, :]
    take = lambda table: jnp.sum(jnp.where(pick, table[None, :], 0), axis=1)
    nb_g, ng_g = take(nb), jnp.maximum(take(ng), 1)
    j = gg - take(g_starts)
    base, rem = nb_g // ng_g, nb_g % ng_g
    nsub = jnp.where(g < n_groups, base + (j < rem).astype(i32), 0)
    start_block = take(pad_starts) // ROW_BLOCK + j * base + jnp.minimum(j, rem)
    return pad_starts.astype(i32), e.astype(i32), start_block.astype(i32), nsub.astype(i32)


def _rope_constants():
    lane = np.arange(LANES)
    half = QK_ROPE // 2
    inv = 1.0 / (ROPE_THETA ** (np.arange(0, QK_ROPE, 2, dtype=np.float32) / QK_ROPE))
    cst = np.zeros((8, LANES), np.float32)
    cst[0, :QK_ROPE] = inv.astype(np.float32)[lane[:QK_ROPE] % half]
    cst[1, :QK_ROPE] = 1.0
    cst[2, :half] = -1.0
    cst[2, half:QK_ROPE] = 1.0
    return jnp.asarray(cst)


def _swap_halves(a):
    half = QK_ROPE // 2
    return jnp.concatenate([a[..., half:], a[..., :half]], axis=-1)


def kernel(x, mem, positions, ln1_g, w_in, w_gate, b_gate, gmlp_ln_g, gmlp_ln_b, gmlp_ws, gmlp_bs, w_pa,
           mla_cq_g, mla_w_uq, mla_ckv_g, mla_w_ukv, mla_qn_g, mla_kn_g, w_pb, mem_ln_g, mem_w_k, mem_w_v,
           mem_qn_g, mem_kn_g, w_pc, w_o, ln2_g, router_w, router_b, moe_w_gu, moe_b_gu, moe_w_down,
           moe_b_down):
    B, S, D = x.shape
    T = B * S
    x2 = x.reshape(T, D)
    for l in range(ln1_g.shape[0]):
        o_kr = 2 * GM_WIDTH + Q_LORA + KV_LORA
        o_qm = o_kr + QK_ROPE
        wi = w_in[l]
        w1 = jnp.concatenate([wi[:, :o_kr], wi[:, o_qm:], w_gate[l]], axis=1).astype(BF16)
        b1 = jnp.concatenate([jnp.zeros((PROJ_COLS,), F32), b_gate[l]])[None, :]
        w_kr = wi[:, o_kr:o_qm]
        wr = jnp.concatenate([w_kr, _swap_halves(w_kr)], axis=1).astype(BF16)

        wq3 = mla_w_uq[l].reshape(Q_LORA, MLA_HEADS, QK_DIM)
        wq = jnp.concatenate([wq3, _swap_halves(wq3[..., QK_NOPE:])], axis=-1)
        wq = wq.reshape(Q_LORA, MLA_HEADS * QK_PAD).astype(BF16)
        wkv = mla_w_ukv[l].astype(BF16)
        gq = jnp.concatenate([mla_qn_g[l], _swap_halves(mla_qn_g[l][QK_NOPE:])])[None, :]
        gk = jnp.concatenate([mla_kn_g[l], _swap_halves(mla_kn_g[l][QK_NOPE:])])[None, :]

        bias_full = jnp.broadcast_to(gmlp_bs[l].T[:, :, None], (GM_CHUNK, GM_GROUPS, GM_CHUNK))
        bias_full = bias_full.reshape(GM_CHUNK, GM_WIDTH)

        rw = jnp.pad(router_w[l], ((0, 0), (0, LANES - N_EXPERTS))).astype(BF16)
        rb = jnp.pad(router_b[l], (0, LANES - N_EXPERTS))[None, :]

        pg, kr = _norm_proj(x2, ln1_g[l][None, :], w1, b1, wr)
        m0 = _gmlp(pg, gmlp_ln_g[l][None, :], gmlp_ln_b[l][None, :], gmlp_ws[l].astype(BF16), bias_full,
                   w_pa[l].astype(BF16))
        q, k, vt = _mla_qkv(pg, kr, positions.reshape(T, 1), mla_cq_g[l][None, :], mla_ckv_g[l][None, :],
                           wq, wkv, gq, gk, _rope_constants(), B, S)
        b_attn = _flash(q, k, vt)
        km, vm = _mem_kv(mem, mem_ln_g[l][None, :], mem_w_k[l].astype(BF16), mem_w_v[l].astype(BF16),
                         mem_kn_g[l][None, :])
        m1 = _mem_attn(pg, km, vm, m0, mem_qn_g[l][None, :], w_pc[l].astype(BF16), S)
        n_rows = T * TOP_K + N_EXPERTS * ROW_BLOCK
        merged, xs_zero = _attn_proj(b_attn, pg, m1, w_pb[l].astype(BF16), n_rows + GROUP_ROWS)
        x1, h2p, code, gates, counts = _merge(merged, x2, w_o[l].astype(BF16), ln2_g[l][None, :], rw, rb)

        max_groups = n_rows // GROUP_ROWS + N_EXPERTS
        pad_starts, g_exp, g_start, g_nsub = _group_tables(counts, max_groups)
        codes = code[:, :TOP_K]
        pick = (codes & (N_EXPERTS - 1))[..., None] == jnp.arange(N_EXPERTS)
        dest = (codes >> EXPERT_BITS) + jnp.sum(jnp.where(pick, pad_starts, 0), axis=-1)
        xs = _dispatch(dest.reshape(T // DP_TM, 1, DP_TM * TOP_K), h2p, xs_zero)
        ys = _experts(g_exp, g_start, g_nsub, xs, moe_w_gu[l], moe_b_gu[l][:, None, :],
                      moe_w_down[l], moe_b_down[l][:, None, :])
        x2 = _combine(dest.reshape(T // CB_TM, 1, CB_TM * TOP_K), ys, x1, gates)
    return x2.reshape(B, S, D)
```
